```python
import math
import jax, jax.numpy as jnp
from jax import lax
import numpy as np

D_MODEL = 1024
BATCH = 4
SEQ = 8192
DEPTH = 1

CHUNK = 64
Q_BLOCK = 128
HEAD_DIM = 64
ROPE_DIM = 64
ROPE_THETA = 10000.0
A_WIDTH = D_MODEL // 2
A_HEADS = A_WIDTH // HEAD_DIM
IDX_HEADS = 8
IDX_DIM = 64
IDX_TOPK_MAX = 256
B_WIDTH = D_MODEL - A_WIDTH
B_QK_DIM = HEAD_DIM
B_V_DIM = 2 * B_QK_DIM
B_HEADS = B_WIDTH // B_V_DIM
SPLIT_SIZES = (
    A_HEADS * HEAD_DIM, A_HEADS * HEAD_DIM, A_HEADS * HEAD_DIM,
    IDX_HEADS * IDX_DIM, IDX_DIM, IDX_HEADS,
    B_HEADS * 2 * B_QK_DIM, B_HEADS * 2 * B_QK_DIM, B_HEADS * B_V_DIM,
)
PROJ_WIDTH = sum(SPLIT_SIZES)
SPLIT_POINTS = tuple(int(v) for v in np.cumsum(SPLIT_SIZES)[:-1])
N_EXPERTS = 64
TOP_K = 8
N_GROUPS = 8
TOPK_GROUPS = 4
EXPERT_DIM = 256
SHARED_DIM = 256
ROUTED_SCALE = 2.5
MOE_BLOCK = 256
PLE_DIM = 256
LN_EPS = 1e-5
RMS_EPS = 1e-5
DEEPNORM_ALPHA = (2.0 * DEPTH) ** 0.25
DEEPNORM_BETA = (8.0 * DEPTH) ** -0.25

kernel_name = "hybrid_dsa_diffattn_moe_deepnorm_block"


def layer_norm(x, g, b):
    xf = x.astype(jnp.float32)
    mu = jnp.mean(xf, axis=-1, keepdims=True)
    var = jnp.mean(jnp.square(xf - mu), axis=-1, keepdims=True)
    return ((xf - mu) * lax.rsqrt(var + LN_EPS) * g + b).astype(x.dtype)


def rope_tables(positions):
    inv = ROPE_THETA ** (-jnp.arange(0, ROPE_DIM, 2, dtype=jnp.float32) / ROPE_DIM)
    ang = positions.astype(jnp.float32)[..., None] * inv
    return jnp.cos(ang), jnp.sin(ang)


def apply_rope(x, cos, sin):
    x1, x2 = jnp.split(x.astype(jnp.float32), 2, axis=-1)
    c = cos[:, :, None, :]
    s = sin[:, :, None, :]
    return jnp.concatenate([x1 * c - x2 * s, x2 * c + x1 * s], axis=-1).astype(x.dtype)


def dsa_mixer(q, k, v, q_idx, k_idx, w_idx, n_keep):
    bsz, seq, n_heads, dh = q.shape
    n_blocks = seq // Q_BLOCK
    key_chunk = jnp.arange(seq) // CHUNK
    attn_scale = dh ** -0.5
    w_scale = (IDX_HEADS ** -0.5) * (IDX_DIM ** -0.5)

    def block(i):
        t0 = i * Q_BLOCK
        qb = lax.dynamic_slice_in_dim(q, t0, Q_BLOCK, axis=1)
        qib = lax.dynamic_slice_in_dim(q_idx, t0, Q_BLOCK, axis=1)
        wb = lax.dynamic_slice_in_dim(w_idx, t0, Q_BLOCK, axis=1).astype(jnp.float32) * w_scale
        q_chunk = (t0 + jnp.arange(Q_BLOCK)) // CHUNK
        admissible = key_chunk[None, :] <= q_chunk[:, None]
        rel = jax.nn.relu(jnp.einsum('bthd,bsd->bths', qib, k_idx,
                                     preferred_element_type=jnp.float32))
        score = jnp.einsum('bth,bths->bts', wb, rel)
        score = jnp.where(admissible[None], score, -jnp.inf)
        _, sel = lax.top_k(score, n_keep)
        sel_ok = key_chunk[sel] <= q_chunk[None, :, None]
        k_sel = jax.vmap(lambda kb, ib: kb[ib])(k, sel)
        v_sel = jax.vmap(lambda vb, ib: vb[ib])(v, sel)
        logits = jnp.einsum('bthd,btkhd->bhtk', qb, k_sel,
                            preferred_element_type=jnp.float32) * attn_scale
        logits = jnp.where(sel_ok[:, None], logits, -jnp.inf)
        prob = jax.nn.softmax(logits, axis=-1).astype(v.dtype)
        return jnp.einsum('bhtk,btkhd->bthd', prob, v_sel)

    out = lax.map(block, jnp.arange(n_blocks))
    return out.transpose(1, 0, 2, 3, 4).reshape(bsz, seq, n_heads * dh)


def diff_mixer(q, k, v, lam, subln_g, lambda_init):
    bsz, seq, n_heads, _, dq = q.shape
    dv = v.shape[-1]
    n_blocks = seq // Q_BLOCK
    key_chunk = jnp.arange(seq) // CHUNK
    scale = dq ** -0.5

    def block(i):
        t0 = i * Q_BLOCK
        qb = lax.dynamic_slice_in_dim(q, t0, Q_BLOCK, axis=1)
        q_chunk = (t0 + jnp.arange(Q_BLOCK)) // CHUNK
        mask = key_chunk[None, :] <= q_chunk[:, None]
        logits = jnp.einsum('bthmd,bshmd->bhmts', qb, k,
                            preferred_element_type=jnp.float32) * scale
        logits = jnp.where(mask[None, None, None], logits, -jnp.inf)
        prob = jax.nn.softmax(logits, axis=-1)
        attn = prob[:, :, 0] - lam * prob[:, :, 1]
        return jnp.einsum('bhts,bshd->bthd', attn.astype(v.dtype), v)

    out = lax.map(block, jnp.arange(n_blocks))
    out = out.transpose(1, 0, 2, 3, 4).reshape(bsz, seq, n_heads, dv)
    of = out.astype(jnp.float32)
    of = of * lax.rsqrt(jnp.mean(jnp.square(of), axis=-1, keepdims=True) + RMS_EPS)
    of = of * subln_g * (1.0 - lambda_init)
    return of.astype(v.dtype).reshape(bsz, seq, n_heads * dv)


def moe(h, w_router, router_bias, w_exp_gate, w_exp_up, w_exp_down, w_sh_gate, w_sh_up, w_sh_down):
    bsz, seq, d = h.shape
    n_tok = bsz * seq
    xt = h.reshape(n_tok, d)
    scores = jax.nn.sigmoid(jnp.einsum('td,de->te', xt, w_router,
                                       preferred_element_type=jnp.float32))
    biased = scores + router_bias.astype(jnp.float32)
    grp = biased.reshape(n_tok, N_GROUPS, N_EXPERTS // N_GROUPS)
    grp_score = jnp.sum(lax.top_k(grp, 2)[0], axis=-1)
    _, top_g = lax.top_k(grp_score, TOPK_GROUPS)
    gmask = jnp.any(top_g[..., None] == jnp.arange(N_GROUPS), axis=-2)
    emask = jnp.repeat(gmask, N_EXPERTS // N_GROUPS, axis=-1)
    _, eidx = lax.top_k(jnp.where(emask, biased, -jnp.inf), TOP_K)
    gate = jnp.take_along_axis(scores, eidx, axis=-1)
    gate = gate / jnp.sum(gate, axis=-1, keepdims=True) * ROUTED_SCALE

    n_assign = n_tok * TOP_K
    flat_e = eidx.reshape(n_assign)
    flat_tok = jnp.repeat(jnp.arange(n_tok, dtype=jnp.int32), TOP_K)
    flat_w = gate.reshape(n_assign)
    order = jnp.argsort(flat_e)
    e_sorted = flat_e[order]
    counts = jnp.zeros((N_EXPERTS,), jnp.int32).at[flat_e].add(1)
    padded = (counts + MOE_BLOCK - 1) // MOE_BLOCK * MOE_BLOCK
    start = jnp.cumsum(counts) - counts
    pend = jnp.cumsum(padded)
    pstart = pend - padded
    dest = pstart[e_sorted] + (jnp.arange(n_assign, dtype=jnp.int32) - start[e_sorted])
    n_blocks = (n_assign + N_EXPERTS * (MOE_BLOCK - 1) + MOE_BLOCK - 1) // MOE_BLOCK
    n_rows = n_blocks * MOE_BLOCK
    tok_buf = jnp.zeros((n_rows,), jnp.int32).at[dest].set(flat_tok[order])
    w_buf = jnp.zeros((n_rows,), gate.dtype).at[dest].set(flat_w[order])
    blk_e = jnp.minimum(jnp.searchsorted(pend, jnp.arange(n_blocks, dtype=jnp.int32) * MOE_BLOCK,
                                         side='right'), N_EXPERTS - 1)

    def step(acc, b):
        toks = lax.dynamic_slice_in_dim(tok_buf, b * MOE_BLOCK, MOE_BLOCK)
        wts = lax.dynamic_slice_in_dim(w_buf, b * MOE_BLOCK, MOE_BLOCK)
        e = blk_e[b]
        xb = xt[toks]
        hid = jax.nn.silu(xb @ w_exp_gate[e]) * (xb @ w_exp_up[e])
        y = (hid @ w_exp_down[e]) * wts[:, None]
        return acc.at[toks].add(y.astype(acc.dtype)), None

    routed, _ = lax.scan(step, jnp.zeros_like(xt), jnp.arange(n_blocks))
    shared = (jax.nn.silu(xt @ w_sh_gate) * (xt @ w_sh_up)) @ w_sh_down
    return (routed + shared).reshape(bsz, seq, d)


def setup_inputs(seed: int = 0) -> dict:
    key = jax.random.key(seed)
    ks = jax.random.split(key, 24)

    def nrm(k, shape, scale):
        return jax.random.normal(k, shape, jnp.float32) * scale

    d = D_MODEL
    col_scale = jnp.concatenate([
        jnp.ones((2 * A_HEADS * HEAD_DIM,), jnp.float32),
        jnp.full((A_HEADS * HEAD_DIM,), DEEPNORM_BETA, jnp.float32),
        jnp.ones((IDX_HEADS * IDX_DIM + IDX_DIM + IDX_HEADS + 2 * B_HEADS * 2 * B_QK_DIM,), jnp.float32),
        jnp.full((B_HEADS * B_V_DIM,), DEEPNORM_BETA, jnp.float32),
    ])
    return {
        "x": nrm(ks[0], (BATCH, SEQ, d), 1.0),
        "p": nrm(ks[1], (DEPTH, BATCH, SEQ, PLE_DIM), 1.0),
        "positions": jnp.broadcast_to(jnp.arange(SEQ, dtype=jnp.int32), (BATCH, SEQ)),
        "ln_emb_g": 1.0 + nrm(ks[2], (d,), 0.02),
        "ln_emb_b": nrm(ks[3], (d,), 0.02),
        "w_in": nrm(ks[4], (DEPTH, d, PROJ_WIDTH), d ** -0.5) * col_scale,
        "w_o": nrm(ks[5], (DEPTH, d, d), DEEPNORM_BETA * d ** -0.5),
        "diff_lq1": nrm(ks[6], (DEPTH, B_QK_DIM), 0.1),
        "diff_lk1": nrm(ks[7], (DEPTH, B_QK_DIM), 0.1),
        "diff_lq2": nrm(ks[8], (DEPTH, B_QK_DIM), 0.1),
        "diff_lk2": nrm(ks[9], (DEPTH, B_QK_DIM), 0.1),
        "diff_subln_g": 1.0 + nrm(ks[10], (DEPTH, B_V_DIM), 0.02),
        "ln1_g": 1.0 + nrm(ks[11], (DEPTH, d), 0.02),
        "ln1_b": nrm(ks[12], (DEPTH, d), 0.02),
        "w_router": nrm(ks[13], (DEPTH, d, N_EXPERTS), d ** -0.5),
        "router_bias": nrm(ks[14], (DEPTH, N_EXPERTS), 0.01),
        "w_exp_gate": nrm(ks[15], (DEPTH, N_EXPERTS, d, EXPERT_DIM), d ** -0.5),
        "w_exp_up": nrm(ks[16], (DEPTH, N_EXPERTS, d, EXPERT_DIM), d ** -0.5),
        "w_exp_down": nrm(ks[17], (DEPTH, N_EXPERTS, EXPERT_DIM, d), DEEPNORM_BETA * EXPERT_DIM ** -0.5),
        "w_sh_gate": nrm(ks[18], (DEPTH, d, SHARED_DIM), d ** -0.5),
        "w_sh_up": nrm(ks[19], (DEPTH, d, SHARED_DIM), d ** -0.5),
        "w_sh_down": nrm(ks[20], (DEPTH, SHARED_DIM, d), DEEPNORM_BETA * SHARED_DIM ** -0.5),
        "ln2_g": 1.0 + nrm(ks[21], (DEPTH, d), 0.02),
        "ln2_b": nrm(ks[22], (DEPTH, d), 0.02),
        "w_ple_gate": nrm(jax.random.fold_in(ks[23], 0), (DEPTH, d, d), d ** -0.5),
        "b_ple_gate": nrm(jax.random.fold_in(ks[23], 1), (DEPTH, d), 0.02),
        "w_ple_proj": nrm(jax.random.fold_in(ks[23], 2), (DEPTH, PLE_DIM, d), 0.5 * PLE_DIM ** -0.5),
    }


def reference(x, p, positions, ln_emb_g, ln_emb_b, w_in, w_o, diff_lq1, diff_lk1, diff_lq2, diff_lk2,
              diff_subln_g, ln1_g, ln1_b, w_router, router_bias, w_exp_gate, w_exp_up, w_exp_down,
              w_sh_gate, w_sh_up, w_sh_down, ln2_g, ln2_b, w_ple_gate, b_ple_gate, w_ple_proj):
    bsz, seq, _ = x.shape
    n_keep = min(IDX_TOPK_MAX, seq // 4)
    cos, sin = rope_tables(positions)
    h = layer_norm(x, ln_emb_g, ln_emb_b)
    for i in range(DEPTH):
        lambda_init = 0.8 - 0.6 * math.exp(-0.3 * i)
        proj = jnp.einsum('bsd,de->bse', h, w_in[i])
        a_q, a_k, a_v, i_q, i_k, i_w, b_q, b_k, b_v = jnp.split(proj, SPLIT_POINTS, axis=-1)
        a_q = apply_rope(a_q.reshape(bsz, seq, A_HEADS, HEAD_DIM), cos, sin)
        a_k = apply_rope(a_k.reshape(bsz, seq, A_HEADS, HEAD_DIM), cos, sin)
        a_v = a_v.reshape(bsz, seq, A_HEADS, HEAD_DIM)
        i_q = apply_rope(i_q.reshape(bsz, seq, IDX_HEADS, IDX_DIM), cos, sin)
        i_k = apply_rope(i_k.reshape(bsz, seq, 1, IDX_DIM), cos, sin)[:, :, 0]
        out_a = dsa_mixer(a_q, a_k, a_v, i_q, i_k, i_w, n_keep)
        b_q = apply_rope(b_q.reshape(bsz, seq, B_HEADS * 2, B_QK_DIM), cos, sin)
        b_k = apply_rope(b_k.reshape(bsz, seq, B_HEADS * 2, B_QK_DIM), cos, sin)
        b_q = b_q.reshape(bsz, seq, B_HEADS, 2, B_QK_DIM)
        b_k = b_k.reshape(bsz, seq, B_HEADS, 2, B_QK_DIM)
        b_v = b_v.reshape(bsz, seq, B_HEADS, B_V_DIM)
        lam = (jnp.exp(jnp.sum(diff_lq1[i].astype(jnp.float32) * diff_lk1[i]))
               - jnp.exp(jnp.sum(diff_lq2[i].astype(jnp.float32) * diff_lk2[i])) + lambda_init)
        out_b = diff_mixer(b_q, b_k, b_v, lam, diff_subln_g[i], lambda_init)
        mix = jnp.einsum('bse,ed->bsd', jnp.concatenate([out_a, out_b], axis=-1), w_o[i])
        h = layer_norm(DEEPNORM_ALPHA * h + mix, ln1_g[i], ln1_b[i])
        ffn = moe(h, w_router[i], router_bias[i], w_exp_gate[i], w_exp_up[i], w_exp_down[i],
                  w_sh_gate[i], w_sh_up[i], w_sh_down[i])
        h = layer_norm(DEEPNORM_ALPHA * h + ffn, ln2_g[i], ln2_b[i])
        gate = jax.nn.sigmoid(jnp.einsum('bsd,de->bse', h, w_ple_gate[i]) + b_ple_gate[i])
        h = h + gate * jnp.einsum('bsk,kd->bsd', p[i], w_ple_proj[i])
    return h
```

```python
import functools
import math

import jax
import jax.numpy as jnp
from jax import lax
from jax.experimental import pallas as pl
from jax.experimental.pallas import tpu as pltpu

CHUNK = 64
HEAD_DIM = 64
ROPE_DIM = 64
ROPE_THETA = 10000.0
A_HEADS = 8
IDX_HEADS = 8
IDX_DIM = 64
IDX_TOPK_MAX = 256
B_HEADS = 4
B_QK_DIM = 64
B_V_DIM = 128
N_EXPERTS = 64
TOP_K = 8
N_GROUPS = 8
TOPK_GROUPS = 4
ROUTED_SCALE = 2.5
LN_EPS = 1e-5
RMS_EPS = 1e-5

LANES = 128
VMEM_LIMIT = 56 * 1024 * 1024
NEG_BIG = -1e30
INT_MIN = -(2 ** 31)

F32 = jnp.float32
BF16 = jnp.bfloat16


def _cparams(*sem):
    return pltpu.CompilerParams(dimension_semantics=sem, vmem_limit_bytes=VMEM_LIMIT)


def _layer_norm(x, g, b):
    mu = jnp.mean(x, axis=-1, keepdims=True)
    xc = x - mu
    var = jnp.mean(xc * xc, axis=-1, keepdims=True)
    return xc * lax.rsqrt(var + LN_EPS) * g + b


def _dot_nt(a, b):
    return lax.dot_general(a, b, (((1,), (1,)), ((), ())), preferred_element_type=F32)


def _split_lane_halves(pair):
    lo = lax.broadcasted_iota(jnp.int32, pair.shape, 1) < (LANES // 2)
    pf = pair.astype(F32)
    return (jnp.where(lo, pf, 0.0).astype(pair.dtype), jnp.where(lo, 0.0, pf).astype(pair.dtype))


def _proj_kernel(x_ref, pos_ref, inv_ref, g_ref, b_ref, wr_ref, wp_ref,
                 h0_ref, aq_ref, ak_ref, iq_ref, ik_ref, bq_ref, bk_ref,
                 av_ref, bv_ref, iw_ref):
    hn = _layer_norm(x_ref[...], g_ref[...], b_ref[...])
    h0_ref[...] = hn
    hb = hn.astype(BF16)

    ang = pos_ref[...].astype(F32) * inv_ref[...]
    cos = jnp.cos(ang)
    sin = jnp.sin(ang)
    lane = lax.broadcasted_iota(jnp.int32, ang.shape, 1)
    first_half = (lane % ROPE_DIM) < (ROPE_DIM // 2)
    sin_signed = jnp.where(first_half, -sin, sin)

    def rope(v):
        partner = jnp.where(first_half,
                            pltpu.roll(v, LANES - ROPE_DIM // 2, axis=1),
                            pltpu.roll(v, ROPE_DIM // 2, axis=1))
        return v * cos + partner * sin_signed

    pr = jnp.dot(hb, wr_ref[...], preferred_element_type=F32)
    outs = ((aq_ref, 0, 4, HEAD_DIM ** -0.5), (ak_ref, 4, 4, 1.0), (iq_ref, 8, 4, 1.0),
            (ik_ref, 12, 1, 1.0), (bq_ref, 13, 4, B_QK_DIM ** -0.5), (bk_ref, 17, 4, 1.0))
    for ref, g0, ng, scale in outs:
        for j in range(ng):
            v = rope(pr[:, (g0 + j) * LANES:(g0 + j + 1) * LANES])
            if scale != 1.0:
                v = v * scale
            ref[:, j * LANES:(j + 1) * LANES] = v.astype(ref.dtype)

    pp = jnp.dot(hb, wp_ref[...], preferred_element_type=F32)
    av_ref[...] = pp[:, 0:512].astype(BF16)
    bv_ref[...] = pp[:, 512:1024].astype(BF16)
    iw_ref[...] = pp[:, 1024:1152]


def _proj(x2, pos2, inv, g, b, w_rope, w_plain, tm):
    n, d = x2.shape
    nr = w_rope.shape[1]
    npl = w_plain.shape[1]
    row = lambda w: pl.BlockSpec((tm, w), lambda i: (i, 0))
    full = lambda a: pl.BlockSpec(a.shape, lambda i: (0, 0))
    out_shape = (
        jax.ShapeDtypeStruct((n, d), F32),
        jax.ShapeDtypeStruct((n, 512), BF16),
        jax.ShapeDtypeStruct((n, 512), BF16),
        jax.ShapeDtypeStruct((n, 512), BF16),
        jax.ShapeDtypeStruct((n, 128), BF16),
        jax.ShapeDtypeStruct((n, 512), BF16),
        jax.ShapeDtypeStruct((n, 512), BF16),
        jax.ShapeDtypeStruct((n, 512), BF16),
        jax.ShapeDtypeStruct((n, 512), BF16),
        jax.ShapeDtypeStruct((n, 128), F32),
    )
    return pl.pallas_call(
        _proj_kernel,
        grid=(n // tm,),
        in_specs=[row(d), row(1), full(inv), full(g), full(b), full(w_rope), full(w_plain)],
        out_specs=(row(d), row(512), row(512), row(512), row(128), row(512), row(512),
                   row(512), row(512), row(128)),
        out_shape=out_shape,
        compiler_params=_cparams("parallel"),
    )(x2, pos2, inv, g, b, w_rope, w_plain)


def _sortable(s):
    i = lax.bitcast_convert_type(s, jnp.int32)
    return i ^ ((i >> 31) & 0x7FFFFFFF)


def _select_kernel(iq_ref, iw_ref, ik_ref, bias_ref, keys_ref, *, tq, tk, n_keep, w_scale):
    i = pl.program_id(1)
    seq = ik_ref.shape[1]
    n_t = ((i + 1) * tq + tk - 1) // tk

    q_chunk = (i * tq + lax.broadcasted_iota(jnp.int32, (tq, tk), 0)) // CHUNK
    col = lax.broadcasted_iota(jnp.int32, (tq, tk), 1)
    wb = iw_ref[0] * w_scale
    w_cols = [wb[:, h:h + 1] for h in range(IDX_HEADS)]
    q_halves = []
    for hp in range(IDX_HEADS // 2):
        q_halves.extend(_split_lane_halves(iq_ref[0, :, hp * LANES:(hp + 1) * LANES]))

    def score_tile(t, carry):
        off = pl.multiple_of(t * tk, tk)
        kt = ik_ref[0, pl.ds(off, tk), :]
        sc = jnp.zeros((tq, tk), F32)
        for h in range(IDX_HEADS):
            rel = jnp.maximum(_dot_nt(q_halves[h], kt), 0.0)
            sc = sc + w_cols[h] * rel
        key = _sortable(sc + 0.0)
        adm = ((off + col) // CHUNK) <= q_chunk
        keys_ref[:, pl.ds(off, tk)] = jnp.where(adm, key, INT_MIN)
        return carry

    lax.fori_loop(0, n_t, score_tile, 0)

    def count_ge(cand):
        def body(t, acc):
            off = pl.multiple_of(t * tk, tk)
            m = (keys_ref[:, pl.ds(off, tk)] >= cand).astype(jnp.int32)
            s = m[:, 0:LANES]
            for g in range(1, tk // LANES):
                s = s + m[:, g * LANES:(g + 1) * LANES]
            return acc + s
        acc = lax.fori_loop(0, n_t, body, jnp.zeros((tq, LANES), jnp.int32))
        return jnp.sum(acc, axis=1, keepdims=True)

    tau0 = jnp.full((tq, 1), INT_MIN, jnp.int32)
    cand0 = jnp.zeros((tq, 1), jnp.int32)
    tau = jnp.where(count_ge(cand0) >= n_keep, cand0, tau0)

    def bit_step(j, tau):
        cand = tau | (jnp.int32(1) << (30 - j))
        return jnp.where(count_ge(cand) >= n_keep, cand, tau)

    tau = lax.fori_loop(0, 31, bit_step, tau)

    n_ge = count_ge(tau)
    tied = (n_ge > n_keep) & (tau > INT_MIN)
    any_tied = jnp.max(tied.astype(jnp.int32)) > 0

    @pl.when(jnp.logical_not(any_tied))
    def _():
        def write_tile(t, carry):
            off = pl.multiple_of(t * tk, tk)
            key = keys_ref[:, pl.ds(off, tk)]
            keep = (key >= tau) & (key > INT_MIN)
            bias_ref[0, :, pl.ds(off, tk)] = jnp.where(keep, 0.0, NEG_BIG).astype(BF16)
            return carry
        lax.fori_loop(0, n_t, write_tile, 0)

    @pl.when(any_tied)
    def _():
        room = (n_keep - count_ge(tau + 1)).astype(F32)
        tri = (lax.broadcasted_iota(jnp.int32, (tk, tk), 0)
               <= lax.broadcasted_iota(jnp.int32, (tk, tk), 1)).astype(BF16)

        def write_tile(t, seen_eq):
            off = pl.multiple_of(t * tk, tk)
            key = keys_ref[:, pl.ds(off, tk)]
            eq = (key == tau) & tied
            eq_rank = seen_eq + jnp.dot(jnp.where(eq, 1.0, 0.0).astype(BF16), tri,
                                        preferred_element_type=F32)
            keep = (key > INT_MIN) & ((key > tau) | (eq & (eq_rank <= room))
                                      | ((key == tau) & jnp.logical_not(tied)))
            bias_ref[0, :, pl.ds(off, tk)] = jnp.where(keep, 0.0, NEG_BIG).astype(BF16)
            return seen_eq + jnp.sum(jnp.where(eq, 1.0, 0.0), axis=1, keepdims=True)

        lax.fori_loop(0, n_t, write_tile, jnp.zeros((tq, 1), F32))

    def blank_tile(t, carry):
        off = pl.multiple_of(t * tk, tk)
        bias_ref[0, :, pl.ds(off, tk)] = jnp.full((tq, tk), NEG_BIG, BF16)
        return carry

    lax.fori_loop(n_t, seq // tk, blank_tile, 0)


def _select(iq, iw, ik2, n_keep, tq, tk):
    bsz, seq, _ = iq.shape
    kern = functools.partial(_select_kernel, tq=tq, tk=tk, n_keep=n_keep,
                             w_scale=(IDX_HEADS ** -0.5) * (IDX_DIM ** -0.5))
    return pl.pallas_call(
        kern,
        grid=(bsz, seq // tq),
        in_specs=[pl.BlockSpec((1, tq, 512), lambda b, i: (b, i, 0)),
                  pl.BlockSpec((1, tq, 128), lambda b, i: (b, i, 0)),
                  pl.BlockSpec((1, seq, 128), lambda b, i: (b, 0, 0))],
        out_specs=pl.BlockSpec((1, tq, seq), lambda b, i: (b, i, 0)),
        out_shape=jax.ShapeDtypeStruct((bsz, seq, seq), BF16),
        scratch_shapes=[pltpu.VMEM((tq, seq), jnp.int32)],
        compiler_params=_cparams("parallel", "parallel"),
    )(iq, iw, ik2)


def _dsa_kernel(q_ref, k_ref, v_ref, bias_ref, o_ref, m_ref, l_ref, acc_ref, *, tq, tk):
    i = pl.program_id(1)
    n_t = ((i + 1) * tq + tk - 1) // tk
    lo_q = lax.broadcasted_iota(jnp.int32, (tq, LANES), 1) < HEAD_DIM
    q_halves = []
    for hp in range(A_HEADS // 2):
        q_halves.extend(_split_lane_halves(q_ref[0, :, hp * LANES:(hp + 1) * LANES]))

    m_ref[...] = jnp.full(m_ref.shape, NEG_BIG, F32)
    l_ref[...] = jnp.zeros(l_ref.shape, F32)
    acc_ref[...] = jnp.zeros(acc_ref.shape, F32)

    def kv_step(t, carry):
        off = pl.multiple_of(t * tk, tk)
        bias = bias_ref[0, :, pl.ds(off, tk)].astype(F32)
        for hp in range(A_HEADS // 2):
            kp = k_ref[0, pl.ds(off, tk), hp * LANES:(hp + 1) * LANES]
            vp = v_ref[0, pl.ds(off, tk), hp * LANES:(hp + 1) * LANES]
            upd = []
            for half in range(2):
                h = 2 * hp + half
                s = _dot_nt(q_halves[h], kp) + bias
                m_old = m_ref[h]
                m_new = jnp.maximum(m_old, jnp.max(s, axis=1, keepdims=True))
                alpha = jnp.exp(m_old - m_new)
                p = jnp.exp(s - m_new)
                l_ref[h] = alpha * l_ref[h] + jnp.sum(p, axis=1, keepdims=True)
                m_ref[h] = m_new
                upd.append((alpha, jnp.dot(p.astype(BF16), vp, preferred_element_type=F32)))
            alpha2 = jnp.where(lo_q, upd[0][0], upd[1][0])
            acc_ref[hp] = acc_ref[hp] * alpha2 + jnp.where(lo_q, upd[0][1], upd[1][1])
        return carry

    lax.fori_loop(0, n_t, kv_step, 0)

    for hp in range(A_HEADS // 2):
        inv = jnp.where(lo_q, 1.0 / l_ref[2 * hp], 1.0 / l_ref[2 * hp + 1])
        o_ref[0, :, hp * LANES:(hp + 1) * LANES] = (acc_ref[hp] * inv).astype(o_ref.dtype)


def _dsa(aq, ak, av, bias, tq, tk):
    bsz, seq, w = aq.shape
    kern = functools.partial(_dsa_kernel, tq=tq, tk=tk)
    resident = lambda: pl.BlockSpec((1, seq, w), lambda b, i: (b, 0, 0),
                                    pipeline_mode=pl.Buffered(1))
    return pl.pallas_call(
        kern,
        grid=(bsz, seq // tq),
        in_specs=[pl.BlockSpec((1, tq, w), lambda b, i: (b, i, 0)),
                  resident(), resident(),
                  pl.BlockSpec((1, tq, seq), lambda b, i: (b, i, 0))],
        out_specs=pl.BlockSpec((1, tq, w), lambda b, i: (b, i, 0)),
        out_shape=jax.ShapeDtypeStruct((bsz, seq, w), BF16),
        scratch_shapes=[pltpu.VMEM((A_HEADS, tq, 1), F32),
                        pltpu.VMEM((A_HEADS, tq, 1), F32),
                        pltpu.VMEM((A_HEADS // 2, tq, LANES), F32)],
        compiler_params=_cparams("parallel", "arbitrary"),
    )(aq, ak, av, bias)


def _diff_kernel(q_ref, k_ref, v_ref, lq1_ref, lk1_ref, lq2_ref, lk2_ref, g_ref, o_ref,
                 m_ref, l_ref, acc_ref, *, tq, lambda_init):
    i = pl.program_id(1)
    tk = tq
    q_halves = []
    for hb in range(B_HEADS):
        q_halves.extend(_split_lane_halves(q_ref[0, :, hb * LANES:(hb + 1) * LANES]))
    lam = (jnp.exp(jnp.sum(lq1_ref[...] * lk1_ref[...], keepdims=True))
           - jnp.exp(jnp.sum(lq2_ref[...] * lk2_ref[...], keepdims=True)) + lambda_init)

    m_ref[...] = jnp.full(m_ref.shape, NEG_BIG, F32)
    l_ref[...] = jnp.zeros(l_ref.shape, F32)
    acc_ref[...] = jnp.zeros(acc_ref.shape, F32)

    row_chunk = (i * tq + lax.broadcasted_iota(jnp.int32, (tq, tk), 0)) // CHUNK
    col_chunk = (i * tq + lax.broadcasted_iota(jnp.int32, (tq, tk), 1)) // CHUNK
    diag_ok = col_chunk <= row_chunk

    def tile(off, masked):
        for hb in range(B_HEADS):
            kp = k_ref[0, pl.ds(off, tk), hb * LANES:(hb + 1) * LANES]
            vp = v_ref[0, pl.ds(off, tk), hb * LANES:(hb + 1) * LANES]
            for mp in range(2):
                j = 2 * hb + mp
                s = _dot_nt(q_halves[j], kp)
                if masked:
                    s = jnp.where(diag_ok, s, NEG_BIG)
                m_old = m_ref[j]
                m_new = jnp.maximum(m_old, jnp.max(s, axis=1, keepdims=True))
                alpha = jnp.exp(m_old - m_new)
                p = jnp.exp(s - m_new)
                l_ref[j] = alpha * l_ref[j] + jnp.sum(p, axis=1, keepdims=True)
                m_ref[j] = m_new
                acc_ref[j] = acc_ref[j] * alpha + jnp.dot(p.astype(BF16), vp,
                                                          preferred_element_type=F32)

    def kv_step(t, carry):
        tile(pl.multiple_of(t * tk, tk), False)
        return carry

    lax.fori_loop(0, i, kv_step, 0)
    tile(pl.multiple_of(i * tk, tk), True)

    for hb in range(B_HEADS):
        o = acc_ref[2 * hb] / l_ref[2 * hb] - lam * (acc_ref[2 * hb + 1] / l_ref[2 * hb + 1])
        o = o * lax.rsqrt(jnp.mean(o * o, axis=-1, keepdims=True) + RMS_EPS)
        o = o * g_ref[...] * (1.0 - lambda_init)
        o_ref[0, :, hb * LANES:(hb + 1) * LANES] = o.astype(o_ref.dtype)


def _diff(bq, bk, bv, lq1, lk1, lq2, lk2, subln_g, lambda_init, tq):
    bsz, seq, w = bq.shape
    kern = functools.partial(_diff_kernel, tq=tq, lambda_init=lambda_init)
    resident = lambda: pl.BlockSpec((1, seq, w), lambda b, i: (b, 0, 0),
                                    pipeline_mode=pl.Buffered(1))
    small = lambda a: pl.BlockSpec(a.shape, lambda b, i: (0, 0))
    return pl.pallas_call(
        kern,
        grid=(bsz, seq // tq),
        in_specs=[pl.BlockSpec((1, tq, w), lambda b, i: (b, i, 0)),
                  resident(), resident(),
                  small(lq1), small(lk1), small(lq2), small(lk2), small(subln_g)],
        out_specs=pl.BlockSpec((1, tq, w), lambda b, i: (b, i, 0)),
        out_shape=jax.ShapeDtypeStruct((bsz, seq, w), BF16),
        scratch_shapes=[pltpu.VMEM((2 * B_HEADS, tq, 1), F32),
                        pltpu.VMEM((2 * B_HEADS, tq, 1), F32),
                        pltpu.VMEM((2 * B_HEADS, tq, B_V_DIM), F32)],
        compiler_params=_cparams("parallel", "arbitrary"),
    )(bq, bk, bv, lq1, lk1, lq2, lk2, subln_g)


def _first_argmax_rows(v, row):
    m = jnp.max(v, axis=0, keepdims=True)
    idx = jnp.min(jnp.where(v == m, row, v.shape[0]), axis=0, keepdims=True)
    return m, idx


def _oproj_kernel(oa_ref, ob_ref, h0_ref, woa_ref, wob_ref, g_ref, b_ref, wrt_ref, rb_ref,
                  h1_ref, h1b_ref, gate_ref, *, alpha):
    mix = (jnp.dot(oa_ref[...], woa_ref[...], preferred_element_type=F32)
           + jnp.dot(ob_ref[...], wob_ref[...], preferred_element_type=F32))
    h1 = _layer_norm(alpha * h0_ref[...] + mix, g_ref[...], b_ref[...])
    h1_ref[...] = h1
    h1b_ref[...] = h1.astype(BF16)

    logits = lax.dot_general(wrt_ref[...], h1, (((1,), (1,)), ((), ())),
                             precision=lax.Precision.HIGHEST, preferred_element_type=F32)
    scores = jax.nn.sigmoid(logits)
    biased = scores + rb_ref[...]
    tm = scores.shape[1]
    per_g = N_EXPERTS // N_GROUPS
    row8 = lax.broadcasted_iota(jnp.int32, (per_g, tm), 0)

    gs = []
    for g in range(N_GROUPS):
        blk = biased[g * per_g:(g + 1) * per_g, :]
        m1, i1 = _first_argmax_rows(blk, row8)
        m2 = jnp.max(jnp.where(row8 == i1, -jnp.inf, blk), axis=0, keepdims=True)
        gs.append(m1 + m2)
    gscore = jnp.concatenate(gs, axis=0)
    rowg = lax.broadcasted_iota(jnp.int32, (N_GROUPS, tm), 0)
    gsel = jnp.zeros((N_GROUPS, tm), F32)
    for _ in range(TOPK_GROUPS):
        _, ig = _first_argmax_rows(jnp.where(gsel > 0.0, -jnp.inf, gscore), rowg)
        gsel = jnp.where(rowg == ig, 1.0, gsel)

    rowe = lax.broadcasted_iota(jnp.int32, (N_EXPERTS, tm), 0)
    live = jnp.concatenate(
        [jnp.broadcast_to(gsel[g:g + 1, :], (per_g, tm)) for g in range(N_GROUPS)], axis=0)
    esel = jnp.zeros((N_EXPERTS, tm), F32)
    for _ in range(TOP_K):
        cand = jnp.where(live > 0.0, biased, -jnp.inf)
        m = jnp.max(cand, axis=0, keepdims=True)
        idx = jnp.min(jnp.where((live > 0.0) & (cand == m), rowe, N_EXPERTS),
                      axis=0, keepdims=True)
        hit = rowe == idx
        esel = jnp.where(hit, 1.0, esel)
        live = jnp.where(hit, 0.0, live)
    picked = jnp.where(esel > 0.0, scores, 0.0)
    denom = jnp.sum(picked, axis=0, keepdims=True)
    gate = picked / denom * ROUTED_SCALE
    gate_ref[...] = gate.T


def _oproj(oa, ob, h0, woa, wob, g, b, wrt, rb, alpha, tm):
    n, d = h0.shape
    row = lambda w: pl.BlockSpec((tm, w), lambda i: (i, 0))
    full = lambda a: pl.BlockSpec(a.shape, lambda i: (0, 0))
    return pl.pallas_call(
        functools.partial(_oproj_kernel, alpha=alpha),
        grid=(n // tm,),
        in_specs=[row(512), row(512), row(d), full(woa), full(wob), full(g), full(b),
                  full(wrt), full(rb)],
        out_specs=(row(d), row(d), row(N_EXPERTS)),
        out_shape=(jax.ShapeDtypeStruct((n, d), F32),
                   jax.ShapeDtypeStruct((n, d), BF16),
                   jax.ShapeDtypeStruct((n, N_EXPERTS), F32)),
        compiler_params=_cparams("parallel"),
    )(oa, ob, h0, woa, wob, g, b, wrt, rb)


def _moe_kernel(x_ref, gate_ref, wg_ref, wu_ref, wd_ref, sg_ref, su_ref, sd_ref, o_ref):
    e = pl.program_id(1)
    x = x_ref[...]

    @pl.when(e == 0)
    def _():
        hs = (jax.nn.silu(jnp.dot(x, sg_ref[...], preferred_element_type=F32))
              * jnp.dot(x, su_ref[...], preferred_element_type=F32))
        o_ref[...] = jnp.dot(hs.astype(BF16), sd_ref[...], preferred_element_type=F32)

    lane = lax.broadcasted_iota(jnp.int32, gate_ref.shape, 1)
    w = jnp.sum(jnp.where(lane == e, gate_ref[...], 0.0), axis=1, keepdims=True)
    hid = (jax.nn.silu(jnp.dot(x, wg_ref[0], preferred_element_type=F32))
           * jnp.dot(x, wu_ref[0], preferred_element_type=F32))
    o_ref[...] += jnp.dot((hid * w).astype(BF16), wd_ref[0], preferred_element_type=F32)


def _moe(h1b, gate, wg, wu, wd, sg, su, sd, t):
    n, d = h1b.shape
    f = wg.shape[2]
    full = lambda a: pl.BlockSpec(a.shape, lambda i, e: (0, 0))
    return pl.pallas_call(
        _moe_kernel,
        grid=(n // t, N_EXPERTS),
        in_specs=[pl.BlockSpec((t, d), lambda i, e: (i, 0)),
                  pl.BlockSpec((t, N_EXPERTS), lambda i, e: (i, 0)),
                  pl.BlockSpec((1, d, f), lambda i, e: (e, 0, 0)),
                  pl.BlockSpec((1, d, f), lambda i, e: (e, 0, 0)),
                  pl.BlockSpec((1, f, d), lambda i, e: (e, 0, 0)),
                  full(sg), full(su), full(sd)],
        out_specs=pl.BlockSpec((t, d), lambda i, e: (i, 0)),
        out_shape=jax.ShapeDtypeStruct((n, d), F32),
        compiler_params=_cparams("parallel", "arbitrary"),
    )(h1b, gate, wg, wu, wd, sg, su, sd)


def _final_kernel(h1_ref, ffn_ref, p_ref, g_ref, b_ref, wg_ref, bg_ref, wp_ref, o_ref, *, alpha):
    h2 = _layer_norm(alpha * h1_ref[...] + ffn_ref[...], g_ref[...], b_ref[...])
    gate = jax.nn.sigmoid(jnp.dot(h2.astype(BF16), wg_ref[...], preferred_element_type=F32)
                          + bg_ref[...])
    emb = jnp.dot(p_ref[...].astype(BF16), wp_ref[...], preferred_element_type=F32)
    o_ref[...] = h2 + gate * emb


def _final(h1, ffn, p2, g, b, wg, bg, wp, alpha, tm):
    n, d = h1.shape
    row = lambda w: pl.BlockSpec((tm, w), lambda i: (i, 0))
    full = lambda a: pl.BlockSpec(a.shape, lambda i: (0, 0))
    return pl.pallas_call(
        functools.partial(_final_kernel, alpha=alpha),
        grid=(n // tm,),
        in_specs=[row(d), row(d), row(p2.shape[1]), full(g), full(b), full(wg), full(bg), full(wp)],
        out_specs=row(d),
        out_shape=jax.ShapeDtypeStruct((n, d), F32),
        compiler_params=_cparams("parallel"),
    )(h1, ffn, p2, g, b, wg, bg, wp)


def _tiles(seq):
    tm = min(256, seq)
    tq_sel = min(128, seq)
    tk_sel = min(512, seq)
    tq_att = min(256, seq)
    tk_att = min(512, seq)
    t_moe = min(1024, seq)
    return tm, tq_sel, tk_sel, tq_att, tk_att, t_moe


def kernel(x, p, positions, ln_emb_g, ln_emb_b, w_in, w_o, diff_lq1, diff_lk1, diff_lq2, diff_lk2,
           diff_subln_g, ln1_g, ln1_b, w_router, router_bias, w_exp_gate, w_exp_up, w_exp_down,
           w_sh_gate, w_sh_up, w_sh_down, ln2_g, ln2_b, w_ple_gate, b_ple_gate, w_ple_proj):
    bsz, seq, d = x.shape
    depth = w_in.shape[0]
    n = bsz * seq
    n_keep = min(IDX_TOPK_MAX, seq // 4)
    alpha = (2.0 * depth) ** 0.25
    tm, tq_sel, tk_sel, tq_att, tk_att, t_moe = _tiles(seq)
    row1 = lambda v: v.reshape(1, -1)

    inv = ROPE_THETA ** (-jnp.arange(0, ROPE_DIM, 2, dtype=F32) / ROPE_DIM)
    inv_lanes = jnp.tile(inv, LANES // (ROPE_DIM // 2)).reshape(1, LANES)
    pos2 = positions.reshape(n, 1)

    h = x.reshape(n, d)
    out = None
    for li in range(depth):
        lambda_init = 0.8 - 0.6 * math.exp(-0.3 * li)
        w = w_in[li]
        aqw, akw, avw = w[:, 0:512], w[:, 512:1024], w[:, 1024:1536]
        iqw, ikw, iww = w[:, 1536:2048], w[:, 2048:2112], w[:, 2112:2120]
        bqw, bkw, bvw = w[:, 2120:2632], w[:, 2632:3144], w[:, 3144:3656]
        w_rope = jnp.concatenate([aqw, akw, iqw, ikw, ikw, bqw, bkw], axis=1).astype(BF16)
        w_plain = jnp.concatenate(
            [avw, bvw, iww, jnp.zeros((d, LANES - IDX_HEADS), F32)], axis=1).astype(BF16)

        (h0, aq, ak, iq, ik2, bq, bk, av, bv, iw) = _proj(
            h, pos2, inv_lanes, row1(ln_emb_g), row1(ln_emb_b), w_rope, w_plain, tm)
        assert depth == 1

        r3 = lambda a: a.reshape(bsz, seq, a.shape[-1])
        bias = _select(r3(iq), r3(iw), r3(ik2), n_keep, tq_sel, tk_sel)
        out_a = _dsa(r3(aq), r3(ak), r3(av), bias, tq_att, tk_att)
        out_b = _diff(r3(bq), r3(bk), r3(bv), row1(diff_lq1[li]), row1(diff_lk1[li]),
                      row1(diff_lq2[li]), row1(diff_lk2[li]), row1(diff_subln_g[li]),
                      lambda_init, tq_att)

        wo = w_o[li].astype(BF16)
        h1, h1b, gate = _oproj(out_a.reshape(n, -1), out_b.reshape(n, -1), h0,
                               wo[0:512], wo[512:1024], row1(ln1_g[li]), row1(ln1_b[li]),
                               w_router[li].T, router_bias[li].reshape(-1, 1), alpha, tm)

        ffn = _moe(h1b, gate, w_exp_gate[li].astype(BF16), w_exp_up[li].astype(BF16),
                   w_exp_down[li].astype(BF16), w_sh_gate[li].astype(BF16),
                   w_sh_up[li].astype(BF16), w_sh_down[li].astype(BF16), t_moe)

        out = _final(h1, ffn, p[li].reshape(n, -1), row1(ln2_g[li]), row1(ln2_b[li]),
                     w_ple_gate[li].astype(BF16), row1(b_ple_gate[li]),
                     w_ple_proj[li].astype(BF16), alpha, tm)
        h = out
    return out.reshape(bsz, seq, d)
```

```python
import functools
import math

import jax
import jax.numpy as jnp
from jax import lax
from jax.experimental import pallas as pl
from jax.experimental.pallas import tpu as pltpu

CHUNK = 64
HEAD_DIM = 64
ROPE_DIM = 64
ROPE_THETA = 10000.0
A_HEADS = 8
IDX_HEADS = 8
IDX_DIM = 64
IDX_TOPK_MAX = 256
B_HEADS = 4
B_QK_DIM = 64
B_V_DIM = 128
N_EXPERTS = 64
TOP_K = 8
N_GROUPS = 8
TOPK_GROUPS = 4
ROUTED_SCALE = 2.5
LN_EPS = 1e-5
RMS_EPS = 1e-5

LANES = 128
VMEM_LIMIT = 56 * 1024 * 1024
NEG_BIG = -1e30
INT_MIN = -(2 ** 31)

F32 = jnp.float32
BF16 = jnp.bfloat16


def _cparams(*sem):
    return pltpu.CompilerParams(dimension_semantics=sem, vmem_limit_bytes=VMEM_LIMIT)


def _layer_norm(x, g, b):
    mu = jnp.mean(x, axis=-1, keepdims=True)
    xc = x - mu
    var = jnp.mean(xc * xc, axis=-1, keepdims=True)
    return xc * lax.rsqrt(var + LN_EPS) * g + b


def _dot_nt(a, b):
    return lax.dot_general(a, b, (((1,), (1,)), ((), ())), preferred_element_type=F32)


def _split_lane_halves(pair):
    lo = lax.broadcasted_iota(jnp.int32, pair.shape, 1) < (LANES // 2)
    pf = pair.astype(F32)
    return (jnp.where(lo, pf, 0.0).astype(pair.dtype), jnp.where(lo, 0.0, pf).astype(pair.dtype))


def _proj_kernel(x_ref, pos_ref, inv_ref, g_ref, b_ref, wr_ref, wp_ref,
                 h0_ref, aq_ref, ak_ref, iq_ref, ik_ref, bq_ref, bk_ref,
                 av_ref, bv_ref, iw_ref):
    hn = _layer_norm(x_ref[...], g_ref[...], b_ref[...])
    h0_ref[...] = hn
    hb = hn.astype(BF16)

    ang = pos_ref[...].astype(F32) * inv_ref[...]
    cos = jnp.cos(ang)
    sin = jnp.sin(ang)
    lane = lax.broadcasted_iota(jnp.int32, ang.shape, 1)
    first_half = (lane % ROPE_DIM) < (ROPE_DIM // 2)
    sin_signed = jnp.where(first_half, -sin, sin)

    def rope(v):
        partner = jnp.where(first_half,
                            pltpu.roll(v, LANES - ROPE_DIM // 2, axis=1),
                            pltpu.roll(v, ROPE_DIM // 2, axis=1))
        return v * cos + partner * sin_signed

    pr = jnp.dot(hb, wr_ref[...], preferred_element_type=F32)
    outs = ((aq_ref, 0, 4, HEAD_DIM ** -0.5), (ak_ref, 4, 4, 1.0), (iq_ref, 8, 4, 1.0),
            (ik_ref, 12, 1, 1.0), (bq_ref, 13, 4, B_QK_DIM ** -0.5), (bk_ref, 17, 4, 1.0))
    for ref, g0, ng, scale in outs:
        for j in range(ng):
            v = rope(pr[:, (g0 + j) * LANES:(g0 + j + 1) * LANES])
            if scale != 1.0:
                v = v * scale
            ref[:, j * LANES:(j + 1) * LANES] = v.astype(ref.dtype)

    pp = jnp.dot(hb, wp_ref[...], preferred_element_type=F32)
    av_ref[...] = pp[:, 0:512].astype(BF16)
    bv_ref[...] = pp[:, 512:1024].astype(BF16)
    iw_ref[...] = pp[:, 1024:1152]


def _proj(x2, pos2, inv, g, b, w_rope, w_plain, tm):
    n, d = x2.shape
    nr = w_rope.shape[1]
    npl = w_plain.shape[1]
    row = lambda w: pl.BlockSpec((tm, w), lambda i: (i, 0))
    full = lambda a: pl.BlockSpec(a.shape, lambda i: (0, 0))
    out_shape = (
        jax.ShapeDtypeStruct((n, d), F32),
        jax.ShapeDtypeStruct((n, 512), BF16),
        jax.ShapeDtypeStruct((n, 512), BF16),
        jax.ShapeDtypeStruct((n, 512), BF16),
        jax.ShapeDtypeStruct((n, 128), BF16),
        jax.ShapeDtypeStruct((n, 512), BF16),
        jax.ShapeDtypeStruct((n, 512), BF16),
        jax.ShapeDtypeStruct((n, 512), BF16),
        jax.ShapeDtypeStruct((n, 512), BF16),
        jax.ShapeDtypeStruct((n, 128), F32),
    )
    return pl.pallas_call(
        _proj_kernel,
        grid=(n // tm,),
        in_specs=[row(d), row(1), full(inv), full(g), full(b), full(w_rope), full(w_plain)],
        out_specs=(row(d), row(512), row(512), row(512), row(128), row(512), row(512),
                   row(512), row(512), row(128)),
        out_shape=out_shape,
        compiler_params=_cparams("parallel"),
    )(x2, pos2, inv, g, b, w_rope, w_plain)


def _sortable(s):
    i = lax.bitcast_convert_type(s, jnp.int32)
    return i ^ ((i >> 31) & 0x7FFFFFFF)


def _select_kernel(iq_ref, iw_ref, ik_ref, bias_ref, keys_ref, *, tq, tk, n_keep, w_scale):
    i = pl.program_id(1)
    seq = ik_ref.shape[1]
    n_t = ((i + 1) * tq + tk - 1) // tk

    q_chunk = (i * tq + lax.broadcasted_iota(jnp.int32, (tq, tk), 0)) // CHUNK
    col = lax.broadcasted_iota(jnp.int32, (tq, tk), 1)
    wb = iw_ref[0] * w_scale
    w_cols = [wb[:, h:h + 1] for h in range(IDX_HEADS)]
    q_halves = []
    for hp in range(IDX_HEADS // 2):
        q_halves.extend(_split_lane_halves(iq_ref[0, :, hp * LANES:(hp + 1) * LANES]))

    def score_tile(t, carry):
        off = pl.multiple_of(t * tk, tk)
        kt = ik_ref[0, pl.ds(off, tk), :]
        sc = jnp.zeros((tq, tk), F32)
        for h in range(IDX_HEADS):
            rel = jnp.maximum(_dot_nt(q_halves[h], kt), 0.0)
            sc = sc + w_cols[h] * rel
        key = _sortable(sc + 0.0)
        adm = ((off + col) // CHUNK) <= q_chunk
        keys_ref[:, pl.ds(off, tk)] = jnp.where(adm, key, INT_MIN)
        return carry

    lax.fori_loop(0, n_t, score_tile, 0)

    def count_ge(cand):
        def body(t, acc):
            off = pl.multiple_of(t * tk, tk)
            m = (keys_ref[:, pl.ds(off, tk)] >= cand).astype(jnp.int32)
            s = m[:, 0:LANES]
            for g in range(1, tk // LANES):
                s = s + m[:, g * LANES:(g + 1) * LANES]
            return acc + s
        acc = lax.fori_loop(0, n_t, body, jnp.zeros((tq, LANES), jnp.int32))
        return jnp.sum(acc, axis=1, keepdims=True)

    tau0 = jnp.full((tq, 1), INT_MIN, jnp.int32)
    cand0 = jnp.zeros((tq, 1), jnp.int32)
    tau = jnp.where(count_ge(cand0) >= n_keep, cand0, tau0)

    def bit_step(j, tau):
        cand = tau | (jnp.int32(1) << (30 - j))
        return jnp.where(count_ge(cand) >= n_keep, cand, tau)

    tau = lax.fori_loop(0, 31, bit_step, tau)

    n_ge = count_ge(tau)
    tied = (n_ge > n_keep) & (tau > INT_MIN)
    any_tied = jnp.max(tied.astype(jnp.int32)) > 0

    @pl.when(jnp.logical_not(any_tied))
    def _():
        def write_tile(t, carry):
            off = pl.multiple_of(t * tk, tk)
            key = keys_ref[:, pl.ds(off, tk)]
            keep = (key >= tau) & (key > INT_MIN)
            bias_ref[0, :, pl.ds(off, tk)] = jnp.where(keep, 0.0, NEG_BIG).astype(BF16)
            return carry
        lax.fori_loop(0, n_t, write_tile, 0)

    @pl.when(any_tied)
    def _():
        room = (n_keep - count_ge(tau + 1)).astype(F32)
        tri = (lax.broadcasted_iota(jnp.int32, (tk, tk), 0)
               <= lax.broadcasted_iota(jnp.int32, (tk, tk), 1)).astype(BF16)

        def write_tile(t, seen_eq):
            off = pl.multiple_of(t * tk, tk)
            key = keys_ref[:, pl.ds(off, tk)]
            eq = (key == tau) & tied
            eq_rank = seen_eq + jnp.dot(jnp.where(eq, 1.0, 0.0).astype(BF16), tri,
                                        preferred_element_type=F32)
            keep = (key > INT_MIN) & ((key > tau) | (eq & (eq_rank <= room))
                                      | ((key == tau) & jnp.logical_not(tied)))
            bias_ref[0, :, pl.ds(off, tk)] = jnp.where(keep, 0.0, NEG_BIG).astype(BF16)
            return seen_eq + jnp.sum(jnp.where(eq, 1.0, 0.0), axis=1, keepdims=True)

        lax.fori_loop(0, n_t, write_tile, jnp.zeros((tq, 1), F32))

    def blank_tile(t, carry):
        off = pl.multiple_of(t * tk, tk)
        bias_ref[0, :, pl.ds(off, tk)] = jnp.full((tq, tk), NEG_BIG, BF16)
        return carry

    lax.fori_loop(n_t, seq // tk, blank_tile, 0)


def _select(iq, iw, ik2, n_keep, tq, tk):
    bsz, seq, _ = iq.shape
    kern = functools.partial(_select_kernel, tq=tq, tk=tk, n_keep=n_keep,
                             w_scale=(IDX_HEADS ** -0.5) * (IDX_DIM ** -0.5))
    return pl.pallas_call(
        kern,
        grid=(bsz, seq // tq),
        in_specs=[pl.BlockSpec((1, tq, 512), lambda b, i: (b, i, 0)),
                  pl.BlockSpec((1, tq, 128), lambda b, i: (b, i, 0)),
                  pl.BlockSpec((1, seq, 128), lambda b, i: (b, 0, 0))],
        out_specs=pl.BlockSpec((1, tq, seq), lambda b, i: (b, i, 0)),
        out_shape=jax.ShapeDtypeStruct((bsz, seq, seq), BF16),
        scratch_shapes=[pltpu.VMEM((tq, seq), jnp.int32)],
        compiler_params=_cparams("parallel", "parallel"),
    )(iq, iw, ik2)


def _flash_update(j, s, v_ext, m_ref, acc_ref):
    tk = s.shape[1]
    m_old = m_ref[j]
    m_new = jnp.maximum(m_old, jnp.max(s, axis=1, keepdims=True))
    alpha = jnp.exp(m_old - m_new)
    p = jnp.exp(s - jnp.concatenate([m_new] * (tk // LANES), axis=1))
    m_ref[j] = m_new
    acc_ref[j] = (acc_ref[j] * jnp.concatenate([alpha, alpha], axis=1)
                  + jnp.dot(p.astype(BF16), v_ext, preferred_element_type=F32))


def _dsa_kernel(q_ref, k_ref, v_ref, bias_ref, o_ref, m_ref, acc_ref, *, tq, tk):
    i = pl.program_id(1)
    n_t = ((i + 1) * tq + tk - 1) // tk
    lo_q = lax.broadcasted_iota(jnp.int32, (tq, LANES), 1) < HEAD_DIM
    q_halves = []
    for hp in range(A_HEADS // 2):
        q_halves.extend(_split_lane_halves(q_ref[0, :, hp * LANES:(hp + 1) * LANES]))
    ones = jnp.ones((tk, LANES), BF16)

    m_ref[...] = jnp.full(m_ref.shape, NEG_BIG, F32)
    acc_ref[...] = jnp.zeros(acc_ref.shape, F32)

    def kv_step(t, carry):
        off = pl.multiple_of(t * tk, tk)
        bias = bias_ref[0, :, pl.ds(off, tk)].astype(F32)
        for hp in range(A_HEADS // 2):
            kp = k_ref[0, pl.ds(off, tk), hp * LANES:(hp + 1) * LANES]
            v_ext = jnp.concatenate([v_ref[0, pl.ds(off, tk), hp * LANES:(hp + 1) * LANES], ones],
                                    axis=1)
            for half in range(2):
                h = 2 * hp + half
                _flash_update(h, _dot_nt(q_halves[h], kp) + bias, v_ext, m_ref, acc_ref)
        return carry

    lax.fori_loop(0, n_t, kv_step, 0)

    for hp in range(A_HEADS // 2):
        a0 = acc_ref[2 * hp]
        a1 = acc_ref[2 * hp + 1]
        o = jnp.where(lo_q, a0[:, :LANES] / a0[:, LANES:], a1[:, :LANES] / a1[:, LANES:])
        o_ref[0, :, hp * LANES:(hp + 1) * LANES] = o.astype(o_ref.dtype)


def _dsa(aq, ak, av, bias, tq, tk):
    bsz, seq, w = aq.shape
    kern = functools.partial(_dsa_kernel, tq=tq, tk=tk)
    resident = lambda: pl.BlockSpec((1, seq, w), lambda b, i: (b, 0, 0),
                                    pipeline_mode=pl.Buffered(1))
    return pl.pallas_call(
        kern,
        grid=(bsz, seq // tq),
        in_specs=[pl.BlockSpec((1, tq, w), lambda b, i: (b, i, 0)),
                  resident(), resident(),
                  pl.BlockSpec((1, tq, seq), lambda b, i: (b, i, 0))],
        out_specs=pl.BlockSpec((1, tq, w), lambda b, i: (b, i, 0)),
        out_shape=jax.ShapeDtypeStruct((bsz, seq, w), BF16),
        scratch_shapes=[pltpu.VMEM((A_HEADS, tq, LANES), F32),
                        pltpu.VMEM((A_HEADS, tq, 2 * LANES), F32)],
        compiler_params=_cparams("parallel", "arbitrary"),
    )(aq, ak, av, bias)


def _diff_kernel(q_ref, k_ref, v_ref, lq1_ref, lk1_ref, lq2_ref, lk2_ref, g_ref, o_ref,
                 m_ref, acc_ref, *, tq, tk, lambda_init):
    i = pl.program_id(1)
    n_full = (i * tq + CHUNK) // tk
    q_halves = []
    for hb in range(B_HEADS):
        q_halves.extend(_split_lane_halves(q_ref[0, :, hb * LANES:(hb + 1) * LANES]))
    lam = (jnp.exp(jnp.sum(lq1_ref[...] * lk1_ref[...], keepdims=True))
           - jnp.exp(jnp.sum(lq2_ref[...] * lk2_ref[...], keepdims=True)) + lambda_init)

    ones = jnp.ones((tk, LANES), BF16)

    m_ref[...] = jnp.full(m_ref.shape, NEG_BIG, F32)
    acc_ref[...] = jnp.zeros(acc_ref.shape, F32)

    def tile(off, masked):
        if masked:
            row_chunk = (i * tq + lax.broadcasted_iota(jnp.int32, (tq, tk), 0)) // CHUNK
            col_chunk = (off + lax.broadcasted_iota(jnp.int32, (tq, tk), 1)) // CHUNK
            ok = col_chunk <= row_chunk
        for hb in range(B_HEADS):
            kp = k_ref[0, pl.ds(off, tk), hb * LANES:(hb + 1) * LANES]
            v_ext = jnp.concatenate([v_ref[0, pl.ds(off, tk), hb * LANES:(hb + 1) * LANES], ones],
                                    axis=1)
            for mp in range(2):
                j = 2 * hb + mp
                s = _dot_nt(q_halves[j], kp)
                if masked:
                    s = jnp.where(ok, s, NEG_BIG)
                _flash_update(j, s, v_ext, m_ref, acc_ref)

    def kv_step(t, carry):
        tile(pl.multiple_of(t * tk, tk), False)
        return carry

    lax.fori_loop(0, n_full, kv_step, 0)
    tile(pl.multiple_of(n_full * tk, tk), True)

    for hb in range(B_HEADS):
        a1 = acc_ref[2 * hb]
        a2 = acc_ref[2 * hb + 1]
        o = a1[:, :LANES] / a1[:, LANES:] - lam * (a2[:, :LANES] / a2[:, LANES:])
        o = o * lax.rsqrt(jnp.mean(o * o, axis=-1, keepdims=True) + RMS_EPS)
        o = o * g_ref[...] * (1.0 - lambda_init)
        o_ref[0, :, hb * LANES:(hb + 1) * LANES] = o.astype(o_ref.dtype)


def _diff(bq, bk, bv, lq1, lk1, lq2, lk2, subln_g, lambda_init, tq, tk):
    bsz, seq, w = bq.shape
    kern = functools.partial(_diff_kernel, tq=tq, tk=tk, lambda_init=lambda_init)
    resident = lambda: pl.BlockSpec((1, seq, w), lambda b, i: (b, 0, 0),
                                    pipeline_mode=pl.Buffered(1))
    small = lambda a: pl.BlockSpec(a.shape, lambda b, i: (0, 0))
    return pl.pallas_call(
        kern,
        grid=(bsz, seq // tq),
        in_specs=[pl.BlockSpec((1, tq, w), lambda b, i: (b, i, 0)),
                  resident(), resident(),
                  small(lq1), small(lk1), small(lq2), small(lk2), small(subln_g)],
        out_specs=pl.BlockSpec((1, tq, w), lambda b, i: (b, i, 0)),
        out_shape=jax.ShapeDtypeStruct((bsz, seq, w), BF16),
        scratch_shapes=[pltpu.VMEM((2 * B_HEADS, tq, LANES), F32),
                        pltpu.VMEM((2 * B_HEADS, tq, 2 * LANES), F32)],
        compiler_params=_cparams("parallel", "arbitrary"),
    )(bq, bk, bv, lq1, lk1, lq2, lk2, subln_g)


def _first_argmax_rows(v, row):
    m = jnp.max(v, axis=0, keepdims=True)
    idx = jnp.min(jnp.where(v == m, row, v.shape[0]), axis=0, keepdims=True)
    return m, idx


def _oproj_kernel(oa_ref, ob_ref, h0_ref, woa_ref, wob_ref, g_ref, b_ref, wrt_ref, rb_ref,
                  h1_ref, h1b_ref, gate_ref, *, alpha):
    mix = (jnp.dot(oa_ref[...], woa_ref[...], preferred_element_type=F32)
           + jnp.dot(ob_ref[...], wob_ref[...], preferred_element_type=F32))
    h1 = _layer_norm(alpha * h0_ref[...] + mix, g_ref[...], b_ref[...])
    h1_ref[...] = h1
    h1b_ref[...] = h1.astype(BF16)

    logits = lax.dot_general(wrt_ref[...], h1, (((1,), (1,)), ((), ())),
                             precision=lax.Precision.HIGHEST, preferred_element_type=F32)
    scores = jax.nn.sigmoid(logits)
    biased = scores + rb_ref[...]
    tm = scores.shape[1]
    per_g = N_EXPERTS // N_GROUPS
    row8 = lax.broadcasted_iota(jnp.int32, (per_g, tm), 0)

    gs = []
    for g in range(N_GROUPS):
        blk = biased[g * per_g:(g + 1) * per_g, :]
        m1, i1 = _first_argmax_rows(blk, row8)
        m2 = jnp.max(jnp.where(row8 == i1, -jnp.inf, blk), axis=0, keepdims=True)
        gs.append(m1 + m2)
    gscore = jnp.concatenate(gs, axis=0)
    rowg = lax.broadcasted_iota(jnp.int32, (N_GROUPS, tm), 0)
    gsel = jnp.zeros((N_GROUPS, tm), F32)
    for _ in range(TOPK_GROUPS):
        _, ig = _first_argmax_rows(jnp.where(gsel > 0.0, -jnp.inf, gscore), rowg)
        gsel = jnp.where(rowg == ig, 1.0, gsel)

    rowe = lax.broadcasted_iota(jnp.int32, (N_EXPERTS, tm), 0)
    live = jnp.concatenate(
        [jnp.broadcast_to(gsel[g:g + 1, :], (per_g, tm)) for g in range(N_GROUPS)], axis=0)
    esel = jnp.zeros((N_EXPERTS, tm), F32)
    for _ in range(TOP_K):
        cand = jnp.where(live > 0.0, biased, -jnp.inf)
        m = jnp.max(cand, axis=0, keepdims=True)
        idx = jnp.min(jnp.where((live > 0.0) & (cand == m), rowe, N_EXPERTS),
                      axis=0, keepdims=True)
        hit = rowe == idx
        esel = jnp.where(hit, 1.0, esel)
        live = jnp.where(hit, 0.0, live)
    picked = jnp.where(esel > 0.0, scores, 0.0)
    denom = jnp.sum(picked, axis=0, keepdims=True)
    gate = picked / denom * ROUTED_SCALE
    gate_ref[...] = gate.T


def _oproj(oa, ob, h0, woa, wob, g, b, wrt, rb, alpha, tm):
    n, d = h0.shape
    row = lambda w: pl.BlockSpec((tm, w), lambda i: (i, 0))
    full = lambda a: pl.BlockSpec(a.shape, lambda i: (0, 0))
    return pl.pallas_call(
        functools.partial(_oproj_kernel, alpha=alpha),
        grid=(n // tm,),
        in_specs=[row(512), row(512), row(d), full(woa), full(wob), full(g), full(b),
                  full(wrt), full(rb)],
        out_specs=(row(d), row(d), row(N_EXPERTS)),
        out_shape=(jax.ShapeDtypeStruct((n, d), F32),
                   jax.ShapeDtypeStruct((n, d), BF16),
                   jax.ShapeDtypeStruct((n, N_EXPERTS), F32)),
        compiler_params=_cparams("parallel"),
    )(oa, ob, h0, woa, wob, g, b, wrt, rb)


def _moe_kernel(x_ref, gate_ref, wg_ref, wu_ref, wd_ref, sg_ref, su_ref, sd_ref, o_ref):
    e = pl.program_id(1)
    x = x_ref[...]

    @pl.when(e == 0)
    def _():
        hs = (jax.nn.silu(jnp.dot(x, sg_ref[...], preferred_element_type=F32))
              * jnp.dot(x, su_ref[...], preferred_element_type=F32))
        o_ref[...] = jnp.dot(hs.astype(BF16), sd_ref[...], preferred_element_type=F32)

    lane = lax.broadcasted_iota(jnp.int32, gate_ref.shape, 1)
    w = jnp.sum(jnp.where(lane == e, gate_ref[...], 0.0), axis=1, keepdims=True)
    hid = (jax.nn.silu(jnp.dot(x, wg_ref[0], preferred_element_type=F32))
           * jnp.dot(x, wu_ref[0], preferred_element_type=F32))
    o_ref[...] += jnp.dot((hid * w).astype(BF16), wd_ref[0], preferred_element_type=F32)


def _moe(h1b, gate, wg, wu, wd, sg, su, sd, t):
    n, d = h1b.shape
    f = wg.shape[2]
    full = lambda a: pl.BlockSpec(a.shape, lambda i, e: (0, 0))
    return pl.pallas_call(
        _moe_kernel,
        grid=(n // t, N_EXPERTS),
        in_specs=[pl.BlockSpec((t, d), lambda i, e: (i, 0)),
                  pl.BlockSpec((t, N_EXPERTS), lambda i, e: (i, 0)),
                  pl.BlockSpec((1, d, f), lambda i, e: (e, 0, 0)),
                  pl.BlockSpec((1, d, f), lambda i, e: (e, 0, 0)),
                  pl.BlockSpec((1, f, d), lambda i, e: (e, 0, 0)),
                  full(sg), full(su), full(sd)],
        out_specs=pl.BlockSpec((t, d), lambda i, e: (i, 0)),
        out_shape=jax.ShapeDtypeStruct((n, d), F32),
        compiler_params=_cparams("parallel", "arbitrary"),
    )(h1b, gate, wg, wu, wd, sg, su, sd)


def _final_kernel(h1_ref, ffn_ref, p_ref, g_ref, b_ref, wg_ref, bg_ref, wp_ref, o_ref, *, alpha):
    h2 = _layer_norm(alpha * h1_ref[...] + ffn_ref[...], g_ref[...], b_ref[...])
    gate = jax.nn.sigmoid(jnp.dot(h2.astype(BF16), wg_ref[...], preferred_element_type=F32)
                          + bg_ref[...])
    emb = jnp.dot(p_ref[...].astype(BF16), wp_ref[...], preferred_element_type=F32)
    o_ref[...] = h2 + gate * emb


def _final(h1, ffn, p2, g, b, wg, bg, wp, alpha, tm):
    n, d = h1.shape
    row = lambda w: pl.BlockSpec((tm, w), lambda i: (i, 0))
    full = lambda a: pl.BlockSpec(a.shape, lambda i: (0, 0))
    return pl.pallas_call(
        functools.partial(_final_kernel, alpha=alpha),
        grid=(n // tm,),
        in_specs=[row(d), row(d), row(p2.shape[1]), full(g), full(b), full(wg), full(bg), full(wp)],
        out_specs=row(d),
        out_shape=jax.ShapeDtypeStruct((n, d), F32),
        compiler_params=_cparams("parallel"),
    )(h1, ffn, p2, g, b, wg, bg, wp)


def _tiles(seq):
    tm = min(256, seq)
    tq_sel = min(128, seq)
    tk_sel = min(512, seq)
    tq_att = min(256, seq)
    tk_att = min(512, seq)
    t_moe = min(1024, seq)
    return tm, tq_sel, tk_sel, tq_att, tk_att, t_moe


def kernel(x, p, positions, ln_emb_g, ln_emb_b, w_in, w_o, diff_lq1, diff_lk1, diff_lq2, diff_lk2,
           diff_subln_g, ln1_g, ln1_b, w_router, router_bias, w_exp_gate, w_exp_up, w_exp_down,
           w_sh_gate, w_sh_up, w_sh_down, ln2_g, ln2_b, w_ple_gate, b_ple_gate, w_ple_proj):
    bsz, seq, d = x.shape
    depth = w_in.shape[0]
    n = bsz * seq
    n_keep = min(IDX_TOPK_MAX, seq // 4)
    alpha = (2.0 * depth) ** 0.25
    tm, tq_sel, tk_sel, tq_att, tk_att, t_moe = _tiles(seq)
    row1 = lambda v: v.reshape(1, -1)

    inv = ROPE_THETA ** (-jnp.arange(0, ROPE_DIM, 2, dtype=F32) / ROPE_DIM)
    inv_lanes = jnp.tile(inv, LANES // (ROPE_DIM // 2)).reshape(1, LANES)
    pos2 = positions.reshape(n, 1)

    h = x.reshape(n, d)
    out = None
    for li in range(depth):
        lambda_init = 0.8 - 0.6 * math.exp(-0.3 * li)
        w = w_in[li]
        aqw, akw, avw = w[:, 0:512], w[:, 512:1024], w[:, 1024:1536]
        iqw, ikw, iww = w[:, 1536:2048], w[:, 2048:2112], w[:, 2112:2120]
        bqw, bkw, bvw = w[:, 2120:2632], w[:, 2632:3144], w[:, 3144:3656]
        w_rope = jnp.concatenate([aqw, akw, iqw, ikw, ikw, bqw, bkw], axis=1).astype(BF16)
        w_plain = jnp.concatenate(
            [avw, bvw, iww, jnp.zeros((d, LANES - IDX_HEADS), F32)], axis=1).astype(BF16)

        (h0, aq, ak, iq, ik2, bq, bk, av, bv, iw) = _proj(
            h, pos2, inv_lanes, row1(ln_emb_g), row1(ln_emb_b), w_rope, w_plain, tm)
        assert depth == 1

        r3 = lambda a: a.reshape(bsz, seq, a.shape[-1])
        bias = _select(r3(iq), r3(iw), r3(ik2), n_keep, tq_sel, tk_sel)
        out_a = _dsa(r3(aq), r3(ak), r3(av), bias, tq_att, tk_att)
        out_b = _diff(r3(bq), r3(bk), r3(bv), row1(diff_lq1[li]), row1(diff_lk1[li]),
                      row1(diff_lq2[li]), row1(diff_lk2[li]), row1(diff_subln_g[li]),
                      lambda_init, tq_att, tk_att)

        wo = w_o[li].astype(BF16)
        h1, h1b, gate = _oproj(out_a.reshape(n, -1), out_b.reshape(n, -1), h0,
                               wo[0:512], wo[512:1024], row1(ln1_g[li]), row1(ln1_b[li]),
                               w_router[li].T, router_bias[li].reshape(-1, 1), alpha, tm)

        ffn = _moe(h1b, gate, w_exp_gate[li].astype(BF16), w_exp_up[li].astype(BF16),
                   w_exp_down[li].astype(BF16), w_sh_gate[li].astype(BF16),
                   w_sh_up[li].astype(BF16), w_sh_down[li].astype(BF16), t_moe)

        out = _final(h1, ffn, p[li].reshape(n, -1), row1(ln2_g[li]), row1(ln2_b[li]),
                     w_ple_gate[li].astype(BF16), row1(b_ple_gate[li]),
                     w_ple_proj[li].astype(BF16), alpha, tm)
        h = out
    return out.reshape(bsz, seq, d)
```

```python
import functools
import math

import jax
import jax.numpy as jnp
from jax import lax
from jax.experimental import pallas as pl
from jax.experimental.pallas import tpu as pltpu

CHUNK = 64
HEAD_DIM = 64
ROPE_DIM = 64
ROPE_THETA = 10000.0
A_HEADS = 8
IDX_HEADS = 8
IDX_DIM = 64
IDX_TOPK_MAX = 256
B_HEADS = 4
B_QK_DIM = 64
B_V_DIM = 128
N_EXPERTS = 64
TOP_K = 8
N_GROUPS = 8
TOPK_GROUPS = 4
ROUTED_SCALE = 2.5
LN_EPS = 1e-5
RMS_EPS = 1e-5

LANES = 128
VMEM_LIMIT = 56 * 1024 * 1024
NEG_BIG = -1e30
INT_MIN = -(2 ** 31)

F32 = jnp.float32
BF16 = jnp.bfloat16


def _cparams(*sem):
    return pltpu.CompilerParams(dimension_semantics=sem, vmem_limit_bytes=VMEM_LIMIT)


def _layer_norm(x, g, b):
    mu = jnp.mean(x, axis=-1, keepdims=True)
    xc = x - mu
    var = jnp.mean(xc * xc, axis=-1, keepdims=True)
    return xc * lax.rsqrt(var + LN_EPS) * g + b


def _dot_nt(a, b):
    return lax.dot_general(a, b, (((1,), (1,)), ((), ())), preferred_element_type=F32)


def _split_lane_halves(pair):
    lo = lax.broadcasted_iota(jnp.int32, pair.shape, 1) < (LANES // 2)
    pf = pair.astype(F32)
    return (jnp.where(lo, pf, 0.0).astype(pair.dtype), jnp.where(lo, 0.0, pf).astype(pair.dtype))


def _proj_kernel(x_ref, pos_ref, inv_ref, g_ref, b_ref, wr_ref, wp_ref,
                 h0_ref, aq_ref, ak_ref, iq_ref, ik_ref, bq_ref, bk_ref,
                 av_ref, bv_ref, iw_ref):
    hn = _layer_norm(x_ref[...], g_ref[...], b_ref[...])
    h0_ref[...] = hn
    hb = hn.astype(BF16)

    ang = pos_ref[...].astype(F32) * inv_ref[...]
    cos = jnp.cos(ang)
    sin = jnp.sin(ang)
    lane = lax.broadcasted_iota(jnp.int32, ang.shape, 1)
    first_half = (lane % ROPE_DIM) < (ROPE_DIM // 2)
    sin_signed = jnp.where(first_half, -sin, sin)

    def rope(v):
        partner = jnp.where(first_half,
                            pltpu.roll(v, LANES - ROPE_DIM // 2, axis=1),
                            pltpu.roll(v, ROPE_DIM // 2, axis=1))
        return v * cos + partner * sin_signed

    pr = jnp.dot(hb, wr_ref[...], preferred_element_type=F32)
    outs = ((aq_ref, 0, 4, HEAD_DIM ** -0.5), (ak_ref, 4, 4, 1.0), (iq_ref, 8, 4, 1.0),
            (ik_ref, 12, 1, 1.0), (bq_ref, 13, 4, B_QK_DIM ** -0.5), (bk_ref, 17, 4, 1.0))
    for ref, g0, ng, scale in outs:
        for j in range(ng):
            v = rope(pr[:, (g0 + j) * LANES:(g0 + j + 1) * LANES])
            if scale != 1.0:
                v = v * scale
            ref[:, j * LANES:(j + 1) * LANES] = v.astype(ref.dtype)

    pp = jnp.dot(hb, wp_ref[...], preferred_element_type=F32)
    av_ref[...] = pp[:, 0:512].astype(BF16)
    bv_ref[...] = pp[:, 512:1024].astype(BF16)
    iw_ref[...] = pp[:, 1024:1152]


def _proj(x2, pos2, inv, g, b, w_rope, w_plain, tm):
    n, d = x2.shape
    nr = w_rope.shape[1]
    npl = w_plain.shape[1]
    row = lambda w: pl.BlockSpec((tm, w), lambda i: (i, 0))
    full = lambda a: pl.BlockSpec(a.shape, lambda i: (0, 0))
    out_shape = (
        jax.ShapeDtypeStruct((n, d), F32),
        jax.ShapeDtypeStruct((n, 512), BF16),
        jax.ShapeDtypeStruct((n, 512), BF16),
        jax.ShapeDtypeStruct((n, 512), BF16),
        jax.ShapeDtypeStruct((n, 128), BF16),
        jax.ShapeDtypeStruct((n, 512), BF16),
        jax.ShapeDtypeStruct((n, 512), BF16),
        jax.ShapeDtypeStruct((n, 512), BF16),
        jax.ShapeDtypeStruct((n, 512), BF16),
        jax.ShapeDtypeStruct((n, 128), F32),
    )
    return pl.pallas_call(
        _proj_kernel,
        grid=(n // tm,),
        in_specs=[row(d), row(1), full(inv), full(g), full(b), full(w_rope), full(w_plain)],
        out_specs=(row(d), row(512), row(512), row(512), row(128), row(512), row(512),
                   row(512), row(512), row(128)),
        out_shape=out_shape,
        compiler_params=_cparams("parallel"),
    )(x2, pos2, inv, g, b, w_rope, w_plain)


def _sortable(s):
    i = lax.bitcast_convert_type(s, jnp.int32)
    return i ^ ((i >> 31) & 0x7FFFFFFF)


def _select_kernel(iq_ref, iw_ref, ik_ref, bias_ref, keys_ref, *, tq, tk, n_keep, w_scale):
    i = pl.program_id(1)
    seq = ik_ref.shape[1]
    n_t = ((i + 1) * tq + tk - 1) // tk

    q_chunk = (i * tq + lax.broadcasted_iota(jnp.int32, (tq, tk), 0)) // CHUNK
    col = lax.broadcasted_iota(jnp.int32, (tq, tk), 1)
    wb = iw_ref[0] * w_scale
    w_cols = [wb[:, h:h + 1] for h in range(IDX_HEADS)]
    q_halves = []
    for hp in range(IDX_HEADS // 2):
        q_halves.extend(_split_lane_halves(iq_ref[0, :, hp * LANES:(hp + 1) * LANES]))

    def score_tile(t, carry):
        off = pl.multiple_of(t * tk, tk)
        kt = ik_ref[0, pl.ds(off, tk), :]
        sc = jnp.zeros((tq, tk), F32)
        for h in range(IDX_HEADS):
            rel = jnp.maximum(_dot_nt(q_halves[h], kt), 0.0)
            sc = sc + w_cols[h] * rel
        key = _sortable(sc + 0.0)
        adm = ((off + col) // CHUNK) <= q_chunk
        keys_ref[:, pl.ds(off, tk)] = jnp.where(adm, key, INT_MIN)
        return carry

    lax.fori_loop(0, n_t, score_tile, 0)

    def count_ge(cand):
        def body(t, acc):
            off = pl.multiple_of(t * tk, tk)
            m = (keys_ref[:, pl.ds(off, tk)] >= cand).astype(jnp.int32)
            s = m[:, 0:LANES]
            for g in range(1, tk // LANES):
                s = s + m[:, g * LANES:(g + 1) * LANES]
            return acc + s
        acc = lax.fori_loop(0, n_t, body, jnp.zeros((tq, LANES), jnp.int32))
        return jnp.sum(acc, axis=1, keepdims=True)

    tau0 = jnp.full((tq, 1), INT_MIN, jnp.int32)
    cand0 = jnp.zeros((tq, 1), jnp.int32)
    tau = jnp.where(count_ge(cand0) >= n_keep, cand0, tau0)

    def bit_step(j, tau):
        cand = tau | (jnp.int32(1) << (30 - j))
        return jnp.where(count_ge(cand) >= n_keep, cand, tau)

    tau = lax.fori_loop(0, 31, bit_step, tau)

    n_ge = count_ge(tau)
    tied = (n_ge > n_keep) & (tau > INT_MIN)
    any_tied = jnp.max(tied.astype(jnp.int32)) > 0

    @pl.when(jnp.logical_not(any_tied))
    def _():
        def write_tile(t, carry):
            off = pl.multiple_of(t * tk, tk)
            key = keys_ref[:, pl.ds(off, tk)]
            keep = (key >= tau) & (key > INT_MIN)
            bias_ref[0, :, pl.ds(off, tk)] = jnp.where(keep, 0.0, NEG_BIG).astype(BF16)
            return carry
        lax.fori_loop(0, n_t, write_tile, 0)

    @pl.when(any_tied)
    def _():
        room = (n_keep - count_ge(tau + 1)).astype(F32)
        tri = (lax.broadcasted_iota(jnp.int32, (tk, tk), 0)
               <= lax.broadcasted_iota(jnp.int32, (tk, tk), 1)).astype(BF16)

        def write_tile(t, seen_eq):
            off = pl.multiple_of(t * tk, tk)
            key = keys_ref[:, pl.ds(off, tk)]
            eq = (key == tau) & tied
            eq_rank = seen_eq + jnp.dot(jnp.where(eq, 1.0, 0.0).astype(BF16), tri,
                                        preferred_element_type=F32)
            keep = (key > INT_MIN) & ((key > tau) | (eq & (eq_rank <= room))
                                      | ((key == tau) & jnp.logical_not(tied)))
            bias_ref[0, :, pl.ds(off, tk)] = jnp.where(keep, 0.0, NEG_BIG).astype(BF16)
            return seen_eq + jnp.sum(jnp.where(eq, 1.0, 0.0), axis=1, keepdims=True)

        lax.fori_loop(0, n_t, write_tile, jnp.zeros((tq, 1), F32))

    def blank_tile(t, carry):
        off = pl.multiple_of(t * tk, tk)
        bias_ref[0, :, pl.ds(off, tk)] = jnp.full((tq, tk), NEG_BIG, BF16)
        return carry

    lax.fori_loop(n_t, seq // tk, blank_tile, 0)


def _select(iq, iw, ik2, n_keep, tq, tk):
    bsz, seq, _ = iq.shape
    kern = functools.partial(_select_kernel, tq=tq, tk=tk, n_keep=n_keep,
                             w_scale=(IDX_HEADS ** -0.5) * (IDX_DIM ** -0.5))
    return pl.pallas_call(
        kern,
        grid=(bsz, seq // tq),
        in_specs=[pl.BlockSpec((1, tq, 512), lambda b, i: (b, i, 0)),
                  pl.BlockSpec((1, tq, 128), lambda b, i: (b, i, 0)),
                  pl.BlockSpec((1, seq, 128), lambda b, i: (b, 0, 0))],
        out_specs=pl.BlockSpec((1, tq, seq), lambda b, i: (b, i, 0)),
        out_shape=jax.ShapeDtypeStruct((bsz, seq, seq), BF16),
        scratch_shapes=[pltpu.VMEM((tq, seq), jnp.int32)],
        compiler_params=_cparams("parallel", "parallel"),
    )(iq, iw, ik2)


def _flash_update(j, s, v_ext, m_ref, acc_ref):
    tk = s.shape[1]
    m_old = m_ref[j]
    m_new = jnp.maximum(m_old, jnp.max(s, axis=1, keepdims=True))
    alpha = jnp.exp(m_old - m_new)
    p = jnp.exp(s - jnp.concatenate([m_new] * (tk // LANES), axis=1))
    m_ref[j] = m_new
    acc_ref[j] = (acc_ref[j] * jnp.concatenate([alpha, alpha], axis=1)
                  + jnp.dot(p.astype(BF16), v_ext, preferred_element_type=F32))


def _dsa_kernel(q_ref, k_ref, v_ref, bias_ref, o_ref, m_ref, acc_ref, *, tq, tk):
    i = pl.program_id(1)
    n_t = ((i + 1) * tq + tk - 1) // tk
    lo_q = lax.broadcasted_iota(jnp.int32, (tq, LANES), 1) < HEAD_DIM
    q_halves = []
    for hp in range(A_HEADS // 2):
        q_halves.extend(_split_lane_halves(q_ref[0, :, hp * LANES:(hp + 1) * LANES]))
    ones = jnp.ones((tk, LANES), BF16)

    m_ref[...] = jnp.full(m_ref.shape, NEG_BIG, F32)
    acc_ref[...] = jnp.zeros(acc_ref.shape, F32)

    def kv_step(t, carry):
        off = pl.multiple_of(t * tk, tk)
        bias = bias_ref[0, :, pl.ds(off, tk)].astype(F32)
        for hp in range(A_HEADS // 2):
            kp = k_ref[0, pl.ds(off, tk), hp * LANES:(hp + 1) * LANES]
            v_ext = jnp.concatenate([v_ref[0, pl.ds(off, tk), hp * LANES:(hp + 1) * LANES], ones],
                                    axis=1)
            for half in range(2):
                h = 2 * hp + half
                _flash_update(h, _dot_nt(q_halves[h], kp) + bias, v_ext, m_ref, acc_ref)
        return carry

    lax.fori_loop(0, n_t, kv_step, 0)

    for hp in range(A_HEADS // 2):
        a0 = acc_ref[2 * hp]
        a1 = acc_ref[2 * hp + 1]
        o = jnp.where(lo_q, a0[:, :LANES] / a0[:, LANES:], a1[:, :LANES] / a1[:, LANES:])
        o_ref[0, :, hp * LANES:(hp + 1) * LANES] = o.astype(o_ref.dtype)


def _dsa(aq, ak, av, bias, tq, tk):
    bsz, seq, w = aq.shape
    kern = functools.partial(_dsa_kernel, tq=tq, tk=tk)
    resident = lambda: pl.BlockSpec((1, seq, w), lambda b, i: (b, 0, 0),
                                    pipeline_mode=pl.Buffered(1))
    return pl.pallas_call(
        kern,
        grid=(bsz, seq // tq),
        in_specs=[pl.BlockSpec((1, tq, w), lambda b, i: (b, i, 0)),
                  resident(), resident(),
                  pl.BlockSpec((1, tq, seq), lambda b, i: (b, i, 0))],
        out_specs=pl.BlockSpec((1, tq, w), lambda b, i: (b, i, 0)),
        out_shape=jax.ShapeDtypeStruct((bsz, seq, w), BF16),
        scratch_shapes=[pltpu.VMEM((A_HEADS, tq, LANES), F32),
                        pltpu.VMEM((A_HEADS, tq, 2 * LANES), F32)],
        compiler_params=_cparams("parallel", "arbitrary"),
    )(aq, ak, av, bias)


def _diff_kernel(q_ref, k_ref, v_ref, lq1_ref, lk1_ref, lq2_ref, lk2_ref, g_ref, o_ref,
                 m_ref, acc_ref, *, tq, tk, lambda_init):
    i = pl.program_id(1)
    n_full = (i * tq + CHUNK) // tk
    q_halves = []
    for hb in range(B_HEADS):
        q_halves.extend(_split_lane_halves(q_ref[0, :, hb * LANES:(hb + 1) * LANES]))
    lam = (jnp.exp(jnp.sum(lq1_ref[...] * lk1_ref[...], keepdims=True))
           - jnp.exp(jnp.sum(lq2_ref[...] * lk2_ref[...], keepdims=True)) + lambda_init)

    ones = jnp.ones((tk, LANES), BF16)

    m_ref[...] = jnp.full(m_ref.shape, NEG_BIG, F32)
    acc_ref[...] = jnp.zeros(acc_ref.shape, F32)

    def tile(off, masked):
        if masked:
            row_chunk = (i * tq + lax.broadcasted_iota(jnp.int32, (tq, tk), 0)) // CHUNK
            col_chunk = (off + lax.broadcasted_iota(jnp.int32, (tq, tk), 1)) // CHUNK
            ok = col_chunk <= row_chunk
        for hb in range(B_HEADS):
            kp = k_ref[0, pl.ds(off, tk), hb * LANES:(hb + 1) * LANES]
            v_ext = jnp.concatenate([v_ref[0, pl.ds(off, tk), hb * LANES:(hb + 1) * LANES], ones],
                                    axis=1)
            for mp in range(2):
                j = 2 * hb + mp
                s = _dot_nt(q_halves[j], kp)
                if masked:
                    s = jnp.where(ok, s, NEG_BIG)
                _flash_update(j, s, v_ext, m_ref, acc_ref)

    def kv_step(t, carry):
        tile(pl.multiple_of(t * tk, tk), False)
        return carry

    lax.fori_loop(0, n_full, kv_step, 0)
    tile(pl.multiple_of(n_full * tk, tk), True)

    for hb in range(B_HEADS):
        a1 = acc_ref[2 * hb]
        a2 = acc_ref[2 * hb + 1]
        o = a1[:, :LANES] / a1[:, LANES:] - lam * (a2[:, :LANES] / a2[:, LANES:])
        o = o * lax.rsqrt(jnp.mean(o * o, axis=-1, keepdims=True) + RMS_EPS)
        o = o * g_ref[...] * (1.0 - lambda_init)
        o_ref[0, :, hb * LANES:(hb + 1) * LANES] = o.astype(o_ref.dtype)


def _diff(bq, bk, bv, lq1, lk1, lq2, lk2, subln_g, lambda_init, tq, tk):
    bsz, seq, w = bq.shape
    kern = functools.partial(_diff_kernel, tq=tq, tk=tk, lambda_init=lambda_init)
    resident = lambda: pl.BlockSpec((1, seq, w), lambda b, i: (b, 0, 0),
                                    pipeline_mode=pl.Buffered(1))
    small = lambda a: pl.BlockSpec(a.shape, lambda b, i: (0, 0))
    return pl.pallas_call(
        kern,
        grid=(bsz, seq // tq),
        in_specs=[pl.BlockSpec((1, tq, w), lambda b, i: (b, i, 0)),
                  resident(), resident(),
                  small(lq1), small(lk1), small(lq2), small(lk2), small(subln_g)],
        out_specs=pl.BlockSpec((1, tq, w), lambda b, i: (b, i, 0)),
        out_shape=jax.ShapeDtypeStruct((bsz, seq, w), BF16),
        scratch_shapes=[pltpu.VMEM((2 * B_HEADS, tq, LANES), F32),
                        pltpu.VMEM((2 * B_HEADS, tq, 2 * LANES), F32)],
        compiler_params=_cparams("parallel", "arbitrary"),
    )(bq, bk, bv, lq1, lk1, lq2, lk2, subln_g)


def _first_argmax_rows(v, row):
    m = jnp.max(v, axis=0, keepdims=True)
    idx = jnp.min(jnp.where(v == m, row, v.shape[0]), axis=0, keepdims=True)
    return m, idx


def _slot_rows(slot_row, values_row, cap):
    s_iota = lax.broadcasted_iota(jnp.int32, (cap, slot_row.shape[1]), 0).astype(F32)
    return jnp.where(s_iota == slot_row, values_row, 0.0)


def _oproj_kernel(oa_ref, ob_ref, h0_ref, woa_ref, wob_ref, g_ref, b_ref, wrt_ref, rb_ref,
                  h1_ref, h1b_ref, slot_ref, gate_ref, over_ref, xt_ref, *, alpha, cap, eg):
    mix = (jnp.dot(oa_ref[...], woa_ref[...], preferred_element_type=F32)
           + jnp.dot(ob_ref[...], wob_ref[...], preferred_element_type=F32))
    h1 = _layer_norm(alpha * h0_ref[...] + mix, g_ref[...], b_ref[...])
    h1_ref[...] = h1
    h1b = h1.astype(BF16)
    h1b_ref[...] = h1b

    logits = lax.dot_general(wrt_ref[...], h1, (((1,), (1,)), ((), ())),
                             precision=lax.Precision.HIGHEST, preferred_element_type=F32)
    scores = jax.nn.sigmoid(logits)
    biased = scores + rb_ref[...]
    tm = scores.shape[1]
    per_g = N_EXPERTS // N_GROUPS
    row8 = lax.broadcasted_iota(jnp.int32, (per_g, tm), 0)

    gs = []
    for g in range(N_GROUPS):
        blk = biased[g * per_g:(g + 1) * per_g, :]
        m1, i1 = _first_argmax_rows(blk, row8)
        m2 = jnp.max(jnp.where(row8 == i1, -jnp.inf, blk), axis=0, keepdims=True)
        gs.append(m1 + m2)
    gscore = jnp.concatenate(gs, axis=0)
    rowg = lax.broadcasted_iota(jnp.int32, (N_GROUPS, tm), 0)
    gsel = jnp.zeros((N_GROUPS, tm), F32)
    for _ in range(TOPK_GROUPS):
        _, ig = _first_argmax_rows(jnp.where(gsel > 0.0, -jnp.inf, gscore), rowg)
        gsel = jnp.where(rowg == ig, 1.0, gsel)

    rowe = lax.broadcasted_iota(jnp.int32, (N_EXPERTS, tm), 0)
    live = jnp.concatenate(
        [jnp.broadcast_to(gsel[g:g + 1, :], (per_g, tm)) for g in range(N_GROUPS)], axis=0)
    esel = jnp.zeros((N_EXPERTS, tm), F32)
    for _ in range(TOP_K):
        cand = jnp.where(live > 0.0, biased, -jnp.inf)
        m = jnp.max(cand, axis=0, keepdims=True)
        idx = jnp.min(jnp.where((live > 0.0) & (cand == m), rowe, N_EXPERTS),
                      axis=0, keepdims=True)
        hit = rowe == idx
        esel = jnp.where(hit, 1.0, esel)
        live = jnp.where(hit, 0.0, live)
    picked = jnp.where(esel > 0.0, scores, 0.0)
    denom = jnp.sum(picked, axis=0, keepdims=True)
    gate = picked / denom * ROUTED_SCALE

    before = (lax.broadcasted_iota(jnp.int32, (tm, tm), 0)
              < lax.broadcasted_iota(jnp.int32, (tm, tm), 1)).astype(F32).astype(BF16)
    rank = jnp.dot(esel.astype(BF16), before, preferred_element_type=F32)
    fits = (esel > 0.0) & (rank < cap)
    slot = jnp.where(fits, rank, -1.0)
    slot_ref[...] = slot
    gate_ref[...] = jnp.where(fits, gate, 0.0)
    over_ref[...] = jnp.where(fits, 0.0, gate)
    d = h1.shape[1]
    for g0 in range(0, N_EXPERTS, eg):
        onehot = jnp.concatenate(
            [_slot_rows(slot[e:e + 1, :], 1.0, cap) for e in range(g0, g0 + eg)], axis=0)
        rows = jnp.dot(onehot.astype(BF16), h1b, preferred_element_type=F32)
        xt_ref[0, g0:g0 + eg] = rows.astype(BF16).reshape(eg, cap, d)


def _oproj(oa, ob, h0, woa, wob, g, b, wrt, rb, alpha, tm, cap):
    n, d = h0.shape
    nt = n // tm
    row = lambda w: pl.BlockSpec((tm, w), lambda i: (i, 0))
    col = lambda: pl.BlockSpec((N_EXPERTS, tm), lambda i: (0, i))
    full = lambda a: pl.BlockSpec(a.shape, lambda i: (0, 0))
    et = jax.ShapeDtypeStruct((N_EXPERTS, n), F32)
    return pl.pallas_call(
        functools.partial(_oproj_kernel, alpha=alpha, cap=cap, eg=8),
        grid=(nt,),
        in_specs=[row(512), row(512), row(d), full(woa), full(wob), full(g), full(b),
                  full(wrt), full(rb)],
        out_specs=(row(d), row(d), col(), col(), col(),
                   pl.BlockSpec((1, N_EXPERTS, cap, d), lambda i: (i, 0, 0, 0))),
        out_shape=(jax.ShapeDtypeStruct((n, d), F32),
                   jax.ShapeDtypeStruct((n, d), BF16),
                   et,
                   et,
                   et,
                   jax.ShapeDtypeStruct((nt, N_EXPERTS, cap, d), BF16)),
        compiler_params=_cparams("parallel"),
    )(oa, ob, h0, woa, wob, g, b, wrt, rb)


def _swiglu(x, wg, wu, wd):
    hid = (jax.nn.silu(jnp.dot(x, wg, preferred_element_type=F32))
           * jnp.dot(x, wu, preferred_element_type=F32))
    return jnp.dot(hid.astype(BF16), wd, preferred_element_type=F32)


def _expert_kernel(x_ref, wg_ref, wu_ref, wd_ref, y_ref):
    tg, cap, d = x_ref.shape
    y = _swiglu(x_ref[...].reshape(tg * cap, d), wg_ref[...], wu_ref[...], wd_ref[...])
    y_ref[...] = y.astype(BF16).reshape(tg, cap, d)


def _experts(xt, wg, wu, wd, tg):
    nt, ne, cap, d = xt.shape
    f = wg.shape[2]
    rows = pl.BlockSpec((tg, None, cap, d), lambda e, j: (j, e, 0, 0))
    return pl.pallas_call(
        _expert_kernel,
        grid=(ne, nt // tg),
        in_specs=[rows,
                  pl.BlockSpec((None, d, f), lambda e, j: (e, 0, 0)),
                  pl.BlockSpec((None, d, f), lambda e, j: (e, 0, 0)),
                  pl.BlockSpec((None, f, d), lambda e, j: (e, 0, 0))],
        out_specs=rows,
        out_shape=jax.ShapeDtypeStruct(xt.shape, BF16),
        compiler_params=_cparams("parallel", "parallel"),
    )(xt, wg, wu, wd)


def _overflow_kernel(x_ref, gate_ref, wg_ref, wu_ref, wd_ref, o_ref):
    e = pl.program_id(1)

    @pl.when(e == 0)
    def _():
        o_ref[...] = jnp.zeros(o_ref.shape, F32)

    x = x_ref[...]
    lane = lax.broadcasted_iota(jnp.int32, gate_ref.shape, 1)
    w = jnp.sum(jnp.where(lane == e, gate_ref[...], 0.0), axis=1, keepdims=True)
    hid = (jax.nn.silu(jnp.dot(x, wg_ref[...], preferred_element_type=F32))
           * jnp.dot(x, wu_ref[...], preferred_element_type=F32))
    o_ref[...] += jnp.dot((hid * w).astype(BF16), wd_ref[...], preferred_element_type=F32)


def _overflow(h1b, gate, wg, wu, wd, t):
    n, d = h1b.shape
    f = wg.shape[2]
    return pl.pallas_call(
        _overflow_kernel,
        grid=(n // t, N_EXPERTS),
        in_specs=[pl.BlockSpec((t, d), lambda i, e: (i, 0)),
                  pl.BlockSpec((t, N_EXPERTS), lambda i, e: (i, 0)),
                  pl.BlockSpec((None, d, f), lambda i, e: (e, 0, 0)),
                  pl.BlockSpec((None, d, f), lambda i, e: (e, 0, 0)),
                  pl.BlockSpec((None, f, d), lambda i, e: (e, 0, 0))],
        out_specs=pl.BlockSpec((t, d), lambda i, e: (i, 0)),
        out_shape=jax.ShapeDtypeStruct((n, d), F32),
        compiler_params=_cparams("parallel", "arbitrary"),
    )(h1b, gate, wg, wu, wd)


def _combine_kernel(*refs, alpha, cap, eg, with_extra):
    if with_extra:
        extra_ref, refs = refs[0], refs[1:]
    (y_ref, slot_ref, gate_ref, h1_ref, h1b_ref, p_ref, sg_ref, su_ref, sd_ref,
     g_ref, b_ref, wg_ref, bg_ref, wp_ref, o_ref) = refs
    slot = slot_ref[...]
    gate = gate_ref[...]
    tm, d = h1_ref.shape
    ffn = _swiglu(h1b_ref[...], sg_ref[...], su_ref[...], sd_ref[...])
    if with_extra:
        ffn = ffn + extra_ref[...]
    for g0 in range(0, N_EXPERTS, eg):
        scatter = jnp.concatenate(
            [_slot_rows(slot[e:e + 1, :], gate[e:e + 1, :], cap) for e in range(g0, g0 + eg)],
            axis=0).astype(BF16)
        y = y_ref[0, g0:g0 + eg].reshape(eg * cap, d)
        ffn = ffn + lax.dot_general(scatter, y, (((0,), (0,)), ((), ())),
                                    preferred_element_type=F32)
    h2 = _layer_norm(alpha * h1_ref[...] + ffn, g_ref[...], b_ref[...])
    emb_gate = jax.nn.sigmoid(jnp.dot(h2.astype(BF16), wg_ref[...], preferred_element_type=F32)
                              + bg_ref[...])
    emb = jnp.dot(p_ref[...].astype(BF16), wp_ref[...], preferred_element_type=F32)
    o_ref[...] = h2 + emb_gate * emb


def _combine(extra, y, slot, gate, h1, h1b, p2, sg, su, sd, g, b, wg, bg, wp, alpha, tm):
    n, d = h1.shape
    nt, ne, cap, _ = y.shape
    row = lambda w: pl.BlockSpec((tm, w), lambda i: (i, 0))
    col = lambda: pl.BlockSpec((ne, tm), lambda i: (0, i))
    full = lambda a: pl.BlockSpec(a.shape, lambda i: (0, 0))
    with_extra = extra is not None
    args = (y, slot, gate, h1, h1b, p2, sg, su, sd, g, b, wg, bg, wp)
    specs = [pl.BlockSpec((1, ne, cap, d), lambda i: (i, 0, 0, 0)), col(), col(), row(d), row(d),
             row(p2.shape[1]), full(sg), full(su), full(sd), full(g), full(b), full(wg),
             full(bg), full(wp)]
    if with_extra:
        args = (extra,) + args
        specs = [row(d)] + specs
    return pl.pallas_call(
        functools.partial(_combine_kernel, alpha=alpha, cap=cap, eg=8, with_extra=with_extra),
        grid=(nt,),
        in_specs=specs,
        out_specs=row(d),
        out_shape=jax.ShapeDtypeStruct((n, d), F32),
        compiler_params=_cparams("parallel"),
    )(*args)


def _tiles(seq):
    tm = min(256, seq)
    tq_sel = min(128, seq)
    tk_sel = min(512, seq)
    tq_att = min(256, seq)
    tk_att = min(512, seq)
    t_moe = min(1024, seq)
    cap = 2 * tm * TOP_K // N_EXPERTS
    tg = 16
    return tm, tq_sel, tk_sel, tq_att, tk_att, t_moe, cap, tg


def kernel(x, p, positions, ln_emb_g, ln_emb_b, w_in, w_o, diff_lq1, diff_lk1, diff_lq2, diff_lk2,
           diff_subln_g, ln1_g, ln1_b, w_router, router_bias, w_exp_gate, w_exp_up, w_exp_down,
           w_sh_gate, w_sh_up, w_sh_down, ln2_g, ln2_b, w_ple_gate, b_ple_gate, w_ple_proj):
    bsz, seq, d = x.shape
    depth = w_in.shape[0]
    n = bsz * seq
    n_keep = min(IDX_TOPK_MAX, seq // 4)
    alpha = (2.0 * depth) ** 0.25
    tm, tq_sel, tk_sel, tq_att, tk_att, t_moe, cap, tg = _tiles(seq)
    tg = min(tg, n // tm)
    row1 = lambda v: v.reshape(1, -1)

    inv = ROPE_THETA ** (-jnp.arange(0, ROPE_DIM, 2, dtype=F32) / ROPE_DIM)
    inv_lanes = jnp.tile(inv, LANES // (ROPE_DIM // 2)).reshape(1, LANES)
    pos2 = positions.reshape(n, 1)

    h = x.reshape(n, d)
    out = None
    for li in range(depth):
        lambda_init = 0.8 - 0.6 * math.exp(-0.3 * li)
        w = w_in[li]
        aqw, akw, avw = w[:, 0:512], w[:, 512:1024], w[:, 1024:1536]
        iqw, ikw, iww = w[:, 1536:2048], w[:, 2048:2112], w[:, 2112:2120]
        bqw, bkw, bvw = w[:, 2120:2632], w[:, 2632:3144], w[:, 3144:3656]
        w_rope = jnp.concatenate([aqw, akw, iqw, ikw, ikw, bqw, bkw], axis=1).astype(BF16)
        w_plain = jnp.concatenate(
            [avw, bvw, iww, jnp.zeros((d, LANES - IDX_HEADS), F32)], axis=1).astype(BF16)

        (h0, aq, ak, iq, ik2, bq, bk, av, bv, iw) = _proj(
            h, pos2, inv_lanes, row1(ln_emb_g), row1(ln_emb_b), w_rope, w_plain, tm)
        assert depth == 1

        r3 = lambda a: a.reshape(bsz, seq, a.shape[-1])
        bias = _select(r3(iq), r3(iw), r3(ik2), n_keep, tq_sel, tk_sel)
        out_a = _dsa(r3(aq), r3(ak), r3(av), bias, tq_att, tk_att)
        out_b = _diff(r3(bq), r3(bk), r3(bv), row1(diff_lq1[li]), row1(diff_lk1[li]),
                      row1(diff_lq2[li]), row1(diff_lk2[li]), row1(diff_subln_g[li]),
                      lambda_init, tq_att, tk_att)

        wo = w_o[li].astype(BF16)
        h1, h1b, slot, gate, over, xt = _oproj(
            out_a.reshape(n, -1), out_b.reshape(n, -1), h0, wo[0:512], wo[512:1024],
            row1(ln1_g[li]), row1(ln1_b[li]), w_router[li].T, router_bias[li].reshape(-1, 1),
            alpha, tm, cap)

        weg, weu, wed = (w_exp_gate[li].astype(BF16), w_exp_up[li].astype(BF16),
                         w_exp_down[li].astype(BF16))
        y = _experts(xt, weg, weu, wed, tg)
        tail = (y, slot, gate, h1, h1b, p[li].reshape(n, -1), w_sh_gate[li].astype(BF16),
                w_sh_up[li].astype(BF16), w_sh_down[li].astype(BF16), row1(ln2_g[li]),
                row1(ln2_b[li]), w_ple_gate[li].astype(BF16), row1(b_ple_gate[li]),
                w_ple_proj[li].astype(BF16), alpha, tm)
        out = lax.cond(jnp.max(over) > 0.0,
                       lambda: _combine(_overflow(h1b, over.T, weg, weu, wed, t_moe), *tail),
                       lambda: _combine(None, *tail))
        h = out
    return out.reshape(bsz, seq, d)
```

```python
import functools
import math

import jax
import jax.numpy as jnp
from jax import lax
from jax.experimental import pallas as pl
from jax.experimental.pallas import tpu as pltpu

CHUNK = 64
HEAD_DIM = 64
ROPE_DIM = 64
ROPE_THETA = 10000.0
A_HEADS = 8
IDX_HEADS = 8
IDX_DIM = 64
IDX_TOPK_MAX = 256
B_HEADS = 4
B_QK_DIM = 64
B_V_DIM = 128
N_EXPERTS = 64
TOP_K = 8
N_GROUPS = 8
TOPK_GROUPS = 4
ROUTED_SCALE = 2.5
LN_EPS = 1e-5
RMS_EPS = 1e-5

LANES = 128
ROW_ALIGN = 16
MOE_BLOCK = 256
VMEM_LIMIT = 56 * 1024 * 1024
NEG_BIG = -1e30
INT_MIN = -(2 ** 31)

F32 = jnp.float32
BF16 = jnp.bfloat16


def _cparams(*sem):
    return pltpu.CompilerParams(dimension_semantics=sem, vmem_limit_bytes=VMEM_LIMIT)


def _layer_norm(x, g, b):
    mu = jnp.mean(x, axis=-1, keepdims=True)
    xc = x - mu
    var = jnp.mean(xc * xc, axis=-1, keepdims=True)
    return xc * lax.rsqrt(var + LN_EPS) * g + b


def _dot_nt(a, b):
    return lax.dot_general(a, b, (((1,), (1,)), ((), ())), preferred_element_type=F32)


def _split_lane_halves(pair):
    lo = lax.broadcasted_iota(jnp.int32, pair.shape, 1) < (LANES // 2)
    pf = pair.astype(F32)
    return (jnp.where(lo, pf, 0.0).astype(pair.dtype), jnp.where(lo, 0.0, pf).astype(pair.dtype))


def _proj_kernel(x_ref, pos_ref, inv_ref, g_ref, b_ref, wr_ref, wp_ref,
                 h0_ref, aq_ref, ak_ref, iq_ref, ik_ref, bq_ref, bk_ref,
                 av_ref, bv_ref, iw_ref):
    hn = _layer_norm(x_ref[...], g_ref[...], b_ref[...])
    h0_ref[...] = hn
    hb = hn.astype(BF16)

    ang = pos_ref[...].astype(F32) * inv_ref[...]
    cos = jnp.cos(ang)
    sin = jnp.sin(ang)
    lane = lax.broadcasted_iota(jnp.int32, ang.shape, 1)
    first_half = (lane % ROPE_DIM) < (ROPE_DIM // 2)
    sin_signed = jnp.where(first_half, -sin, sin)

    def rope(v):
        partner = jnp.where(first_half,
                            pltpu.roll(v, LANES - ROPE_DIM // 2, axis=1),
                            pltpu.roll(v, ROPE_DIM // 2, axis=1))
        return v * cos + partner * sin_signed

    pr = jnp.dot(hb, wr_ref[...], preferred_element_type=F32)
    outs = ((aq_ref, 0, 4, HEAD_DIM ** -0.5), (ak_ref, 4, 4, 1.0), (iq_ref, 8, 4, 1.0),
            (ik_ref, 12, 1, 1.0), (bq_ref, 13, 4, B_QK_DIM ** -0.5), (bk_ref, 17, 4, 1.0))
    for ref, g0, ng, scale in outs:
        for j in range(ng):
            v = rope(pr[:, (g0 + j) * LANES:(g0 + j + 1) * LANES])
            if scale != 1.0:
                v = v * scale
            ref[:, j * LANES:(j + 1) * LANES] = v.astype(ref.dtype)

    pp = jnp.dot(hb, wp_ref[...], preferred_element_type=F32)
    av_ref[...] = pp[:, 0:512].astype(BF16)
    bv_ref[...] = pp[:, 512:1024].astype(BF16)
    iw_ref[...] = pp[:, 1024:1152]


def _proj(x2, pos2, inv, g, b, w_rope, w_plain, tm):
    n, d = x2.shape
    nr = w_rope.shape[1]
    npl = w_plain.shape[1]
    row = lambda w: pl.BlockSpec((tm, w), lambda i: (i, 0))
    full = lambda a: pl.BlockSpec(a.shape, lambda i: (0, 0))
    out_shape = (
        jax.ShapeDtypeStruct((n, d), F32),
        jax.ShapeDtypeStruct((n, 512), BF16),
        jax.ShapeDtypeStruct((n, 512), BF16),
        jax.ShapeDtypeStruct((n, 512), BF16),
        jax.ShapeDtypeStruct((n, 128), BF16),
        jax.ShapeDtypeStruct((n, 512), BF16),
        jax.ShapeDtypeStruct((n, 512), BF16),
        jax.ShapeDtypeStruct((n, 512), BF16),
        jax.ShapeDtypeStruct((n, 512), BF16),
        jax.ShapeDtypeStruct((n, 128), F32),
    )
    return pl.pallas_call(
        _proj_kernel,
        grid=(n // tm,),
        in_specs=[row(d), row(1), full(inv), full(g), full(b), full(w_rope), full(w_plain)],
        out_specs=(row(d), row(512), row(512), row(512), row(128), row(512), row(512),
                   row(512), row(512), row(128)),
        out_shape=out_shape,
        compiler_params=_cparams("parallel"),
    )(x2, pos2, inv, g, b, w_rope, w_plain)


def _sortable(s):
    i = lax.bitcast_convert_type(s, jnp.int32)
    return i ^ ((i >> 31) & 0x7FFFFFFF)


def _select_kernel(iq_ref, iw_ref, ik_ref, bias_ref, keys_ref, *, tq, tk, n_keep, w_scale):
    i = pl.program_id(1)
    seq = ik_ref.shape[1]
    n_t = ((i + 1) * tq + tk - 1) // tk

    q_chunk = (i * tq + lax.broadcasted_iota(jnp.int32, (tq, tk), 0)) // CHUNK
    col = lax.broadcasted_iota(jnp.int32, (tq, tk), 1)
    wb = iw_ref[0] * w_scale
    w_cols = [wb[:, h:h + 1] for h in range(IDX_HEADS)]
    q_halves = []
    for hp in range(IDX_HEADS // 2):
        q_halves.extend(_split_lane_halves(iq_ref[0, :, hp * LANES:(hp + 1) * LANES]))

    def score_tile(t, carry):
        off = pl.multiple_of(t * tk, tk)
        kt = ik_ref[0, pl.ds(off, tk), :]
        sc = jnp.zeros((tq, tk), F32)
        for h in range(IDX_HEADS):
            rel = jnp.maximum(_dot_nt(q_halves[h], kt), 0.0)
            sc = sc + w_cols[h] * rel
        key = _sortable(sc + 0.0)
        adm = ((off + col) // CHUNK) <= q_chunk
        keys_ref[:, pl.ds(off, tk)] = jnp.where(adm, key, INT_MIN)
        return carry

    lax.fori_loop(0, n_t, score_tile, 0)

    def count_ge(cand):
        def body(t, acc):
            off = pl.multiple_of(t * tk, tk)
            m = (keys_ref[:, pl.ds(off, tk)] >= cand).astype(jnp.int32)
            s = m[:, 0:LANES]
            for g in range(1, tk // LANES):
                s = s + m[:, g * LANES:(g + 1) * LANES]
            return acc + s
        acc = lax.fori_loop(0, n_t, body, jnp.zeros((tq, LANES), jnp.int32))
        return jnp.sum(acc, axis=1, keepdims=True)

    tau0 = jnp.full((tq, 1), INT_MIN, jnp.int32)
    cand0 = jnp.zeros((tq, 1), jnp.int32)
    tau = jnp.where(count_ge(cand0) >= n_keep, cand0, tau0)

    def bit_step(j, tau):
        cand = tau | (jnp.int32(1) << (30 - j))
        return jnp.where(count_ge(cand) >= n_keep, cand, tau)

    tau = lax.fori_loop(0, 31, bit_step, tau)

    n_ge = count_ge(tau)
    tied = (n_ge > n_keep) & (tau > INT_MIN)
    any_tied = jnp.max(tied.astype(jnp.int32)) > 0

    @pl.when(jnp.logical_not(any_tied))
    def _():
        def write_tile(t, carry):
            off = pl.multiple_of(t * tk, tk)
            key = keys_ref[:, pl.ds(off, tk)]
            keep = (key >= tau) & (key > INT_MIN)
            bias_ref[0, :, pl.ds(off, tk)] = jnp.where(keep, 0.0, NEG_BIG).astype(BF16)
            return carry
        lax.fori_loop(0, n_t, write_tile, 0)

    @pl.when(any_tied)
    def _():
        room = (n_keep - count_ge(tau + 1)).astype(F32)
        tri = (lax.broadcasted_iota(jnp.int32, (tk, tk), 0)
               <= lax.broadcasted_iota(jnp.int32, (tk, tk), 1)).astype(BF16)

        def write_tile(t, seen_eq):
            off = pl.multiple_of(t * tk, tk)
            key = keys_ref[:, pl.ds(off, tk)]
            eq = (key == tau) & tied
            eq_rank = seen_eq + jnp.dot(jnp.where(eq, 1.0, 0.0).astype(BF16), tri,
                                        preferred_element_type=F32)
            keep = (key > INT_MIN) & ((key > tau) | (eq & (eq_rank <= room))
                                      | ((key == tau) & jnp.logical_not(tied)))
            bias_ref[0, :, pl.ds(off, tk)] = jnp.where(keep, 0.0, NEG_BIG).astype(BF16)
            return seen_eq + jnp.sum(jnp.where(eq, 1.0, 0.0), axis=1, keepdims=True)

        lax.fori_loop(0, n_t, write_tile, jnp.zeros((tq, 1), F32))

    def blank_tile(t, carry):
        off = pl.multiple_of(t * tk, tk)
        bias_ref[0, :, pl.ds(off, tk)] = jnp.full((tq, tk), NEG_BIG, BF16)
        return carry

    lax.fori_loop(n_t, seq // tk, blank_tile, 0)


def _select(iq, iw, ik2, n_keep, tq, tk):
    bsz, seq, _ = iq.shape
    kern = functools.partial(_select_kernel, tq=tq, tk=tk, n_keep=n_keep,
                             w_scale=(IDX_HEADS ** -0.5) * (IDX_DIM ** -0.5))
    return pl.pallas_call(
        kern,
        grid=(bsz, seq // tq),
        in_specs=[pl.BlockSpec((1, tq, 512), lambda b, i: (b, i, 0)),
                  pl.BlockSpec((1, tq, 128), lambda b, i: (b, i, 0)),
                  pl.BlockSpec((1, seq, 128), lambda b, i: (b, 0, 0))],
        out_specs=pl.BlockSpec((1, tq, seq), lambda b, i: (b, i, 0)),
        out_shape=jax.ShapeDtypeStruct((bsz, seq, seq), BF16),
        scratch_shapes=[pltpu.VMEM((tq, seq), jnp.int32)],
        compiler_params=_cparams("parallel", "parallel"),
    )(iq, iw, ik2)


def _flash_update(j, s, v_ext, m_ref, acc_ref):
    tk = s.shape[1]
    m_old = m_ref[j]
    m_new = jnp.maximum(m_old, jnp.max(s, axis=1, keepdims=True))
    alpha = jnp.exp(m_old - m_new)
    p = jnp.exp(s - jnp.concatenate([m_new] * (tk // LANES), axis=1))
    m_ref[j] = m_new
    acc_ref[j] = (acc_ref[j] * jnp.concatenate([alpha, alpha], axis=1)
                  + jnp.dot(p.astype(BF16), v_ext, preferred_element_type=F32))


def _dsa_kernel(q_ref, k_ref, v_ref, bias_ref, o_ref, m_ref, acc_ref, *, tq, tk):
    i = pl.program_id(1)
    n_t = ((i + 1) * tq + tk - 1) // tk
    lo_q = lax.broadcasted_iota(jnp.int32, (tq, LANES), 1) < HEAD_DIM
    q_halves = []
    for hp in range(A_HEADS // 2):
        q_halves.extend(_split_lane_halves(q_ref[0, :, hp * LANES:(hp + 1) * LANES]))
    ones = jnp.ones((tk, LANES), BF16)

    m_ref[...] = jnp.full(m_ref.shape, NEG_BIG, F32)
    acc_ref[...] = jnp.zeros(acc_ref.shape, F32)

    def kv_step(t, carry):
        off = pl.multiple_of(t * tk, tk)
        bias = bias_ref[0, :, pl.ds(off, tk)].astype(F32)
        for hp in range(A_HEADS // 2):
            kp = k_ref[0, pl.ds(off, tk), hp * LANES:(hp + 1) * LANES]
            v_ext = jnp.concatenate([v_ref[0, pl.ds(off, tk), hp * LANES:(hp + 1) * LANES], ones],
                                    axis=1)
            for half in range(2):
                h = 2 * hp + half
                _flash_update(h, _dot_nt(q_halves[h], kp) + bias, v_ext, m_ref, acc_ref)
        return carry

    lax.fori_loop(0, n_t, kv_step, 0)

    for hp in range(A_HEADS // 2):
        a0 = acc_ref[2 * hp]
        a1 = acc_ref[2 * hp + 1]
        o = jnp.where(lo_q, a0[:, :LANES] / a0[:, LANES:], a1[:, :LANES] / a1[:, LANES:])
        o_ref[0, :, hp * LANES:(hp + 1) * LANES] = o.astype(o_ref.dtype)


def _dsa(aq, ak, av, bias, tq, tk):
    bsz, seq, w = aq.shape
    kern = functools.partial(_dsa_kernel, tq=tq, tk=tk)
    resident = lambda: pl.BlockSpec((1, seq, w), lambda b, i: (b, 0, 0),
                                    pipeline_mode=pl.Buffered(1))
    return pl.pallas_call(
        kern,
        grid=(bsz, seq // tq),
        in_specs=[pl.BlockSpec((1, tq, w), lambda b, i: (b, i, 0)),
                  resident(), resident(),
                  pl.BlockSpec((1, tq, seq), lambda b, i: (b, i, 0))],
        out_specs=pl.BlockSpec((1, tq, w), lambda b, i: (b, i, 0)),
        out_shape=jax.ShapeDtypeStruct((bsz, seq, w), BF16),
        scratch_shapes=[pltpu.VMEM((A_HEADS, tq, LANES), F32),
                        pltpu.VMEM((A_HEADS, tq, 2 * LANES), F32)],
        compiler_params=_cparams("parallel", "arbitrary"),
    )(aq, ak, av, bias)


def _diff_kernel(q_ref, k_ref, v_ref, lq1_ref, lk1_ref, lq2_ref, lk2_ref, g_ref, o_ref,
                 m_ref, acc_ref, *, tq, tk, lambda_init):
    i = pl.program_id(1)
    n_full = (i * tq + CHUNK) // tk
    q_halves = []
    for hb in range(B_HEADS):
        q_halves.extend(_split_lane_halves(q_ref[0, :, hb * LANES:(hb + 1) * LANES]))
    lam = (jnp.exp(jnp.sum(lq1_ref[...] * lk1_ref[...], keepdims=True))
           - jnp.exp(jnp.sum(lq2_ref[...] * lk2_ref[...], keepdims=True)) + lambda_init)

    ones = jnp.ones((tk, LANES), BF16)

    m_ref[...] = jnp.full(m_ref.shape, NEG_BIG, F32)
    acc_ref[...] = jnp.zeros(acc_ref.shape, F32)

    def tile(off, masked):
        if masked:
            row_chunk = (i * tq + lax.broadcasted_iota(jnp.int32, (tq, tk), 0)) // CHUNK
            col_chunk = (off + lax.broadcasted_iota(jnp.int32, (tq, tk), 1)) // CHUNK
            ok = col_chunk <= row_chunk
        for hb in range(B_HEADS):
            kp = k_ref[0, pl.ds(off, tk), hb * LANES:(hb + 1) * LANES]
            v_ext = jnp.concatenate([v_ref[0, pl.ds(off, tk), hb * LANES:(hb + 1) * LANES], ones],
                                    axis=1)
            for mp in range(2):
                j = 2 * hb + mp
                s = _dot_nt(q_halves[j], kp)
                if masked:
                    s = jnp.where(ok, s, NEG_BIG)
                _flash_update(j, s, v_ext, m_ref, acc_ref)

    def kv_step(t, carry):
        tile(pl.multiple_of(t * tk, tk), False)
        return carry

    lax.fori_loop(0, n_full, kv_step, 0)
    tile(pl.multiple_of(n_full * tk, tk), True)

    for hb in range(B_HEADS):
        a1 = acc_ref[2 * hb]
        a2 = acc_ref[2 * hb + 1]
        o = a1[:, :LANES] / a1[:, LANES:] - lam * (a2[:, :LANES] / a2[:, LANES:])
        o = o * lax.rsqrt(jnp.mean(o * o, axis=-1, keepdims=True) + RMS_EPS)
        o = o * g_ref[...] * (1.0 - lambda_init)
        o_ref[0, :, hb * LANES:(hb + 1) * LANES] = o.astype(o_ref.dtype)


def _diff(bq, bk, bv, lq1, lk1, lq2, lk2, subln_g, lambda_init, tq, tk):
    bsz, seq, w = bq.shape
    kern = functools.partial(_diff_kernel, tq=tq, tk=tk, lambda_init=lambda_init)
    resident = lambda: pl.BlockSpec((1, seq, w), lambda b, i: (b, 0, 0),
                                    pipeline_mode=pl.Buffered(1))
    small = lambda a: pl.BlockSpec(a.shape, lambda b, i: (0, 0))
    return pl.pallas_call(
        kern,
        grid=(bsz, seq // tq),
        in_specs=[pl.BlockSpec((1, tq, w), lambda b, i: (b, i, 0)),
                  resident(), resident(),
                  small(lq1), small(lk1), small(lq2), small(lk2), small(subln_g)],
        out_specs=pl.BlockSpec((1, tq, w), lambda b, i: (b, i, 0)),
        out_shape=jax.ShapeDtypeStruct((bsz, seq, w), BF16),
        scratch_shapes=[pltpu.VMEM((2 * B_HEADS, tq, LANES), F32),
                        pltpu.VMEM((2 * B_HEADS, tq, 2 * LANES), F32)],
        compiler_params=_cparams("parallel", "arbitrary"),
    )(bq, bk, bv, lq1, lk1, lq2, lk2, subln_g)


def _first_argmax_rows(v, row):
    m = jnp.max(v, axis=0, keepdims=True)
    idx = jnp.min(jnp.where(v == m, row, v.shape[0]), axis=0, keepdims=True)
    return m, idx


def _scatter_rows(dk, gk, r0, nrows):
    r_iota = (lax.broadcasted_iota(jnp.int32, (nrows, dk.shape[1]), 0) + r0).astype(F32)
    out = jnp.zeros(r_iota.shape, F32)
    for k in range(dk.shape[0]):
        out = jnp.where(r_iota == dk[k:k + 1, :], 1.0 if gk is None else gk[k:k + 1, :], out)
    return out


def _oproj_kernel(oa_ref, ob_ref, h0_ref, woa_ref, wob_ref, g_ref, b_ref, wrt_ref, rb_ref,
                  h1_ref, h1b_ref, dk_ref, gk_ref, cnt_ref, xs_ref, *, alpha, rb):
    mix = (jnp.dot(oa_ref[...], woa_ref[...], preferred_element_type=F32)
           + jnp.dot(ob_ref[...], wob_ref[...], preferred_element_type=F32))
    h1 = _layer_norm(alpha * h0_ref[...] + mix, g_ref[...], b_ref[...])
    h1_ref[...] = h1
    h1b = h1.astype(BF16)
    h1b_ref[...] = h1b

    logits = lax.dot_general(wrt_ref[...], h1, (((1,), (1,)), ((), ())),
                             precision=lax.Precision.HIGHEST, preferred_element_type=F32)
    scores = jax.nn.sigmoid(logits)
    biased = scores + rb_ref[...]
    tm = scores.shape[1]
    per_g = N_EXPERTS // N_GROUPS
    row8 = lax.broadcasted_iota(jnp.int32, (per_g, tm), 0)

    gs = []
    for g in range(N_GROUPS):
        blk = biased[g * per_g:(g + 1) * per_g, :]
        m1, i1 = _first_argmax_rows(blk, row8)
        m2 = jnp.max(jnp.where(row8 == i1, -jnp.inf, blk), axis=0, keepdims=True)
        gs.append(m1 + m2)
    gscore = jnp.concatenate(gs, axis=0)
    rowg = lax.broadcasted_iota(jnp.int32, (N_GROUPS, tm), 0)
    gsel = jnp.zeros((N_GROUPS, tm), F32)
    for _ in range(TOPK_GROUPS):
        _, ig = _first_argmax_rows(jnp.where(gsel > 0.0, -jnp.inf, gscore), rowg)
        gsel = jnp.where(rowg == ig, 1.0, gsel)

    rowe = lax.broadcasted_iota(jnp.int32, (N_EXPERTS, tm), 0)
    live = jnp.concatenate(
        [jnp.broadcast_to(gsel[g:g + 1, :], (per_g, tm)) for g in range(N_GROUPS)], axis=0)
    esel = jnp.zeros((N_EXPERTS, tm), F32)
    for _ in range(TOP_K):
        cand = jnp.where(live > 0.0, biased, -jnp.inf)
        m = jnp.max(cand, axis=0, keepdims=True)
        idx = jnp.min(jnp.where((live > 0.0) & (cand == m), rowe, N_EXPERTS),
                      axis=0, keepdims=True)
        hit = rowe == idx
        esel = jnp.where(hit, 1.0, esel)
        live = jnp.where(hit, 0.0, live)
    picked = jnp.where(esel > 0.0, scores, 0.0)
    denom = jnp.sum(picked, axis=0, keepdims=True)
    gate = picked / denom * ROUTED_SCALE

    sel_b = esel.astype(BF16)
    before_t = (lax.broadcasted_iota(jnp.int32, (tm, tm), 0)
                < lax.broadcasted_iota(jnp.int32, (tm, tm), 1)).astype(F32).astype(BF16)
    before_e = (lax.broadcasted_iota(jnp.int32, (N_EXPERTS, N_EXPERTS), 1)
                < lax.broadcasted_iota(jnp.int32, (N_EXPERTS, N_EXPERTS), 0)).astype(F32).astype(BF16)
    rank = jnp.dot(sel_b, before_t, preferred_element_type=F32)
    order = jnp.dot(before_e, sel_b, preferred_element_type=F32)
    cnt = jnp.sum(esel, axis=1, keepdims=True)
    cnt16 = jnp.floor((cnt + (ROW_ALIGN - 1)) / ROW_ALIGN) * ROW_ALIGN
    off = jnp.dot(before_e, jnp.broadcast_to(cnt16, (N_EXPERTS, tm)).astype(BF16),
                  preferred_element_type=F32)
    dest = off + rank
    dks, gks = [], []
    for k in range(TOP_K):
        kth = (esel > 0.0) & (order == k)
        dks.append(jnp.sum(jnp.where(kth, dest, 0.0), axis=0, keepdims=True))
        gks.append(jnp.sum(jnp.where(kth, gate, 0.0), axis=0, keepdims=True))
    dk = jnp.concatenate(dks, axis=0)
    dk_ref[...] = dk
    gk_ref[...] = jnp.concatenate(gks, axis=0)
    counts = _dot_nt(jnp.ones((8, tm), BF16), sel_b)
    cnt_ref[0] = jnp.concatenate([counts, jnp.zeros((8, LANES - N_EXPERTS), F32)], axis=1)

    rt = xs_ref.shape[1]
    for r0 in range(0, rt, rb):
        onehot = _scatter_rows(dk, None, r0, rb)
        rows = jnp.dot(onehot.astype(BF16), h1b, preferred_element_type=F32)
        xs_ref[0, r0:r0 + rb, :] = rows.astype(BF16)


def _oproj(oa, ob, h0, woa, wob, g, b, wrt, rb, alpha, tm, rt):
    n, d = h0.shape
    nt = n // tm
    row = lambda w: pl.BlockSpec((tm, w), lambda i: (i, 0))
    col = lambda: pl.BlockSpec((TOP_K, tm), lambda i: (0, i))
    full = lambda a: pl.BlockSpec(a.shape, lambda i: (0, 0))
    kt = jax.ShapeDtypeStruct((TOP_K, n), F32)
    return pl.pallas_call(
        functools.partial(_oproj_kernel, alpha=alpha, rb=512),
        grid=(nt,),
        in_specs=[row(512), row(512), row(d), full(woa), full(wob), full(g), full(b),
                  full(wrt), full(rb)],
        out_specs=(row(d), row(d), col(), col(),
                   pl.BlockSpec((1, 8, LANES), lambda i: (i, 0, 0)),
                   pl.BlockSpec((1, rt, d), lambda i: (i, 0, 0))),
        out_shape=(jax.ShapeDtypeStruct((n, d), F32),
                   jax.ShapeDtypeStruct((n, d), BF16),
                   kt,
                   kt,
                   jax.ShapeDtypeStruct((nt, 8, LANES), F32),
                   jax.ShapeDtypeStruct((nt, rt, d), BF16)),
        compiler_params=_cparams("parallel"),
    )(oa, ob, h0, woa, wob, g, b, wrt, rb)


def _swiglu(x, wg, wu, wd):
    hid = (jax.nn.silu(jnp.dot(x, wg, preferred_element_type=F32))
           * jnp.dot(x, wu, preferred_element_type=F32))
    return jnp.dot(hid.astype(BF16), wd, preferred_element_type=F32)


def _move_rows_kernel(src_off_ref, dst_off_ref, len_ref, gap_off_ref, gap_len_ref,
                      src_ref, zero_ref, dst_ref, sem):
    def issue_run(r, carry):
        n = pl.multiple_of(len_ref[r], ROW_ALIGN)

        @pl.when(n > 0)
        def _():
            pltpu.make_async_copy(
                src_ref.at[pl.ds(pl.multiple_of(src_off_ref[r], ROW_ALIGN), n), :],
                dst_ref.at[pl.ds(pl.multiple_of(dst_off_ref[r], ROW_ALIGN), n), :], sem).start()
        return carry

    def issue_gap(g, carry):
        n = pl.multiple_of(gap_len_ref[g], ROW_ALIGN)

        @pl.when(n > 0)
        def _():
            pltpu.make_async_copy(
                zero_ref.at[pl.ds(0, n), :],
                dst_ref.at[pl.ds(pl.multiple_of(gap_off_ref[g], ROW_ALIGN), n), :], sem).start()
        return carry

    lax.fori_loop(0, len_ref.shape[0], issue_run, 0)
    lax.fori_loop(0, gap_len_ref.shape[0], issue_gap, 0)
    pltpu.make_async_copy(dst_ref, dst_ref, sem).wait()


def _move_rows(src, zero_rows, src_off, dst_off, lens, gap_off, gap_len, n_dst_rows):
    return pl.pallas_call(
        _move_rows_kernel,
        grid_spec=pltpu.PrefetchScalarGridSpec(
            num_scalar_prefetch=5,
            grid=(1,),
            in_specs=[pl.BlockSpec(memory_space=pl.ANY), pl.BlockSpec(memory_space=pl.ANY)],
            out_specs=pl.BlockSpec(memory_space=pl.ANY),
            scratch_shapes=[pltpu.SemaphoreType.DMA(())]),
        out_shape=jax.ShapeDtypeStruct((n_dst_rows, src.shape[1]), src.dtype),
        compiler_params=_cparams("arbitrary"),
    )(src_off, dst_off, lens, gap_off, gap_len, src, zero_rows)


def _expert_kernel(blk_e_ref, blk_valid_ref, x_ref, wg_ref, wu_ref, wd_ref, y_ref):
    b = pl.program_id(0)
    valid = blk_valid_ref[b]

    @pl.when(valid > 0)
    def _():
        y_ref[...] = _swiglu(x_ref[...], wg_ref[...], wu_ref[...], wd_ref[...]).astype(BF16)

    @pl.when(valid <= 0)
    def _():
        y_ref[...] = jnp.zeros(y_ref.shape, BF16)


def _experts(xg, blk_e, blk_valid, wg, wu, wd):
    nr, d = xg.shape
    f = wg.shape[2]
    rows = pl.BlockSpec((MOE_BLOCK, d), lambda b, be, bv: (b, 0))
    return pl.pallas_call(
        _expert_kernel,
        grid_spec=pltpu.PrefetchScalarGridSpec(
            num_scalar_prefetch=2,
            grid=(nr // MOE_BLOCK,),
            in_specs=[rows,
                      pl.BlockSpec((None, d, f), lambda b, be, bv: (be[b], 0, 0)),
                      pl.BlockSpec((None, d, f), lambda b, be, bv: (be[b], 0, 0)),
                      pl.BlockSpec((None, f, d), lambda b, be, bv: (be[b], 0, 0))],
            out_specs=rows),
        out_shape=jax.ShapeDtypeStruct((nr, d), BF16),
        compiler_params=_cparams("parallel"),
    )(blk_e, blk_valid, xg, wg, wu, wd)


def _combine_kernel(y_ref, dk_ref, gk_ref, h1_ref, h1b_ref, p_ref, sg_ref, su_ref, sd_ref,
                    g_ref, b_ref, wg_ref, bg_ref, wp_ref, o_ref, *, alpha, rb):
    dk = dk_ref[...]
    gk = gk_ref[...]
    rt = y_ref.shape[1]
    ffn = _swiglu(h1b_ref[...], sg_ref[...], su_ref[...], sd_ref[...])
    for r0 in range(0, rt, rb):
        weights = _scatter_rows(dk, gk, r0, rb).astype(BF16)
        ffn = ffn + lax.dot_general(weights, y_ref[0, r0:r0 + rb, :], (((0,), (0,)), ((), ())),
                                    preferred_element_type=F32)
    h2 = _layer_norm(alpha * h1_ref[...] + ffn, g_ref[...], b_ref[...])
    emb_gate = jax.nn.sigmoid(jnp.dot(h2.astype(BF16), wg_ref[...], preferred_element_type=F32)
                              + bg_ref[...])
    emb = jnp.dot(p_ref[...].astype(BF16), wp_ref[...], preferred_element_type=F32)
    o_ref[...] = h2 + emb_gate * emb


def _combine(ys, dk, gk, h1, h1b, p2, sg, su, sd, g, b, wg, bg, wp, alpha, tm):
    n, d = h1.shape
    nt, rt, _ = ys.shape
    row = lambda w: pl.BlockSpec((tm, w), lambda i: (i, 0))
    col = lambda: pl.BlockSpec((TOP_K, tm), lambda i: (0, i))
    full = lambda a: pl.BlockSpec(a.shape, lambda i: (0, 0))
    return pl.pallas_call(
        functools.partial(_combine_kernel, alpha=alpha, rb=512),
        grid=(nt,),
        in_specs=[pl.BlockSpec((1, rt, d), lambda i: (i, 0, 0)), col(), col(), row(d),
                  row(d), row(p2.shape[1]), full(sg), full(su), full(sd), full(g), full(b),
                  full(wg), full(bg), full(wp)],
        out_specs=row(d),
        out_shape=jax.ShapeDtypeStruct((n, d), F32),
        compiler_params=_cparams("parallel"),
    )(ys, dk, gk, h1, h1b, p2, sg, su, sd, g, b, wg, bg, wp)


def _tiles(seq):
    tm = min(256, seq)
    tq_sel = min(128, seq)
    tk_sel = min(512, seq)
    tq_att = min(256, seq)
    tk_att = min(512, seq)
    rt = -(-(tm * TOP_K + N_EXPERTS * (ROW_ALIGN - 1)) // 512) * 512
    return tm, tq_sel, tk_sel, tq_att, tk_att, rt


def kernel(x, p, positions, ln_emb_g, ln_emb_b, w_in, w_o, diff_lq1, diff_lk1, diff_lq2, diff_lk2,
           diff_subln_g, ln1_g, ln1_b, w_router, router_bias, w_exp_gate, w_exp_up, w_exp_down,
           w_sh_gate, w_sh_up, w_sh_down, ln2_g, ln2_b, w_ple_gate, b_ple_gate, w_ple_proj):
    bsz, seq, d = x.shape
    depth = w_in.shape[0]
    n = bsz * seq
    n_keep = min(IDX_TOPK_MAX, seq // 4)
    alpha = (2.0 * depth) ** 0.25
    tm, tq_sel, tk_sel, tq_att, tk_att, rt = _tiles(seq)
    n_blocks = -(-((n // tm) * (tm * TOP_K + N_EXPERTS * (ROW_ALIGN - 1))
                   + N_EXPERTS * (MOE_BLOCK - 1)) // MOE_BLOCK)
    row1 = lambda v: v.reshape(1, -1)

    inv = ROPE_THETA ** (-jnp.arange(0, ROPE_DIM, 2, dtype=F32) / ROPE_DIM)
    inv_lanes = jnp.tile(inv, LANES // (ROPE_DIM // 2)).reshape(1, LANES)
    pos2 = positions.reshape(n, 1)

    h = x.reshape(n, d)
    out = None
    for li in range(depth):
        lambda_init = 0.8 - 0.6 * math.exp(-0.3 * li)
        w = w_in[li]
        aqw, akw, avw = w[:, 0:512], w[:, 512:1024], w[:, 1024:1536]
        iqw, ikw, iww = w[:, 1536:2048], w[:, 2048:2112], w[:, 2112:2120]
        bqw, bkw, bvw = w[:, 2120:2632], w[:, 2632:3144], w[:, 3144:3656]
        w_rope = jnp.concatenate([aqw, akw, iqw, ikw, ikw, bqw, bkw], axis=1).astype(BF16)
        w_plain = jnp.concatenate(
            [avw, bvw, iww, jnp.zeros((d, LANES - IDX_HEADS), F32)], axis=1).astype(BF16)

        (h0, aq, ak, iq, ik2, bq, bk, av, bv, iw) = _proj(
            h, pos2, inv_lanes, row1(ln_emb_g), row1(ln_emb_b), w_rope, w_plain, tm)
        assert depth == 1

        r3 = lambda a: a.reshape(bsz, seq, a.shape[-1])
        bias = _select(r3(iq), r3(iw), r3(ik2), n_keep, tq_sel, tk_sel)
        out_a = _dsa(r3(aq), r3(ak), r3(av), bias, tq_att, tk_att)
        out_b = _diff(r3(bq), r3(bk), r3(bv), row1(diff_lq1[li]), row1(diff_lk1[li]),
                      row1(diff_lq2[li]), row1(diff_lk2[li]), row1(diff_subln_g[li]),
                      lambda_init, tq_att, tk_att)

        wo = w_o[li].astype(BF16)
        h1, h1b, dk, gk, cnt_out, xs = _oproj(
            out_a.reshape(n, -1), out_b.reshape(n, -1), h0, wo[0:512], wo[512:1024],
            row1(ln1_g[li]), row1(ln1_b[li]), w_router[li].T, router_bias[li].reshape(-1, 1),
            alpha, tm, rt)

        nt = n // tm
        cnt = cnt_out[:, 0, :N_EXPERTS].astype(jnp.int32)
        c16 = (cnt + ROW_ALIGN - 1) // ROW_ALIGN * ROW_ALIGN
        tile_off = jnp.cumsum(c16, axis=1) - c16
        tot = jnp.sum(c16, axis=0)
        totp = (tot + MOE_BLOCK - 1) // MOE_BLOCK * MOE_BLOCK
        pend = jnp.cumsum(totp)
        ebase = pend - totp
        exp_off = ebase[None, :] + jnp.cumsum(c16, axis=0) - c16
        sorted_off = (jnp.arange(nt, dtype=jnp.int32) * rt)[:, None] + tile_off
        used = jnp.sum(c16, axis=1)
        blk0 = jnp.arange(n_blocks, dtype=jnp.int32) * MOE_BLOCK
        blk_e = jnp.minimum(jnp.searchsorted(pend, blk0, side='right'), N_EXPERTS - 1).astype(jnp.int32)
        blk_valid = jnp.clip(ebase[blk_e] + tot[blk_e] - blk0, 0, MOE_BLOCK).astype(jnp.int32)
        flat = lambda a: a.reshape(-1).astype(jnp.int32)
        n_rows = n_blocks * MOE_BLOCK
        tail0 = pend[-1] + jnp.arange(-(-n_rows // rt), dtype=jnp.int32) * rt
        exp_gap_off = jnp.concatenate([ebase + tot, tail0])
        exp_gap_len = jnp.concatenate([totp - tot, jnp.clip(n_rows - tail0, 0, rt)])
        sorted_gap_off = jnp.arange(nt, dtype=jnp.int32) * rt + used
        zero_rows = jnp.zeros((rt, d), BF16)

        weg, weu, wed = (w_exp_gate[li].astype(BF16), w_exp_up[li].astype(BF16),
                         w_exp_down[li].astype(BF16))
        xg = _move_rows(xs.reshape(nt * rt, d), zero_rows, flat(sorted_off), flat(exp_off),
                        flat(c16), flat(exp_gap_off), flat(exp_gap_len), n_rows)
        yg = _experts(xg, blk_e, blk_valid, weg, weu, wed)
        ys = _move_rows(yg, zero_rows, flat(exp_off), flat(sorted_off), flat(c16),
                        flat(sorted_gap_off), flat(rt - used), nt * rt)
        out = _combine(ys.reshape(nt, rt, d), dk, gk, h1, h1b,
                       p[li].reshape(n, -1), w_sh_gate[li].astype(BF16),
                       w_sh_up[li].astype(BF16), w_sh_down[li].astype(BF16), row1(ln2_g[li]),
                       row1(ln2_b[li]), w_ple_gate[li].astype(BF16), row1(b_ple_gate[li]),
                       w_ple_proj[li].astype(BF16), alpha, tm)
        h = out
    return out.reshape(bsz, seq, d)
```

```python
import functools
import math

import jax
import jax.numpy as jnp
from jax import lax
from jax.experimental import pallas as pl
from jax.experimental.pallas import tpu as pltpu

CHUNK = 64
HEAD_DIM = 64
ROPE_DIM = 64
ROPE_THETA = 10000.0
A_HEADS = 8
IDX_HEADS = 8
IDX_DIM = 64
IDX_TOPK_MAX = 256
B_HEADS = 4
B_QK_DIM = 64
B_V_DIM = 128
N_EXPERTS = 64
TOP_K = 8
N_GROUPS = 8
TOPK_GROUPS = 4
ROUTED_SCALE = 2.5
LN_EPS = 1e-5
RMS_EPS = 1e-5

LANES = 128
ROW_ALIGN = 16
MOE_BLOCK = 512
VMEM_LIMIT = 56 * 1024 * 1024
NEG_BIG = -1e30
INT_MIN = -(2 ** 31)

F32 = jnp.float32
BF16 = jnp.bfloat16


def _cparams(*sem):
    return pltpu.CompilerParams(dimension_semantics=sem, vmem_limit_bytes=VMEM_LIMIT)


def _layer_norm(x, g, b):
    mu = jnp.mean(x, axis=-1, keepdims=True)
    xc = x - mu
    var = jnp.mean(xc * xc, axis=-1, keepdims=True)
    return xc * lax.rsqrt(var + LN_EPS) * g + b


def _dot_nt(a, b):
    return lax.dot_general(a, b, (((1,), (1,)), ((), ())), preferred_element_type=F32)


def _split_lane_halves(pair):
    lo = lax.broadcasted_iota(jnp.int32, pair.shape, 1) < (LANES // 2)
    pf = pair.astype(F32)
    return (jnp.where(lo, pf, 0.0).astype(pair.dtype), jnp.where(lo, 0.0, pf).astype(pair.dtype))


def _proj_kernel(x_ref, pos_ref, inv_ref, g_ref, b_ref, wr_ref, wp_ref,
                 h0_ref, aq_ref, ak_ref, iq_ref, ik_ref, bq_ref, bk_ref,
                 av_ref, bv_ref, iw_ref):
    hn = _layer_norm(x_ref[...], g_ref[...], b_ref[...])
    h0_ref[...] = hn
    hb = hn.astype(BF16)

    ang = pos_ref[...].astype(F32) * inv_ref[...]
    cos = jnp.cos(ang)
    sin = jnp.sin(ang)
    lane = lax.broadcasted_iota(jnp.int32, ang.shape, 1)
    first_half = (lane % ROPE_DIM) < (ROPE_DIM // 2)
    sin_signed = jnp.where(first_half, -sin, sin)

    def rope(v):
        partner = jnp.where(first_half,
                            pltpu.roll(v, LANES - ROPE_DIM // 2, axis=1),
                            pltpu.roll(v, ROPE_DIM // 2, axis=1))
        return v * cos + partner * sin_signed

    pr = jnp.dot(hb, wr_ref[...], preferred_element_type=F32)
    outs = ((aq_ref, 0, 4, HEAD_DIM ** -0.5), (ak_ref, 4, 4, 1.0), (iq_ref, 8, 4, 1.0),
            (ik_ref, 12, 1, 1.0), (bq_ref, 13, 4, B_QK_DIM ** -0.5), (bk_ref, 17, 4, 1.0))
    for ref, g0, ng, scale in outs:
        for j in range(ng):
            v = rope(pr[:, (g0 + j) * LANES:(g0 + j + 1) * LANES])
            if scale != 1.0:
                v = v * scale
            ref[:, j * LANES:(j + 1) * LANES] = v.astype(ref.dtype)

    pp = jnp.dot(hb, wp_ref[...], preferred_element_type=F32)
    av_ref[...] = pp[:, 0:512].astype(BF16)
    bv_ref[...] = pp[:, 512:1024].astype(BF16)
    iw_ref[...] = pp[:, 1024:1152]


def _proj(x2, pos2, inv, g, b, w_rope, w_plain, tm):
    n, d = x2.shape
    nr = w_rope.shape[1]
    npl = w_plain.shape[1]
    row = lambda w: pl.BlockSpec((tm, w), lambda i: (i, 0))
    full = lambda a: pl.BlockSpec(a.shape, lambda i: (0, 0))
    out_shape = (
        jax.ShapeDtypeStruct((n, d), F32),
        jax.ShapeDtypeStruct((n, 512), BF16),
        jax.ShapeDtypeStruct((n, 512), BF16),
        jax.ShapeDtypeStruct((n, 512), BF16),
        jax.ShapeDtypeStruct((n, 128), BF16),
        jax.ShapeDtypeStruct((n, 512), BF16),
        jax.ShapeDtypeStruct((n, 512), BF16),
        jax.ShapeDtypeStruct((n, 512), BF16),
        jax.ShapeDtypeStruct((n, 512), BF16),
        jax.ShapeDtypeStruct((n, 128), F32),
    )
    return pl.pallas_call(
        _proj_kernel,
        grid=(n // tm,),
        in_specs=[row(d), row(1), full(inv), full(g), full(b), full(w_rope), full(w_plain)],
        out_specs=(row(d), row(512), row(512), row(512), row(128), row(512), row(512),
                   row(512), row(512), row(128)),
        out_shape=out_shape,
        compiler_params=_cparams("parallel"),
    )(x2, pos2, inv, g, b, w_rope, w_plain)


def _sortable(s):
    i = lax.bitcast_convert_type(s, jnp.int32)
    return i ^ ((i >> 31) & 0x7FFFFFFF)


def _select_kernel(iq_ref, iw_ref, ik_ref, bias_ref, keys_ref, *, tq, tk, n_keep, w_scale):
    i = pl.program_id(1)
    seq = ik_ref.shape[1]
    n_t = ((i + 1) * tq + tk - 1) // tk

    q_chunk = (i * tq + lax.broadcasted_iota(jnp.int32, (tq, tk), 0)) // CHUNK
    col = lax.broadcasted_iota(jnp.int32, (tq, tk), 1)
    wb = iw_ref[0] * w_scale
    w_cols = [wb[:, h:h + 1] for h in range(IDX_HEADS)]
    q_halves = []
    for hp in range(IDX_HEADS // 2):
        q_halves.extend(_split_lane_halves(iq_ref[0, :, hp * LANES:(hp + 1) * LANES]))

    def score_tile(t, carry):
        off = pl.multiple_of(t * tk, tk)
        kt = ik_ref[0, pl.ds(off, tk), :]
        sc = jnp.zeros((tq, tk), F32)
        for h in range(IDX_HEADS):
            rel = jnp.maximum(_dot_nt(q_halves[h], kt), 0.0)
            sc = sc + w_cols[h] * rel
        key = _sortable(sc + 0.0)
        adm = ((off + col) // CHUNK) <= q_chunk
        keys_ref[:, pl.ds(off, tk)] = jnp.where(adm, key, INT_MIN)
        return carry

    lax.fori_loop(0, n_t, score_tile, 0)

    def count_ge(cand):
        def body(t, acc):
            off = pl.multiple_of(t * tk, tk)
            m = (keys_ref[:, pl.ds(off, tk)] >= cand).astype(jnp.int32)
            s = m[:, 0:LANES]
            for g in range(1, tk // LANES):
                s = s + m[:, g * LANES:(g + 1) * LANES]
            return acc + s
        acc = lax.fori_loop(0, n_t, body, jnp.zeros((tq, LANES), jnp.int32))
        return jnp.sum(acc, axis=1, keepdims=True)

    tau0 = jnp.full((tq, 1), INT_MIN, jnp.int32)
    cand0 = jnp.zeros((tq, 1), jnp.int32)
    tau = jnp.where(count_ge(cand0) >= n_keep, cand0, tau0)

    def bit_step(j, tau):
        cand = tau | (jnp.int32(1) << (30 - j))
        return jnp.where(count_ge(cand) >= n_keep, cand, tau)

    tau = lax.fori_loop(0, 31, bit_step, tau)

    n_ge = count_ge(tau)
    tied = (n_ge > n_keep) & (tau > INT_MIN)
    any_tied = jnp.max(tied.astype(jnp.int32)) > 0

    @pl.when(jnp.logical_not(any_tied))
    def _():
        def write_tile(t, carry):
            off = pl.multiple_of(t * tk, tk)
            key = keys_ref[:, pl.ds(off, tk)]
            keep = (key >= tau) & (key > INT_MIN)
            bias_ref[0, :, pl.ds(off, tk)] = jnp.where(keep, 0.0, NEG_BIG).astype(BF16)
            return carry
        lax.fori_loop(0, n_t, write_tile, 0)

    @pl.when(any_tied)
    def _():
        room = (n_keep - count_ge(tau + 1)).astype(F32)
        tri = (lax.broadcasted_iota(jnp.int32, (tk, tk), 0)
               <= lax.broadcasted_iota(jnp.int32, (tk, tk), 1)).astype(BF16)

        def write_tile(t, seen_eq):
            off = pl.multiple_of(t * tk, tk)
            key = keys_ref[:, pl.ds(off, tk)]
            eq = (key == tau) & tied
            eq_rank = seen_eq + jnp.dot(jnp.where(eq, 1.0, 0.0).astype(BF16), tri,
                                        preferred_element_type=F32)
            keep = (key > INT_MIN) & ((key > tau) | (eq & (eq_rank <= room))
                                      | ((key == tau) & jnp.logical_not(tied)))
            bias_ref[0, :, pl.ds(off, tk)] = jnp.where(keep, 0.0, NEG_BIG).astype(BF16)
            return seen_eq + jnp.sum(jnp.where(eq, 1.0, 0.0), axis=1, keepdims=True)

        lax.fori_loop(0, n_t, write_tile, jnp.zeros((tq, 1), F32))

    def blank_tile(t, carry):
        off = pl.multiple_of(t * tk, tk)
        bias_ref[0, :, pl.ds(off, tk)] = jnp.full((tq, tk), NEG_BIG, BF16)
        return carry

    lax.fori_loop(n_t, seq // tk, blank_tile, 0)


def _select(iq, iw, ik2, n_keep, tq, tk):
    bsz, seq, _ = iq.shape
    kern = functools.partial(_select_kernel, tq=tq, tk=tk, n_keep=n_keep,
                             w_scale=(IDX_HEADS ** -0.5) * (IDX_DIM ** -0.5))
    return pl.pallas_call(
        kern,
        grid=(bsz, seq // tq),
        in_specs=[pl.BlockSpec((1, tq, 512), lambda b, i: (b, i, 0)),
                  pl.BlockSpec((1, tq, 128), lambda b, i: (b, i, 0)),
                  pl.BlockSpec((1, seq, 128), lambda b, i: (b, 0, 0))],
        out_specs=pl.BlockSpec((1, tq, seq), lambda b, i: (b, i, 0)),
        out_shape=jax.ShapeDtypeStruct((bsz, seq, seq), BF16),
        scratch_shapes=[pltpu.VMEM((tq, seq), jnp.int32)],
        compiler_params=_cparams("parallel", "parallel"),
    )(iq, iw, ik2)


def _flash_update(j, s, v_ext, m_ref, acc_ref):
    tk = s.shape[1]
    m_old = m_ref[j]
    m_new = jnp.maximum(m_old, jnp.max(s, axis=1, keepdims=True))
    alpha = jnp.exp(m_old - m_new)
    p = jnp.exp(s - jnp.concatenate([m_new] * (tk // LANES), axis=1))
    m_ref[j] = m_new
    acc_ref[j] = (acc_ref[j] * jnp.concatenate([alpha, alpha], axis=1)
                  + jnp.dot(p.astype(BF16), v_ext, preferred_element_type=F32))


def _dsa_kernel(q_ref, k_ref, v_ref, bias_ref, o_ref, m_ref, acc_ref, *, tq, tk):
    i = pl.program_id(1)
    n_t = ((i + 1) * tq + tk - 1) // tk
    lo_q = lax.broadcasted_iota(jnp.int32, (tq, LANES), 1) < HEAD_DIM
    q_halves = []
    for hp in range(A_HEADS // 2):
        q_halves.extend(_split_lane_halves(q_ref[0, :, hp * LANES:(hp + 1) * LANES]))
    ones = jnp.ones((tk, LANES), BF16)

    m_ref[...] = jnp.full(m_ref.shape, NEG_BIG, F32)
    acc_ref[...] = jnp.zeros(acc_ref.shape, F32)

    def kv_step(t, carry):
        off = pl.multiple_of(t * tk, tk)
        bias = bias_ref[0, :, pl.ds(off, tk)].astype(F32)
        for hp in range(A_HEADS // 2):
            kp = k_ref[0, pl.ds(off, tk), hp * LANES:(hp + 1) * LANES]
            v_ext = jnp.concatenate([v_ref[0, pl.ds(off, tk), hp * LANES:(hp + 1) * LANES], ones],
                                    axis=1)
            for half in range(2):
                h = 2 * hp + half
                _flash_update(h, _dot_nt(q_halves[h], kp) + bias, v_ext, m_ref, acc_ref)
        return carry

    lax.fori_loop(0, n_t, kv_step, 0)

    for hp in range(A_HEADS // 2):
        a0 = acc_ref[2 * hp]
        a1 = acc_ref[2 * hp + 1]
        o = jnp.where(lo_q, a0[:, :LANES] / a0[:, LANES:], a1[:, :LANES] / a1[:, LANES:])
        o_ref[0, :, hp * LANES:(hp + 1) * LANES] = o.astype(o_ref.dtype)


def _dsa(aq, ak, av, bias, tq, tk):
    bsz, seq, w = aq.shape
    kern = functools.partial(_dsa_kernel, tq=tq, tk=tk)
    resident = lambda: pl.BlockSpec((1, seq, w), lambda b, i: (b, 0, 0),
                                    pipeline_mode=pl.Buffered(1))
    return pl.pallas_call(
        kern,
        grid=(bsz, seq // tq),
        in_specs=[pl.BlockSpec((1, tq, w), lambda b, i: (b, i, 0)),
                  resident(), resident(),
                  pl.BlockSpec((1, tq, seq), lambda b, i: (b, i, 0))],
        out_specs=pl.BlockSpec((1, tq, w), lambda b, i: (b, i, 0)),
        out_shape=jax.ShapeDtypeStruct((bsz, seq, w), BF16),
        scratch_shapes=[pltpu.VMEM((A_HEADS, tq, LANES), F32),
                        pltpu.VMEM((A_HEADS, tq, 2 * LANES), F32)],
        compiler_params=_cparams("parallel", "arbitrary"),
    )(aq, ak, av, bias)


def _diff_kernel(q_ref, k_ref, v_ref, lq1_ref, lk1_ref, lq2_ref, lk2_ref, g_ref, o_ref,
                 m_ref, acc_ref, *, tq, tk, lambda_init):
    i = pl.program_id(1)
    n_full = (i * tq + CHUNK) // tk
    q_halves = []
    for hb in range(B_HEADS):
        q_halves.extend(_split_lane_halves(q_ref[0, :, hb * LANES:(hb + 1) * LANES]))
    lam = (jnp.exp(jnp.sum(lq1_ref[...] * lk1_ref[...], keepdims=True))
           - jnp.exp(jnp.sum(lq2_ref[...] * lk2_ref[...], keepdims=True)) + lambda_init)

    ones = jnp.ones((tk, LANES), BF16)

    m_ref[...] = jnp.full(m_ref.shape, NEG_BIG, F32)
    acc_ref[...] = jnp.zeros(acc_ref.shape, F32)

    def tile(off, masked):
        if masked:
            row_chunk = (i * tq + lax.broadcasted_iota(jnp.int32, (tq, tk), 0)) // CHUNK
            col_chunk = (off + lax.broadcasted_iota(jnp.int32, (tq, tk), 1)) // CHUNK
            ok = col_chunk <= row_chunk
        for hb in range(B_HEADS):
            kp = k_ref[0, pl.ds(off, tk), hb * LANES:(hb + 1) * LANES]
            v_ext = jnp.concatenate([v_ref[0, pl.ds(off, tk), hb * LANES:(hb + 1) * LANES], ones],
                                    axis=1)
            for mp in range(2):
                j = 2 * hb + mp
                s = _dot_nt(q_halves[j], kp)
                if masked:
                    s = jnp.where(ok, s, NEG_BIG)
                _flash_update(j, s, v_ext, m_ref, acc_ref)

    def kv_step(t, carry):
        tile(pl.multiple_of(t * tk, tk), False)
        return carry

    lax.fori_loop(0, n_full, kv_step, 0)
    tile(pl.multiple_of(n_full * tk, tk), True)

    for hb in range(B_HEADS):
        a1 = acc_ref[2 * hb]
        a2 = acc_ref[2 * hb + 1]
        o = a1[:, :LANES] / a1[:, LANES:] - lam * (a2[:, :LANES] / a2[:, LANES:])
        o = o * lax.rsqrt(jnp.mean(o * o, axis=-1, keepdims=True) + RMS_EPS)
        o = o * g_ref[...] * (1.0 - lambda_init)
        o_ref[0, :, hb * LANES:(hb + 1) * LANES] = o.astype(o_ref.dtype)


def _diff(bq, bk, bv, lq1, lk1, lq2, lk2, subln_g, lambda_init, tq, tk):
    bsz, seq, w = bq.shape
    kern = functools.partial(_diff_kernel, tq=tq, tk=tk, lambda_init=lambda_init)
    resident = lambda: pl.BlockSpec((1, seq, w), lambda b, i: (b, 0, 0),
                                    pipeline_mode=pl.Buffered(1))
    small = lambda a: pl.BlockSpec(a.shape, lambda b, i: (0, 0))
    return pl.pallas_call(
        kern,
        grid=(bsz, seq // tq),
        in_specs=[pl.BlockSpec((1, tq, w), lambda b, i: (b, i, 0)),
                  resident(), resident(),
                  small(lq1), small(lk1), small(lq2), small(lk2), small(subln_g)],
        out_specs=pl.BlockSpec((1, tq, w), lambda b, i: (b, i, 0)),
        out_shape=jax.ShapeDtypeStruct((bsz, seq, w), BF16),
        scratch_shapes=[pltpu.VMEM((2 * B_HEADS, tq, LANES), F32),
                        pltpu.VMEM((2 * B_HEADS, tq, 2 * LANES), F32)],
        compiler_params=_cparams("parallel", "arbitrary"),
    )(bq, bk, bv, lq1, lk1, lq2, lk2, subln_g)


def _first_argmax_rows(v, row):
    m = jnp.max(v, axis=0, keepdims=True)
    idx = jnp.min(jnp.where(v == m, row, v.shape[0]), axis=0, keepdims=True)
    return m, idx


def _scatter_rows(dk, gk, r0, nrows):
    r_iota = (lax.broadcasted_iota(jnp.int32, (nrows, dk.shape[1]), 0) + r0).astype(F32)
    out = jnp.zeros(r_iota.shape, F32)
    for k in range(dk.shape[0]):
        out = jnp.where(r_iota == dk[k:k + 1, :], 1.0 if gk is None else gk[k:k + 1, :], out)
    return out


def _oproj_kernel(oa_ref, ob_ref, h0_ref, woa_ref, wob_ref, g_ref, b_ref, wrt_ref, rb_ref,
                  h1_ref, h1b_ref, dk_ref, gk_ref, cnt_ref, xs_ref, *, alpha, rb):
    mix = (jnp.dot(oa_ref[...], woa_ref[...], preferred_element_type=F32)
           + jnp.dot(ob_ref[...], wob_ref[...], preferred_element_type=F32))
    h1 = _layer_norm(alpha * h0_ref[...] + mix, g_ref[...], b_ref[...])
    h1_ref[...] = h1
    h1b = h1.astype(BF16)
    h1b_ref[...] = h1b

    logits = lax.dot_general(wrt_ref[...], h1, (((1,), (1,)), ((), ())),
                             precision=lax.Precision.HIGHEST, preferred_element_type=F32)
    scores = jax.nn.sigmoid(logits)
    biased = scores + rb_ref[...]
    tm = scores.shape[1]
    per_g = N_EXPERTS // N_GROUPS
    row8 = lax.broadcasted_iota(jnp.int32, (per_g, tm), 0)

    gs = []
    for g in range(N_GROUPS):
        blk = biased[g * per_g:(g + 1) * per_g, :]
        m1, i1 = _first_argmax_rows(blk, row8)
        m2 = jnp.max(jnp.where(row8 == i1, -jnp.inf, blk), axis=0, keepdims=True)
        gs.append(m1 + m2)
    gscore = jnp.concatenate(gs, axis=0)
    rowg = lax.broadcasted_iota(jnp.int32, (N_GROUPS, tm), 0)
    gsel = jnp.zeros((N_GROUPS, tm), F32)
    for _ in range(TOPK_GROUPS):
        _, ig = _first_argmax_rows(jnp.where(gsel > 0.0, -jnp.inf, gscore), rowg)
        gsel = jnp.where(rowg == ig, 1.0, gsel)

    rowe = lax.broadcasted_iota(jnp.int32, (N_EXPERTS, tm), 0)
    live = jnp.concatenate(
        [jnp.broadcast_to(gsel[g:g + 1, :], (per_g, tm)) for g in range(N_GROUPS)], axis=0)
    esel = jnp.zeros((N_EXPERTS, tm), F32)
    for _ in range(TOP_K):
        cand = jnp.where(live > 0.0, biased, -jnp.inf)
        m = jnp.max(cand, axis=0, keepdims=True)
        idx = jnp.min(jnp.where((live > 0.0) & (cand == m), rowe, N_EXPERTS),
                      axis=0, keepdims=True)
        hit = rowe == idx
        esel = jnp.where(hit, 1.0, esel)
        live = jnp.where(hit, 0.0, live)
    picked = jnp.where(esel > 0.0, scores, 0.0)
    denom = jnp.sum(picked, axis=0, keepdims=True)
    gate = picked / denom * ROUTED_SCALE

    sel_b = esel.astype(BF16)
    before_t = (lax.broadcasted_iota(jnp.int32, (tm, tm), 0)
                < lax.broadcasted_iota(jnp.int32, (tm, tm), 1)).astype(F32).astype(BF16)
    before_e = (lax.broadcasted_iota(jnp.int32, (N_EXPERTS, N_EXPERTS), 1)
                < lax.broadcasted_iota(jnp.int32, (N_EXPERTS, N_EXPERTS), 0)).astype(F32).astype(BF16)
    rank = jnp.dot(sel_b, before_t, preferred_element_type=F32)
    order = jnp.dot(before_e, sel_b, preferred_element_type=F32)
    cnt = jnp.sum(esel, axis=1, keepdims=True)
    cnt16 = jnp.floor((cnt + (ROW_ALIGN - 1)) / ROW_ALIGN) * ROW_ALIGN
    off = jnp.dot(before_e, jnp.broadcast_to(cnt16, (N_EXPERTS, tm)).astype(BF16),
                  preferred_element_type=F32)
    dest = off + rank
    dks, gks = [], []
    for k in range(TOP_K):
        kth = (esel > 0.0) & (order == k)
        dks.append(jnp.sum(jnp.where(kth, dest, 0.0), axis=0, keepdims=True))
        gks.append(jnp.sum(jnp.where(kth, gate, 0.0), axis=0, keepdims=True))
    dk = jnp.concatenate(dks, axis=0)
    dk_ref[...] = dk
    gk_ref[...] = jnp.concatenate(gks, axis=0)
    counts = _dot_nt(jnp.ones((8, tm), BF16), sel_b)
    cnt_ref[0] = jnp.concatenate([counts, jnp.zeros((8, LANES - N_EXPERTS), F32)], axis=1)

    rt = xs_ref.shape[1]
    for r0 in range(0, rt, rb):
        onehot = _scatter_rows(dk, None, r0, rb)
        rows = jnp.dot(onehot.astype(BF16), h1b, preferred_element_type=F32)
        xs_ref[0, r0:r0 + rb, :] = rows.astype(BF16)


def _oproj(oa, ob, h0, woa, wob, g, b, wrt, rb, alpha, tm, rt):
    n, d = h0.shape
    nt = n // tm
    row = lambda w: pl.BlockSpec((tm, w), lambda i: (i, 0))
    col = lambda: pl.BlockSpec((TOP_K, tm), lambda i: (0, i))
    full = lambda a: pl.BlockSpec(a.shape, lambda i: (0, 0))
    kt = jax.ShapeDtypeStruct((TOP_K, n), F32)
    return pl.pallas_call(
        functools.partial(_oproj_kernel, alpha=alpha, rb=512),
        grid=(nt,),
        in_specs=[row(512), row(512), row(d), full(woa), full(wob), full(g), full(b),
                  full(wrt), full(rb)],
        out_specs=(row(d), row(d), col(), col(),
                   pl.BlockSpec((1, 8, LANES), lambda i: (i, 0, 0)),
                   pl.BlockSpec((1, rt, d), lambda i: (i, 0, 0))),
        out_shape=(jax.ShapeDtypeStruct((n, d), F32),
                   jax.ShapeDtypeStruct((n, d), BF16),
                   kt,
                   kt,
                   jax.ShapeDtypeStruct((nt, 8, LANES), F32),
                   jax.ShapeDtypeStruct((nt, rt, d), BF16)),
        compiler_params=_cparams("parallel"),
    )(oa, ob, h0, woa, wob, g, b, wrt, rb)


def _swiglu(x, wg, wu, wd):
    hid = (jax.nn.silu(jnp.dot(x, wg, preferred_element_type=F32))
           * jnp.dot(x, wu, preferred_element_type=F32))
    return jnp.dot(hid.astype(BF16), wd, preferred_element_type=F32)


def _scatter_runs_kernel(tile_off_ref, exp_off_ref, len_ref, gap_off_ref, gap_len_ref,
                         xs_ref, xg_ref, zero_ref, sem):
    i = pl.program_id(0)
    rt = xs_ref.shape[1]

    def rows_wait(n):
        pltpu.make_async_copy(xg_ref.at[pl.ds(0, n), :], xg_ref.at[pl.ds(0, n), :], sem).wait()

    @pl.when(i == 0)
    def _():
        zero_ref[...] = jnp.zeros(zero_ref.shape, zero_ref.dtype)

        def issue_gap(g, rows):
            n = pl.multiple_of(gap_len_ref[g], ROW_ALIGN)

            @pl.when(n > 0)
            def _():
                pltpu.make_async_copy(
                    zero_ref.at[pl.ds(0, n), :],
                    xg_ref.at[pl.ds(pl.multiple_of(gap_off_ref[g], ROW_ALIGN), n), :], sem).start()
            return rows + n

        gap_rows = lax.fori_loop(0, gap_len_ref.shape[0], issue_gap, 0)

        @pl.when(gap_rows > 0)
        def _():
            rows_wait(pl.multiple_of(gap_rows, ROW_ALIGN))

    def issue_run(e, rows):
        r = i * N_EXPERTS + e
        n = pl.multiple_of(len_ref[r], ROW_ALIGN)

        @pl.when(n > 0)
        def _():
            pltpu.make_async_copy(
                xs_ref.at[0, pl.ds(pl.multiple_of(tile_off_ref[r], ROW_ALIGN), n), :],
                xg_ref.at[pl.ds(pl.multiple_of(exp_off_ref[r], ROW_ALIGN), n), :], sem).start()
        return rows + n

    run_rows = lax.fori_loop(0, N_EXPERTS, issue_run, 0)
    rows_wait(pl.multiple_of(run_rows, ROW_ALIGN))


def _scatter_runs(xs, tile_off, exp_off, lens, gap_off, gap_len, n_dst_rows):
    nt, rt, d = xs.shape
    return pl.pallas_call(
        _scatter_runs_kernel,
        grid_spec=pltpu.PrefetchScalarGridSpec(
            num_scalar_prefetch=5,
            grid=(nt,),
            in_specs=[pl.BlockSpec((1, rt, d), lambda i, *_: (i, 0, 0))],
            out_specs=pl.BlockSpec(memory_space=pl.ANY),
            scratch_shapes=[pltpu.VMEM((rt, d), xs.dtype), pltpu.SemaphoreType.DMA(())]),
        out_shape=jax.ShapeDtypeStruct((n_dst_rows, d), xs.dtype),
        compiler_params=_cparams("arbitrary"),
    )(tile_off, exp_off, lens, gap_off, gap_len, xs)


def _expert_kernel(blk_e_ref, blk_valid_ref, x_ref, wg_ref, wu_ref, wd_ref, y_ref):
    b = pl.program_id(0)
    valid = blk_valid_ref[b]

    @pl.when(valid > 0)
    def _():
        y_ref[...] = _swiglu(x_ref[...], wg_ref[...], wu_ref[...], wd_ref[...]).astype(BF16)

    @pl.when(valid <= 0)
    def _():
        y_ref[...] = jnp.zeros(y_ref.shape, BF16)


def _experts(xg, blk_e, blk_valid, wg, wu, wd):
    nr, d = xg.shape
    f = wg.shape[2]
    rows = pl.BlockSpec((MOE_BLOCK, d), lambda b, be, bv: (b, 0))
    return pl.pallas_call(
        _expert_kernel,
        grid_spec=pltpu.PrefetchScalarGridSpec(
            num_scalar_prefetch=2,
            grid=(nr // MOE_BLOCK,),
            in_specs=[rows,
                      pl.BlockSpec((None, d, f), lambda b, be, bv: (be[b], 0, 0)),
                      pl.BlockSpec((None, d, f), lambda b, be, bv: (be[b], 0, 0)),
                      pl.BlockSpec((None, f, d), lambda b, be, bv: (be[b], 0, 0))],
            out_specs=rows),
        out_shape=jax.ShapeDtypeStruct((nr, d), BF16),
        compiler_params=_cparams("parallel"),
    )(blk_e, blk_valid, xg, wg, wu, wd)


def _combine_kernel(tile_off_ref, exp_off_ref, len_ref, yg_ref, dk_ref, gk_ref, h1_ref, h1b_ref,
                    p_ref, sg_ref, su_ref, sd_ref, g_ref, b_ref, wg_ref, bg_ref, wp_ref, o_ref,
                    ys_ref, sem, *, alpha, rb):
    i = pl.program_id(0)
    rt = ys_ref.shape[0]

    @pl.when(i == 0)
    def _():
        ys_ref[...] = jnp.zeros(ys_ref.shape, ys_ref.dtype)

    def issue_run(e, rows):
        r = i * N_EXPERTS + e
        n = pl.multiple_of(len_ref[r], ROW_ALIGN)

        @pl.when(n > 0)
        def _():
            pltpu.make_async_copy(
                yg_ref.at[pl.ds(pl.multiple_of(exp_off_ref[r], ROW_ALIGN), n), :],
                ys_ref.at[pl.ds(pl.multiple_of(tile_off_ref[r], ROW_ALIGN), n), :], sem).start()
        return rows + n

    run_rows = pl.multiple_of(lax.fori_loop(0, N_EXPERTS, issue_run, 0), ROW_ALIGN)
    dk = dk_ref[...]
    gk = gk_ref[...]
    ffn = _swiglu(h1b_ref[...], sg_ref[...], su_ref[...], sd_ref[...])
    pltpu.make_async_copy(yg_ref.at[pl.ds(0, run_rows), :], ys_ref.at[pl.ds(0, run_rows), :],
                          sem).wait()
    for r0 in range(0, rt, rb):
        weights = _scatter_rows(dk, gk, r0, rb).astype(BF16)
        ffn = ffn + lax.dot_general(weights, ys_ref[r0:r0 + rb, :], (((0,), (0,)), ((), ())),
                                    preferred_element_type=F32)
    h2 = _layer_norm(alpha * h1_ref[...] + ffn, g_ref[...], b_ref[...])
    emb_gate = jax.nn.sigmoid(jnp.dot(h2.astype(BF16), wg_ref[...], preferred_element_type=F32)
                              + bg_ref[...])
    emb = jnp.dot(p_ref[...].astype(BF16), wp_ref[...], preferred_element_type=F32)
    o_ref[...] = h2 + emb_gate * emb


def _combine(tile_off, exp_off, lens, yg, dk, gk, h1, h1b, p2, sg, su, sd, g, b, wg, bg, wp,
             alpha, tm, rt):
    n, d = h1.shape
    row = lambda w: pl.BlockSpec((tm, w), lambda i, *_: (i, 0))
    col = lambda: pl.BlockSpec((TOP_K, tm), lambda i, *_: (0, i))
    full = lambda a: pl.BlockSpec(a.shape, lambda i, *_: (0, 0))
    return pl.pallas_call(
        functools.partial(_combine_kernel, alpha=alpha, rb=512),
        grid_spec=pltpu.PrefetchScalarGridSpec(
            num_scalar_prefetch=3,
            grid=(n // tm,),
            in_specs=[pl.BlockSpec(memory_space=pl.ANY), col(), col(), row(d),
                      row(d), row(p2.shape[1]), full(sg), full(su), full(sd), full(g), full(b),
                      full(wg), full(bg), full(wp)],
            out_specs=row(d),
            scratch_shapes=[pltpu.VMEM((rt, d), yg.dtype), pltpu.SemaphoreType.DMA(())]),
        out_shape=jax.ShapeDtypeStruct((n, d), F32),
        compiler_params=_cparams("arbitrary"),
    )(tile_off, exp_off, lens, yg, dk, gk, h1, h1b, p2, sg, su, sd, g, b, wg, bg, wp)


def _tiles(seq):
    tm = min(256, seq)
    tq_sel = min(128, seq)
    tk_sel = min(512, seq)
    tq_att = min(256, seq)
    tk_att = min(512, seq)
    rt = -(-(tm * TOP_K + N_EXPERTS * (ROW_ALIGN - 1)) // 512) * 512
    return tm, tq_sel, tk_sel, tq_att, tk_att, rt


def kernel(x, p, positions, ln_emb_g, ln_emb_b, w_in, w_o, diff_lq1, diff_lk1, diff_lq2, diff_lk2,
           diff_subln_g, ln1_g, ln1_b, w_router, router_bias, w_exp_gate, w_exp_up, w_exp_down,
           w_sh_gate, w_sh_up, w_sh_down, ln2_g, ln2_b, w_ple_gate, b_ple_gate, w_ple_proj):
    bsz, seq, d = x.shape
    depth = w_in.shape[0]
    n = bsz * seq
    n_keep = min(IDX_TOPK_MAX, seq // 4)
    alpha = (2.0 * depth) ** 0.25
    tm, tq_sel, tk_sel, tq_att, tk_att, rt = _tiles(seq)
    n_blocks = -(-((n // tm) * (tm * TOP_K + N_EXPERTS * (ROW_ALIGN - 1))
                   + N_EXPERTS * (MOE_BLOCK - 1)) // MOE_BLOCK)
    row1 = lambda v: v.reshape(1, -1)

    inv = ROPE_THETA ** (-jnp.arange(0, ROPE_DIM, 2, dtype=F32) / ROPE_DIM)
    inv_lanes = jnp.tile(inv, LANES // (ROPE_DIM // 2)).reshape(1, LANES)
    pos2 = positions.reshape(n, 1)

    h = x.reshape(n, d)
    out = None
    for li in range(depth):
        lambda_init = 0.8 - 0.6 * math.exp(-0.3 * li)
        w = w_in[li]
        aqw, akw, avw = w[:, 0:512], w[:, 512:1024], w[:, 1024:1536]
        iqw, ikw, iww = w[:, 1536:2048], w[:, 2048:2112], w[:, 2112:2120]
        bqw, bkw, bvw = w[:, 2120:2632], w[:, 2632:3144], w[:, 3144:3656]
        w_rope = jnp.concatenate([aqw, akw, iqw, ikw, ikw, bqw, bkw], axis=1).astype(BF16)
        w_plain = jnp.concatenate(
            [avw, bvw, iww, jnp.zeros((d, LANES - IDX_HEADS), F32)], axis=1).astype(BF16)

        (h0, aq, ak, iq, ik2, bq, bk, av, bv, iw) = _proj(
            h, pos2, inv_lanes, row1(ln_emb_g), row1(ln_emb_b), w_rope, w_plain, tm)
        assert depth == 1

        r3 = lambda a: a.reshape(bsz, seq, a.shape[-1])
        bias = _select(r3(iq), r3(iw), r3(ik2), n_keep, tq_sel, tk_sel)
        out_a = _dsa(r3(aq), r3(ak), r3(av), bias, tq_att, tk_att)
        out_b = _diff(r3(bq), r3(bk), r3(bv), row1(diff_lq1[li]), row1(diff_lk1[li]),
                      row1(diff_lq2[li]), row1(diff_lk2[li]), row1(diff_subln_g[li]),
                      lambda_init, tq_att, tk_att)

        wo = w_o[li].astype(BF16)
        h1, h1b, dk, gk, cnt_out, xs = _oproj(
            out_a.reshape(n, -1), out_b.reshape(n, -1), h0, wo[0:512], wo[512:1024],
            row1(ln1_g[li]), row1(ln1_b[li]), w_router[li].T, router_bias[li].reshape(-1, 1),
            alpha, tm, rt)

        nt = n // tm
        c16 = jnp.ceil(cnt_out[:, 0, :N_EXPERTS] / ROW_ALIGN) * ROW_ALIGN
        tri = lambda m: jnp.tril(jnp.ones((m, m), F32))
        hi = lax.Precision.HIGHEST
        tile_off = jnp.dot(c16, tri(N_EXPERTS).T, precision=hi) - c16
        cum_tiles = jnp.dot(tri(nt), c16, precision=hi)
        tot = cum_tiles[-1]
        totp = jnp.ceil(tot / MOE_BLOCK) * MOE_BLOCK
        pend = jnp.dot(totp, tri(N_EXPERTS).T, precision=hi)
        ebase = pend - totp
        exp_off = ebase[None, :] + cum_tiles - c16
        blk0 = jnp.arange(n_blocks, dtype=F32) * MOE_BLOCK
        blk_e = jnp.minimum(jnp.sum(pend[None, :] <= blk0[:, None], axis=1), N_EXPERTS - 1)
        onehot_e = (blk_e[:, None] == jnp.arange(N_EXPERTS)[None, :]).astype(F32)
        blk_end = jnp.dot(onehot_e, ebase + tot, precision=hi)
        blk_valid = jnp.clip(blk_end - blk0, 0, MOE_BLOCK)
        flat = lambda a: a.reshape(-1).astype(jnp.int32)
        n_rows = n_blocks * MOE_BLOCK
        tail0 = pend[-1] + jnp.arange(-(-n_rows // rt), dtype=F32) * rt
        gap_off = jnp.concatenate([ebase + tot, tail0])
        gap_len = jnp.concatenate([totp - tot, jnp.clip(n_rows - tail0, 0, rt)])

        weg, weu, wed = (w_exp_gate[li].astype(BF16), w_exp_up[li].astype(BF16),
                         w_exp_down[li].astype(BF16))
        xg = _scatter_runs(xs, flat(tile_off), flat(exp_off), flat(c16), flat(gap_off),
                           flat(gap_len), n_rows)
        yg = _experts(xg, flat(blk_e), flat(blk_valid), weg, weu, wed)
        out = _combine(flat(tile_off), flat(exp_off), flat(c16), yg, dk, gk, h1, h1b,
                       p[li].reshape(n, -1), w_sh_gate[li].astype(BF16),
                       w_sh_up[li].astype(BF16), w_sh_down[li].astype(BF16), row1(ln2_g[li]),
                       row1(ln2_b[li]), w_ple_gate[li].astype(BF16), row1(b_ple_gate[li]),
                       w_ple_proj[li].astype(BF16), alpha, tm, rt)
        h = out
    return out.reshape(bsz, seq, d)
```

```python
import functools
import math

import jax
import jax.numpy as jnp
from jax import lax
from jax.experimental import pallas as pl
from jax.experimental.pallas import tpu as pltpu

CHUNK = 64
HEAD_DIM = 64
ROPE_DIM = 64
ROPE_THETA = 10000.0
A_HEADS = 8
IDX_HEADS = 8
IDX_DIM = 64
IDX_TOPK_MAX = 256
B_HEADS = 4
B_QK_DIM = 64
B_V_DIM = 128
N_EXPERTS = 64
TOP_K = 8
N_GROUPS = 8
TOPK_GROUPS = 4
ROUTED_SCALE = 2.5
LN_EPS = 1e-5
RMS_EPS = 1e-5

LANES = 128
ROW_ALIGN = 16
MOE_BLOCK = 512
VMEM_LIMIT = 56 * 1024 * 1024
NEG_BIG = -1e30
INT_MIN = -(2 ** 31)
COUNT_ROWS = 128
GROUPS = 256

F32 = jnp.float32
BF16 = jnp.bfloat16


def _cparams(*sem):
    return pltpu.CompilerParams(dimension_semantics=sem, vmem_limit_bytes=VMEM_LIMIT)


def _layer_norm(x, g, b):
    mu = jnp.mean(x, axis=-1, keepdims=True)
    xc = x - mu
    var = jnp.mean(xc * xc, axis=-1, keepdims=True)
    return xc * lax.rsqrt(var + LN_EPS) * g + b


def _dot_nt(a, b):
    return lax.dot_general(a, b, (((1,), (1,)), ((), ())), preferred_element_type=F32)


def _split_lane_halves(pair):
    lo = lax.broadcasted_iota(jnp.int32, pair.shape, 1) < (LANES // 2)
    pf = pair.astype(F32)
    return (jnp.where(lo, pf, 0.0).astype(pair.dtype), jnp.where(lo, 0.0, pf).astype(pair.dtype))


def _proj_kernel(x_ref, pos_ref, inv_ref, g_ref, b_ref, wr_ref, wp_ref,
                 h0_ref, aq_ref, ak_ref, iq_ref, ik_ref, bq_ref, bk_ref,
                 av_ref, bv_ref, iw_ref):
    hn = _layer_norm(x_ref[...], g_ref[...], b_ref[...])
    h0_ref[...] = hn
    hb = hn.astype(BF16)

    ang = pos_ref[...].astype(F32) * inv_ref[...]
    cos = jnp.cos(ang)
    sin = jnp.sin(ang)
    lane = lax.broadcasted_iota(jnp.int32, ang.shape, 1)
    first_half = (lane % ROPE_DIM) < (ROPE_DIM // 2)
    sin_signed = jnp.where(first_half, -sin, sin)

    def rope(v):
        partner = jnp.where(first_half,
                            pltpu.roll(v, LANES - ROPE_DIM // 2, axis=1),
                            pltpu.roll(v, ROPE_DIM // 2, axis=1))
        return v * cos + partner * sin_signed

    pr = jnp.dot(hb, wr_ref[...], preferred_element_type=F32)
    outs = ((aq_ref, 0, 4, HEAD_DIM ** -0.5), (ak_ref, 4, 4, 1.0), (iq_ref, 8, 4, 1.0),
            (ik_ref, 12, 1, 1.0), (bq_ref, 13, 4, B_QK_DIM ** -0.5), (bk_ref, 17, 4, 1.0))
    for ref, g0, ng, scale in outs:
        for j in range(ng):
            v = rope(pr[:, (g0 + j) * LANES:(g0 + j + 1) * LANES])
            if scale != 1.0:
                v = v * scale
            ref[:, j * LANES:(j + 1) * LANES] = v.astype(ref.dtype)

    pp = jnp.dot(hb, wp_ref[...], preferred_element_type=F32)
    av_ref[...] = pp[:, 0:512].astype(BF16)
    bv_ref[...] = pp[:, 512:1024].astype(BF16)
    iw_ref[...] = pp[:, 1024:1152]


def _proj(x2, pos2, inv, g, b, w_rope, w_plain, tm):
    n, d = x2.shape
    nr = w_rope.shape[1]
    npl = w_plain.shape[1]
    row = lambda w: pl.BlockSpec((tm, w), lambda i: (i, 0))
    full = lambda a: pl.BlockSpec(a.shape, lambda i: (0, 0))
    out_shape = (
        jax.ShapeDtypeStruct((n, d), F32),
        jax.ShapeDtypeStruct((n, 512), BF16),
        jax.ShapeDtypeStruct((n, 512), BF16),
        jax.ShapeDtypeStruct((n, 512), BF16),
        jax.ShapeDtypeStruct((n, 128), BF16),
        jax.ShapeDtypeStruct((n, 512), BF16),
        jax.ShapeDtypeStruct((n, 512), BF16),
        jax.ShapeDtypeStruct((n, 512), BF16),
        jax.ShapeDtypeStruct((n, 512), BF16),
        jax.ShapeDtypeStruct((n, 128), F32),
    )
    return pl.pallas_call(
        _proj_kernel,
        grid=(n // tm,),
        in_specs=[row(d), row(1), full(inv), full(g), full(b), full(w_rope), full(w_plain)],
        out_specs=(row(d), row(512), row(512), row(512), row(128), row(512), row(512),
                   row(512), row(512), row(128)),
        out_shape=out_shape,
        compiler_params=_cparams("parallel"),
    )(x2, pos2, inv, g, b, w_rope, w_plain)


def _sortable(s):
    i = lax.bitcast_convert_type(s, jnp.int32)
    return i ^ ((i >> 31) & 0x7FFFFFFF)


def _select_kernel(iq_ref, iw_ref, ik_ref, bias_ref, keys_ref, gmax_ref, *, tq, tk, n_keep,
                   w_scale):
    i = pl.program_id(1)
    seq = ik_ref.shape[1]
    n_t = ((i + 1) * tq + tk - 1) // tk

    q_chunk = (i * tq + lax.broadcasted_iota(jnp.int32, (tq, tk), 0)) // CHUNK
    col = lax.broadcasted_iota(jnp.int32, (tq, tk), 1)
    wb = iw_ref[0] * w_scale
    w_cols = [wb[:, h:h + 1] for h in range(IDX_HEADS)]
    q_halves = []
    for hp in range(IDX_HEADS // 2):
        q_halves.extend(_split_lane_halves(iq_ref[0, :, hp * LANES:(hp + 1) * LANES]))

    def score_tile(t, carry):
        off = pl.multiple_of(t * tk, tk)
        kt = ik_ref[0, pl.ds(off, tk), :]
        sc = jnp.zeros((tq, tk), F32)
        for h in range(IDX_HEADS):
            rel = jnp.maximum(_dot_nt(q_halves[h], kt), 0.0)
            sc = sc + w_cols[h] * rel
        key = jnp.where(((off + col) // CHUNK) <= q_chunk, _sortable(sc + 0.0), INT_MIN)
        keys_ref[:, pl.ds(off, tk)] = key
        gm = key[:, 0:GROUPS]
        for g0 in range(GROUPS, tk, GROUPS):
            gm = jnp.maximum(gm, key[:, g0:g0 + GROUPS])
        gmax_ref[...] = jnp.maximum(gmax_ref[...], gm)
        return carry

    gmax_ref[...] = jnp.full(gmax_ref.shape, INT_MIN, jnp.int32)
    lax.fori_loop(0, n_t, score_tile, 0)

    def count_ge(cand):
        counts = []
        for r0 in range(0, tq, COUNT_ROWS):
            rows = min(COUNT_ROWS, tq - r0)
            c = cand[r0:r0 + rows]

            def body(t, acc, r0=r0, rows=rows, c=c):
                off = pl.multiple_of(t * tk, tk)
                m = (keys_ref[r0:r0 + rows, pl.ds(off, tk)] >= c).astype(jnp.int32)
                s = m[:, 0:LANES]
                for g in range(1, tk // LANES):
                    s = s + m[:, g * LANES:(g + 1) * LANES]
                return acc + s
            acc = lax.fori_loop(0, n_t, body, jnp.zeros((rows, LANES), jnp.int32))
            counts.append(jnp.sum(acc, axis=1, keepdims=True))
        return counts[0] if len(counts) == 1 else jnp.concatenate(counts, axis=0)

    gmax = gmax_ref[...]
    row = lax.broadcasted_iota(jnp.int32, (tq, 1), 0)
    n_adm = ((i * tq + row) // CHUNK + 1) * CHUNK
    few = n_adm <= n_keep
    lo0 = jnp.where(few, INT_MIN + 1, jnp.min(gmax, axis=1, keepdims=True))
    hi0 = jnp.where(few, INT_MIN + 2, jnp.max(gmax, axis=1, keepdims=True) + 1)

    def probe(state):
        lo, hi, thr, done, _, it = state
        mid = (lo >> 1) + (hi >> 1) + (lo & hi & 1)
        closed = mid == lo
        cnt = count_ge(mid)
        exact = cnt == n_keep
        up = cnt >= n_keep
        live = done == 0
        thr = jnp.where(live & exact, mid, jnp.where(live & closed, lo, thr))
        lo = jnp.where(live & up, mid, lo)
        hi = jnp.where(live & jnp.logical_not(up), mid, hi)
        done = jnp.where(exact | closed, 1, done)
        return lo, hi, thr, done, jnp.min(done), it + 1

    def searching(state):
        return (state[4] == 0) & (state[5] < 34)

    done0 = few.astype(jnp.int32)
    state = lax.while_loop(
        searching, probe, (lo0, hi0, lo0, done0, jnp.min(done0), jnp.int32(0)))
    thr = state[2]

    n_ge = count_ge(thr)
    tied = n_ge > n_keep
    any_tied = jnp.max(tied.astype(jnp.int32)) > 0
    tau = thr

    @pl.when(jnp.logical_not(any_tied))
    def _():
        def write_tile(t, carry):
            off = pl.multiple_of(t * tk, tk)
            keep = keys_ref[:, pl.ds(off, tk)] >= tau
            bias_ref[0, :, pl.ds(off, tk)] = jnp.where(keep, 0.0, NEG_BIG).astype(BF16)
            return carry
        lax.fori_loop(0, n_t, write_tile, 0)

    @pl.when(any_tied)
    def _():
        room = (n_keep - count_ge(tau + 1)).astype(F32)
        tri = (lax.broadcasted_iota(jnp.int32, (tk, tk), 0)
               <= lax.broadcasted_iota(jnp.int32, (tk, tk), 1)).astype(BF16)

        def write_tile(t, seen_eq):
            off = pl.multiple_of(t * tk, tk)
            key = keys_ref[:, pl.ds(off, tk)]
            eq = (key == tau) & tied
            eq_rank = seen_eq + jnp.dot(jnp.where(eq, 1.0, 0.0).astype(BF16), tri,
                                        preferred_element_type=F32)
            keep = (key > INT_MIN) & ((key > tau) | (eq & (eq_rank <= room))
                                      | ((key == tau) & jnp.logical_not(tied)))
            bias_ref[0, :, pl.ds(off, tk)] = jnp.where(keep, 0.0, NEG_BIG).astype(BF16)
            return seen_eq + jnp.sum(jnp.where(eq, 1.0, 0.0), axis=1, keepdims=True)

        lax.fori_loop(0, n_t, write_tile, jnp.zeros((tq, 1), F32))

    def blank_tile(t, carry):
        off = pl.multiple_of(t * tk, tk)
        bias_ref[0, :, pl.ds(off, tk)] = jnp.full((tq, tk), NEG_BIG, BF16)
        return carry

    lax.fori_loop(n_t, seq // tk, blank_tile, 0)


def _select(iq, iw, ik2, n_keep, tq, tk):
    bsz, seq, _ = iq.shape
    kern = functools.partial(_select_kernel, tq=tq, tk=tk, n_keep=n_keep,
                             w_scale=(IDX_HEADS ** -0.5) * (IDX_DIM ** -0.5))
    return pl.pallas_call(
        kern,
        grid=(bsz, seq // tq),
        in_specs=[pl.BlockSpec((1, tq, 512), lambda b, i: (b, i, 0)),
                  pl.BlockSpec((1, tq, 128), lambda b, i: (b, i, 0)),
                  pl.BlockSpec((1, seq, 128), lambda b, i: (b, 0, 0))],
        out_specs=pl.BlockSpec((1, tq, seq), lambda b, i: (b, i, 0)),
        out_shape=jax.ShapeDtypeStruct((bsz, seq, seq), BF16),
        scratch_shapes=[pltpu.VMEM((tq, seq), jnp.int32), pltpu.VMEM((tq, GROUPS), jnp.int32)],
        compiler_params=_cparams("parallel", "parallel"),
    )(iq, iw, ik2)


def _flash_update(j, s, v_ext, m_ref, acc_ref):
    tk = s.shape[1]
    m_old = m_ref[j]
    m_new = jnp.maximum(m_old, jnp.max(s, axis=1, keepdims=True))
    alpha = jnp.exp(m_old - m_new)
    p = jnp.exp(s - jnp.concatenate([m_new] * (tk // LANES), axis=1))
    m_ref[j] = m_new
    acc_ref[j] = (acc_ref[j] * jnp.concatenate([alpha, alpha], axis=1)
                  + jnp.dot(p.astype(BF16), v_ext, preferred_element_type=F32))


def _dsa_kernel(q_ref, k_ref, v_ref, bias_ref, o_ref, m_ref, acc_ref, *, tq, tk):
    i = pl.program_id(1)
    n_t = ((i + 1) * tq + tk - 1) // tk
    lo_q = lax.broadcasted_iota(jnp.int32, (tq, LANES), 1) < HEAD_DIM
    q_halves = []
    for hp in range(A_HEADS // 2):
        q_halves.extend(_split_lane_halves(q_ref[0, :, hp * LANES:(hp + 1) * LANES]))
    ones = jnp.ones((tk, LANES), BF16)

    m_ref[...] = jnp.full(m_ref.shape, NEG_BIG, F32)
    acc_ref[...] = jnp.zeros(acc_ref.shape, F32)

    def kv_step(t, carry):
        off = pl.multiple_of(t * tk, tk)
        bias = bias_ref[0, :, pl.ds(off, tk)].astype(F32)
        for hp in range(A_HEADS // 2):
            kp = k_ref[0, pl.ds(off, tk), hp * LANES:(hp + 1) * LANES]
            v_ext = jnp.concatenate([v_ref[0, pl.ds(off, tk), hp * LANES:(hp + 1) * LANES], ones],
                                    axis=1)
            for half in range(2):
                h = 2 * hp + half
                _flash_update(h, _dot_nt(q_halves[h], kp) + bias, v_ext, m_ref, acc_ref)
        return carry

    lax.fori_loop(0, n_t, kv_step, 0)

    for hp in range(A_HEADS // 2):
        a0 = acc_ref[2 * hp]
        a1 = acc_ref[2 * hp + 1]
        o = jnp.where(lo_q, a0[:, :LANES] / a0[:, LANES:], a1[:, :LANES] / a1[:, LANES:])
        o_ref[0, :, hp * LANES:(hp + 1) * LANES] = o.astype(o_ref.dtype)


def _dsa(aq, ak, av, bias, tq, tk):
    bsz, seq, w = aq.shape
    kern = functools.partial(_dsa_kernel, tq=tq, tk=tk)
    resident = lambda: pl.BlockSpec((1, seq, w), lambda b, i: (b, 0, 0),
                                    pipeline_mode=pl.Buffered(1))
    return pl.pallas_call(
        kern,
        grid=(bsz, seq // tq),
        in_specs=[pl.BlockSpec((1, tq, w), lambda b, i: (b, i, 0)),
                  resident(), resident(),
                  pl.BlockSpec((1, tq, seq), lambda b, i: (b, i, 0))],
        out_specs=pl.BlockSpec((1, tq, w), lambda b, i: (b, i, 0)),
        out_shape=jax.ShapeDtypeStruct((bsz, seq, w), BF16),
        scratch_shapes=[pltpu.VMEM((A_HEADS, tq, LANES), F32),
                        pltpu.VMEM((A_HEADS, tq, 2 * LANES), F32)],
        compiler_params=_cparams("parallel", "arbitrary"),
    )(aq, ak, av, bias)


def _diff_kernel(q_ref, k_ref, v_ref, lq1_ref, lk1_ref, lq2_ref, lk2_ref, g_ref, o_ref,
                 m_ref, acc_ref, *, tq, tk, lambda_init):
    i = pl.program_id(1)
    n_full = (i * tq + CHUNK) // tk
    q_halves = []
    for hb in range(B_HEADS):
        q_halves.extend(_split_lane_halves(q_ref[0, :, hb * LANES:(hb + 1) * LANES]))
    lam = (jnp.exp(jnp.sum(lq1_ref[...] * lk1_ref[...], keepdims=True))
           - jnp.exp(jnp.sum(lq2_ref[...] * lk2_ref[...], keepdims=True)) + lambda_init)

    ones = jnp.ones((tk, LANES), BF16)

    m_ref[...] = jnp.full(m_ref.shape, NEG_BIG, F32)
    acc_ref[...] = jnp.zeros(acc_ref.shape, F32)

    def tile(off, masked):
        if masked:
            row_chunk = (i * tq + lax.broadcasted_iota(jnp.int32, (tq, tk), 0)) // CHUNK
            col_chunk = (off + lax.broadcasted_iota(jnp.int32, (tq, tk), 1)) // CHUNK
            ok = col_chunk <= row_chunk
        for hb in range(B_HEADS):
            kp = k_ref[0, pl.ds(off, tk), hb * LANES:(hb + 1) * LANES]
            v_ext = jnp.concatenate([v_ref[0, pl.ds(off, tk), hb * LANES:(hb + 1) * LANES], ones],
                                    axis=1)
            for mp in range(2):
                j = 2 * hb + mp
                s = _dot_nt(q_halves[j], kp)
                if masked:
                    s = jnp.where(ok, s, NEG_BIG)
                _flash_update(j, s, v_ext, m_ref, acc_ref)

    def kv_step(t, carry):
        tile(pl.multiple_of(t * tk, tk), False)
        return carry

    lax.fori_loop(0, n_full, kv_step, 0)
    tile(pl.multiple_of(n_full * tk, tk), True)

    for hb in range(B_HEADS):
        a1 = acc_ref[2 * hb]
        a2 = acc_ref[2 * hb + 1]
        o = a1[:, :LANES] / a1[:, LANES:] - lam * (a2[:, :LANES] / a2[:, LANES:])
        o = o * lax.rsqrt(jnp.mean(o * o, axis=-1, keepdims=True) + RMS_EPS)
        o = o * g_ref[...] * (1.0 - lambda_init)
        o_ref[0, :, hb * LANES:(hb + 1) * LANES] = o.astype(o_ref.dtype)


def _diff(bq, bk, bv, lq1, lk1, lq2, lk2, subln_g, lambda_init, tq, tk):
    bsz, seq, w = bq.shape
    kern = functools.partial(_diff_kernel, tq=tq, tk=tk, lambda_init=lambda_init)
    resident = lambda: pl.BlockSpec((1, seq, w), lambda b, i: (b, 0, 0),
                                    pipeline_mode=pl.Buffered(1))
    small = lambda a: pl.BlockSpec(a.shape, lambda b, i: (0, 0))
    return pl.pallas_call(
        kern,
        grid=(bsz, seq // tq),
        in_specs=[pl.BlockSpec((1, tq, w), lambda b, i: (b, i, 0)),
                  resident(), resident(),
                  small(lq1), small(lk1), small(lq2), small(lk2), small(subln_g)],
        out_specs=pl.BlockSpec((1, tq, w), lambda b, i: (b, i, 0)),
        out_shape=jax.ShapeDtypeStruct((bsz, seq, w), BF16),
        scratch_shapes=[pltpu.VMEM((2 * B_HEADS, tq, LANES), F32),
                        pltpu.VMEM((2 * B_HEADS, tq, 2 * LANES), F32)],
        compiler_params=_cparams("parallel", "arbitrary"),
    )(bq, bk, bv, lq1, lk1, lq2, lk2, subln_g)


def _first_argmax_rows(v, row):
    m = jnp.max(v, axis=0, keepdims=True)
    idx = jnp.min(jnp.where(v == m, row, v.shape[0]), axis=0, keepdims=True)
    return m, idx


def _scatter_rows(dk, gk, r0, nrows):
    r_iota = (lax.broadcasted_iota(jnp.int32, (nrows, dk.shape[1]), 0) + r0).astype(F32)
    out = jnp.zeros(r_iota.shape, F32)
    for k in range(dk.shape[0]):
        out = jnp.where(r_iota == dk[k:k + 1, :], 1.0 if gk is None else gk[k:k + 1, :], out)
    return out


def _oproj_kernel(oa_ref, ob_ref, h0_ref, woa_ref, wob_ref, g_ref, b_ref, wrt_ref, rb_ref,
                  h1_ref, h1b_ref, dk_ref, gk_ref, cnt_ref, xs_ref, *, alpha, rb):
    mix = (jnp.dot(oa_ref[...], woa_ref[...], preferred_element_type=F32)
           + jnp.dot(ob_ref[...], wob_ref[...], preferred_element_type=F32))
    h1 = _layer_norm(alpha * h0_ref[...] + mix, g_ref[...], b_ref[...])
    h1_ref[...] = h1
    h1b = h1.astype(BF16)
    h1b_ref[...] = h1b

    logits = lax.dot_general(wrt_ref[...], h1, (((1,), (1,)), ((), ())),
                             precision=lax.Precision.HIGHEST, preferred_element_type=F32)
    scores = jax.nn.sigmoid(logits)
    biased = scores + rb_ref[...]
    tm = scores.shape[1]
    per_g = N_EXPERTS // N_GROUPS
    row8 = lax.broadcasted_iota(jnp.int32, (per_g, tm), 0)

    gs = []
    for g in range(N_GROUPS):
        blk = biased[g * per_g:(g + 1) * per_g, :]
        m1, i1 = _first_argmax_rows(blk, row8)
        m2 = jnp.max(jnp.where(row8 == i1, -jnp.inf, blk), axis=0, keepdims=True)
        gs.append(m1 + m2)
    gscore = jnp.concatenate(gs, axis=0)
    rowg = lax.broadcasted_iota(jnp.int32, (N_GROUPS, tm), 0)
    gsel = jnp.zeros((N_GROUPS, tm), F32)
    for _ in range(TOPK_GROUPS):
        _, ig = _first_argmax_rows(jnp.where(gsel > 0.0, -jnp.inf, gscore), rowg)
        gsel = jnp.where(rowg == ig, 1.0, gsel)

    rowe = lax.broadcasted_iota(jnp.int32, (N_EXPERTS, tm), 0)
    live = jnp.concatenate(
        [jnp.broadcast_to(gsel[g:g + 1, :], (per_g, tm)) for g in range(N_GROUPS)], axis=0)
    esel = jnp.zeros((N_EXPERTS, tm), F32)
    for _ in range(TOP_K):
        cand = jnp.where(live > 0.0, biased, -jnp.inf)
        m = jnp.max(cand, axis=0, keepdims=True)
        idx = jnp.min(jnp.where((live > 0.0) & (cand == m), rowe, N_EXPERTS),
                      axis=0, keepdims=True)
        hit = rowe == idx
        esel = jnp.where(hit, 1.0, esel)
        live = jnp.where(hit, 0.0, live)
    picked = jnp.where(esel > 0.0, scores, 0.0)
    denom = jnp.sum(picked, axis=0, keepdims=True)
    gate = picked / denom * ROUTED_SCALE

    sel_b = esel.astype(BF16)
    before_t = (lax.broadcasted_iota(jnp.int32, (tm, tm), 0)
                < lax.broadcasted_iota(jnp.int32, (tm, tm), 1)).astype(F32).astype(BF16)
    before_e = (lax.broadcasted_iota(jnp.int32, (N_EXPERTS, N_EXPERTS), 1)
                < lax.broadcasted_iota(jnp.int32, (N_EXPERTS, N_EXPERTS), 0)).astype(F32).astype(BF16)
    rank = jnp.dot(sel_b, before_t, preferred_element_type=F32)
    order = jnp.dot(before_e, sel_b, preferred_element_type=F32)
    cnt = jnp.sum(esel, axis=1, keepdims=True)
    cnt16 = jnp.floor((cnt + (ROW_ALIGN - 1)) / ROW_ALIGN) * ROW_ALIGN
    off = jnp.dot(before_e, jnp.broadcast_to(cnt16, (N_EXPERTS, tm)).astype(BF16),
                  preferred_element_type=F32)
    dest = off + rank
    dks, gks = [], []
    for k in range(TOP_K):
        kth = (esel > 0.0) & (order == k)
        dks.append(jnp.sum(jnp.where(kth, dest, 0.0), axis=0, keepdims=True))
        gks.append(jnp.sum(jnp.where(kth, gate, 0.0), axis=0, keepdims=True))
    dk = jnp.concatenate(dks, axis=0)
    dk_ref[...] = dk
    gk_ref[...] = jnp.concatenate(gks, axis=0)
    counts = _dot_nt(jnp.ones((8, tm), BF16), sel_b)
    cnt_ref[0] = jnp.concatenate([counts, jnp.zeros((8, LANES - N_EXPERTS), F32)], axis=1)

    rt = xs_ref.shape[1]
    for r0 in range(0, rt, rb):
        onehot = _scatter_rows(dk, None, r0, rb)
        rows = jnp.dot(onehot.astype(BF16), h1b, preferred_element_type=F32)
        xs_ref[0, r0:r0 + rb, :] = rows.astype(BF16)


def _oproj(oa, ob, h0, woa, wob, g, b, wrt, rb, alpha, tm, rt):
    n, d = h0.shape
    nt = n // tm
    row = lambda w: pl.BlockSpec((tm, w), lambda i: (i, 0))
    col = lambda: pl.BlockSpec((TOP_K, tm), lambda i: (0, i))
    full = lambda a: pl.BlockSpec(a.shape, lambda i: (0, 0))
    kt = jax.ShapeDtypeStruct((TOP_K, n), F32)
    return pl.pallas_call(
        functools.partial(_oproj_kernel, alpha=alpha, rb=512),
        grid=(nt,),
        in_specs=[row(512), row(512), row(d), full(woa), full(wob), full(g), full(b),
                  full(wrt), full(rb)],
        out_specs=(row(d), row(d), col(), col(),
                   pl.BlockSpec((1, 8, LANES), lambda i: (i, 0, 0)),
                   pl.BlockSpec((1, rt, d), lambda i: (i, 0, 0))),
        out_shape=(jax.ShapeDtypeStruct((n, d), F32),
                   jax.ShapeDtypeStruct((n, d), BF16),
                   kt,
                   kt,
                   jax.ShapeDtypeStruct((nt, 8, LANES), F32),
                   jax.ShapeDtypeStruct((nt, rt, d), BF16)),
        compiler_params=_cparams("parallel"),
    )(oa, ob, h0, woa, wob, g, b, wrt, rb)


def _swiglu(x, wg, wu, wd):
    hid = (jax.nn.silu(jnp.dot(x, wg, preferred_element_type=F32))
           * jnp.dot(x, wu, preferred_element_type=F32))
    return jnp.dot(hid.astype(BF16), wd, preferred_element_type=F32)


def _scatter_runs_kernel(tile_off_ref, exp_off_ref, len_ref, gap_off_ref, gap_len_ref,
                         xs_ref, xg_ref, zero_ref, sem):
    i = pl.program_id(0)
    rt = xs_ref.shape[1]

    def rows_wait(n):
        pltpu.make_async_copy(xg_ref.at[pl.ds(0, n), :], xg_ref.at[pl.ds(0, n), :], sem).wait()

    @pl.when(i == 0)
    def _():
        zero_ref[...] = jnp.zeros(zero_ref.shape, zero_ref.dtype)

        def issue_gap(g, rows):
            n = pl.multiple_of(gap_len_ref[g], ROW_ALIGN)

            @pl.when(n > 0)
            def _():
                pltpu.make_async_copy(
                    zero_ref.at[pl.ds(0, n), :],
                    xg_ref.at[pl.ds(pl.multiple_of(gap_off_ref[g], ROW_ALIGN), n), :], sem).start()
            return rows + n

        gap_rows = lax.fori_loop(0, gap_len_ref.shape[0], issue_gap, 0)

        @pl.when(gap_rows > 0)
        def _():
            rows_wait(pl.multiple_of(gap_rows, ROW_ALIGN))

    def issue_run(e, rows):
        r = i * N_EXPERTS + e
        n = pl.multiple_of(len_ref[r], ROW_ALIGN)

        @pl.when(n > 0)
        def _():
            pltpu.make_async_copy(
                xs_ref.at[0, pl.ds(pl.multiple_of(tile_off_ref[r], ROW_ALIGN), n), :],
                xg_ref.at[pl.ds(pl.multiple_of(exp_off_ref[r], ROW_ALIGN), n), :], sem).start()
        return rows + n

    run_rows = lax.fori_loop(0, N_EXPERTS, issue_run, 0)
    rows_wait(pl.multiple_of(run_rows, ROW_ALIGN))


def _scatter_runs(xs, tile_off, exp_off, lens, gap_off, gap_len, n_dst_rows):
    nt, rt, d = xs.shape
    return pl.pallas_call(
        _scatter_runs_kernel,
        grid_spec=pltpu.PrefetchScalarGridSpec(
            num_scalar_prefetch=5,
            grid=(nt,),
            in_specs=[pl.BlockSpec((1, rt, d), lambda i, *_: (i, 0, 0))],
            out_specs=pl.BlockSpec(memory_space=pl.ANY),
            scratch_shapes=[pltpu.VMEM((rt, d), xs.dtype), pltpu.SemaphoreType.DMA(())]),
        out_shape=jax.ShapeDtypeStruct((n_dst_rows, d), xs.dtype),
        compiler_params=_cparams("arbitrary"),
    )(tile_off, exp_off, lens, gap_off, gap_len, xs)


def _expert_kernel(blk_e_ref, blk_valid_ref, x_ref, wg_ref, wu_ref, wd_ref, y_ref):
    b = pl.program_id(0)
    valid = blk_valid_ref[b]

    @pl.when(valid > 0)
    def _():
        y_ref[...] = _swiglu(x_ref[...], wg_ref[...], wu_ref[...], wd_ref[...]).astype(BF16)

    @pl.when(valid <= 0)
    def _():
        y_ref[...] = jnp.zeros(y_ref.shape, BF16)


def _experts(xg, blk_e, blk_valid, wg, wu, wd):
    nr, d = xg.shape
    f = wg.shape[2]
    rows = pl.BlockSpec((MOE_BLOCK, d), lambda b, be, bv: (b, 0))
    return pl.pallas_call(
        _expert_kernel,
        grid_spec=pltpu.PrefetchScalarGridSpec(
            num_scalar_prefetch=2,
            grid=(nr // MOE_BLOCK,),
            in_specs=[rows,
                      pl.BlockSpec((None, d, f), lambda b, be, bv: (be[b], 0, 0)),
                      pl.BlockSpec((None, d, f), lambda b, be, bv: (be[b], 0, 0)),
                      pl.BlockSpec((None, f, d), lambda b, be, bv: (be[b], 0, 0))],
            out_specs=rows),
        out_shape=jax.ShapeDtypeStruct((nr, d), BF16),
        compiler_params=_cparams("parallel"),
    )(blk_e, blk_valid, xg, wg, wu, wd)


def _combine_kernel(tile_off_ref, exp_off_ref, len_ref, yg_ref, dk_ref, gk_ref, h1_ref, h1b_ref,
                    p_ref, sg_ref, su_ref, sd_ref, g_ref, b_ref, wg_ref, bg_ref, wp_ref, o_ref,
                    ys_ref, sem, *, alpha, rb):
    i = pl.program_id(0)
    rt = ys_ref.shape[0]

    @pl.when(i == 0)
    def _():
        ys_ref[...] = jnp.zeros(ys_ref.shape, ys_ref.dtype)

    def issue_run(e, rows):
        r = i * N_EXPERTS + e
        n = pl.multiple_of(len_ref[r], ROW_ALIGN)

        @pl.when(n > 0)
        def _():
            pltpu.make_async_copy(
                yg_ref.at[pl.ds(pl.multiple_of(exp_off_ref[r], ROW_ALIGN), n), :],
                ys_ref.at[pl.ds(pl.multiple_of(tile_off_ref[r], ROW_ALIGN), n), :], sem).start()
        return rows + n

    run_rows = pl.multiple_of(lax.fori_loop(0, N_EXPERTS, issue_run, 0), ROW_ALIGN)
    dk = dk_ref[...]
    gk = gk_ref[...]
    ffn = _swiglu(h1b_ref[...], sg_ref[...], su_ref[...], sd_ref[...])
    pltpu.make_async_copy(yg_ref.at[pl.ds(0, run_rows), :], ys_ref.at[pl.ds(0, run_rows), :],
                          sem).wait()
    for r0 in range(0, rt, rb):
        weights = _scatter_rows(dk, gk, r0, rb).astype(BF16)
        ffn = ffn + lax.dot_general(weights, ys_ref[r0:r0 + rb, :], (((0,), (0,)), ((), ())),
                                    preferred_element_type=F32)
    h2 = _layer_norm(alpha * h1_ref[...] + ffn, g_ref[...], b_ref[...])
    emb_gate = jax.nn.sigmoid(jnp.dot(h2.astype(BF16), wg_ref[...], preferred_element_type=F32)
                              + bg_ref[...])
    emb = jnp.dot(p_ref[...].astype(BF16), wp_ref[...], preferred_element_type=F32)
    o_ref[...] = h2 + emb_gate * emb


def _combine(tile_off, exp_off, lens, yg, dk, gk, h1, h1b, p2, sg, su, sd, g, b, wg, bg, wp,
             alpha, tm, rt):
    n, d = h1.shape
    row = lambda w: pl.BlockSpec((tm, w), lambda i, *_: (i, 0))
    col = lambda: pl.BlockSpec((TOP_K, tm), lambda i, *_: (0, i))
    full = lambda a: pl.BlockSpec(a.shape, lambda i, *_: (0, 0))
    return pl.pallas_call(
        functools.partial(_combine_kernel, alpha=alpha, rb=512),
        grid_spec=pltpu.PrefetchScalarGridSpec(
            num_scalar_prefetch=3,
            grid=(n // tm,),
            in_specs=[pl.BlockSpec(memory_space=pl.ANY), col(), col(), row(d),
                      row(d), row(p2.shape[1]), full(sg), full(su), full(sd), full(g), full(b),
                      full(wg), full(bg), full(wp)],
            out_specs=row(d),
            scratch_shapes=[pltpu.VMEM((rt, d), yg.dtype), pltpu.SemaphoreType.DMA(())]),
        out_shape=jax.ShapeDtypeStruct((n, d), F32),
        compiler_params=_cparams("arbitrary"),
    )(tile_off, exp_off, lens, yg, dk, gk, h1, h1b, p2, sg, su, sd, g, b, wg, bg, wp)


def _tiles(seq):
    tm = min(256, seq)
    tq_sel = min(128, seq)
    tk_sel = min(512, seq)
    tq_att = min(256, seq)
    tk_att = min(512, seq)
    rt = -(-(tm * TOP_K + N_EXPERTS * (ROW_ALIGN - 1)) // 512) * 512
    return tm, tq_sel, tk_sel, tq_att, tk_att, rt


def kernel(x, p, positions, ln_emb_g, ln_emb_b, w_in, w_o, diff_lq1, diff_lk1, diff_lq2, diff_lk2,
           diff_subln_g, ln1_g, ln1_b, w_router, router_bias, w_exp_gate, w_exp_up, w_exp_down,
           w_sh_gate, w_sh_up, w_sh_down, ln2_g, ln2_b, w_ple_gate, b_ple_gate, w_ple_proj):
    bsz, seq, d = x.shape
    depth = w_in.shape[0]
    n = bsz * seq
    n_keep = min(IDX_TOPK_MAX, seq // 4)
    alpha = (2.0 * depth) ** 0.25
    tm, tq_sel, tk_sel, tq_att, tk_att, rt = _tiles(seq)
    n_blocks = -(-((n // tm) * (tm * TOP_K + N_EXPERTS * (ROW_ALIGN - 1))
                   + N_EXPERTS * (MOE_BLOCK - 1)) // MOE_BLOCK)
    row1 = lambda v: v.reshape(1, -1)

    inv = ROPE_THETA ** (-jnp.arange(0, ROPE_DIM, 2, dtype=F32) / ROPE_DIM)
    inv_lanes = jnp.tile(inv, LANES // (ROPE_DIM // 2)).reshape(1, LANES)
    pos2 = positions.reshape(n, 1)

    h = x.reshape(n, d)
    out = None
    for li in range(depth):
        lambda_init = 0.8 - 0.6 * math.exp(-0.3 * li)
        w = w_in[li]
        aqw, akw, avw = w[:, 0:512], w[:, 512:1024], w[:, 1024:1536]
        iqw, ikw, iww = w[:, 1536:2048], w[:, 2048:2112], w[:, 2112:2120]
        bqw, bkw, bvw = w[:, 2120:2632], w[:, 2632:3144], w[:, 3144:3656]
        w_rope = jnp.concatenate([aqw, akw, iqw, ikw, ikw, bqw, bkw], axis=1).astype(BF16)
        w_plain = jnp.concatenate(
            [avw, bvw, iww, jnp.zeros((d, LANES - IDX_HEADS), F32)], axis=1).astype(BF16)

        (h0, aq, ak, iq, ik2, bq, bk, av, bv, iw) = _proj(
            h, pos2, inv_lanes, row1(ln_emb_g), row1(ln_emb_b), w_rope, w_plain, tm)
        assert depth == 1

        r3 = lambda a: a.reshape(bsz, seq, a.shape[-1])
        bias = _select(r3(iq), r3(iw), r3(ik2), n_keep, tq_sel, tk_sel)
        out_a = _dsa(r3(aq), r3(ak), r3(av), bias, tq_att, tk_att)
        out_b = _diff(r3(bq), r3(bk), r3(bv), row1(diff_lq1[li]), row1(diff_lk1[li]),
                      row1(diff_lq2[li]), row1(diff_lk2[li]), row1(diff_subln_g[li]),
                      lambda_init, tq_att, tk_att)

        wo = w_o[li].astype(BF16)
        h1, h1b, dk, gk, cnt_out, xs = _oproj(
            out_a.reshape(n, -1), out_b.reshape(n, -1), h0, wo[0:512], wo[512:1024],
            row1(ln1_g[li]), row1(ln1_b[li]), w_router[li].T, router_bias[li].reshape(-1, 1),
            alpha, tm, rt)

        nt = n // tm
        c16 = jnp.ceil(cnt_out[:, 0, :N_EXPERTS] / ROW_ALIGN) * ROW_ALIGN
        tri = lambda m: jnp.tril(jnp.ones((m, m), F32))
        hi = lax.Precision.HIGHEST
        tile_off = jnp.dot(c16, tri(N_EXPERTS).T, precision=hi) - c16
        cum_tiles = jnp.dot(tri(nt), c16, precision=hi)
        tot = cum_tiles[-1]
        totp = jnp.ceil(tot / MOE_BLOCK) * MOE_BLOCK
        pend = jnp.dot(totp, tri(N_EXPERTS).T, precision=hi)
        ebase = pend - totp
        exp_off = ebase[None, :] + cum_tiles - c16
        blk0 = jnp.arange(n_blocks, dtype=F32) * MOE_BLOCK
        blk_e = jnp.minimum(jnp.sum(pend[None, :] <= blk0[:, None], axis=1), N_EXPERTS - 1)
        onehot_e = (blk_e[:, None] == jnp.arange(N_EXPERTS)[None, :]).astype(F32)
        blk_end = jnp.dot(onehot_e, ebase + tot, precision=hi)
        blk_valid = jnp.clip(blk_end - blk0, 0, MOE_BLOCK)
        flat = lambda a: a.reshape(-1).astype(jnp.int32)
        n_rows = n_blocks * MOE_BLOCK
        tail0 = pend[-1] + jnp.arange(-(-n_rows // rt), dtype=F32) * rt
        gap_off = jnp.concatenate([ebase + tot, tail0])
        gap_len = jnp.concatenate([totp - tot, jnp.clip(n_rows - tail0, 0, rt)])

        weg, weu, wed = (w_exp_gate[li].astype(BF16), w_exp_up[li].astype(BF16),
                         w_exp_down[li].astype(BF16))
        xg = _scatter_runs(xs, flat(tile_off), flat(exp_off), flat(c16), flat(gap_off),
                           flat(gap_len), n_rows)
        yg = _experts(xg, flat(blk_e), flat(blk_valid), weg, weu, wed)
        out = _combine(flat(tile_off), flat(exp_off), flat(c16), yg, dk, gk, h1, h1b,
                       p[li].reshape(n, -1), w_sh_gate[li].astype(BF16),
                       w_sh_up[li].astype(BF16), w_sh_down[li].astype(BF16), row1(ln2_g[li]),
                       row1(ln2_b[li]), w_ple_gate[li].astype(BF16), row1(b_ple_gate[li]),
                       w_ple_proj[li].astype(BF16), alpha, tm, rt)
        h = out
    return out.reshape(bsz, seq, d)
```

```python
import functools
import math

import jax
import jax.numpy as jnp
from jax import lax
from jax.experimental import pallas as pl
from jax.experimental.pallas import tpu as pltpu

CHUNK = 64
HEAD_DIM = 64
ROPE_DIM = 64
ROPE_THETA = 10000.0
A_HEADS = 8
IDX_HEADS = 8
IDX_DIM = 64
IDX_TOPK_MAX = 256
B_HEADS = 4
B_QK_DIM = 64
B_V_DIM = 128
N_EXPERTS = 64
TOP_K = 8
N_GROUPS = 8
TOPK_GROUPS = 4
ROUTED_SCALE = 2.5
LN_EPS = 1e-5
RMS_EPS = 1e-5

LANES = 128
ROW_ALIGN = 16
MOE_BLOCK = 512
VMEM_LIMIT = 56 * 1024 * 1024
NEG_BIG = -1e30
INT_MIN = -(2 ** 31)
GROUPS = 256

F32 = jnp.float32
BF16 = jnp.bfloat16


def _cparams(*sem):
    return pltpu.CompilerParams(dimension_semantics=sem, vmem_limit_bytes=VMEM_LIMIT)


def _layer_norm(x, g, b):
    mu = jnp.mean(x, axis=-1, keepdims=True)
    xc = x - mu
    var = jnp.mean(xc * xc, axis=-1, keepdims=True)
    return xc * lax.rsqrt(var + LN_EPS) * g + b


def _dot_nt(a, b):
    return lax.dot_general(a, b, (((1,), (1,)), ((), ())), preferred_element_type=F32)


def _split_lane_halves(pair):
    lo = lax.broadcasted_iota(jnp.int32, pair.shape, 1) < (LANES // 2)
    pf = pair.astype(F32)
    return (jnp.where(lo, pf, 0.0).astype(pair.dtype), jnp.where(lo, 0.0, pf).astype(pair.dtype))


def _proj_kernel(x_ref, pos_ref, inv_ref, g_ref, b_ref, wr_ref, wp_ref, wi_ref,
                 h0_ref, aq_ref, ak_ref, iq_ref, ik_ref, bq_ref, bk_ref,
                 av_ref, bv_ref, iw_ref):
    hn = _layer_norm(x_ref[...], g_ref[...], b_ref[...])
    h0_ref[...] = hn
    hb = hn.astype(BF16)

    ang = pos_ref[...].astype(F32) * inv_ref[...]
    cos = jnp.cos(ang)
    sin = jnp.sin(ang)
    lane = lax.broadcasted_iota(jnp.int32, ang.shape, 1)
    first_half = (lane % ROPE_DIM) < (ROPE_DIM // 2)
    sin_signed = jnp.where(first_half, -sin, sin)

    def rope(v):
        partner = jnp.where(first_half,
                            pltpu.roll(v, LANES - ROPE_DIM // 2, axis=1),
                            pltpu.roll(v, ROPE_DIM // 2, axis=1))
        return v * cos + partner * sin_signed

    pr = jnp.dot(hb, wr_ref[...], preferred_element_type=F32)
    outs = ((aq_ref, 0, 4, HEAD_DIM ** -0.5), (ak_ref, 4, 4, 1.0), (iq_ref, 8, 4, 1.0),
            (ik_ref, 12, 1, 1.0), (bq_ref, 13, 4, B_QK_DIM ** -0.5), (bk_ref, 17, 4, 1.0))
    for ref, g0, ng, scale in outs:
        for j in range(ng):
            v = rope(pr[:, (g0 + j) * LANES:(g0 + j + 1) * LANES])
            if scale != 1.0:
                v = v * scale
            ref[:, j * LANES:(j + 1) * LANES] = v.astype(ref.dtype)

    pp = jnp.dot(hb, wp_ref[...], preferred_element_type=F32)
    av_ref[...] = pp[:, 0:512].astype(BF16)
    bv_ref[...] = pp[:, 512:1024].astype(BF16)
    iw_ref[...] = _dot_nt(wi_ref[...], hb)[0:IDX_HEADS, :]


def _proj(x2, pos2, inv, g, b, w_rope, w_plain, w_idx_t, tm):
    n, d = x2.shape
    nr = w_rope.shape[1]
    npl = w_plain.shape[1]
    row = lambda w: pl.BlockSpec((tm, w), lambda i: (i, 0))
    full = lambda a: pl.BlockSpec(a.shape, lambda i: (0, 0))
    out_shape = (
        jax.ShapeDtypeStruct((n, d), F32),
        jax.ShapeDtypeStruct((n, 512), BF16),
        jax.ShapeDtypeStruct((n, 512), BF16),
        jax.ShapeDtypeStruct((n, 512), BF16),
        jax.ShapeDtypeStruct((n, 128), BF16),
        jax.ShapeDtypeStruct((n, 512), BF16),
        jax.ShapeDtypeStruct((n, 512), BF16),
        jax.ShapeDtypeStruct((n, 512), BF16),
        jax.ShapeDtypeStruct((n, 512), BF16),
        jax.ShapeDtypeStruct((IDX_HEADS, n), F32),
    )
    return pl.pallas_call(
        _proj_kernel,
        grid=(n // tm,),
        in_specs=[row(d), row(1), full(inv), full(g), full(b), full(w_rope), full(w_plain),
                  full(w_idx_t)],
        out_specs=(row(d), row(512), row(512), row(512), row(128), row(512), row(512),
                   row(512), row(512), pl.BlockSpec((IDX_HEADS, tm), lambda i: (0, i))),
        out_shape=out_shape,
        compiler_params=_cparams("parallel"),
    )(x2, pos2, inv, g, b, w_rope, w_plain, w_idx_t)


def _sortable(s):
    i = lax.bitcast_convert_type(s, jnp.int32)
    return i ^ ((i >> 31) & 0x7FFFFFFF)


def _select_kernel(iq_ref, iw_ref, ik_ref, bias_ref, keys_ref, gmax_ref, *, tq, tk, n_keep,
                   w_scale):
    i = pl.program_id(1)
    seq = ik_ref.shape[1]
    n_t = ((i + 1) * tq + tk - 1) // tk

    q_chunk = (i * tq + lax.broadcasted_iota(jnp.int32, (tk, tq), 1)) // CHUNK
    k_row = lax.broadcasted_iota(jnp.int32, (tk, tq), 0)
    wb = iw_ref[...] * w_scale
    q_halves = []
    for hp in range(IDX_HEADS // 2):
        q_halves.extend(_split_lane_halves(iq_ref[0, :, hp * LANES:(hp + 1) * LANES]))

    def score_tile(t, carry):
        off = pl.multiple_of(t * tk, tk)
        kt = ik_ref[0, pl.ds(off, tk), :]
        sc = jnp.zeros((tk, tq), F32)
        for h in range(IDX_HEADS):
            rel = jnp.maximum(_dot_nt(kt, q_halves[h]), 0.0)
            sc = sc + wb[h:h + 1, :] * rel
        key = jnp.where(((off + k_row) // CHUNK) <= q_chunk, _sortable(sc + 0.0), INT_MIN)
        keys_ref[pl.ds(off, tk), :] = key
        gm = key[0:GROUPS, :]
        for g0 in range(GROUPS, tk, GROUPS):
            gm = jnp.maximum(gm, key[g0:g0 + GROUPS, :])
        gmax_ref[...] = jnp.maximum(gmax_ref[...], gm)
        return carry

    gmax_ref[...] = jnp.full(gmax_ref.shape, INT_MIN, jnp.int32)
    lax.fori_loop(0, n_t, score_tile, 0)

    def count_ge(cand):
        def body(t, acc):
            off = pl.multiple_of(t * tk, tk)
            m = (keys_ref[pl.ds(off, tk), :] >= cand).astype(jnp.int32)
            return acc + jnp.sum(m.reshape(tk // 8, 8, tq), axis=0)
        acc = lax.fori_loop(0, n_t, body, jnp.zeros((8, tq), jnp.int32))
        return jnp.sum(acc, axis=0, keepdims=True)

    gmax = gmax_ref[...]
    n_adm = ((i * tq + lax.broadcasted_iota(jnp.int32, (1, tq), 1)) // CHUNK + 1) * CHUNK
    few = n_adm <= n_keep
    lo0 = jnp.where(few, INT_MIN + 1, jnp.min(gmax, axis=0, keepdims=True))
    hi0 = jnp.where(few, INT_MIN + 2, jnp.max(gmax, axis=0, keepdims=True) + 1)

    def probe(state):
        lo, hi, thr, done, _, it = state
        mid = (lo >> 1) + (hi >> 1) + (lo & hi & 1)
        closed = mid == lo
        cnt = count_ge(mid)
        exact = cnt == n_keep
        up = cnt >= n_keep
        live = done == 0
        thr = jnp.where(live & exact, mid, jnp.where(live & closed, lo, thr))
        lo = jnp.where(live & up, mid, lo)
        hi = jnp.where(live & jnp.logical_not(up), mid, hi)
        done = jnp.where(exact | closed, 1, done)
        return lo, hi, thr, done, jnp.min(done), it + 1

    def searching(state):
        return (state[4] == 0) & (state[5] < 34)

    done0 = few.astype(jnp.int32)
    state = lax.while_loop(
        searching, probe, (lo0, hi0, lo0, done0, jnp.min(done0), jnp.int32(0)))
    thr = state[2]

    n_ge = count_ge(thr)
    tied = n_ge > n_keep
    any_tied = jnp.max(tied.astype(jnp.int32)) > 0
    tau = thr

    def write_bias(off, keep_t):
        bias_t = jnp.where(keep_t, 0.0, NEG_BIG)
        bias_ref[0, :, pl.ds(off, tk)] = bias_t.T.astype(BF16)

    @pl.when(jnp.logical_not(any_tied))
    def _():
        def write_tile(t, carry):
            off = pl.multiple_of(t * tk, tk)
            write_bias(off, keys_ref[pl.ds(off, tk), :] >= tau)
            return carry
        lax.fori_loop(0, n_t, write_tile, 0)

    @pl.when(any_tied)
    def _():
        room = (n_keep - count_ge(tau + 1)).astype(F32)
        upto = (lax.broadcasted_iota(jnp.int32, (tk, tk), 1)
                <= lax.broadcasted_iota(jnp.int32, (tk, tk), 0)).astype(F32).astype(BF16)

        def write_tile(t, seen_eq):
            off = pl.multiple_of(t * tk, tk)
            key = keys_ref[pl.ds(off, tk), :]
            eq = (key == tau) & tied
            eq_f = jnp.where(eq, 1.0, 0.0)
            eq_rank = seen_eq + jnp.dot(upto, eq_f.astype(BF16), preferred_element_type=F32)
            write_bias(off, (key > tau) | (eq & (eq_rank <= room))
                       | ((key == tau) & jnp.logical_not(tied)))
            return seen_eq + jnp.sum(eq_f, axis=0, keepdims=True)

        lax.fori_loop(0, n_t, write_tile, jnp.zeros((1, tq), F32))

    def blank_tile(t, carry):
        off = pl.multiple_of(t * tk, tk)
        bias_ref[0, :, pl.ds(off, tk)] = jnp.full((tq, tk), NEG_BIG, BF16)
        return carry

    lax.fori_loop(n_t, seq // tk, blank_tile, 0)


def _select(iq, iw, ik2, n_keep, tq, tk):
    bsz, seq, _ = iq.shape
    kern = functools.partial(_select_kernel, tq=tq, tk=tk, n_keep=n_keep,
                             w_scale=(IDX_HEADS ** -0.5) * (IDX_DIM ** -0.5))
    return pl.pallas_call(
        kern,
        grid=(bsz, seq // tq),
        in_specs=[pl.BlockSpec((1, tq, 512), lambda b, i: (b, i, 0)),
                  pl.BlockSpec((IDX_HEADS, tq), lambda b, i: (0, b * (seq // tq) + i)),
                  pl.BlockSpec((1, seq, 128), lambda b, i: (b, 0, 0))],
        out_specs=pl.BlockSpec((1, tq, seq), lambda b, i: (b, i, 0)),
        out_shape=jax.ShapeDtypeStruct((bsz, seq, seq), BF16),
        scratch_shapes=[pltpu.VMEM((seq, tq), jnp.int32), pltpu.VMEM((GROUPS, tq), jnp.int32)],
        compiler_params=_cparams("parallel", "parallel"),
    )(iq, iw, ik2)


def _flash_update(j, s, v_ext, m_ref, acc_ref):
    tk = s.shape[1]
    m_old = m_ref[j]
    m_new = jnp.maximum(m_old, jnp.max(s, axis=1, keepdims=True))
    alpha = jnp.exp(m_old - m_new)
    p = jnp.exp(s - jnp.concatenate([m_new] * (tk // LANES), axis=1))
    m_ref[j] = m_new
    acc_ref[j] = (acc_ref[j] * jnp.concatenate([alpha, alpha], axis=1)
                  + jnp.dot(p.astype(BF16), v_ext, preferred_element_type=F32))


def _dsa_kernel(q_ref, k_ref, v_ref, bias_ref, o_ref, m_ref, acc_ref, *, tq, tk):
    i = pl.program_id(1)
    n_t = ((i + 1) * tq + tk - 1) // tk
    lo_q = lax.broadcasted_iota(jnp.int32, (tq, LANES), 1) < HEAD_DIM
    q_halves = []
    for hp in range(A_HEADS // 2):
        q_halves.extend(_split_lane_halves(q_ref[0, :, hp * LANES:(hp + 1) * LANES]))
    ones = jnp.ones((tk, LANES), BF16)

    m_ref[...] = jnp.full(m_ref.shape, NEG_BIG, F32)
    acc_ref[...] = jnp.zeros(acc_ref.shape, F32)

    def kv_step(t, carry):
        off = pl.multiple_of(t * tk, tk)
        bias = bias_ref[0, :, pl.ds(off, tk)].astype(F32)
        for hp in range(A_HEADS // 2):
            kp = k_ref[0, pl.ds(off, tk), hp * LANES:(hp + 1) * LANES]
            v_ext = jnp.concatenate([v_ref[0, pl.ds(off, tk), hp * LANES:(hp + 1) * LANES], ones],
                                    axis=1)
            for half in range(2):
                h = 2 * hp + half
                _flash_update(h, _dot_nt(q_halves[h], kp) + bias, v_ext, m_ref, acc_ref)
        return carry

    lax.fori_loop(0, n_t, kv_step, 0)

    for hp in range(A_HEADS // 2):
        a0 = acc_ref[2 * hp]
        a1 = acc_ref[2 * hp + 1]
        o = jnp.where(lo_q, a0[:, :LANES] / a0[:, LANES:], a1[:, :LANES] / a1[:, LANES:])
        o_ref[0, :, hp * LANES:(hp + 1) * LANES] = o.astype(o_ref.dtype)


def _dsa(aq, ak, av, bias, tq, tk):
    bsz, seq, w = aq.shape
    kern = functools.partial(_dsa_kernel, tq=tq, tk=tk)
    resident = lambda: pl.BlockSpec((1, seq, w), lambda b, i: (b, 0, 0),
                                    pipeline_mode=pl.Buffered(1))
    return pl.pallas_call(
        kern,
        grid=(bsz, seq // tq),
        in_specs=[pl.BlockSpec((1, tq, w), lambda b, i: (b, i, 0)),
                  resident(), resident(),
                  pl.BlockSpec((1, tq, seq), lambda b, i: (b, i, 0))],
        out_specs=pl.BlockSpec((1, tq, w), lambda b, i: (b, i, 0)),
        out_shape=jax.ShapeDtypeStruct((bsz, seq, w), BF16),
        scratch_shapes=[pltpu.VMEM((A_HEADS, tq, LANES), F32),
                        pltpu.VMEM((A_HEADS, tq, 2 * LANES), F32)],
        compiler_params=_cparams("parallel", "arbitrary"),
    )(aq, ak, av, bias)


def _diff_kernel(q_ref, k_ref, v_ref, lq1_ref, lk1_ref, lq2_ref, lk2_ref, g_ref, o_ref,
                 m_ref, acc_ref, *, tq, tk, lambda_init):
    i = pl.program_id(1)
    n_full = (i * tq + CHUNK) // tk
    q_halves = []
    for hb in range(B_HEADS):
        q_halves.extend(_split_lane_halves(q_ref[0, :, hb * LANES:(hb + 1) * LANES]))
    lam = (jnp.exp(jnp.sum(lq1_ref[...] * lk1_ref[...], keepdims=True))
           - jnp.exp(jnp.sum(lq2_ref[...] * lk2_ref[...], keepdims=True)) + lambda_init)

    ones = jnp.ones((tk, LANES), BF16)

    m_ref[...] = jnp.full(m_ref.shape, NEG_BIG, F32)
    acc_ref[...] = jnp.zeros(acc_ref.shape, F32)

    def tile(off, masked):
        if masked:
            row_chunk = (i * tq + lax.broadcasted_iota(jnp.int32, (tq, tk), 0)) // CHUNK
            col_chunk = (off + lax.broadcasted_iota(jnp.int32, (tq, tk), 1)) // CHUNK
            ok = col_chunk <= row_chunk
        for hb in range(B_HEADS):
            kp = k_ref[0, pl.ds(off, tk), hb * LANES:(hb + 1) * LANES]
            v_ext = jnp.concatenate([v_ref[0, pl.ds(off, tk), hb * LANES:(hb + 1) * LANES], ones],
                                    axis=1)
            for mp in range(2):
                j = 2 * hb + mp
                s = _dot_nt(q_halves[j], kp)
                if masked:
                    s = jnp.where(ok, s, NEG_BIG)
                _flash_update(j, s, v_ext, m_ref, acc_ref)

    def kv_step(t, carry):
        tile(pl.multiple_of(t * tk, tk), False)
        return carry

    lax.fori_loop(0, n_full, kv_step, 0)
    tile(pl.multiple_of(n_full * tk, tk), True)

    for hb in range(B_HEADS):
        a1 = acc_ref[2 * hb]
        a2 = acc_ref[2 * hb + 1]
        o = a1[:, :LANES] / a1[:, LANES:] - lam * (a2[:, :LANES] / a2[:, LANES:])
        o = o * lax.rsqrt(jnp.mean(o * o, axis=-1, keepdims=True) + RMS_EPS)
        o = o * g_ref[...] * (1.0 - lambda_init)
        o_ref[0, :, hb * LANES:(hb + 1) * LANES] = o.astype(o_ref.dtype)


def _diff(bq, bk, bv, lq1, lk1, lq2, lk2, subln_g, lambda_init, tq, tk):
    bsz, seq, w = bq.shape
    kern = functools.partial(_diff_kernel, tq=tq, tk=tk, lambda_init=lambda_init)
    resident = lambda: pl.BlockSpec((1, seq, w), lambda b, i: (b, 0, 0),
                                    pipeline_mode=pl.Buffered(1))
    small = lambda a: pl.BlockSpec(a.shape, lambda b, i: (0, 0))
    return pl.pallas_call(
        kern,
        grid=(bsz, seq // tq),
        in_specs=[pl.BlockSpec((1, tq, w), lambda b, i: (b, i, 0)),
                  resident(), resident(),
                  small(lq1), small(lk1), small(lq2), small(lk2), small(subln_g)],
        out_specs=pl.BlockSpec((1, tq, w), lambda b, i: (b, i, 0)),
        out_shape=jax.ShapeDtypeStruct((bsz, seq, w), BF16),
        scratch_shapes=[pltpu.VMEM((2 * B_HEADS, tq, LANES), F32),
                        pltpu.VMEM((2 * B_HEADS, tq, 2 * LANES), F32)],
        compiler_params=_cparams("parallel", "arbitrary"),
    )(bq, bk, bv, lq1, lk1, lq2, lk2, subln_g)


def _first_argmax_rows(v, row):
    m = jnp.max(v, axis=0, keepdims=True)
    idx = jnp.min(jnp.where(v == m, row, v.shape[0]), axis=0, keepdims=True)
    return m, idx


def _scatter_rows(dk, gk, r0, nrows):
    r_iota = (lax.broadcasted_iota(jnp.int32, (nrows, dk.shape[1]), 0) + r0).astype(F32)
    out = jnp.zeros(r_iota.shape, F32)
    for k in range(dk.shape[0]):
        out = jnp.where(r_iota == dk[k:k + 1, :], 1.0 if gk is None else gk[k:k + 1, :], out)
    return out


def _oproj_kernel(oa_ref, ob_ref, h0_ref, woa_ref, wob_ref, g_ref, b_ref, wrt_ref, rb_ref,
                  h1_ref, h1b_ref, dk_ref, gk_ref, cnt_ref, xs_ref, *, alpha, rb):
    mix = (jnp.dot(oa_ref[...], woa_ref[...], preferred_element_type=F32)
           + jnp.dot(ob_ref[...], wob_ref[...], preferred_element_type=F32))
    h1 = _layer_norm(alpha * h0_ref[...] + mix, g_ref[...], b_ref[...])
    h1_ref[...] = h1
    h1b = h1.astype(BF16)
    h1b_ref[...] = h1b

    logits = lax.dot_general(wrt_ref[...], h1, (((1,), (1,)), ((), ())),
                             precision=lax.Precision.HIGHEST, preferred_element_type=F32)
    scores = jax.nn.sigmoid(logits)
    biased = scores + rb_ref[...]
    tm = scores.shape[1]
    per_g = N_EXPERTS // N_GROUPS
    row8 = lax.broadcasted_iota(jnp.int32, (per_g, tm), 0)

    gs = []
    for g in range(N_GROUPS):
        blk = biased[g * per_g:(g + 1) * per_g, :]
        m1, i1 = _first_argmax_rows(blk, row8)
        m2 = jnp.max(jnp.where(row8 == i1, -jnp.inf, blk), axis=0, keepdims=True)
        gs.append(m1 + m2)
    gscore = jnp.concatenate(gs, axis=0)
    rowg = lax.broadcasted_iota(jnp.int32, (N_GROUPS, tm), 0)
    gsel = jnp.zeros((N_GROUPS, tm), F32)
    for _ in range(TOPK_GROUPS):
        _, ig = _first_argmax_rows(jnp.where(gsel > 0.0, -jnp.inf, gscore), rowg)
        gsel = jnp.where(rowg == ig, 1.0, gsel)

    rowe = lax.broadcasted_iota(jnp.int32, (N_EXPERTS, tm), 0)
    live = jnp.concatenate(
        [jnp.broadcast_to(gsel[g:g + 1, :], (per_g, tm)) for g in range(N_GROUPS)], axis=0)
    esel = jnp.zeros((N_EXPERTS, tm), F32)
    for _ in range(TOP_K):
        cand = jnp.where(live > 0.0, biased, -jnp.inf)
        m = jnp.max(cand, axis=0, keepdims=True)
        idx = jnp.min(jnp.where((live > 0.0) & (cand == m), rowe, N_EXPERTS),
                      axis=0, keepdims=True)
        hit = rowe == idx
        esel = jnp.where(hit, 1.0, esel)
        live = jnp.where(hit, 0.0, live)
    picked = jnp.where(esel > 0.0, scores, 0.0)
    denom = jnp.sum(picked, axis=0, keepdims=True)
    gate = picked / denom * ROUTED_SCALE

    sel_b = esel.astype(BF16)
    before_t = (lax.broadcasted_iota(jnp.int32, (tm, tm), 0)
                < lax.broadcasted_iota(jnp.int32, (tm, tm), 1)).astype(F32).astype(BF16)
    before_e = (lax.broadcasted_iota(jnp.int32, (N_EXPERTS, N_EXPERTS), 1)
                < lax.broadcasted_iota(jnp.int32, (N_EXPERTS, N_EXPERTS), 0)).astype(F32).astype(BF16)
    rank = jnp.dot(sel_b, before_t, preferred_element_type=F32)
    order = jnp.dot(before_e, sel_b, preferred_element_type=F32)
    cnt = jnp.sum(esel, axis=1, keepdims=True)
    cnt16 = jnp.floor((cnt + (ROW_ALIGN - 1)) / ROW_ALIGN) * ROW_ALIGN
    off = jnp.dot(before_e, jnp.broadcast_to(cnt16, (N_EXPERTS, tm)).astype(BF16),
                  preferred_element_type=F32)
    dest = off + rank
    dks, gks = [], []
    for k in range(TOP_K):
        kth = (esel > 0.0) & (order == k)
        dks.append(jnp.sum(jnp.where(kth, dest, 0.0), axis=0, keepdims=True))
        gks.append(jnp.sum(jnp.where(kth, gate, 0.0), axis=0, keepdims=True))
    dk = jnp.concatenate(dks, axis=0)
    dk_ref[...] = dk
    gk_ref[...] = jnp.concatenate(gks, axis=0)
    counts = _dot_nt(jnp.ones((8, tm), BF16), sel_b)
    cnt_ref[0] = jnp.concatenate([counts, jnp.zeros((8, LANES - N_EXPERTS), F32)], axis=1)

    rt = xs_ref.shape[1]
    for r0 in range(0, rt, rb):
        onehot = _scatter_rows(dk, None, r0, rb)
        rows = jnp.dot(onehot.astype(BF16), h1b, preferred_element_type=F32)
        xs_ref[0, r0:r0 + rb, :] = rows.astype(BF16)


def _oproj(oa, ob, h0, woa, wob, g, b, wrt, rb, alpha, tm, rt):
    n, d = h0.shape
    nt = n // tm
    row = lambda w: pl.BlockSpec((tm, w), lambda i: (i, 0))
    col = lambda: pl.BlockSpec((TOP_K, tm), lambda i: (0, i))
    full = lambda a: pl.BlockSpec(a.shape, lambda i: (0, 0))
    kt = jax.ShapeDtypeStruct((TOP_K, n), F32)
    return pl.pallas_call(
        functools.partial(_oproj_kernel, alpha=alpha, rb=512),
        grid=(nt,),
        in_specs=[row(512), row(512), row(d), full(woa), full(wob), full(g), full(b),
                  full(wrt), full(rb)],
        out_specs=(row(d), row(d), col(), col(),
                   pl.BlockSpec((1, 8, LANES), lambda i: (i, 0, 0)),
                   pl.BlockSpec((1, rt, d), lambda i: (i, 0, 0))),
        out_shape=(jax.ShapeDtypeStruct((n, d), F32),
                   jax.ShapeDtypeStruct((n, d), BF16),
                   kt,
                   kt,
                   jax.ShapeDtypeStruct((nt, 8, LANES), F32),
                   jax.ShapeDtypeStruct((nt, rt, d), BF16)),
        compiler_params=_cparams("parallel"),
    )(oa, ob, h0, woa, wob, g, b, wrt, rb)


def _swiglu(x, wg, wu, wd):
    hid = (jax.nn.silu(jnp.dot(x, wg, preferred_element_type=F32))
           * jnp.dot(x, wu, preferred_element_type=F32))
    return jnp.dot(hid.astype(BF16), wd, preferred_element_type=F32)


def _scatter_runs_kernel(tile_off_ref, exp_off_ref, len_ref, gap_off_ref, gap_len_ref,
                         xs_ref, xg_ref, zero_ref, sem):
    i = pl.program_id(0)
    rt = xs_ref.shape[1]

    def rows_wait(n):
        pltpu.make_async_copy(xg_ref.at[pl.ds(0, n), :], xg_ref.at[pl.ds(0, n), :], sem).wait()

    @pl.when(i == 0)
    def _():
        zero_ref[...] = jnp.zeros(zero_ref.shape, zero_ref.dtype)

        def issue_gap(g, rows):
            n = pl.multiple_of(gap_len_ref[g], ROW_ALIGN)

            @pl.when(n > 0)
            def _():
                pltpu.make_async_copy(
                    zero_ref.at[pl.ds(0, n), :],
                    xg_ref.at[pl.ds(pl.multiple_of(gap_off_ref[g], ROW_ALIGN), n), :], sem).start()
            return rows + n

        gap_rows = lax.fori_loop(0, gap_len_ref.shape[0], issue_gap, 0)

        @pl.when(gap_rows > 0)
        def _():
            rows_wait(pl.multiple_of(gap_rows, ROW_ALIGN))

    def issue_run(e, rows):
        r = i * N_EXPERTS + e
        n = pl.multiple_of(len_ref[r], ROW_ALIGN)

        @pl.when(n > 0)
        def _():
            pltpu.make_async_copy(
                xs_ref.at[0, pl.ds(pl.multiple_of(tile_off_ref[r], ROW_ALIGN), n), :],
                xg_ref.at[pl.ds(pl.multiple_of(exp_off_ref[r], ROW_ALIGN), n), :], sem).start()
        return rows + n

    run_rows = lax.fori_loop(0, N_EXPERTS, issue_run, 0)
    rows_wait(pl.multiple_of(run_rows, ROW_ALIGN))


def _scatter_runs(xs, tile_off, exp_off, lens, gap_off, gap_len, n_dst_rows):
    nt, rt, d = xs.shape
    return pl.pallas_call(
        _scatter_runs_kernel,
        grid_spec=pltpu.PrefetchScalarGridSpec(
            num_scalar_prefetch=5,
            grid=(nt,),
            in_specs=[pl.BlockSpec((1, rt, d), lambda i, *_: (i, 0, 0))],
            out_specs=pl.BlockSpec(memory_space=pl.ANY),
            scratch_shapes=[pltpu.VMEM((rt, d), xs.dtype), pltpu.SemaphoreType.DMA(())]),
        out_shape=jax.ShapeDtypeStruct((n_dst_rows, d), xs.dtype),
        compiler_params=_cparams("arbitrary"),
    )(tile_off, exp_off, lens, gap_off, gap_len, xs)


def _expert_kernel(blk_e_ref, blk_valid_ref, x_ref, wg_ref, wu_ref, wd_ref, y_ref):
    b = pl.program_id(0)
    valid = blk_valid_ref[b]

    @pl.when(valid > 0)
    def _():
        y_ref[...] = _swiglu(x_ref[...], wg_ref[...], wu_ref[...], wd_ref[...]).astype(BF16)

    @pl.when(valid <= 0)
    def _():
        y_ref[...] = jnp.zeros(y_ref.shape, BF16)


def _experts(xg, blk_e, blk_valid, wg, wu, wd):
    nr, d = xg.shape
    f = wg.shape[2]
    rows = pl.BlockSpec((MOE_BLOCK, d), lambda b, be, bv: (b, 0))
    return pl.pallas_call(
        _expert_kernel,
        grid_spec=pltpu.PrefetchScalarGridSpec(
            num_scalar_prefetch=2,
            grid=(nr // MOE_BLOCK,),
            in_specs=[rows,
                      pl.BlockSpec((None, d, f), lambda b, be, bv: (be[b], 0, 0)),
                      pl.BlockSpec((None, d, f), lambda b, be, bv: (be[b], 0, 0)),
                      pl.BlockSpec((None, f, d), lambda b, be, bv: (be[b], 0, 0))],
            out_specs=rows),
        out_shape=jax.ShapeDtypeStruct((nr, d), BF16),
        compiler_params=_cparams("parallel"),
    )(blk_e, blk_valid, xg, wg, wu, wd)


def _combine_kernel(tile_off_ref, exp_off_ref, len_ref, yg_ref, dk_ref, gk_ref, h1_ref, h1b_ref,
                    p_ref, sg_ref, su_ref, sd_ref, g_ref, b_ref, wg_ref, bg_ref, wp_ref, o_ref,
                    ys_ref, sem, *, alpha, rb):
    i = pl.program_id(0)
    rt = ys_ref.shape[0]

    @pl.when(i == 0)
    def _():
        ys_ref[...] = jnp.zeros(ys_ref.shape, ys_ref.dtype)

    def issue_run(e, rows):
        r = i * N_EXPERTS + e
        n = pl.multiple_of(len_ref[r], ROW_ALIGN)

        @pl.when(n > 0)
        def _():
            pltpu.make_async_copy(
                yg_ref.at[pl.ds(pl.multiple_of(exp_off_ref[r], ROW_ALIGN), n), :],
                ys_ref.at[pl.ds(pl.multiple_of(tile_off_ref[r], ROW_ALIGN), n), :], sem).start()
        return rows + n

    run_rows = pl.multiple_of(lax.fori_loop(0, N_EXPERTS, issue_run, 0), ROW_ALIGN)
    dk = dk_ref[...]
    gk = gk_ref[...]
    ffn = _swiglu(h1b_ref[...], sg_ref[...], su_ref[...], sd_ref[...])
    pltpu.make_async_copy(yg_ref.at[pl.ds(0, run_rows), :], ys_ref.at[pl.ds(0, run_rows), :],
                          sem).wait()
    for r0 in range(0, rt, rb):
        weights = _scatter_rows(dk, gk, r0, rb).astype(BF16)
        ffn = ffn + lax.dot_general(weights, ys_ref[r0:r0 + rb, :], (((0,), (0,)), ((), ())),
                                    preferred_element_type=F32)
    h2 = _layer_norm(alpha * h1_ref[...] + ffn, g_ref[...], b_ref[...])
    emb_gate = jax.nn.sigmoid(jnp.dot(h2.astype(BF16), wg_ref[...], preferred_element_type=F32)
                              + bg_ref[...])
    emb = jnp.dot(p_ref[...].astype(BF16), wp_ref[...], preferred_element_type=F32)
    o_ref[...] = h2 + emb_gate * emb


def _combine(tile_off, exp_off, lens, yg, dk, gk, h1, h1b, p2, sg, su, sd, g, b, wg, bg, wp,
             alpha, tm, rt):
    n, d = h1.shape
    row = lambda w: pl.BlockSpec((tm, w), lambda i, *_: (i, 0))
    col = lambda: pl.BlockSpec((TOP_K, tm), lambda i, *_: (0, i))
    full = lambda a: pl.BlockSpec(a.shape, lambda i, *_: (0, 0))
    return pl.pallas_call(
        functools.partial(_combine_kernel, alpha=alpha, rb=512),
        grid_spec=pltpu.PrefetchScalarGridSpec(
            num_scalar_prefetch=3,
            grid=(n // tm,),
            in_specs=[pl.BlockSpec(memory_space=pl.ANY), col(), col(), row(d),
                      row(d), row(p2.shape[1]), full(sg), full(su), full(sd), full(g), full(b),
                      full(wg), full(bg), full(wp)],
            out_specs=row(d),
            scratch_shapes=[pltpu.VMEM((rt, d), yg.dtype), pltpu.SemaphoreType.DMA(())]),
        out_shape=jax.ShapeDtypeStruct((n, d), F32),
        compiler_params=_cparams("arbitrary"),
    )(tile_off, exp_off, lens, yg, dk, gk, h1, h1b, p2, sg, su, sd, g, b, wg, bg, wp)


def _tiles(seq):
    tm = min(256, seq)
    tq_sel = min(256, seq)
    tk_sel = min(512, seq)
    tq_att = min(256, seq)
    tk_att = min(512, seq)
    rt = -(-(tm * TOP_K + N_EXPERTS * (ROW_ALIGN - 1)) // 512) * 512
    return tm, tq_sel, tk_sel, tq_att, tk_att, rt


def kernel(x, p, positions, ln_emb_g, ln_emb_b, w_in, w_o, diff_lq1, diff_lk1, diff_lq2, diff_lk2,
           diff_subln_g, ln1_g, ln1_b, w_router, router_bias, w_exp_gate, w_exp_up, w_exp_down,
           w_sh_gate, w_sh_up, w_sh_down, ln2_g, ln2_b, w_ple_gate, b_ple_gate, w_ple_proj):
    bsz, seq, d = x.shape
    depth = w_in.shape[0]
    n = bsz * seq
    n_keep = min(IDX_TOPK_MAX, seq // 4)
    alpha = (2.0 * depth) ** 0.25
    tm, tq_sel, tk_sel, tq_att, tk_att, rt = _tiles(seq)
    n_blocks = -(-((n // tm) * (tm * TOP_K + N_EXPERTS * (ROW_ALIGN - 1))
                   + N_EXPERTS * (MOE_BLOCK - 1)) // MOE_BLOCK)
    row1 = lambda v: v.reshape(1, -1)

    inv = ROPE_THETA ** (-jnp.arange(0, ROPE_DIM, 2, dtype=F32) / ROPE_DIM)
    inv_lanes = jnp.tile(inv, LANES // (ROPE_DIM // 2)).reshape(1, LANES)
    pos2 = positions.reshape(n, 1)

    h = x.reshape(n, d)
    out = None
    for li in range(depth):
        lambda_init = 0.8 - 0.6 * math.exp(-0.3 * li)
        w = w_in[li]
        aqw, akw, avw = w[:, 0:512], w[:, 512:1024], w[:, 1024:1536]
        iqw, ikw, iww = w[:, 1536:2048], w[:, 2048:2112], w[:, 2112:2120]
        bqw, bkw, bvw = w[:, 2120:2632], w[:, 2632:3144], w[:, 3144:3656]
        w_rope = jnp.concatenate([aqw, akw, iqw, ikw, ikw, bqw, bkw], axis=1).astype(BF16)
        w_plain = jnp.concatenate([avw, bvw], axis=1).astype(BF16)
        w_idx_t = jnp.concatenate(
            [iww.T, jnp.zeros((LANES - IDX_HEADS, d), F32)], axis=0).astype(BF16)

        (h0, aq, ak, iq, ik2, bq, bk, av, bv, iw) = _proj(
            h, pos2, inv_lanes, row1(ln_emb_g), row1(ln_emb_b), w_rope, w_plain, w_idx_t, tm)
        assert depth == 1

        r3 = lambda a: a.reshape(bsz, seq, a.shape[-1])
        bias = _select(r3(iq), iw, r3(ik2), n_keep, tq_sel, tk_sel)
        out_a = _dsa(r3(aq), r3(ak), r3(av), bias, tq_att, tk_att)
        out_b = _diff(r3(bq), r3(bk), r3(bv), row1(diff_lq1[li]), row1(diff_lk1[li]),
                      row1(diff_lq2[li]), row1(diff_lk2[li]), row1(diff_subln_g[li]),
                      lambda_init, tq_att, tk_att)

        wo = w_o[li].astype(BF16)
        h1, h1b, dk, gk, cnt_out, xs = _oproj(
            out_a.reshape(n, -1), out_b.reshape(n, -1), h0, wo[0:512], wo[512:1024],
            row1(ln1_g[li]), row1(ln1_b[li]), w_router[li].T, router_bias[li].reshape(-1, 1),
            alpha, tm, rt)

        nt = n // tm
        c16 = jnp.ceil(cnt_out[:, 0, :N_EXPERTS] / ROW_ALIGN) * ROW_ALIGN
        tri = lambda m: jnp.tril(jnp.ones((m, m), F32))
        hi = lax.Precision.HIGHEST
        tile_off = jnp.dot(c16, tri(N_EXPERTS).T, precision=hi) - c16
        cum_tiles = jnp.dot(tri(nt), c16, precision=hi)
        tot = cum_tiles[-1]
        totp = jnp.ceil(tot / MOE_BLOCK) * MOE_BLOCK
        pend = jnp.dot(totp, tri(N_EXPERTS).T, precision=hi)
        ebase = pend - totp
        exp_off = ebase[None, :] + cum_tiles - c16
        blk0 = jnp.arange(n_blocks, dtype=F32) * MOE_BLOCK
        blk_e = jnp.minimum(jnp.sum(pend[None, :] <= blk0[:, None], axis=1), N_EXPERTS - 1)
        onehot_e = (blk_e[:, None] == jnp.arange(N_EXPERTS)[None, :]).astype(F32)
        blk_end = jnp.dot(onehot_e, ebase + tot, precision=hi)
        blk_valid = jnp.clip(blk_end - blk0, 0, MOE_BLOCK)
        flat = lambda a: a.reshape(-1).astype(jnp.int32)
        n_rows = n_blocks * MOE_BLOCK
        tail0 = pend[-1] + jnp.arange(-(-n_rows // rt), dtype=F32) * rt
        gap_off = jnp.concatenate([ebase + tot, tail0])
        gap_len = jnp.concatenate([totp - tot, jnp.clip(n_rows - tail0, 0, rt)])

        weg, weu, wed = (w_exp_gate[li].astype(BF16), w_exp_up[li].astype(BF16),
                         w_exp_down[li].astype(BF16))
        xg = _scatter_runs(xs, flat(tile_off), flat(exp_off), flat(c16), flat(gap_off),
                           flat(gap_len), n_rows)
        yg = _experts(xg, flat(blk_e), flat(blk_valid), weg, weu, wed)
        out = _combine(flat(tile_off), flat(exp_off), flat(c16), yg, dk, gk, h1, h1b,
                       p[li].reshape(n, -1), w_sh_gate[li].astype(BF16),
                       w_sh_up[li].astype(BF16), w_sh_down[li].astype(BF16), row1(ln2_g[li]),
                       row1(ln2_b[li]), w_ple_gate[li].astype(BF16), row1(b_ple_gate[li]),
                       w_ple_proj[li].astype(BF16), alpha, tm, rt)
        h = out
    return out.reshape(bsz, seq, d)
```

```python
import functools
import math

import jax
import jax.numpy as jnp
from jax import lax
from jax.experimental import pallas as pl
from jax.experimental.pallas import tpu as pltpu

CHUNK = 64
HEAD_DIM = 64
ROPE_DIM = 64
ROPE_THETA = 10000.0
A_HEADS = 8
IDX_HEADS = 8
IDX_DIM = 64
IDX_TOPK_MAX = 256
B_HEADS = 4
B_QK_DIM = 64
B_V_DIM = 128
N_EXPERTS = 64
TOP_K = 8
N_GROUPS = 8
TOPK_GROUPS = 4
ROUTED_SCALE = 2.5
LN_EPS = 1e-5
RMS_EPS = 1e-5

LANES = 128
ROW_ALIGN = 16
MOE_BLOCK = 512
VMEM_LIMIT = 56 * 1024 * 1024
NEG_BIG = -1e30
INT_MIN = -(2 ** 31)
GROUPS = 256

F32 = jnp.float32
BF16 = jnp.bfloat16


def _cparams(*sem):
    return pltpu.CompilerParams(dimension_semantics=sem, vmem_limit_bytes=VMEM_LIMIT)


def _layer_norm(x, g, b):
    mu = jnp.mean(x, axis=-1, keepdims=True)
    xc = x - mu
    var = jnp.mean(xc * xc, axis=-1, keepdims=True)
    return xc * lax.rsqrt(var + LN_EPS) * g + b


def _dot_nt(a, b):
    return lax.dot_general(a, b, (((1,), (1,)), ((), ())), preferred_element_type=F32)


def _split_lane_halves(pair):
    lo = lax.broadcasted_iota(jnp.int32, pair.shape, 1) < (LANES // 2)
    pf = pair.astype(F32)
    return (jnp.where(lo, pf, 0.0).astype(pair.dtype), jnp.where(lo, 0.0, pf).astype(pair.dtype))


def _proj_kernel(x_ref, pos_ref, inv_ref, g_ref, b_ref, wr_ref, wp_ref, wi_ref,
                 h0_ref, aq_ref, ak_ref, iq_ref, ik_ref, bq_ref, bk_ref,
                 av_ref, bv_ref, iw_ref):
    hn = _layer_norm(x_ref[...], g_ref[...], b_ref[...])
    h0_ref[...] = hn
    hb = hn.astype(BF16)

    ang = pos_ref[...].astype(F32) * inv_ref[...]
    cos = jnp.cos(ang)
    sin = jnp.sin(ang)
    lane = lax.broadcasted_iota(jnp.int32, ang.shape, 1)
    first_half = (lane % ROPE_DIM) < (ROPE_DIM // 2)
    sin_signed = jnp.where(first_half, -sin, sin)

    def rope(v):
        partner = jnp.where(first_half,
                            pltpu.roll(v, LANES - ROPE_DIM // 2, axis=1),
                            pltpu.roll(v, ROPE_DIM // 2, axis=1))
        return v * cos + partner * sin_signed

    pr = jnp.dot(hb, wr_ref[...], preferred_element_type=F32)
    outs = ((aq_ref, 0, 4, HEAD_DIM ** -0.5), (ak_ref, 4, 4, 1.0), (iq_ref, 8, 4, 1.0),
            (ik_ref, 12, 1, 1.0), (bq_ref, 13, 4, B_QK_DIM ** -0.5), (bk_ref, 17, 4, 1.0))
    for ref, g0, ng, scale in outs:
        for j in range(ng):
            v = rope(pr[:, (g0 + j) * LANES:(g0 + j + 1) * LANES])
            if scale != 1.0:
                v = v * scale
            ref[:, j * LANES:(j + 1) * LANES] = v.astype(ref.dtype)

    pp = jnp.dot(hb, wp_ref[...], preferred_element_type=F32)
    av_ref[...] = pp[:, 0:512].astype(BF16)
    bv_ref[...] = pp[:, 512:1024].astype(BF16)
    iw_ref[...] = _dot_nt(wi_ref[...], hb)[0:IDX_HEADS, :]


def _proj(x2, pos2, inv, g, b, w_rope, w_plain, w_idx_t, tm):
    n, d = x2.shape
    nr = w_rope.shape[1]
    npl = w_plain.shape[1]
    row = lambda w: pl.BlockSpec((tm, w), lambda i: (i, 0))
    full = lambda a: pl.BlockSpec(a.shape, lambda i: (0, 0))
    out_shape = (
        jax.ShapeDtypeStruct((n, d), F32),
        jax.ShapeDtypeStruct((n, 512), BF16),
        jax.ShapeDtypeStruct((n, 512), BF16),
        jax.ShapeDtypeStruct((n, 512), BF16),
        jax.ShapeDtypeStruct((n, 128), BF16),
        jax.ShapeDtypeStruct((n, 512), BF16),
        jax.ShapeDtypeStruct((n, 512), BF16),
        jax.ShapeDtypeStruct((n, 512), BF16),
        jax.ShapeDtypeStruct((n, 512), BF16),
        jax.ShapeDtypeStruct((IDX_HEADS, n), F32),
    )
    return pl.pallas_call(
        _proj_kernel,
        grid=(n // tm,),
        in_specs=[row(d), row(1), full(inv), full(g), full(b), full(w_rope), full(w_plain),
                  full(w_idx_t)],
        out_specs=(row(d), row(512), row(512), row(512), row(128), row(512), row(512),
                   row(512), row(512), pl.BlockSpec((IDX_HEADS, tm), lambda i: (0, i))),
        out_shape=out_shape,
        compiler_params=_cparams("parallel"),
    )(x2, pos2, inv, g, b, w_rope, w_plain, w_idx_t)


def _sortable(s):
    i = lax.bitcast_convert_type(s, jnp.int32)
    return i ^ ((i >> 31) & 0x7FFFFFFF)


def _select_kernel(iq_ref, iw_ref, ik_ref, bias_ref, keys_ref, gmax_ref, *, tq, tk, n_keep,
                   w_scale):
    i = pl.program_id(1)
    seq = ik_ref.shape[1]
    n_t = ((i + 1) * tq + tk - 1) // tk

    q_chunk = (i * tq + lax.broadcasted_iota(jnp.int32, (tk, tq), 1)) // CHUNK
    k_row = lax.broadcasted_iota(jnp.int32, (tk, tq), 0)
    wb = iw_ref[...] * w_scale
    q_halves = []
    for hp in range(IDX_HEADS // 2):
        q_halves.extend(_split_lane_halves(iq_ref[0, :, hp * LANES:(hp + 1) * LANES]))

    def score_tile(t, carry):
        off = pl.multiple_of(t * tk, tk)
        kt = ik_ref[0, pl.ds(off, tk), :]
        sc = jnp.zeros((tk, tq), F32)
        for h in range(IDX_HEADS):
            rel = jnp.maximum(_dot_nt(kt, q_halves[h]), 0.0)
            sc = sc + wb[h:h + 1, :] * rel
        key = jnp.where(((off + k_row) // CHUNK) <= q_chunk, _sortable(sc + 0.0), INT_MIN)
        keys_ref[pl.ds(off, tk), :] = key
        gm = key[0:GROUPS, :]
        for g0 in range(GROUPS, tk, GROUPS):
            gm = jnp.maximum(gm, key[g0:g0 + GROUPS, :])
        gmax_ref[...] = jnp.maximum(gmax_ref[...], gm)
        return carry

    gmax_ref[...] = jnp.full(gmax_ref.shape, INT_MIN, jnp.int32)
    lax.fori_loop(0, n_t, score_tile, 0)

    def count_ge(cand):
        def body(t, acc):
            off = pl.multiple_of(t * tk, tk)
            m = (keys_ref[pl.ds(off, tk), :] >= cand).astype(jnp.int32)
            return acc + jnp.sum(m.reshape(tk // 8, 8, tq), axis=0)
        acc = lax.fori_loop(0, n_t, body, jnp.zeros((8, tq), jnp.int32))
        return jnp.sum(acc, axis=0, keepdims=True)

    gmax = gmax_ref[...]
    n_adm = ((i * tq + lax.broadcasted_iota(jnp.int32, (1, tq), 1)) // CHUNK + 1) * CHUNK
    few = n_adm <= n_keep
    lo0 = jnp.where(few, INT_MIN + 1, jnp.min(gmax, axis=0, keepdims=True))
    hi0 = jnp.where(few, INT_MIN + 2, jnp.max(gmax, axis=0, keepdims=True) + 1)

    def probe(state):
        lo, hi, thr, done, _, it = state
        mid = (lo >> 1) + (hi >> 1) + (lo & hi & 1)
        closed = mid == lo
        cnt = count_ge(mid)
        exact = cnt == n_keep
        up = cnt >= n_keep
        live = done == 0
        thr = jnp.where(live & exact, mid, jnp.where(live & closed, lo, thr))
        lo = jnp.where(live & up, mid, lo)
        hi = jnp.where(live & jnp.logical_not(up), mid, hi)
        done = jnp.where(exact | closed, 1, done)
        return lo, hi, thr, done, jnp.min(done), it + 1

    def searching(state):
        return (state[4] == 0) & (state[5] < 34)

    done0 = few.astype(jnp.int32)
    state = lax.while_loop(
        searching, probe, (lo0, hi0, lo0, done0, jnp.min(done0), jnp.int32(0)))
    thr = state[2]

    n_ge = count_ge(thr)
    tied = n_ge > n_keep
    any_tied = jnp.max(tied.astype(jnp.int32)) > 0
    tau = thr

    def write_bias(off, keep_t):
        bias_t = jnp.where(keep_t, 0.0, NEG_BIG)
        bias_ref[0, :, pl.ds(off, tk)] = bias_t.T.astype(BF16)

    @pl.when(jnp.logical_not(any_tied))
    def _():
        def write_tile(t, carry):
            off = pl.multiple_of(t * tk, tk)
            write_bias(off, keys_ref[pl.ds(off, tk), :] >= tau)
            return carry
        lax.fori_loop(0, n_t, write_tile, 0)

    @pl.when(any_tied)
    def _():
        room = (n_keep - count_ge(tau + 1)).astype(F32)
        upto = (lax.broadcasted_iota(jnp.int32, (tk, tk), 1)
                <= lax.broadcasted_iota(jnp.int32, (tk, tk), 0)).astype(F32).astype(BF16)

        def write_tile(t, seen_eq):
            off = pl.multiple_of(t * tk, tk)
            key = keys_ref[pl.ds(off, tk), :]
            eq = (key == tau) & tied
            eq_f = jnp.where(eq, 1.0, 0.0)
            eq_rank = seen_eq + jnp.dot(upto, eq_f.astype(BF16), preferred_element_type=F32)
            write_bias(off, (key > tau) | (eq & (eq_rank <= room))
                       | ((key == tau) & jnp.logical_not(tied)))
            return seen_eq + jnp.sum(eq_f, axis=0, keepdims=True)

        lax.fori_loop(0, n_t, write_tile, jnp.zeros((1, tq), F32))

    def blank_tile(t, carry):
        off = pl.multiple_of(t * tk, tk)
        bias_ref[0, :, pl.ds(off, tk)] = jnp.full((tq, tk), NEG_BIG, BF16)
        return carry

    lax.fori_loop(n_t, seq // tk, blank_tile, 0)


def _select(iq, iw, ik2, n_keep, tq, tk):
    bsz, seq, _ = iq.shape
    kern = functools.partial(_select_kernel, tq=tq, tk=tk, n_keep=n_keep,
                             w_scale=(IDX_HEADS ** -0.5) * (IDX_DIM ** -0.5))
    return pl.pallas_call(
        kern,
        grid=(bsz, seq // tq),
        in_specs=[pl.BlockSpec((1, tq, 512), lambda b, i: (b, i, 0)),
                  pl.BlockSpec((IDX_HEADS, tq), lambda b, i: (0, b * (seq // tq) + i)),
                  pl.BlockSpec((1, seq, 128), lambda b, i: (b, 0, 0))],
        out_specs=pl.BlockSpec((1, tq, seq), lambda b, i: (b, i, 0)),
        out_shape=jax.ShapeDtypeStruct((bsz, seq, seq), BF16),
        scratch_shapes=[pltpu.VMEM((seq, tq), jnp.int32), pltpu.VMEM((GROUPS, tq), jnp.int32)],
        compiler_params=_cparams("parallel", "parallel"),
    )(iq, iw, ik2)


def _flash_update(j, s, v_ext, m_ref, acc_ref):
    tk = s.shape[1]
    m_old = m_ref[j]
    m_new = jnp.maximum(m_old, jnp.max(s, axis=1, keepdims=True))
    alpha = jnp.exp(m_old - m_new)
    p = jnp.exp(s - jnp.concatenate([m_new] * (tk // LANES), axis=1))
    m_ref[j] = m_new
    acc_ref[j] = (acc_ref[j] * jnp.concatenate([alpha, alpha], axis=1)
                  + jnp.dot(p.astype(BF16), v_ext, preferred_element_type=F32))


def _dsa_kernel(q_ref, k_ref, v_ref, bias_ref, o_ref, m_ref, acc_ref, *, tq, tk):
    i = pl.program_id(1)
    n_t = ((i + 1) * tq + tk - 1) // tk
    lo_q = lax.broadcasted_iota(jnp.int32, (tq, LANES), 1) < HEAD_DIM
    q_halves = []
    for hp in range(A_HEADS // 2):
        q_halves.extend(_split_lane_halves(q_ref[0, :, hp * LANES:(hp + 1) * LANES]))
    ones = jnp.ones((tk, LANES), BF16)

    m_ref[...] = jnp.full(m_ref.shape, NEG_BIG, F32)
    acc_ref[...] = jnp.zeros(acc_ref.shape, F32)

    def kv_step(t, carry):
        off = pl.multiple_of(t * tk, tk)
        bias = bias_ref[0, :, pl.ds(off, tk)].astype(F32)
        for hp in range(A_HEADS // 2):
            kp = k_ref[0, pl.ds(off, tk), hp * LANES:(hp + 1) * LANES]
            v_ext = jnp.concatenate([v_ref[0, pl.ds(off, tk), hp * LANES:(hp + 1) * LANES], ones],
                                    axis=1)
            for half in range(2):
                h = 2 * hp + half
                _flash_update(h, _dot_nt(q_halves[h], kp) + bias, v_ext, m_ref, acc_ref)
        return carry

    lax.fori_loop(0, n_t, kv_step, 0)

    for hp in range(A_HEADS // 2):
        a0 = acc_ref[2 * hp]
        a1 = acc_ref[2 * hp + 1]
        o = jnp.where(lo_q, a0[:, :LANES] / a0[:, LANES:], a1[:, :LANES] / a1[:, LANES:])
        o_ref[0, :, hp * LANES:(hp + 1) * LANES] = o.astype(o_ref.dtype)


def _dsa(aq, ak, av, bias, tq, tk):
    bsz, seq, w = aq.shape
    kern = functools.partial(_dsa_kernel, tq=tq, tk=tk)
    resident = lambda: pl.BlockSpec((1, seq, w), lambda b, i: (b, 0, 0),
                                    pipeline_mode=pl.Buffered(1))
    return pl.pallas_call(
        kern,
        grid=(bsz, seq // tq),
        in_specs=[pl.BlockSpec((1, tq, w), lambda b, i: (b, i, 0)),
                  resident(), resident(),
                  pl.BlockSpec((1, tq, seq), lambda b, i: (b, i, 0))],
        out_specs=pl.BlockSpec((1, tq, w), lambda b, i: (b, i, 0)),
        out_shape=jax.ShapeDtypeStruct((bsz, seq, w), BF16),
        scratch_shapes=[pltpu.VMEM((A_HEADS, tq, LANES), F32),
                        pltpu.VMEM((A_HEADS, tq, 2 * LANES), F32)],
        compiler_params=_cparams("parallel", "arbitrary"),
    )(aq, ak, av, bias)


def _diff_kernel(q_ref, k_ref, v_ref, lq1_ref, lk1_ref, lq2_ref, lk2_ref, g_ref, o_ref,
                 m_ref, acc_ref, *, tq, tk, lambda_init):
    i = pl.program_id(1)
    n_full = (i * tq + CHUNK) // tk
    q_halves = []
    for hb in range(B_HEADS):
        q_halves.extend(_split_lane_halves(q_ref[0, :, hb * LANES:(hb + 1) * LANES]))
    lam = (jnp.exp(jnp.sum(lq1_ref[...] * lk1_ref[...], keepdims=True))
           - jnp.exp(jnp.sum(lq2_ref[...] * lk2_ref[...], keepdims=True)) + lambda_init)

    ones = jnp.ones((tk, LANES), BF16)

    m_ref[...] = jnp.full(m_ref.shape, NEG_BIG, F32)
    acc_ref[...] = jnp.zeros(acc_ref.shape, F32)

    def tile(off, masked):
        if masked:
            row_chunk = (i * tq + lax.broadcasted_iota(jnp.int32, (tq, tk), 0)) // CHUNK
            col_chunk = (off + lax.broadcasted_iota(jnp.int32, (tq, tk), 1)) // CHUNK
            ok = col_chunk <= row_chunk
        for hb in range(B_HEADS):
            kp = k_ref[0, pl.ds(off, tk), hb * LANES:(hb + 1) * LANES]
            v_ext = jnp.concatenate([v_ref[0, pl.ds(off, tk), hb * LANES:(hb + 1) * LANES], ones],
                                    axis=1)
            for mp in range(2):
                j = 2 * hb + mp
                s = _dot_nt(q_halves[j], kp)
                if masked:
                    s = jnp.where(ok, s, NEG_BIG)
                _flash_update(j, s, v_ext, m_ref, acc_ref)

    def kv_step(t, carry):
        tile(pl.multiple_of(t * tk, tk), False)
        return carry

    lax.fori_loop(0, n_full, kv_step, 0)
    tile(pl.multiple_of(n_full * tk, tk), True)

    for hb in range(B_HEADS):
        a1 = acc_ref[2 * hb]
        a2 = acc_ref[2 * hb + 1]
        o = a1[:, :LANES] / a1[:, LANES:] - lam * (a2[:, :LANES] / a2[:, LANES:])
        o = o * lax.rsqrt(jnp.mean(o * o, axis=-1, keepdims=True) + RMS_EPS)
        o = o * g_ref[...] * (1.0 - lambda_init)
        o_ref[0, :, hb * LANES:(hb + 1) * LANES] = o.astype(o_ref.dtype)


def _diff(bq, bk, bv, lq1, lk1, lq2, lk2, subln_g, lambda_init, tq, tk):
    bsz, seq, w = bq.shape
    kern = functools.partial(_diff_kernel, tq=tq, tk=tk, lambda_init=lambda_init)
    resident = lambda: pl.BlockSpec((1, seq, w), lambda b, i: (b, 0, 0),
                                    pipeline_mode=pl.Buffered(1))
    small = lambda a: pl.BlockSpec(a.shape, lambda b, i: (0, 0))
    return pl.pallas_call(
        kern,
        grid=(bsz, seq // tq),
        in_specs=[pl.BlockSpec((1, tq, w), lambda b, i: (b, i, 0)),
                  resident(), resident(),
                  small(lq1), small(lk1), small(lq2), small(lk2), small(subln_g)],
        out_specs=pl.BlockSpec((1, tq, w), lambda b, i: (b, i, 0)),
        out_shape=jax.ShapeDtypeStruct((bsz, seq, w), BF16),
        scratch_shapes=[pltpu.VMEM((2 * B_HEADS, tq, LANES), F32),
                        pltpu.VMEM((2 * B_HEADS, tq, 2 * LANES), F32)],
        compiler_params=_cparams("parallel", "arbitrary"),
    )(bq, bk, bv, lq1, lk1, lq2, lk2, subln_g)


def _first_argmax_rows(v, row):
    m = jnp.max(v, axis=0, keepdims=True)
    idx = jnp.min(jnp.where(v == m, row, v.shape[0]), axis=0, keepdims=True)
    return m, idx


def _scatter_rows(dk, gk, r0, nrows):
    r_iota = (lax.broadcasted_iota(jnp.int32, (nrows, dk.shape[1]), 0) + r0).astype(F32)
    out = jnp.zeros(r_iota.shape, F32)
    for k in range(dk.shape[0]):
        out = jnp.where(r_iota == dk[k:k + 1, :], 1.0 if gk is None else gk[k:k + 1, :], out)
    return out


def _oproj_kernel(oa_ref, ob_ref, h0_ref, woa_ref, wob_ref, g_ref, b_ref, wrt_ref, rb_ref,
                  h1_ref, h1b_ref, dk_ref, gk_ref, cnt_ref, *, alpha):
    mix = (jnp.dot(oa_ref[...], woa_ref[...], preferred_element_type=F32)
           + jnp.dot(ob_ref[...], wob_ref[...], preferred_element_type=F32))
    h1 = _layer_norm(alpha * h0_ref[...] + mix, g_ref[...], b_ref[...])
    h1_ref[...] = h1
    h1b = h1.astype(BF16)
    h1b_ref[...] = h1b

    logits = lax.dot_general(wrt_ref[...], h1, (((1,), (1,)), ((), ())),
                             precision=lax.Precision.HIGHEST, preferred_element_type=F32)
    scores = jax.nn.sigmoid(logits)
    biased = scores + rb_ref[...]
    tm = scores.shape[1]
    per_g = N_EXPERTS // N_GROUPS
    row8 = lax.broadcasted_iota(jnp.int32, (per_g, tm), 0)

    gs = []
    for g in range(N_GROUPS):
        blk = biased[g * per_g:(g + 1) * per_g, :]
        m1, i1 = _first_argmax_rows(blk, row8)
        m2 = jnp.max(jnp.where(row8 == i1, -jnp.inf, blk), axis=0, keepdims=True)
        gs.append(m1 + m2)
    gscore = jnp.concatenate(gs, axis=0)
    rowg = lax.broadcasted_iota(jnp.int32, (N_GROUPS, tm), 0)
    gsel = jnp.zeros((N_GROUPS, tm), F32)
    for _ in range(TOPK_GROUPS):
        _, ig = _first_argmax_rows(jnp.where(gsel > 0.0, -jnp.inf, gscore), rowg)
        gsel = jnp.where(rowg == ig, 1.0, gsel)

    rowe = lax.broadcasted_iota(jnp.int32, (N_EXPERTS, tm), 0)
    live = jnp.concatenate(
        [jnp.broadcast_to(gsel[g:g + 1, :], (per_g, tm)) for g in range(N_GROUPS)], axis=0)
    esel = jnp.zeros((N_EXPERTS, tm), F32)
    for _ in range(TOP_K):
        cand = jnp.where(live > 0.0, biased, -jnp.inf)
        m = jnp.max(cand, axis=0, keepdims=True)
        idx = jnp.min(jnp.where((live > 0.0) & (cand == m), rowe, N_EXPERTS),
                      axis=0, keepdims=True)
        hit = rowe == idx
        esel = jnp.where(hit, 1.0, esel)
        live = jnp.where(hit, 0.0, live)
    picked = jnp.where(esel > 0.0, scores, 0.0)
    denom = jnp.sum(picked, axis=0, keepdims=True)
    gate = picked / denom * ROUTED_SCALE

    sel_b = esel.astype(BF16)
    before_t = (lax.broadcasted_iota(jnp.int32, (tm, tm), 0)
                < lax.broadcasted_iota(jnp.int32, (tm, tm), 1)).astype(F32).astype(BF16)
    before_e = (lax.broadcasted_iota(jnp.int32, (N_EXPERTS, N_EXPERTS), 1)
                < lax.broadcasted_iota(jnp.int32, (N_EXPERTS, N_EXPERTS), 0)).astype(F32).astype(BF16)
    rank = jnp.dot(sel_b, before_t, preferred_element_type=F32)
    order = jnp.dot(before_e, sel_b, preferred_element_type=F32)
    cnt = jnp.sum(esel, axis=1, keepdims=True)
    cnt16 = jnp.floor((cnt + (ROW_ALIGN - 1)) / ROW_ALIGN) * ROW_ALIGN
    off = jnp.dot(before_e, jnp.broadcast_to(cnt16, (N_EXPERTS, tm)).astype(BF16),
                  preferred_element_type=F32)
    dest = off + rank
    dks, gks = [], []
    for k in range(TOP_K):
        kth = (esel > 0.0) & (order == k)
        dks.append(jnp.sum(jnp.where(kth, dest, 0.0), axis=0, keepdims=True))
        gks.append(jnp.sum(jnp.where(kth, gate, 0.0), axis=0, keepdims=True))
    dk = jnp.concatenate(dks, axis=0)
    dk_ref[...] = dk
    gk_ref[...] = jnp.concatenate(gks, axis=0)
    counts = _dot_nt(jnp.ones((8, tm), BF16), sel_b)
    cnt_ref[0] = jnp.concatenate([counts, jnp.zeros((8, LANES - N_EXPERTS), F32)], axis=1)


def _oproj(oa, ob, h0, woa, wob, g, b, wrt, rb, alpha, tm):
    n, d = h0.shape
    nt = n // tm
    row = lambda w: pl.BlockSpec((tm, w), lambda i: (i, 0))
    col = lambda: pl.BlockSpec((TOP_K, tm), lambda i: (0, i))
    full = lambda a: pl.BlockSpec(a.shape, lambda i: (0, 0))
    kt = jax.ShapeDtypeStruct((TOP_K, n), F32)
    return pl.pallas_call(
        functools.partial(_oproj_kernel, alpha=alpha),
        grid=(nt,),
        in_specs=[row(512), row(512), row(d), full(woa), full(wob), full(g), full(b),
                  full(wrt), full(rb)],
        out_specs=(row(d), row(d), col(), col(),
                   pl.BlockSpec((1, 8, LANES), lambda i: (i, 0, 0))),
        out_shape=(jax.ShapeDtypeStruct((n, d), F32),
                   jax.ShapeDtypeStruct((n, d), BF16),
                   kt,
                   kt,
                   jax.ShapeDtypeStruct((nt, 8, LANES), F32)),
        compiler_params=_cparams("parallel"),
    )(oa, ob, h0, woa, wob, g, b, wrt, rb)


def _swiglu(x, wg, wu, wd):
    hid = (jax.nn.silu(jnp.dot(x, wg, preferred_element_type=F32))
           * jnp.dot(x, wu, preferred_element_type=F32))
    return jnp.dot(hid.astype(BF16), wd, preferred_element_type=F32)


def _dispatch_kernel(tile_off_ref, exp_off_ref, len_ref, used_ref, gap_off_ref, gap_len_ref,
                     h1b_ref, dk_ref, xg_ref, xs_ref, zero_ref, run_sem, gap_sem, *, rb):
    i = pl.program_id(0)
    nt = pl.num_programs(0)
    rt = xs_ref.shape[1]
    slot = i % 2

    def wait_runs(step):
        n = pl.multiple_of(used_ref[step], ROW_ALIGN)
        s = step % 2
        pltpu.make_async_copy(xs_ref.at[s, pl.ds(0, n), :], xg_ref.at[pl.ds(0, n), :],
                              run_sem.at[s]).wait()

    @pl.when(i == 0)
    def _():
        zero_ref[...] = jnp.zeros(zero_ref.shape, zero_ref.dtype)

        def issue_gap(g, carry):
            n = pl.multiple_of(gap_len_ref[g], ROW_ALIGN)

            @pl.when(n > 0)
            def _():
                pltpu.make_async_copy(
                    zero_ref.at[pl.ds(0, n), :],
                    xg_ref.at[pl.ds(pl.multiple_of(gap_off_ref[g], ROW_ALIGN), n), :],
                    gap_sem).start()
            return carry

        lax.fori_loop(0, gap_len_ref.shape[0], issue_gap, 0)

    @pl.when(i >= 2)
    def _():
        wait_runs(i - 2)

    used = used_ref[i]
    dk = dk_ref[...]

    def sort_rows(r0):
        onehot = _scatter_rows(dk, None, r0, rb).astype(BF16)
        rows = jnp.dot(onehot, h1b_ref[...], preferred_element_type=F32)
        xs_ref[slot, r0:r0 + rb, :] = rows.astype(BF16)

    always = h1b_ref.shape[0] * TOP_K // rb * rb
    for r0 in range(0, always, rb):
        sort_rows(r0)
    for r0 in range(always, rt, rb):
        @pl.when(r0 < used)
        def _():
            sort_rows(r0)

    def issue_run(e, carry):
        r = i * N_EXPERTS + e
        n = pl.multiple_of(len_ref[r], ROW_ALIGN)

        @pl.when(n > 0)
        def _():
            pltpu.make_async_copy(
                xs_ref.at[slot, pl.ds(pl.multiple_of(tile_off_ref[r], ROW_ALIGN), n), :],
                xg_ref.at[pl.ds(pl.multiple_of(exp_off_ref[r], ROW_ALIGN), n), :],
                run_sem.at[slot]).start()
        return carry

    lax.fori_loop(0, N_EXPERTS, issue_run, 0)

    @pl.when(i == nt - 1)
    def _():
        @pl.when(i >= 1)
        def _():
            wait_runs(i - 1)
        wait_runs(i)

        def gap_rows(g, rows):
            return rows + gap_len_ref[g]
        n_gap = pl.multiple_of(lax.fori_loop(0, gap_len_ref.shape[0], gap_rows, 0), ROW_ALIGN)

        @pl.when(n_gap > 0)
        def _():
            pltpu.make_async_copy(xg_ref.at[pl.ds(0, n_gap), :], xg_ref.at[pl.ds(0, n_gap), :],
                                  gap_sem).wait()


def _dispatch(h1b, dk, tile_off, exp_off, lens, used, gap_off, gap_len, n_dst_rows, tm, rt):
    n, d = h1b.shape
    return pl.pallas_call(
        functools.partial(_dispatch_kernel, rb=512),
        grid_spec=pltpu.PrefetchScalarGridSpec(
            num_scalar_prefetch=6,
            grid=(n // tm,),
            in_specs=[pl.BlockSpec((tm, d), lambda i, *_: (i, 0)),
                      pl.BlockSpec((TOP_K, tm), lambda i, *_: (0, i))],
            out_specs=pl.BlockSpec(memory_space=pl.ANY),
            scratch_shapes=[pltpu.VMEM((2, rt, d), BF16), pltpu.VMEM((rt, d), BF16),
                            pltpu.SemaphoreType.DMA((2,)), pltpu.SemaphoreType.DMA(())]),
        out_shape=jax.ShapeDtypeStruct((n_dst_rows, d), BF16),
        compiler_params=_cparams("arbitrary"),
    )(tile_off, exp_off, lens, used, gap_off, gap_len, h1b, dk)


def _expert_kernel(blk_e_ref, blk_valid_ref, x_ref, wg_ref, wu_ref, wd_ref, y_ref):
    b = pl.program_id(0)
    valid = blk_valid_ref[b]

    @pl.when(valid > 0)
    def _():
        y_ref[...] = _swiglu(x_ref[...], wg_ref[...], wu_ref[...], wd_ref[...]).astype(BF16)

    @pl.when(valid <= 0)
    def _():
        y_ref[...] = jnp.zeros(y_ref.shape, BF16)


def _experts(xg, blk_e, blk_valid, wg, wu, wd):
    nr, d = xg.shape
    f = wg.shape[2]
    rows = pl.BlockSpec((MOE_BLOCK, d), lambda b, be, bv: (b, 0))
    return pl.pallas_call(
        _expert_kernel,
        grid_spec=pltpu.PrefetchScalarGridSpec(
            num_scalar_prefetch=2,
            grid=(nr // MOE_BLOCK,),
            in_specs=[rows,
                      pl.BlockSpec((None, d, f), lambda b, be, bv: (be[b], 0, 0)),
                      pl.BlockSpec((None, d, f), lambda b, be, bv: (be[b], 0, 0)),
                      pl.BlockSpec((None, f, d), lambda b, be, bv: (be[b], 0, 0))],
            out_specs=rows),
        out_shape=jax.ShapeDtypeStruct((nr, d), BF16),
        compiler_params=_cparams("parallel"),
    )(blk_e, blk_valid, xg, wg, wu, wd)


def _combine_kernel(tile_off_ref, exp_off_ref, len_ref, used_ref, yg_ref, dk_ref, gk_ref, h1_ref,
                    h1b_ref, p_ref, sg_ref, su_ref, sd_ref, g_ref, b_ref, wg_ref, bg_ref, wp_ref,
                    o_ref, ys_ref, ffn_ref, sem, *, alpha, rb):
    i = pl.program_id(0)
    rt = ys_ref.shape[0]

    @pl.when(i == 0)
    def _():
        ys_ref[...] = jnp.zeros(ys_ref.shape, ys_ref.dtype)

    def issue_run(e, rows):
        r = i * N_EXPERTS + e
        n = pl.multiple_of(len_ref[r], ROW_ALIGN)

        @pl.when(n > 0)
        def _():
            pltpu.make_async_copy(
                yg_ref.at[pl.ds(pl.multiple_of(exp_off_ref[r], ROW_ALIGN), n), :],
                ys_ref.at[pl.ds(pl.multiple_of(tile_off_ref[r], ROW_ALIGN), n), :], sem).start()
        return rows + n

    run_rows = pl.multiple_of(lax.fori_loop(0, N_EXPERTS, issue_run, 0), ROW_ALIGN)
    dk = dk_ref[...]
    gk = gk_ref[...]
    ffn = _swiglu(h1b_ref[...], sg_ref[...], su_ref[...], sd_ref[...])
    pltpu.make_async_copy(yg_ref.at[pl.ds(0, run_rows), :], ys_ref.at[pl.ds(0, run_rows), :],
                          sem).wait()

    def routed(r0):
        weights = _scatter_rows(dk, gk, r0, rb).astype(BF16)
        return lax.dot_general(weights, ys_ref[r0:r0 + rb, :], (((0,), (0,)), ((), ())),
                               preferred_element_type=F32)

    used = used_ref[i]
    tm = h1_ref.shape[0]
    always = tm * TOP_K // rb * rb
    for r0 in range(0, always, rb):
        ffn = ffn + routed(r0)
    ffn_ref[...] = ffn
    for r0 in range(always, rt, rb):
        @pl.when(r0 < used)
        def _():
            ffn_ref[...] += routed(r0)
    h2 = _layer_norm(alpha * h1_ref[...] + ffn_ref[...], g_ref[...], b_ref[...])
    emb_gate = jax.nn.sigmoid(jnp.dot(h2.astype(BF16), wg_ref[...], preferred_element_type=F32)
                              + bg_ref[...])
    emb = jnp.dot(p_ref[...].astype(BF16), wp_ref[...], preferred_element_type=F32)
    o_ref[...] = h2 + emb_gate * emb


def _combine(tile_off, exp_off, lens, used, yg, dk, gk, h1, h1b, p2, sg, su, sd, g, b, wg, bg, wp,
             alpha, tm, rt):
    n, d = h1.shape
    row = lambda w: pl.BlockSpec((tm, w), lambda i, *_: (i, 0))
    col = lambda: pl.BlockSpec((TOP_K, tm), lambda i, *_: (0, i))
    full = lambda a: pl.BlockSpec(a.shape, lambda i, *_: (0, 0))
    return pl.pallas_call(
        functools.partial(_combine_kernel, alpha=alpha, rb=512),
        grid_spec=pltpu.PrefetchScalarGridSpec(
            num_scalar_prefetch=4,
            grid=(n // tm,),
            in_specs=[pl.BlockSpec(memory_space=pl.ANY), col(), col(), row(d),
                      row(d), row(p2.shape[1]), full(sg), full(su), full(sd), full(g), full(b),
                      full(wg), full(bg), full(wp)],
            out_specs=row(d),
            scratch_shapes=[pltpu.VMEM((rt, d), yg.dtype), pltpu.VMEM((tm, d), F32),
                            pltpu.SemaphoreType.DMA(())]),
        out_shape=jax.ShapeDtypeStruct((n, d), F32),
        compiler_params=_cparams("arbitrary"),
    )(tile_off, exp_off, lens, used, yg, dk, gk, h1, h1b, p2, sg, su, sd, g, b, wg, bg, wp)


def _tiles(seq):
    tm = min(256, seq)
    tq_sel = min(256, seq)
    tk_sel = min(512, seq)
    tq_att = min(256, seq)
    tk_att = min(512, seq)
    rt = -(-(tm * TOP_K + N_EXPERTS * (ROW_ALIGN - 1)) // 512) * 512
    return tm, tq_sel, tk_sel, tq_att, tk_att, rt


def kernel(x, p, positions, ln_emb_g, ln_emb_b, w_in, w_o, diff_lq1, diff_lk1, diff_lq2, diff_lk2,
           diff_subln_g, ln1_g, ln1_b, w_router, router_bias, w_exp_gate, w_exp_up, w_exp_down,
           w_sh_gate, w_sh_up, w_sh_down, ln2_g, ln2_b, w_ple_gate, b_ple_gate, w_ple_proj):
    bsz, seq, d = x.shape
    depth = w_in.shape[0]
    n = bsz * seq
    n_keep = min(IDX_TOPK_MAX, seq // 4)
    alpha = (2.0 * depth) ** 0.25
    tm, tq_sel, tk_sel, tq_att, tk_att, rt = _tiles(seq)
    n_blocks = -(-((n // tm) * (tm * TOP_K + N_EXPERTS * (ROW_ALIGN - 1))
                   + N_EXPERTS * (MOE_BLOCK - 1)) // MOE_BLOCK)
    row1 = lambda v: v.reshape(1, -1)

    inv = ROPE_THETA ** (-jnp.arange(0, ROPE_DIM, 2, dtype=F32) / ROPE_DIM)
    inv_lanes = jnp.tile(inv, LANES // (ROPE_DIM // 2)).reshape(1, LANES)
    pos2 = positions.reshape(n, 1)

    h = x.reshape(n, d)
    out = None
    for li in range(depth):
        lambda_init = 0.8 - 0.6 * math.exp(-0.3 * li)
        w = w_in[li]
        aqw, akw, avw = w[:, 0:512], w[:, 512:1024], w[:, 1024:1536]
        iqw, ikw, iww = w[:, 1536:2048], w[:, 2048:2112], w[:, 2112:2120]
        bqw, bkw, bvw = w[:, 2120:2632], w[:, 2632:3144], w[:, 3144:3656]
        w_rope = jnp.concatenate([aqw, akw, iqw, ikw, ikw, bqw, bkw], axis=1).astype(BF16)
        w_plain = jnp.concatenate([avw, bvw], axis=1).astype(BF16)
        w_idx_t = jnp.concatenate(
            [iww.T, jnp.zeros((LANES - IDX_HEADS, d), F32)], axis=0).astype(BF16)

        (h0, aq, ak, iq, ik2, bq, bk, av, bv, iw) = _proj(
            h, pos2, inv_lanes, row1(ln_emb_g), row1(ln_emb_b), w_rope, w_plain, w_idx_t, tm)
        assert depth == 1

        r3 = lambda a: a.reshape(bsz, seq, a.shape[-1])
        bias = _select(r3(iq), iw, r3(ik2), n_keep, tq_sel, tk_sel)
        out_a = _dsa(r3(aq), r3(ak), r3(av), bias, tq_att, tk_att)
        out_b = _diff(r3(bq), r3(bk), r3(bv), row1(diff_lq1[li]), row1(diff_lk1[li]),
                      row1(diff_lq2[li]), row1(diff_lk2[li]), row1(diff_subln_g[li]),
                      lambda_init, tq_att, tk_att)

        wo = w_o[li].astype(BF16)
        h1, h1b, dk, gk, cnt_out = _oproj(
            out_a.reshape(n, -1), out_b.reshape(n, -1), h0, wo[0:512], wo[512:1024],
            row1(ln1_g[li]), row1(ln1_b[li]), w_router[li].T, router_bias[li].reshape(-1, 1),
            alpha, tm)

        nt = n // tm
        c16 = jnp.ceil(cnt_out[:, 0, :N_EXPERTS] / ROW_ALIGN) * ROW_ALIGN
        tri = lambda m: jnp.tril(jnp.ones((m, m), F32))
        hi = lax.Precision.HIGHEST
        tile_off = jnp.dot(c16, tri(N_EXPERTS).T, precision=hi) - c16
        cum_tiles = jnp.dot(tri(nt), c16, precision=hi)
        tot = cum_tiles[-1]
        totp = jnp.ceil(tot / MOE_BLOCK) * MOE_BLOCK
        pend = jnp.dot(totp, tri(N_EXPERTS).T, precision=hi)
        ebase = pend - totp
        exp_off = ebase[None, :] + cum_tiles - c16
        blk0 = jnp.arange(n_blocks, dtype=F32) * MOE_BLOCK
        blk_e = jnp.minimum(jnp.sum(pend[None, :] <= blk0[:, None], axis=1), N_EXPERTS - 1)
        onehot_e = (blk_e[:, None] == jnp.arange(N_EXPERTS)[None, :]).astype(F32)
        blk_end = jnp.dot(onehot_e, ebase + tot, precision=hi)
        blk_valid = jnp.clip(blk_end - blk0, 0, MOE_BLOCK)
        flat = lambda a: a.reshape(-1).astype(jnp.int32)
        n_rows = n_blocks * MOE_BLOCK
        tail0 = pend[-1] + jnp.arange(-(-n_rows // rt), dtype=F32) * rt
        gap_off = jnp.concatenate([ebase + tot, tail0])
        gap_len = jnp.concatenate([totp - tot, jnp.clip(n_rows - tail0, 0, rt)])

        weg, weu, wed = (w_exp_gate[li].astype(BF16), w_exp_up[li].astype(BF16),
                         w_exp_down[li].astype(BF16))
        used = jnp.sum(c16, axis=1)
        xg = _dispatch(h1b, dk, flat(tile_off), flat(exp_off), flat(c16), flat(used),
                       flat(gap_off), flat(gap_len), n_rows, tm, rt)
        yg = _experts(xg, flat(blk_e), flat(blk_valid), weg, weu, wed)
        out = _combine(flat(tile_off), flat(exp_off), flat(c16), flat(used), yg, dk, gk, h1, h1b,
                       p[li].reshape(n, -1), w_sh_gate[li].astype(BF16),
                       w_sh_up[li].astype(BF16), w_sh_down[li].astype(BF16), row1(ln2_g[li]),
                       row1(ln2_b[li]), w_ple_gate[li].astype(BF16), row1(b_ple_gate[li]),
                       w_ple_proj[li].astype(BF16), alpha, tm, rt)
        h = out
    return out.reshape(bsz, seq, d)
```

```python
import functools
import math

import jax
import jax.numpy as jnp
from jax import lax
from jax.experimental import pallas as pl
from jax.experimental.pallas import tpu as pltpu

CHUNK = 64
HEAD_DIM = 64
ROPE_DIM = 64
ROPE_THETA = 10000.0
A_HEADS = 8
IDX_HEADS = 8
IDX_DIM = 64
IDX_TOPK_MAX = 256
B_HEADS = 4
B_QK_DIM = 64
B_V_DIM = 128
N_EXPERTS = 64
TOP_K = 8
N_GROUPS = 8
TOPK_GROUPS = 4
ROUTED_SCALE = 2.5
LN_EPS = 1e-5
RMS_EPS = 1e-5

LANES = 128
ROW_ALIGN = 16
MOE_BLOCK = 512
VMEM_LIMIT = 56 * 1024 * 1024
NEG_BIG = -1e30
INT_MIN = -(2 ** 31)
GROUPS = 256

F32 = jnp.float32
BF16 = jnp.bfloat16


def _cparams(*sem):
    return pltpu.CompilerParams(dimension_semantics=sem, vmem_limit_bytes=VMEM_LIMIT)


def _layer_norm(x, g, b):
    mu = jnp.mean(x, axis=-1, keepdims=True)
    xc = x - mu
    var = jnp.mean(xc * xc, axis=-1, keepdims=True)
    return xc * lax.rsqrt(var + LN_EPS) * g + b


def _dot_nt(a, b):
    return lax.dot_general(a, b, (((1,), (1,)), ((), ())), preferred_element_type=F32)


def _split_lane_halves(pair):
    lo = lax.broadcasted_iota(jnp.int32, pair.shape, 1) < (LANES // 2)
    pf = pair.astype(F32)
    return (jnp.where(lo, pf, 0.0).astype(pair.dtype), jnp.where(lo, 0.0, pf).astype(pair.dtype))


def _proj_kernel(x_ref, pos_ref, inv_ref, g_ref, b_ref, wr_ref, wp_ref, wi_ref,
                 h0_ref, aq_ref, ak_ref, iq_ref, ik_ref, bq_ref, bk_ref,
                 av_ref, bv_ref, iw_ref):
    hn = _layer_norm(x_ref[...], g_ref[...], b_ref[...])
    h0_ref[...] = hn
    hb = hn.astype(BF16)

    tm = x_ref.shape[0]
    nq = LANES // (ROPE_DIM // 2)
    tr = tm // nq
    pos = pos_ref[...].astype(F32)
    lane = lax.broadcasted_iota(jnp.int32, (tr, LANES), 1)
    blk = lane // (ROPE_DIM // 2)
    pos_packed = jnp.broadcast_to(pos[0:tr], (tr, LANES))
    for q in range(1, nq):
        pos_packed = jnp.where(blk == q, pos[q * tr:(q + 1) * tr], pos_packed)
    ang = pos_packed * inv_ref[...]
    cos_packed = jnp.cos(ang)
    sin_packed = jnp.sin(ang)
    first_half = (lane % ROPE_DIM) < (ROPE_DIM // 2)

    def spread(table, q):
        own = jnp.where(blk == q, table, 0.0)
        t = own
        for s in range(1, nq):
            t = t + pltpu.roll(own, s * (ROPE_DIM // 2), axis=1)
        return t

    pr = jnp.dot(hb, wr_ref[...], preferred_element_type=F32)
    outs = ((aq_ref, 0, 4, HEAD_DIM ** -0.5), (ak_ref, 4, 4, 1.0), (iq_ref, 8, 4, 1.0),
            (ik_ref, 12, 1, 1.0), (bq_ref, 13, 4, B_QK_DIM ** -0.5), (bk_ref, 17, 4, 1.0))
    for q in range(nq):
        cos = spread(cos_packed, q)
        sin = spread(sin_packed, q)
        sin_signed = jnp.where(first_half, -sin, sin)
        rows = slice(q * tr, (q + 1) * tr)
        for ref, g0, ng, scale in outs:
            for j in range(ng):
                v = pr[rows, (g0 + j) * LANES:(g0 + j + 1) * LANES]
                partner = jnp.where(first_half,
                                    pltpu.roll(v, LANES - ROPE_DIM // 2, axis=1),
                                    pltpu.roll(v, ROPE_DIM // 2, axis=1))
                v = v * cos + partner * sin_signed
                if scale != 1.0:
                    v = v * scale
                ref[rows, j * LANES:(j + 1) * LANES] = v.astype(ref.dtype)

    pp = jnp.dot(hb, wp_ref[...], preferred_element_type=F32)
    av_ref[...] = pp[:, 0:512].astype(BF16)
    bv_ref[...] = pp[:, 512:1024].astype(BF16)
    iw_ref[...] = _dot_nt(wi_ref[...], hb)[0:IDX_HEADS, :]


def _proj(x2, pos2, inv, g, b, w_rope, w_plain, w_idx_t, tm):
    n, d = x2.shape
    nr = w_rope.shape[1]
    npl = w_plain.shape[1]
    row = lambda w: pl.BlockSpec((tm, w), lambda i: (i, 0))
    full = lambda a: pl.BlockSpec(a.shape, lambda i: (0, 0))
    out_shape = (
        jax.ShapeDtypeStruct((n, d), F32),
        jax.ShapeDtypeStruct((n, 512), BF16),
        jax.ShapeDtypeStruct((n, 512), BF16),
        jax.ShapeDtypeStruct((n, 512), BF16),
        jax.ShapeDtypeStruct((n, 128), BF16),
        jax.ShapeDtypeStruct((n, 512), BF16),
        jax.ShapeDtypeStruct((n, 512), BF16),
        jax.ShapeDtypeStruct((n, 512), BF16),
        jax.ShapeDtypeStruct((n, 512), BF16),
        jax.ShapeDtypeStruct((IDX_HEADS, n), F32),
    )
    return pl.pallas_call(
        _proj_kernel,
        grid=(n // tm,),
        in_specs=[row(d), row(1), full(inv), full(g), full(b), full(w_rope), full(w_plain),
                  full(w_idx_t)],
        out_specs=(row(d), row(512), row(512), row(512), row(128), row(512), row(512),
                   row(512), row(512), pl.BlockSpec((IDX_HEADS, tm), lambda i: (0, i))),
        out_shape=out_shape,
        compiler_params=_cparams("parallel"),
    )(x2, pos2, inv, g, b, w_rope, w_plain, w_idx_t)


def _sortable(s):
    i = lax.bitcast_convert_type(s, jnp.int32)
    return i ^ ((i >> 31) & 0x7FFFFFFF)


def _select_kernel(iq_ref, iw_ref, ik_ref, bias_ref, keys_ref, gmax_ref, *, tq, tk, n_keep,
                   w_scale):
    i = pl.program_id(1)
    seq = ik_ref.shape[1]
    n_t = ((i + 1) * tq + tk - 1) // tk

    q_chunk = (i * tq + lax.broadcasted_iota(jnp.int32, (tk, tq), 1)) // CHUNK
    k_row = lax.broadcasted_iota(jnp.int32, (tk, tq), 0)
    wb = iw_ref[...] * w_scale
    q_halves = []
    for hp in range(IDX_HEADS // 2):
        q_halves.extend(_split_lane_halves(iq_ref[0, :, hp * LANES:(hp + 1) * LANES]))

    def score_tile(t, carry, boundary):
        off = pl.multiple_of(t * tk, tk)
        kt = ik_ref[0, pl.ds(off, tk), :]
        sc = jnp.zeros((tk, tq), F32)
        for h in range(IDX_HEADS):
            rel = jnp.maximum(_dot_nt(kt, q_halves[h]), 0.0)
            sc = sc + wb[h:h + 1, :] * rel
        key = _sortable(sc + 0.0)
        if boundary:
            key = jnp.where(((off + k_row) // CHUNK) <= q_chunk, key, INT_MIN)
        keys_ref[pl.ds(off, tk), :] = key
        gm = key[0:GROUPS, :]
        for g0 in range(GROUPS, tk, GROUPS):
            gm = jnp.maximum(gm, key[g0:g0 + GROUPS, :])
        gmax_ref[...] = jnp.maximum(gmax_ref[...], gm)
        return carry

    gmax_ref[...] = jnp.full(gmax_ref.shape, INT_MIN, jnp.int32)
    n_full = (i * tq + CHUNK) // tk
    lax.fori_loop(0, n_full, functools.partial(score_tile, boundary=False), 0)
    lax.fori_loop(n_full, n_t, functools.partial(score_tile, boundary=True), 0)

    def count_ge(cand):
        def body(t, acc):
            off = pl.multiple_of(t * tk, tk)
            m = (keys_ref[pl.ds(off, tk), :] >= cand).astype(jnp.int32)
            return acc + jnp.sum(m.reshape(tk // 8, 8, tq), axis=0)
        acc = lax.fori_loop(0, n_t, body, jnp.zeros((8, tq), jnp.int32))
        return jnp.sum(acc, axis=0, keepdims=True)

    gmax = gmax_ref[...]
    n_adm = ((i * tq + lax.broadcasted_iota(jnp.int32, (1, tq), 1)) // CHUNK + 1) * CHUNK
    few = n_adm <= n_keep
    lo0 = jnp.where(few, INT_MIN + 1, jnp.min(gmax, axis=0, keepdims=True))
    hi0 = jnp.where(few, INT_MIN + 2, jnp.max(gmax, axis=0, keepdims=True) + 1)

    def probe(state):
        lo, hi, thr, done, _, it = state
        mid = (lo >> 1) + (hi >> 1) + (lo & hi & 1)
        closed = mid == lo
        cnt = count_ge(mid)
        exact = cnt == n_keep
        up = cnt >= n_keep
        live = done == 0
        thr = jnp.where(live & exact, mid, jnp.where(live & closed, lo, thr))
        lo = jnp.where(live & up, mid, lo)
        hi = jnp.where(live & jnp.logical_not(up), mid, hi)
        done = jnp.where(exact | closed, 1, done)
        return lo, hi, thr, done, jnp.min(done), it + 1

    def searching(state):
        return (state[4] == 0) & (state[5] < 34)

    done0 = few.astype(jnp.int32)
    state = lax.while_loop(
        searching, probe, (lo0, hi0, lo0, done0, jnp.min(done0), jnp.int32(0)))
    thr = state[2]

    n_ge = count_ge(thr)
    tied = n_ge > n_keep
    any_tied = jnp.max(tied.astype(jnp.int32)) > 0
    tau = thr

    def write_bias(off, keep_t):
        bias_t = jnp.where(keep_t, 0.0, NEG_BIG)
        bias_ref[0, :, pl.ds(off, tk)] = bias_t.T.astype(BF16)

    @pl.when(jnp.logical_not(any_tied))
    def _():
        def write_tile(t, carry):
            off = pl.multiple_of(t * tk, tk)
            write_bias(off, keys_ref[pl.ds(off, tk), :] >= tau)
            return carry
        lax.fori_loop(0, n_t, write_tile, 0)

    @pl.when(any_tied)
    def _():
        room = (n_keep - count_ge(tau + 1)).astype(F32)
        upto = (lax.broadcasted_iota(jnp.int32, (tk, tk), 1)
                <= lax.broadcasted_iota(jnp.int32, (tk, tk), 0)).astype(F32).astype(BF16)

        def write_tile(t, seen_eq):
            off = pl.multiple_of(t * tk, tk)
            key = keys_ref[pl.ds(off, tk), :]
            eq = (key == tau) & tied
            eq_f = jnp.where(eq, 1.0, 0.0)
            eq_rank = seen_eq + jnp.dot(upto, eq_f.astype(BF16), preferred_element_type=F32)
            write_bias(off, (key > tau) | (eq & (eq_rank <= room))
                       | ((key == tau) & jnp.logical_not(tied)))
            return seen_eq + jnp.sum(eq_f, axis=0, keepdims=True)

        lax.fori_loop(0, n_t, write_tile, jnp.zeros((1, tq), F32))

    def blank_tile(t, carry):
        off = pl.multiple_of(t * tk, tk)
        bias_ref[0, :, pl.ds(off, tk)] = jnp.full((tq, tk), NEG_BIG, BF16)
        return carry

    lax.fori_loop(n_t, seq // tk, blank_tile, 0)


def _select(iq, iw, ik2, n_keep, tq, tk):
    bsz, seq, _ = iq.shape
    kern = functools.partial(_select_kernel, tq=tq, tk=tk, n_keep=n_keep,
                             w_scale=(IDX_HEADS ** -0.5) * (IDX_DIM ** -0.5))
    return pl.pallas_call(
        kern,
        grid=(bsz, seq // tq),
        in_specs=[pl.BlockSpec((1, tq, 512), lambda b, i: (b, i, 0)),
                  pl.BlockSpec((IDX_HEADS, tq), lambda b, i: (0, b * (seq // tq) + i)),
                  pl.BlockSpec((1, seq, 128), lambda b, i: (b, 0, 0))],
        out_specs=pl.BlockSpec((1, tq, seq), lambda b, i: (b, i, 0)),
        out_shape=jax.ShapeDtypeStruct((bsz, seq, seq), BF16),
        scratch_shapes=[pltpu.VMEM((seq, tq), jnp.int32), pltpu.VMEM((GROUPS, tq), jnp.int32)],
        compiler_params=_cparams("parallel", "parallel"),
    )(iq, iw, ik2)


def _flash_update(j, s, v_ext, m_ref, acc_ref):
    tk = s.shape[1]
    m_old = m_ref[j]
    m_new = jnp.maximum(m_old, jnp.max(s, axis=1, keepdims=True))
    alpha = jnp.exp(m_old - m_new)
    p = jnp.exp(s - jnp.concatenate([m_new] * (tk // LANES), axis=1))
    m_ref[j] = m_new
    acc_ref[j] = (acc_ref[j] * jnp.concatenate([alpha, alpha], axis=1)
                  + jnp.dot(p.astype(BF16), v_ext, preferred_element_type=F32))


def _dsa_kernel(q_ref, k_ref, v_ref, bias_ref, o_ref, m_ref, acc_ref, *, tq, tk):
    i = pl.program_id(1)
    n_t = ((i + 1) * tq + tk - 1) // tk
    lo_q = lax.broadcasted_iota(jnp.int32, (tq, LANES), 1) < HEAD_DIM
    q_halves = []
    for hp in range(A_HEADS // 2):
        q_halves.extend(_split_lane_halves(q_ref[0, :, hp * LANES:(hp + 1) * LANES]))
    ones = jnp.ones((tk, LANES), BF16)

    m_ref[...] = jnp.full(m_ref.shape, NEG_BIG, F32)
    acc_ref[...] = jnp.zeros(acc_ref.shape, F32)

    def kv_step(t, carry):
        off = pl.multiple_of(t * tk, tk)
        bias = bias_ref[0, :, pl.ds(off, tk)].astype(F32)
        for hp in range(A_HEADS // 2):
            kp = k_ref[0, pl.ds(off, tk), hp * LANES:(hp + 1) * LANES]
            v_ext = jnp.concatenate([v_ref[0, pl.ds(off, tk), hp * LANES:(hp + 1) * LANES], ones],
                                    axis=1)
            for half in range(2):
                h = 2 * hp + half
                _flash_update(h, _dot_nt(q_halves[h], kp) + bias, v_ext, m_ref, acc_ref)
        return carry

    lax.fori_loop(0, n_t, kv_step, 0)

    for hp in range(A_HEADS // 2):
        a0 = acc_ref[2 * hp]
        a1 = acc_ref[2 * hp + 1]
        o = jnp.where(lo_q, a0[:, :LANES] / a0[:, LANES:], a1[:, :LANES] / a1[:, LANES:])
        o_ref[0, :, hp * LANES:(hp + 1) * LANES] = o.astype(o_ref.dtype)


def _dsa(aq, ak, av, bias, tq, tk):
    bsz, seq, w = aq.shape
    kern = functools.partial(_dsa_kernel, tq=tq, tk=tk)
    resident = lambda: pl.BlockSpec((1, seq, w), lambda b, i: (b, 0, 0),
                                    pipeline_mode=pl.Buffered(1))
    return pl.pallas_call(
        kern,
        grid=(bsz, seq // tq),
        in_specs=[pl.BlockSpec((1, tq, w), lambda b, i: (b, i, 0)),
                  resident(), resident(),
                  pl.BlockSpec((1, tq, seq), lambda b, i: (b, i, 0))],
        out_specs=pl.BlockSpec((1, tq, w), lambda b, i: (b, i, 0)),
        out_shape=jax.ShapeDtypeStruct((bsz, seq, w), BF16),
        scratch_shapes=[pltpu.VMEM((A_HEADS, tq, LANES), F32),
                        pltpu.VMEM((A_HEADS, tq, 2 * LANES), F32)],
        compiler_params=_cparams("parallel", "arbitrary"),
    )(aq, ak, av, bias)


def _diff_kernel(q_ref, k_ref, v_ref, lq1_ref, lk1_ref, lq2_ref, lk2_ref, g_ref, o_ref,
                 m_ref, acc_ref, *, tq, tk, lambda_init):
    i = pl.program_id(1)
    n_full = (i * tq + CHUNK) // tk
    q_halves = []
    for hb in range(B_HEADS):
        q_halves.extend(_split_lane_halves(q_ref[0, :, hb * LANES:(hb + 1) * LANES]))
    lam = (jnp.exp(jnp.sum(lq1_ref[...] * lk1_ref[...], keepdims=True))
           - jnp.exp(jnp.sum(lq2_ref[...] * lk2_ref[...], keepdims=True)) + lambda_init)

    ones = jnp.ones((tk, LANES), BF16)

    m_ref[...] = jnp.full(m_ref.shape, NEG_BIG, F32)
    acc_ref[...] = jnp.zeros(acc_ref.shape, F32)

    def tile(off, masked):
        if masked:
            row_chunk = (i * tq + lax.broadcasted_iota(jnp.int32, (tq, tk), 0)) // CHUNK
            col_chunk = (off + lax.broadcasted_iota(jnp.int32, (tq, tk), 1)) // CHUNK
            ok = col_chunk <= row_chunk
        for hb in range(B_HEADS):
            kp = k_ref[0, pl.ds(off, tk), hb * LANES:(hb + 1) * LANES]
            v_ext = jnp.concatenate([v_ref[0, pl.ds(off, tk), hb * LANES:(hb + 1) * LANES], ones],
                                    axis=1)
            for mp in range(2):
                j = 2 * hb + mp
                s = _dot_nt(q_halves[j], kp)
                if masked:
                    s = jnp.where(ok, s, NEG_BIG)
                _flash_update(j, s, v_ext, m_ref, acc_ref)

    def kv_step(t, carry):
        tile(pl.multiple_of(t * tk, tk), False)
        return carry

    lax.fori_loop(0, n_full, kv_step, 0)
    tile(pl.multiple_of(n_full * tk, tk), True)

    for hb in range(B_HEADS):
        a1 = acc_ref[2 * hb]
        a2 = acc_ref[2 * hb + 1]
        o = a1[:, :LANES] / a1[:, LANES:] - lam * (a2[:, :LANES] / a2[:, LANES:])
        o = o * lax.rsqrt(jnp.mean(o * o, axis=-1, keepdims=True) + RMS_EPS)
        o = o * g_ref[...] * (1.0 - lambda_init)
        o_ref[0, :, hb * LANES:(hb + 1) * LANES] = o.astype(o_ref.dtype)


def _diff(bq, bk, bv, lq1, lk1, lq2, lk2, subln_g, lambda_init, tq, tk):
    bsz, seq, w = bq.shape
    kern = functools.partial(_diff_kernel, tq=tq, tk=tk, lambda_init=lambda_init)
    resident = lambda: pl.BlockSpec((1, seq, w), lambda b, i: (b, 0, 0),
                                    pipeline_mode=pl.Buffered(1))
    small = lambda a: pl.BlockSpec(a.shape, lambda b, i: (0, 0))
    return pl.pallas_call(
        kern,
        grid=(bsz, seq // tq),
        in_specs=[pl.BlockSpec((1, tq, w), lambda b, i: (b, i, 0)),
                  resident(), resident(),
                  small(lq1), small(lk1), small(lq2), small(lk2), small(subln_g)],
        out_specs=pl.BlockSpec((1, tq, w), lambda b, i: (b, i, 0)),
        out_shape=jax.ShapeDtypeStruct((bsz, seq, w), BF16),
        scratch_shapes=[pltpu.VMEM((2 * B_HEADS, tq, LANES), F32),
                        pltpu.VMEM((2 * B_HEADS, tq, 2 * LANES), F32)],
        compiler_params=_cparams("parallel", "arbitrary"),
    )(bq, bk, bv, lq1, lk1, lq2, lk2, subln_g)


def _first_argmax_rows(v, row):
    m = jnp.max(v, axis=0, keepdims=True)
    idx = jnp.min(jnp.where(v == m, row, v.shape[0]), axis=0, keepdims=True)
    return m, idx


def _scatter_rows(dk, gk, r0, nrows):
    r_iota = (lax.broadcasted_iota(jnp.int32, (nrows, dk.shape[1]), 0) + r0).astype(F32)
    out = jnp.zeros(r_iota.shape, F32)
    for k in range(dk.shape[0]):
        out = jnp.where(r_iota == dk[k:k + 1, :], 1.0 if gk is None else gk[k:k + 1, :], out)
    return out


def _oproj_kernel(oa_ref, ob_ref, h0_ref, woa_ref, wob_ref, g_ref, b_ref, wrt_ref, rb_ref,
                  h1_ref, h1b_ref, dk_ref, gk_ref, cnt_ref, *, alpha):
    mix = (jnp.dot(oa_ref[...], woa_ref[...], preferred_element_type=F32)
           + jnp.dot(ob_ref[...], wob_ref[...], preferred_element_type=F32))
    h1 = _layer_norm(alpha * h0_ref[...] + mix, g_ref[...], b_ref[...])
    h1_ref[...] = h1
    h1b = h1.astype(BF16)
    h1b_ref[...] = h1b

    logits = lax.dot_general(wrt_ref[...], h1, (((1,), (1,)), ((), ())),
                             precision=lax.Precision.HIGHEST, preferred_element_type=F32)
    scores = jax.nn.sigmoid(logits)
    biased = scores + rb_ref[...]
    tm = scores.shape[1]
    per_g = N_EXPERTS // N_GROUPS
    row8 = lax.broadcasted_iota(jnp.int32, (per_g, tm), 0)

    gs = []
    for g in range(N_GROUPS):
        blk = biased[g * per_g:(g + 1) * per_g, :]
        m1, i1 = _first_argmax_rows(blk, row8)
        m2 = jnp.max(jnp.where(row8 == i1, -jnp.inf, blk), axis=0, keepdims=True)
        gs.append(m1 + m2)
    gscore = jnp.concatenate(gs, axis=0)
    rowg = lax.broadcasted_iota(jnp.int32, (N_GROUPS, tm), 0)
    gsel = jnp.zeros((N_GROUPS, tm), F32)
    for _ in range(TOPK_GROUPS):
        _, ig = _first_argmax_rows(jnp.where(gsel > 0.0, -jnp.inf, gscore), rowg)
        gsel = jnp.where(rowg == ig, 1.0, gsel)

    rowe = lax.broadcasted_iota(jnp.int32, (N_EXPERTS, tm), 0)
    live = jnp.concatenate(
        [jnp.broadcast_to(gsel[g:g + 1, :], (per_g, tm)) for g in range(N_GROUPS)], axis=0)
    esel = jnp.zeros((N_EXPERTS, tm), F32)
    for _ in range(TOP_K):
        cand = jnp.where(live > 0.0, biased, -jnp.inf)
        m = jnp.max(cand, axis=0, keepdims=True)
        idx = jnp.min(jnp.where((live > 0.0) & (cand == m), rowe, N_EXPERTS),
                      axis=0, keepdims=True)
        hit = rowe == idx
        esel = jnp.where(hit, 1.0, esel)
        live = jnp.where(hit, 0.0, live)
    picked = jnp.where(esel > 0.0, scores, 0.0)
    denom = jnp.sum(picked, axis=0, keepdims=True)
    gate = picked / denom * ROUTED_SCALE

    sel_b = esel.astype(BF16)
    before_t = (lax.broadcasted_iota(jnp.int32, (tm, tm), 0)
                < lax.broadcasted_iota(jnp.int32, (tm, tm), 1)).astype(F32).astype(BF16)
    before_e = (lax.broadcasted_iota(jnp.int32, (N_EXPERTS, N_EXPERTS), 1)
                < lax.broadcasted_iota(jnp.int32, (N_EXPERTS, N_EXPERTS), 0)).astype(F32).astype(BF16)
    rank = jnp.dot(sel_b, before_t, preferred_element_type=F32)
    order = jnp.dot(before_e, sel_b, preferred_element_type=F32)
    cnt = jnp.sum(esel, axis=1, keepdims=True)
    cnt16 = jnp.floor((cnt + (ROW_ALIGN - 1)) / ROW_ALIGN) * ROW_ALIGN
    off = jnp.dot(before_e, jnp.broadcast_to(cnt16, (N_EXPERTS, tm)).astype(BF16),
                  preferred_element_type=F32)
    dest = off + rank
    dks, gks = [], []
    for k in range(TOP_K):
        kth = (esel > 0.0) & (order == k)
        dks.append(jnp.sum(jnp.where(kth, dest, 0.0), axis=0, keepdims=True))
        gks.append(jnp.sum(jnp.where(kth, gate, 0.0), axis=0, keepdims=True))
    dk = jnp.concatenate(dks, axis=0)
    dk_ref[...] = dk
    gk_ref[...] = jnp.concatenate(gks, axis=0)
    counts = _dot_nt(jnp.ones((8, tm), BF16), sel_b)
    cnt_ref[0] = jnp.concatenate([counts, jnp.zeros((8, LANES - N_EXPERTS), F32)], axis=1)


def _oproj(oa, ob, h0, woa, wob, g, b, wrt, rb, alpha, tm):
    n, d = h0.shape
    nt = n // tm
    row = lambda w: pl.BlockSpec((tm, w), lambda i: (i, 0))
    col = lambda: pl.BlockSpec((TOP_K, tm), lambda i: (0, i))
    full = lambda a: pl.BlockSpec(a.shape, lambda i: (0, 0))
    kt = jax.ShapeDtypeStruct((TOP_K, n), F32)
    return pl.pallas_call(
        functools.partial(_oproj_kernel, alpha=alpha),
        grid=(nt,),
        in_specs=[row(512), row(512), row(d), full(woa), full(wob), full(g), full(b),
                  full(wrt), full(rb)],
        out_specs=(row(d), row(d), col(), col(),
                   pl.BlockSpec((1, 8, LANES), lambda i: (i, 0, 0))),
        out_shape=(jax.ShapeDtypeStruct((n, d), F32),
                   jax.ShapeDtypeStruct((n, d), BF16),
                   kt,
                   kt,
                   jax.ShapeDtypeStruct((nt, 8, LANES), F32)),
        compiler_params=_cparams("parallel"),
    )(oa, ob, h0, woa, wob, g, b, wrt, rb)


def _swiglu(x, wg, wu, wd):
    hid = (jax.nn.silu(jnp.dot(x, wg, preferred_element_type=F32))
           * jnp.dot(x, wu, preferred_element_type=F32))
    return jnp.dot(hid.astype(BF16), wd, preferred_element_type=F32)


def _dispatch_kernel(tile_off_ref, exp_off_ref, len_ref, used_ref, gap_off_ref, gap_len_ref,
                     h1b_ref, dk_ref, xg_ref, xs_ref, zero_ref, run_sem, gap_sem, *, rb):
    i = pl.program_id(0)
    nt = pl.num_programs(0)
    rt = xs_ref.shape[1]
    slot = i % 2

    def wait_runs(step):
        n = pl.multiple_of(used_ref[step], ROW_ALIGN)
        s = step % 2
        pltpu.make_async_copy(xs_ref.at[s, pl.ds(0, n), :], xg_ref.at[pl.ds(0, n), :],
                              run_sem.at[s]).wait()

    @pl.when(i == 0)
    def _():
        zero_ref[...] = jnp.zeros(zero_ref.shape, zero_ref.dtype)

        def issue_gap(g, carry):
            n = pl.multiple_of(gap_len_ref[g], ROW_ALIGN)

            @pl.when(n > 0)
            def _():
                pltpu.make_async_copy(
                    zero_ref.at[pl.ds(0, n), :],
                    xg_ref.at[pl.ds(pl.multiple_of(gap_off_ref[g], ROW_ALIGN), n), :],
                    gap_sem).start()
            return carry

        lax.fori_loop(0, gap_len_ref.shape[0], issue_gap, 0)

    @pl.when(i >= 2)
    def _():
        wait_runs(i - 2)

    used = used_ref[i]
    dk = dk_ref[...]

    def sort_rows(r0):
        onehot = _scatter_rows(dk, None, r0, rb).astype(BF16)
        rows = jnp.dot(onehot, h1b_ref[...], preferred_element_type=F32)
        xs_ref[slot, r0:r0 + rb, :] = rows.astype(BF16)

    always = h1b_ref.shape[0] * TOP_K // rb * rb
    for r0 in range(0, always, rb):
        sort_rows(r0)
    for r0 in range(always, rt, rb):
        @pl.when(r0 < used)
        def _():
            sort_rows(r0)

    def issue_run(e, carry):
        r = i * N_EXPERTS + e
        n = pl.multiple_of(len_ref[r], ROW_ALIGN)

        @pl.when(n > 0)
        def _():
            pltpu.make_async_copy(
                xs_ref.at[slot, pl.ds(pl.multiple_of(tile_off_ref[r], ROW_ALIGN), n), :],
                xg_ref.at[pl.ds(pl.multiple_of(exp_off_ref[r], ROW_ALIGN), n), :],
                run_sem.at[slot]).start()
        return carry

    lax.fori_loop(0, N_EXPERTS, issue_run, 0)

    @pl.when(i == nt - 1)
    def _():
        @pl.when(i >= 1)
        def _():
            wait_runs(i - 1)
        wait_runs(i)

        def gap_rows(g, rows):
            return rows + gap_len_ref[g]
        n_gap = pl.multiple_of(lax.fori_loop(0, gap_len_ref.shape[0], gap_rows, 0), ROW_ALIGN)

        @pl.when(n_gap > 0)
        def _():
            pltpu.make_async_copy(xg_ref.at[pl.ds(0, n_gap), :], xg_ref.at[pl.ds(0, n_gap), :],
                                  gap_sem).wait()


def _dispatch(h1b, dk, tile_off, exp_off, lens, used, gap_off, gap_len, n_dst_rows, tm, rt):
    n, d = h1b.shape
    return pl.pallas_call(
        functools.partial(_dispatch_kernel, rb=512),
        grid_spec=pltpu.PrefetchScalarGridSpec(
            num_scalar_prefetch=6,
            grid=(n // tm,),
            in_specs=[pl.BlockSpec((tm, d), lambda i, *_: (i, 0)),
                      pl.BlockSpec((TOP_K, tm), lambda i, *_: (0, i))],
            out_specs=pl.BlockSpec(memory_space=pl.ANY),
            scratch_shapes=[pltpu.VMEM((2, rt, d), BF16), pltpu.VMEM((rt, d), BF16),
                            pltpu.SemaphoreType.DMA((2,)), pltpu.SemaphoreType.DMA(())]),
        out_shape=jax.ShapeDtypeStruct((n_dst_rows, d), BF16),
        compiler_params=_cparams("arbitrary"),
    )(tile_off, exp_off, lens, used, gap_off, gap_len, h1b, dk)


def _expert_kernel(blk_e_ref, blk_valid_ref, x_ref, wg_ref, wu_ref, wd_ref, y_ref):
    b = pl.program_id(0)
    valid = blk_valid_ref[b]

    @pl.when(valid > 0)
    def _():
        y_ref[...] = _swiglu(x_ref[...], wg_ref[...], wu_ref[...], wd_ref[...]).astype(BF16)

    @pl.when(valid <= 0)
    def _():
        y_ref[...] = jnp.zeros(y_ref.shape, BF16)


def _experts(xg, blk_e, blk_valid, wg, wu, wd):
    nr, d = xg.shape
    f = wg.shape[2]
    rows = pl.BlockSpec((MOE_BLOCK, d), lambda b, be, bv: (b, 0))
    return pl.pallas_call(
        _expert_kernel,
        grid_spec=pltpu.PrefetchScalarGridSpec(
            num_scalar_prefetch=2,
            grid=(nr // MOE_BLOCK,),
            in_specs=[rows,
                      pl.BlockSpec((None, d, f), lambda b, be, bv: (be[b], 0, 0)),
                      pl.BlockSpec((None, d, f), lambda b, be, bv: (be[b], 0, 0)),
                      pl.BlockSpec((None, f, d), lambda b, be, bv: (be[b], 0, 0))],
            out_specs=rows),
        out_shape=jax.ShapeDtypeStruct((nr, d), BF16),
        compiler_params=_cparams("parallel"),
    )(blk_e, blk_valid, xg, wg, wu, wd)


def _combine_kernel(tile_off_ref, exp_off_ref, len_ref, used_ref, yg_ref, dk_ref, gk_ref, h1_ref,
                    h1b_ref, p_ref, sg_ref, su_ref, sd_ref, g_ref, b_ref, wg_ref, bg_ref, wp_ref,
                    o_ref, ys_ref, ffn_ref, sem, *, alpha, rb):
    i = pl.program_id(0)
    nt = pl.num_programs(0)
    rt = ys_ref.shape[1]
    slot = i % 2

    def fetch_runs(tile):
        s = tile % 2

        def issue_run(e, carry):
            r = tile * N_EXPERTS + e
            n = pl.multiple_of(len_ref[r], ROW_ALIGN)

            @pl.when(n > 0)
            def _():
                pltpu.make_async_copy(
                    yg_ref.at[pl.ds(pl.multiple_of(exp_off_ref[r], ROW_ALIGN), n), :],
                    ys_ref.at[s, pl.ds(pl.multiple_of(tile_off_ref[r], ROW_ALIGN), n), :],
                    sem.at[s]).start()
            return carry

        lax.fori_loop(0, N_EXPERTS, issue_run, 0)

    @pl.when(i == 0)
    def _():
        ys_ref[...] = jnp.zeros(ys_ref.shape, ys_ref.dtype)
        fetch_runs(i)

    @pl.when(i + 1 < nt)
    def _():
        fetch_runs(i + 1)

    dk = dk_ref[...]
    gk = gk_ref[...]
    ffn = _swiglu(h1b_ref[...], sg_ref[...], su_ref[...], sd_ref[...])
    used = used_ref[i]
    n_used = pl.multiple_of(used, ROW_ALIGN)
    pltpu.make_async_copy(yg_ref.at[pl.ds(0, n_used), :], ys_ref.at[slot, pl.ds(0, n_used), :],
                          sem.at[slot]).wait()

    def routed(r0):
        weights = _scatter_rows(dk, gk, r0, rb).astype(BF16)
        return lax.dot_general(weights, ys_ref[slot, r0:r0 + rb, :], (((0,), (0,)), ((), ())),
                               preferred_element_type=F32)

    tm = h1_ref.shape[0]
    always = tm * TOP_K // rb * rb
    for r0 in range(0, always, rb):
        ffn = ffn + routed(r0)
    ffn_ref[...] = ffn
    for r0 in range(always, rt, rb):
        @pl.when(r0 < used)
        def _():
            ffn_ref[...] += routed(r0)
    h2 = _layer_norm(alpha * h1_ref[...] + ffn_ref[...], g_ref[...], b_ref[...])
    emb_gate = jax.nn.sigmoid(jnp.dot(h2.astype(BF16), wg_ref[...], preferred_element_type=F32)
                              + bg_ref[...])
    emb = jnp.dot(p_ref[...].astype(BF16), wp_ref[...], preferred_element_type=F32)
    o_ref[...] = h2 + emb_gate * emb


def _combine(tile_off, exp_off, lens, used, yg, dk, gk, h1, h1b, p2, sg, su, sd, g, b, wg, bg, wp,
             alpha, tm, rt):
    n, d = h1.shape
    row = lambda w: pl.BlockSpec((tm, w), lambda i, *_: (i, 0))
    col = lambda: pl.BlockSpec((TOP_K, tm), lambda i, *_: (0, i))
    full = lambda a: pl.BlockSpec(a.shape, lambda i, *_: (0, 0))
    return pl.pallas_call(
        functools.partial(_combine_kernel, alpha=alpha, rb=512),
        grid_spec=pltpu.PrefetchScalarGridSpec(
            num_scalar_prefetch=4,
            grid=(n // tm,),
            in_specs=[pl.BlockSpec(memory_space=pl.ANY), col(), col(), row(d),
                      row(d), row(p2.shape[1]), full(sg), full(su), full(sd), full(g), full(b),
                      full(wg), full(bg), full(wp)],
            out_specs=row(d),
            scratch_shapes=[pltpu.VMEM((2, rt, d), yg.dtype), pltpu.VMEM((tm, d), F32),
                            pltpu.SemaphoreType.DMA((2,))]),
        out_shape=jax.ShapeDtypeStruct((n, d), F32),
        compiler_params=_cparams("arbitrary"),
    )(tile_off, exp_off, lens, used, yg, dk, gk, h1, h1b, p2, sg, su, sd, g, b, wg, bg, wp)


def _tiles(seq):
    tm = min(256, seq)
    tq_sel = min(256, seq)
    tk_sel = min(512, seq)
    tq_att = min(256, seq)
    tk_att = min(512, seq)
    rt = -(-(tm * TOP_K + N_EXPERTS * (ROW_ALIGN - 1)) // 512) * 512
    return tm, tq_sel, tk_sel, tq_att, tk_att, rt


def kernel(x, p, positions, ln_emb_g, ln_emb_b, w_in, w_o, diff_lq1, diff_lk1, diff_lq2, diff_lk2,
           diff_subln_g, ln1_g, ln1_b, w_router, router_bias, w_exp_gate, w_exp_up, w_exp_down,
           w_sh_gate, w_sh_up, w_sh_down, ln2_g, ln2_b, w_ple_gate, b_ple_gate, w_ple_proj):
    bsz, seq, d = x.shape
    depth = w_in.shape[0]
    n = bsz * seq
    n_keep = min(IDX_TOPK_MAX, seq // 4)
    alpha = (2.0 * depth) ** 0.25
    tm, tq_sel, tk_sel, tq_att, tk_att, rt = _tiles(seq)
    n_blocks = -(-((n // tm) * (tm * TOP_K + N_EXPERTS * (ROW_ALIGN - 1))
                   + N_EXPERTS * (MOE_BLOCK - 1)) // MOE_BLOCK)
    row1 = lambda v: v.reshape(1, -1)

    inv = ROPE_THETA ** (-jnp.arange(0, ROPE_DIM, 2, dtype=F32) / ROPE_DIM)
    inv_lanes = jnp.tile(inv, LANES // (ROPE_DIM // 2)).reshape(1, LANES)
    pos2 = positions.reshape(n, 1)

    h = x.reshape(n, d)
    out = None
    for li in range(depth):
        lambda_init = 0.8 - 0.6 * math.exp(-0.3 * li)
        w = w_in[li]
        aqw, akw, avw = w[:, 0:512], w[:, 512:1024], w[:, 1024:1536]
        iqw, ikw, iww = w[:, 1536:2048], w[:, 2048:2112], w[:, 2112:2120]
        bqw, bkw, bvw = w[:, 2120:2632], w[:, 2632:3144], w[:, 3144:3656]
        w_rope = jnp.concatenate([aqw, akw, iqw, ikw, ikw, bqw, bkw], axis=1).astype(BF16)
        w_plain = jnp.concatenate([avw, bvw], axis=1).astype(BF16)
        w_idx_t = jnp.concatenate(
            [iww.T, jnp.zeros((LANES - IDX_HEADS, d), F32)], axis=0).astype(BF16)

        (h0, aq, ak, iq, ik2, bq, bk, av, bv, iw) = _proj(
            h, pos2, inv_lanes, row1(ln_emb_g), row1(ln_emb_b), w_rope, w_plain, w_idx_t, tm)
        assert depth == 1

        r3 = lambda a: a.reshape(bsz, seq, a.shape[-1])
        bias = _select(r3(iq), iw, r3(ik2), n_keep, tq_sel, tk_sel)
        out_a = _dsa(r3(aq), r3(ak), r3(av), bias, tq_att, tk_att)
        out_b = _diff(r3(bq), r3(bk), r3(bv), row1(diff_lq1[li]), row1(diff_lk1[li]),
                      row1(diff_lq2[li]), row1(diff_lk2[li]), row1(diff_subln_g[li]),
                      lambda_init, tq_att, tk_att)

        wo = w_o[li].astype(BF16)
        h1, h1b, dk, gk, cnt_out = _oproj(
            out_a.reshape(n, -1), out_b.reshape(n, -1), h0, wo[0:512], wo[512:1024],
            row1(ln1_g[li]), row1(ln1_b[li]), w_router[li].T, router_bias[li].reshape(-1, 1),
            alpha, tm)

        nt = n // tm
        c16 = jnp.ceil(cnt_out[:, 0, :N_EXPERTS] / ROW_ALIGN) * ROW_ALIGN
        tri = lambda m: jnp.tril(jnp.ones((m, m), F32))
        hi = lax.Precision.HIGHEST
        tile_off = jnp.dot(c16, tri(N_EXPERTS).T, precision=hi) - c16
        cum_tiles = jnp.dot(tri(nt), c16, precision=hi)
        tot = cum_tiles[-1]
        totp = jnp.ceil(tot / MOE_BLOCK) * MOE_BLOCK
        pend = jnp.dot(totp, tri(N_EXPERTS).T, precision=hi)
        ebase = pend - totp
        exp_off = ebase[None, :] + cum_tiles - c16
        blk0 = jnp.arange(n_blocks, dtype=F32) * MOE_BLOCK
        blk_e = jnp.minimum(jnp.sum(pend[None, :] <= blk0[:, None], axis=1), N_EXPERTS - 1)
        onehot_e = (blk_e[:, None] == jnp.arange(N_EXPERTS)[None, :]).astype(F32)
        blk_end = jnp.dot(onehot_e, ebase + tot, precision=hi)
        blk_valid = jnp.clip(blk_end - blk0, 0, MOE_BLOCK)
        flat = lambda a: a.reshape(-1).astype(jnp.int32)
        n_rows = n_blocks * MOE_BLOCK
        tail0 = pend[-1] + jnp.arange(-(-n_rows // rt), dtype=F32) * rt
        gap_off = jnp.concatenate([ebase + tot, tail0])
        gap_len = jnp.concatenate([totp - tot, jnp.clip(n_rows - tail0, 0, rt)])

        weg, weu, wed = (w_exp_gate[li].astype(BF16), w_exp_up[li].astype(BF16),
                         w_exp_down[li].astype(BF16))
        used = jnp.sum(c16, axis=1)
        xg = _dispatch(h1b, dk, flat(tile_off), flat(exp_off), flat(c16), flat(used),
                       flat(gap_off), flat(gap_len), n_rows, tm, rt)
        yg = _experts(xg, flat(blk_e), flat(blk_valid), weg, weu, wed)
        out = _combine(flat(tile_off), flat(exp_off), flat(c16), flat(used), yg, dk, gk, h1, h1b,
                       p[li].reshape(n, -1), w_sh_gate[li].astype(BF16),
                       w_sh_up[li].astype(BF16), w_sh_down[li].astype(BF16), row1(ln2_g[li]),
                       row1(ln2_b[li]), w_ple_gate[li].astype(BF16), row1(b_ple_gate[li]),
                       w_ple_proj[li].astype(BF16), alpha, tm, rt)
        h = out
    return out.reshape(bsz, seq, d)
```

```python
import functools
import math

import jax
import jax.numpy as jnp
from jax import lax
from jax.experimental import pallas as pl
from jax.experimental.pallas import tpu as pltpu

CHUNK = 64
HEAD_DIM = 64
ROPE_DIM = 64
ROPE_THETA = 10000.0
A_HEADS = 8
IDX_HEADS = 8
IDX_DIM = 64
IDX_TOPK_MAX = 256
B_HEADS = 4
B_QK_DIM = 64
B_V_DIM = 128
N_EXPERTS = 64
TOP_K = 8
N_GROUPS = 8
TOPK_GROUPS = 4
ROUTED_SCALE = 2.5
LN_EPS = 1e-5
RMS_EPS = 1e-5

LANES = 128
ROW_ALIGN = 16
MOE_BLOCK = 512
VMEM_LIMIT = 56 * 1024 * 1024
NEG_BIG = -1e30
INT_MIN = -(2 ** 31)
LOG2E = math.log2(math.e)
GROUPS = 256

F32 = jnp.float32
BF16 = jnp.bfloat16


def _cparams(*sem):
    return pltpu.CompilerParams(dimension_semantics=sem, vmem_limit_bytes=VMEM_LIMIT)


def _layer_norm(x, g, b):
    mu = jnp.mean(x, axis=-1, keepdims=True)
    xc = x - mu
    var = jnp.mean(xc * xc, axis=-1, keepdims=True)
    return xc * lax.rsqrt(var + LN_EPS) * g + b


def _dot_nt(a, b):
    return lax.dot_general(a, b, (((1,), (1,)), ((), ())), preferred_element_type=F32)


def _split_lane_halves(pair):
    lo = lax.broadcasted_iota(jnp.int32, pair.shape, 1) < (LANES // 2)
    pf = pair.astype(F32)
    return (jnp.where(lo, pf, 0.0).astype(pair.dtype), jnp.where(lo, 0.0, pf).astype(pair.dtype))


def _proj_kernel(x_ref, pos_ref, inv_ref, g_ref, b_ref, wr_ref, wp_ref, wi_ref,
                 h0_ref, aq_ref, ak_ref, iq_ref, ik_ref, bq_ref, bk_ref,
                 av_ref, bv_ref, iw_ref):
    hn = _layer_norm(x_ref[...], g_ref[...], b_ref[...])
    h0_ref[...] = hn
    hb = hn.astype(BF16)

    tm = x_ref.shape[0]
    nq = LANES // (ROPE_DIM // 2)
    tr = tm // nq
    pos = pos_ref[...].astype(F32)
    lane = lax.broadcasted_iota(jnp.int32, (tr, LANES), 1)
    blk = lane // (ROPE_DIM // 2)
    pos_packed = jnp.broadcast_to(pos[0:tr], (tr, LANES))
    for q in range(1, nq):
        pos_packed = jnp.where(blk == q, pos[q * tr:(q + 1) * tr], pos_packed)
    ang = pos_packed * inv_ref[...]
    cos_packed = jnp.cos(ang)
    sin_packed = jnp.sin(ang)
    first_half = (lane % ROPE_DIM) < (ROPE_DIM // 2)

    def spread(table, q):
        own = jnp.where(blk == q, table, 0.0)
        t = own
        for s in range(1, nq):
            t = t + pltpu.roll(own, s * (ROPE_DIM // 2), axis=1)
        return t

    pr = jnp.dot(hb, wr_ref[...], preferred_element_type=F32)
    outs = ((aq_ref, 0, 4, HEAD_DIM ** -0.5 * LOG2E), (ak_ref, 4, 4, 1.0), (iq_ref, 8, 4, 1.0),
            (ik_ref, 12, 1, 1.0), (bq_ref, 13, 4, B_QK_DIM ** -0.5 * LOG2E), (bk_ref, 17, 4, 1.0))
    for q in range(nq):
        cos = spread(cos_packed, q)
        sin = spread(sin_packed, q)
        sin_signed = jnp.where(first_half, -sin, sin)
        rows = slice(q * tr, (q + 1) * tr)
        for ref, g0, ng, scale in outs:
            for j in range(ng):
                v = pr[rows, (g0 + j) * LANES:(g0 + j + 1) * LANES]
                partner = jnp.where(first_half,
                                    pltpu.roll(v, LANES - ROPE_DIM // 2, axis=1),
                                    pltpu.roll(v, ROPE_DIM // 2, axis=1))
                v = v * cos + partner * sin_signed
                if scale != 1.0:
                    v = v * scale
                ref[rows, j * LANES:(j + 1) * LANES] = v.astype(ref.dtype)

    pp = jnp.dot(hb, wp_ref[...], preferred_element_type=F32)
    av_ref[...] = pp[:, 0:512].astype(BF16)
    bv_ref[...] = pp[:, 512:1024].astype(BF16)
    iw_ref[...] = _dot_nt(wi_ref[...], hb)[0:IDX_HEADS, :]


def _proj(x2, pos2, inv, g, b, w_rope, w_plain, w_idx_t, tm):
    n, d = x2.shape
    nr = w_rope.shape[1]
    npl = w_plain.shape[1]
    row = lambda w: pl.BlockSpec((tm, w), lambda i: (i, 0))
    full = lambda a: pl.BlockSpec(a.shape, lambda i: (0, 0))
    out_shape = (
        jax.ShapeDtypeStruct((n, d), F32),
        jax.ShapeDtypeStruct((n, 512), BF16),
        jax.ShapeDtypeStruct((n, 512), BF16),
        jax.ShapeDtypeStruct((n, 512), BF16),
        jax.ShapeDtypeStruct((n, 128), BF16),
        jax.ShapeDtypeStruct((n, 512), BF16),
        jax.ShapeDtypeStruct((n, 512), BF16),
        jax.ShapeDtypeStruct((n, 512), BF16),
        jax.ShapeDtypeStruct((n, 512), BF16),
        jax.ShapeDtypeStruct((IDX_HEADS, n), F32),
    )
    return pl.pallas_call(
        _proj_kernel,
        grid=(n // tm,),
        in_specs=[row(d), row(1), full(inv), full(g), full(b), full(w_rope), full(w_plain),
                  full(w_idx_t)],
        out_specs=(row(d), row(512), row(512), row(512), row(128), row(512), row(512),
                   row(512), row(512), pl.BlockSpec((IDX_HEADS, tm), lambda i: (0, i))),
        out_shape=out_shape,
        compiler_params=_cparams("parallel"),
    )(x2, pos2, inv, g, b, w_rope, w_plain, w_idx_t)


def _sortable(s):
    i = lax.bitcast_convert_type(s, jnp.int32)
    return i ^ ((i >> 31) & 0x7FFFFFFF)


def _select_kernel(iq_ref, iw_ref, ik_ref, bias_ref, keys_ref, gmax_ref, *, tq, tk, n_keep,
                   w_scale):
    i = pl.program_id(1)
    seq = ik_ref.shape[1]
    n_t = ((i + 1) * tq + tk - 1) // tk

    q_chunk = (i * tq + lax.broadcasted_iota(jnp.int32, (tk, tq), 1)) // CHUNK
    k_row = lax.broadcasted_iota(jnp.int32, (tk, tq), 0)
    wb = iw_ref[...] * w_scale
    q_halves = []
    for hp in range(IDX_HEADS // 2):
        q_halves.extend(_split_lane_halves(iq_ref[0, :, hp * LANES:(hp + 1) * LANES]))

    def score_tile(t, carry, boundary):
        off = pl.multiple_of(t * tk, tk)
        kt = ik_ref[0, pl.ds(off, tk), :]
        sc = jnp.zeros((tk, tq), F32)
        for h in range(IDX_HEADS):
            rel = jnp.maximum(_dot_nt(kt, q_halves[h]), 0.0)
            sc = sc + wb[h:h + 1, :] * rel
        key = _sortable(sc + 0.0)
        if boundary:
            key = jnp.where(((off + k_row) // CHUNK) <= q_chunk, key, INT_MIN)
        keys_ref[pl.ds(off, tk), :] = key
        gm = key[0:GROUPS, :]
        for g0 in range(GROUPS, tk, GROUPS):
            gm = jnp.maximum(gm, key[g0:g0 + GROUPS, :])
        gmax_ref[...] = jnp.maximum(gmax_ref[...], gm)
        return carry

    gmax_ref[...] = jnp.full(gmax_ref.shape, INT_MIN, jnp.int32)
    n_full = (i * tq + CHUNK) // tk
    lax.fori_loop(0, n_full, functools.partial(score_tile, boundary=False), 0)
    lax.fori_loop(n_full, n_t, functools.partial(score_tile, boundary=True), 0)

    def count_ge(cand):
        def body(t, acc):
            off = pl.multiple_of(t * tk, tk)
            m = (keys_ref[pl.ds(off, tk), :] >= cand).astype(jnp.int32)
            return acc + jnp.sum(m.reshape(tk // 8, 8, tq), axis=0)
        acc = lax.fori_loop(0, n_t, body, jnp.zeros((8, tq), jnp.int32))
        return jnp.sum(acc, axis=0, keepdims=True)

    gmax = gmax_ref[...]
    n_adm = ((i * tq + lax.broadcasted_iota(jnp.int32, (1, tq), 1)) // CHUNK + 1) * CHUNK
    few = n_adm <= n_keep
    lo0 = jnp.where(few, INT_MIN + 1, jnp.min(gmax, axis=0, keepdims=True))
    hi0 = jnp.where(few, INT_MIN + 2, jnp.max(gmax, axis=0, keepdims=True) + 1)

    def probe(state):
        lo, hi, thr, n_thr, done, _, it = state
        mid = (lo >> 1) + (hi >> 1) + (lo & hi & 1)
        closed = mid == lo
        cnt = count_ge(mid)
        exact = cnt == n_keep
        up = cnt >= n_keep
        live = done == 0
        settle = live & (exact | closed)
        thr = jnp.where(settle, mid, thr)
        n_thr = jnp.where(settle, cnt, n_thr)
        lo = jnp.where(live & up, mid, lo)
        hi = jnp.where(live & jnp.logical_not(up), mid, hi)
        done = jnp.where(exact | closed, 1, done)
        return lo, hi, thr, n_thr, done, jnp.min(done), it + 1

    def searching(state):
        return (state[5] == 0) & (state[6] < 34)

    done0 = few.astype(jnp.int32)
    state = lax.while_loop(
        searching, probe,
        (lo0, hi0, lo0, jnp.zeros((1, tq), jnp.int32), done0, jnp.min(done0), jnp.int32(0)))
    thr, n_ge = state[2], state[3]

    tied = n_ge > n_keep
    any_tied = jnp.max(tied.astype(jnp.int32)) > 0
    tau = thr

    def write_bias(off, keep_t):
        bias_t = jnp.where(keep_t, 0.0, NEG_BIG)
        bias_ref[0, :, pl.ds(off, tk)] = bias_t.T.astype(BF16)

    @pl.when(jnp.logical_not(any_tied))
    def _():
        def write_tile(t, carry):
            off = pl.multiple_of(t * tk, tk)
            write_bias(off, keys_ref[pl.ds(off, tk), :] >= tau)
            return carry
        lax.fori_loop(0, n_t, write_tile, 0)

    @pl.when(any_tied)
    def _():
        room = (n_keep - count_ge(tau + 1)).astype(F32)
        upto = (lax.broadcasted_iota(jnp.int32, (tk, tk), 1)
                <= lax.broadcasted_iota(jnp.int32, (tk, tk), 0)).astype(F32).astype(BF16)

        def write_tile(t, seen_eq):
            off = pl.multiple_of(t * tk, tk)
            key = keys_ref[pl.ds(off, tk), :]
            eq = (key == tau) & tied
            eq_f = jnp.where(eq, 1.0, 0.0)
            eq_rank = seen_eq + jnp.dot(upto, eq_f.astype(BF16), preferred_element_type=F32)
            write_bias(off, (key > tau) | (eq & (eq_rank <= room))
                       | ((key == tau) & jnp.logical_not(tied)))
            return seen_eq + jnp.sum(eq_f, axis=0, keepdims=True)

        lax.fori_loop(0, n_t, write_tile, jnp.zeros((1, tq), F32))

    def blank_tile(t, carry):
        off = pl.multiple_of(t * tk, tk)
        bias_ref[0, :, pl.ds(off, tk)] = jnp.full((tq, tk), NEG_BIG, BF16)
        return carry

    lax.fori_loop(n_t, seq // tk, blank_tile, 0)


def _select(iq, iw, ik2, n_keep, tq, tk):
    bsz, seq, _ = iq.shape
    kern = functools.partial(_select_kernel, tq=tq, tk=tk, n_keep=n_keep,
                             w_scale=(IDX_HEADS ** -0.5) * (IDX_DIM ** -0.5))
    return pl.pallas_call(
        kern,
        grid=(bsz, seq // tq),
        in_specs=[pl.BlockSpec((1, tq, 512), lambda b, i: (b, i, 0)),
                  pl.BlockSpec((IDX_HEADS, tq), lambda b, i: (0, b * (seq // tq) + i)),
                  pl.BlockSpec((1, seq, 128), lambda b, i: (b, 0, 0))],
        out_specs=pl.BlockSpec((1, tq, seq), lambda b, i: (b, i, 0)),
        out_shape=jax.ShapeDtypeStruct((bsz, seq, seq), BF16),
        scratch_shapes=[pltpu.VMEM((seq, tq), jnp.int32), pltpu.VMEM((GROUPS, tq), jnp.int32)],
        compiler_params=_cparams("parallel", "parallel"),
    )(iq, iw, ik2)


def _flash_update(j, s, v_ext, m_ref, acc_ref):
    tk = s.shape[1]
    m_old = m_ref[j]
    m_new = jnp.maximum(m_old, jnp.max(s, axis=1, keepdims=True))
    alpha = jnp.exp2(m_old - m_new)
    p = jnp.exp2(s - jnp.concatenate([m_new] * (tk // LANES), axis=1))
    m_ref[j] = m_new
    acc_ref[j] = (acc_ref[j] * jnp.concatenate([alpha, alpha], axis=1)
                  + jnp.dot(p.astype(BF16), v_ext, preferred_element_type=F32))


def _dsa_kernel(q_ref, k_ref, v_ref, bias_ref, o_ref, m_ref, acc_ref, *, tq, tk):
    i = pl.program_id(1)
    n_keys = (i + 1) * tq
    lo_q = lax.broadcasted_iota(jnp.int32, (tq, LANES), 1) < HEAD_DIM
    q_halves = []
    for hp in range(A_HEADS // 2):
        q_halves.extend(_split_lane_halves(q_ref[0, :, hp * LANES:(hp + 1) * LANES]))

    m_ref[...] = jnp.full(m_ref.shape, NEG_BIG, F32)
    acc_ref[...] = jnp.zeros(acc_ref.shape, F32)

    def kv_tile(off, width):
        bias = bias_ref[0, :, pl.ds(off, width)].astype(F32)
        ones = jnp.ones((width, LANES), BF16)
        for hp in range(A_HEADS // 2):
            kp = k_ref[0, pl.ds(off, width), hp * LANES:(hp + 1) * LANES]
            v_ext = jnp.concatenate(
                [v_ref[0, pl.ds(off, width), hp * LANES:(hp + 1) * LANES], ones], axis=1)
            for half in range(2):
                h = 2 * hp + half
                _flash_update(h, _dot_nt(q_halves[h], kp) + bias, v_ext, m_ref, acc_ref)

    def kv_step(t, carry):
        kv_tile(pl.multiple_of(t * tk, tk), tk)
        return carry

    n_whole = n_keys // tk
    lax.fori_loop(0, n_whole, kv_step, 0)
    if tk > tq:
        @pl.when(n_keys % tk != 0)
        def _():
            kv_tile(pl.multiple_of(n_whole * tk, tq), tq)

    for hp in range(A_HEADS // 2):
        a0 = acc_ref[2 * hp]
        a1 = acc_ref[2 * hp + 1]
        o = jnp.where(lo_q, a0[:, :LANES] / a0[:, LANES:], a1[:, :LANES] / a1[:, LANES:])
        o_ref[0, :, hp * LANES:(hp + 1) * LANES] = o.astype(o_ref.dtype)


def _dsa(aq, ak, av, bias, tq, tk):
    bsz, seq, w = aq.shape
    kern = functools.partial(_dsa_kernel, tq=tq, tk=tk)
    resident = lambda: pl.BlockSpec((1, seq, w), lambda b, i: (b, 0, 0),
                                    pipeline_mode=pl.Buffered(1))
    return pl.pallas_call(
        kern,
        grid=(bsz, seq // tq),
        in_specs=[pl.BlockSpec((1, tq, w), lambda b, i: (b, i, 0)),
                  resident(), resident(),
                  pl.BlockSpec((1, tq, seq), lambda b, i: (b, i, 0))],
        out_specs=pl.BlockSpec((1, tq, w), lambda b, i: (b, i, 0)),
        out_shape=jax.ShapeDtypeStruct((bsz, seq, w), BF16),
        scratch_shapes=[pltpu.VMEM((A_HEADS, tq, LANES), F32),
                        pltpu.VMEM((A_HEADS, tq, 2 * LANES), F32)],
        compiler_params=_cparams("parallel", "arbitrary"),
    )(aq, ak, av, bias)


def _diff_kernel(q_ref, k_ref, v_ref, lq1_ref, lk1_ref, lq2_ref, lk2_ref, g_ref, o_ref,
                 m_ref, acc_ref, *, tq, tk, lambda_init):
    i = pl.program_id(1)
    q_halves = []
    for hb in range(B_HEADS):
        q_halves.extend(_split_lane_halves(q_ref[0, :, hb * LANES:(hb + 1) * LANES]))
    lam = (jnp.exp(jnp.sum(lq1_ref[...] * lk1_ref[...], keepdims=True))
           - jnp.exp(jnp.sum(lq2_ref[...] * lk2_ref[...], keepdims=True)) + lambda_init)

    m_ref[...] = jnp.full(m_ref.shape, NEG_BIG, F32)
    acc_ref[...] = jnp.zeros(acc_ref.shape, F32)

    def tile(off, width, masked):
        ones = jnp.ones((width, LANES), BF16)
        if masked:
            row_chunk = (i * tq + lax.broadcasted_iota(jnp.int32, (tq, width), 0)) // CHUNK
            col_chunk = (off + lax.broadcasted_iota(jnp.int32, (tq, width), 1)) // CHUNK
            ok = col_chunk <= row_chunk
        for hb in range(B_HEADS):
            kp = k_ref[0, pl.ds(off, width), hb * LANES:(hb + 1) * LANES]
            v_ext = jnp.concatenate(
                [v_ref[0, pl.ds(off, width), hb * LANES:(hb + 1) * LANES], ones], axis=1)
            for mp in range(2):
                j = 2 * hb + mp
                s = _dot_nt(q_halves[j], kp)
                if masked:
                    s = jnp.where(ok, s, NEG_BIG)
                _flash_update(j, s, v_ext, m_ref, acc_ref)

    def kv_step(t, carry):
        tile(pl.multiple_of(t * tk, tk), tk, False)
        return carry

    n_whole = (i * tq) // tk
    lax.fori_loop(0, n_whole, kv_step, 0)
    if tk > tq:
        @pl.when((i * tq) % tk != 0)
        def _():
            tile(pl.multiple_of(n_whole * tk, tq), tq, False)
    tile(pl.multiple_of(i * tq, tq), tq, True)

    for hb in range(B_HEADS):
        a1 = acc_ref[2 * hb]
        a2 = acc_ref[2 * hb + 1]
        o = a1[:, :LANES] / a1[:, LANES:] - lam * (a2[:, :LANES] / a2[:, LANES:])
        o = o * lax.rsqrt(jnp.mean(o * o, axis=-1, keepdims=True) + RMS_EPS)
        o = o * g_ref[...] * (1.0 - lambda_init)
        o_ref[0, :, hb * LANES:(hb + 1) * LANES] = o.astype(o_ref.dtype)


def _diff(bq, bk, bv, lq1, lk1, lq2, lk2, subln_g, lambda_init, tq, tk):
    bsz, seq, w = bq.shape
    kern = functools.partial(_diff_kernel, tq=tq, tk=tk, lambda_init=lambda_init)
    resident = lambda: pl.BlockSpec((1, seq, w), lambda b, i: (b, 0, 0),
                                    pipeline_mode=pl.Buffered(1))
    small = lambda a: pl.BlockSpec(a.shape, lambda b, i: (0, 0))
    return pl.pallas_call(
        kern,
        grid=(bsz, seq // tq),
        in_specs=[pl.BlockSpec((1, tq, w), lambda b, i: (b, i, 0)),
                  resident(), resident(),
                  small(lq1), small(lk1), small(lq2), small(lk2), small(subln_g)],
        out_specs=pl.BlockSpec((1, tq, w), lambda b, i: (b, i, 0)),
        out_shape=jax.ShapeDtypeStruct((bsz, seq, w), BF16),
        scratch_shapes=[pltpu.VMEM((2 * B_HEADS, tq, LANES), F32),
                        pltpu.VMEM((2 * B_HEADS, tq, 2 * LANES), F32)],
        compiler_params=_cparams("parallel", "arbitrary"),
    )(bq, bk, bv, lq1, lk1, lq2, lk2, subln_g)


def _first_argmax_rows(v, row):
    m = jnp.max(v, axis=0, keepdims=True)
    idx = jnp.min(jnp.where(v == m, row, v.shape[0]), axis=0, keepdims=True)
    return m, idx


def _scatter_rows(dk, gk, r0, nrows):
    r_iota = (lax.broadcasted_iota(jnp.int32, (nrows, dk.shape[1]), 0) + r0).astype(F32)
    out = jnp.zeros(r_iota.shape, F32)
    for k in range(dk.shape[0]):
        out = jnp.where(r_iota == dk[k:k + 1, :], 1.0 if gk is None else gk[k:k + 1, :], out)
    return out


def _oproj_kernel(oa_ref, ob_ref, h0_ref, woa_ref, wob_ref, g_ref, b_ref, wrt_ref, rb_ref,
                  h1_ref, h1b_ref, dk_ref, gk_ref, cnt_ref, *, alpha):
    mix = (jnp.dot(oa_ref[...], woa_ref[...], preferred_element_type=F32)
           + jnp.dot(ob_ref[...], wob_ref[...], preferred_element_type=F32))
    h1 = _layer_norm(alpha * h0_ref[...] + mix, g_ref[...], b_ref[...])
    h1_ref[...] = h1
    h1b = h1.astype(BF16)
    h1b_ref[...] = h1b

    logits = lax.dot_general(wrt_ref[...], h1, (((1,), (1,)), ((), ())),
                             precision=lax.Precision.HIGHEST, preferred_element_type=F32)
    scores = jax.nn.sigmoid(logits)
    biased = scores + rb_ref[...]
    tm = scores.shape[1]
    per_g = N_EXPERTS // N_GROUPS
    row8 = lax.broadcasted_iota(jnp.int32, (per_g, tm), 0)

    gs = []
    for g in range(N_GROUPS):
        blk = biased[g * per_g:(g + 1) * per_g, :]
        m1, i1 = _first_argmax_rows(blk, row8)
        m2 = jnp.max(jnp.where(row8 == i1, -jnp.inf, blk), axis=0, keepdims=True)
        gs.append(m1 + m2)
    gscore = jnp.concatenate(gs, axis=0)
    rowg = lax.broadcasted_iota(jnp.int32, (N_GROUPS, tm), 0)
    gsel = jnp.zeros((N_GROUPS, tm), F32)
    for _ in range(TOPK_GROUPS):
        _, ig = _first_argmax_rows(jnp.where(gsel > 0.0, -jnp.inf, gscore), rowg)
        gsel = jnp.where(rowg == ig, 1.0, gsel)

    rowe = lax.broadcasted_iota(jnp.int32, (N_EXPERTS, tm), 0)
    live = jnp.concatenate(
        [jnp.broadcast_to(gsel[g:g + 1, :], (per_g, tm)) for g in range(N_GROUPS)], axis=0)
    esel = jnp.zeros((N_EXPERTS, tm), F32)
    for _ in range(TOP_K):
        cand = jnp.where(live > 0.0, biased, -jnp.inf)
        m = jnp.max(cand, axis=0, keepdims=True)
        idx = jnp.min(jnp.where((live > 0.0) & (cand == m), rowe, N_EXPERTS),
                      axis=0, keepdims=True)
        hit = rowe == idx
        esel = jnp.where(hit, 1.0, esel)
        live = jnp.where(hit, 0.0, live)
    picked = jnp.where(esel > 0.0, scores, 0.0)
    denom = jnp.sum(picked, axis=0, keepdims=True)
    gate = picked / denom * ROUTED_SCALE

    sel_b = esel.astype(BF16)
    before_t = (lax.broadcasted_iota(jnp.int32, (tm, tm), 0)
                < lax.broadcasted_iota(jnp.int32, (tm, tm), 1)).astype(F32).astype(BF16)
    before_e = (lax.broadcasted_iota(jnp.int32, (N_EXPERTS, N_EXPERTS), 1)
                < lax.broadcasted_iota(jnp.int32, (N_EXPERTS, N_EXPERTS), 0)).astype(F32).astype(BF16)
    rank = jnp.dot(sel_b, before_t, preferred_element_type=F32)
    order = jnp.dot(before_e, sel_b, preferred_element_type=F32)
    cnt = jnp.sum(esel, axis=1, keepdims=True)
    cnt16 = jnp.floor((cnt + (ROW_ALIGN - 1)) / ROW_ALIGN) * ROW_ALIGN
    off = jnp.dot(before_e, jnp.broadcast_to(cnt16, (N_EXPERTS, tm)).astype(BF16),
                  preferred_element_type=F32)
    dest = off + rank
    dks, gks = [], []
    for k in range(TOP_K):
        kth = (esel > 0.0) & (order == k)
        dks.append(jnp.sum(jnp.where(kth, dest, 0.0), axis=0, keepdims=True))
        gks.append(jnp.sum(jnp.where(kth, gate, 0.0), axis=0, keepdims=True))
    dk = jnp.concatenate(dks, axis=0)
    dk_ref[...] = dk
    gk_ref[...] = jnp.concatenate(gks, axis=0)
    counts = _dot_nt(jnp.ones((8, tm), BF16), sel_b)
    cnt_ref[0] = jnp.concatenate([counts, jnp.zeros((8, LANES - N_EXPERTS), F32)], axis=1)


def _oproj(oa, ob, h0, woa, wob, g, b, wrt, rb, alpha, tm):
    n, d = h0.shape
    nt = n // tm
    row = lambda w: pl.BlockSpec((tm, w), lambda i: (i, 0))
    col = lambda: pl.BlockSpec((TOP_K, tm), lambda i: (0, i))
    full = lambda a: pl.BlockSpec(a.shape, lambda i: (0, 0))
    kt = jax.ShapeDtypeStruct((TOP_K, n), F32)
    return pl.pallas_call(
        functools.partial(_oproj_kernel, alpha=alpha),
        grid=(nt,),
        in_specs=[row(512), row(512), row(d), full(woa), full(wob), full(g), full(b),
                  full(wrt), full(rb)],
        out_specs=(row(d), row(d), col(), col(),
                   pl.BlockSpec((1, 8, LANES), lambda i: (i, 0, 0))),
        out_shape=(jax.ShapeDtypeStruct((n, d), F32),
                   jax.ShapeDtypeStruct((n, d), BF16),
                   kt,
                   kt,
                   jax.ShapeDtypeStruct((nt, 8, LANES), F32)),
        compiler_params=_cparams("parallel"),
    )(oa, ob, h0, woa, wob, g, b, wrt, rb)


def _swiglu(x, wg, wu, wd):
    hid = (jax.nn.silu(jnp.dot(x, wg, preferred_element_type=F32))
           * jnp.dot(x, wu, preferred_element_type=F32))
    return jnp.dot(hid.astype(BF16), wd, preferred_element_type=F32)


def _dispatch_kernel(tile_off_ref, exp_off_ref, len_ref, used_ref, gap_off_ref, gap_len_ref,
                     h1b_ref, dk_ref, xg_ref, xs_ref, zero_ref, run_sem, gap_sem, *, rb):
    i = pl.program_id(0)
    nt = pl.num_programs(0)
    rt = xs_ref.shape[1]
    slot = i % 2

    def wait_runs(step):
        n = pl.multiple_of(used_ref[step], ROW_ALIGN)
        s = step % 2
        pltpu.make_async_copy(xs_ref.at[s, pl.ds(0, n), :], xg_ref.at[pl.ds(0, n), :],
                              run_sem.at[s]).wait()

    @pl.when(i == 0)
    def _():
        zero_ref[...] = jnp.zeros(zero_ref.shape, zero_ref.dtype)

        def issue_gap(g, carry):
            n = pl.multiple_of(gap_len_ref[g], ROW_ALIGN)

            @pl.when(n > 0)
            def _():
                pltpu.make_async_copy(
                    zero_ref.at[pl.ds(0, n), :],
                    xg_ref.at[pl.ds(pl.multiple_of(gap_off_ref[g], ROW_ALIGN), n), :],
                    gap_sem).start()
            return carry

        lax.fori_loop(0, gap_len_ref.shape[0], issue_gap, 0)

    @pl.when(i >= 2)
    def _():
        wait_runs(i - 2)

    used = used_ref[i]
    dk = dk_ref[...]

    def sort_rows(r0):
        onehot = _scatter_rows(dk, None, r0, rb).astype(BF16)
        rows = jnp.dot(onehot, h1b_ref[...], preferred_element_type=F32)
        xs_ref[slot, r0:r0 + rb, :] = rows.astype(BF16)

    always = h1b_ref.shape[0] * TOP_K // rb * rb
    for r0 in range(0, always, rb):
        sort_rows(r0)
    for r0 in range(always, rt, rb):
        @pl.when(r0 < used)
        def _():
            sort_rows(r0)

    def issue_run(e, carry):
        r = i * N_EXPERTS + e
        n = pl.multiple_of(len_ref[r], ROW_ALIGN)

        @pl.when(n > 0)
        def _():
            pltpu.make_async_copy(
                xs_ref.at[slot, pl.ds(pl.multiple_of(tile_off_ref[r], ROW_ALIGN), n), :],
                xg_ref.at[pl.ds(pl.multiple_of(exp_off_ref[r], ROW_ALIGN), n), :],
                run_sem.at[slot]).start()
        return carry

    lax.fori_loop(0, N_EXPERTS, issue_run, 0)

    @pl.when(i == nt - 1)
    def _():
        @pl.when(i >= 1)
        def _():
            wait_runs(i - 1)
        wait_runs(i)

        def gap_rows(g, rows):
            return rows + gap_len_ref[g]
        n_gap = pl.multiple_of(lax.fori_loop(0, gap_len_ref.shape[0], gap_rows, 0), ROW_ALIGN)

        @pl.when(n_gap > 0)
        def _():
            pltpu.make_async_copy(xg_ref.at[pl.ds(0, n_gap), :], xg_ref.at[pl.ds(0, n_gap), :],
                                  gap_sem).wait()


def _dispatch(h1b, dk, tile_off, exp_off, lens, used, gap_off, gap_len, n_dst_rows, tm, rt):
    n, d = h1b.shape
    return pl.pallas_call(
        functools.partial(_dispatch_kernel, rb=512),
        grid_spec=pltpu.PrefetchScalarGridSpec(
            num_scalar_prefetch=6,
            grid=(n // tm,),
            in_specs=[pl.BlockSpec((tm, d), lambda i, *_: (i, 0)),
                      pl.BlockSpec((TOP_K, tm), lambda i, *_: (0, i))],
            out_specs=pl.BlockSpec(memory_space=pl.ANY),
            scratch_shapes=[pltpu.VMEM((2, rt, d), BF16), pltpu.VMEM((rt, d), BF16),
                            pltpu.SemaphoreType.DMA((2,)), pltpu.SemaphoreType.DMA(())]),
        out_shape=jax.ShapeDtypeStruct((n_dst_rows, d), BF16),
        compiler_params=_cparams("arbitrary"),
    )(tile_off, exp_off, lens, used, gap_off, gap_len, h1b, dk)


def _expert_kernel(blk_e_ref, blk_valid_ref, x_ref, wg_ref, wu_ref, wd_ref, y_ref):
    b = pl.program_id(0)
    valid = blk_valid_ref[b]

    @pl.when(valid > 0)
    def _():
        y_ref[...] = _swiglu(x_ref[...], wg_ref[...], wu_ref[...], wd_ref[...]).astype(BF16)

    @pl.when(valid <= 0)
    def _():
        y_ref[...] = jnp.zeros(y_ref.shape, BF16)


def _experts(xg, blk_e, blk_valid, wg, wu, wd):
    nr, d = xg.shape
    f = wg.shape[2]
    rows = pl.BlockSpec((MOE_BLOCK, d), lambda b, be, bv: (b, 0))
    return pl.pallas_call(
        _expert_kernel,
        grid_spec=pltpu.PrefetchScalarGridSpec(
            num_scalar_prefetch=2,
            grid=(nr // MOE_BLOCK,),
            in_specs=[rows,
                      pl.BlockSpec((None, d, f), lambda b, be, bv: (be[b], 0, 0)),
                      pl.BlockSpec((None, d, f), lambda b, be, bv: (be[b], 0, 0)),
                      pl.BlockSpec((None, f, d), lambda b, be, bv: (be[b], 0, 0))],
            out_specs=rows),
        out_shape=jax.ShapeDtypeStruct((nr, d), BF16),
        compiler_params=_cparams("parallel"),
    )(blk_e, blk_valid, xg, wg, wu, wd)


def _combine_kernel(tile_off_ref, exp_off_ref, len_ref, used_ref, yg_ref, dk_ref, gk_ref, h1_ref,
                    h1b_ref, p_ref, sg_ref, su_ref, sd_ref, g_ref, b_ref, wg_ref, bg_ref, wp_ref,
                    o_ref, ys_ref, ffn_ref, sem, *, alpha, rb):
    i = pl.program_id(0)
    nt = pl.num_programs(0)
    rt = ys_ref.shape[1]
    slot = i % 2

    def fetch_runs(tile):
        s = tile % 2

        def issue_run(e, carry):
            r = tile * N_EXPERTS + e
            n = pl.multiple_of(len_ref[r], ROW_ALIGN)

            @pl.when(n > 0)
            def _():
                pltpu.make_async_copy(
                    yg_ref.at[pl.ds(pl.multiple_of(exp_off_ref[r], ROW_ALIGN), n), :],
                    ys_ref.at[s, pl.ds(pl.multiple_of(tile_off_ref[r], ROW_ALIGN), n), :],
                    sem.at[s]).start()
            return carry

        lax.fori_loop(0, N_EXPERTS, issue_run, 0)

    @pl.when(i == 0)
    def _():
        ys_ref[...] = jnp.zeros(ys_ref.shape, ys_ref.dtype)
        fetch_runs(i)

    @pl.when(i + 1 < nt)
    def _():
        fetch_runs(i + 1)

    dk = dk_ref[...]
    gk = gk_ref[...]
    ffn = _swiglu(h1b_ref[...], sg_ref[...], su_ref[...], sd_ref[...])
    used = used_ref[i]
    n_used = pl.multiple_of(used, ROW_ALIGN)
    pltpu.make_async_copy(yg_ref.at[pl.ds(0, n_used), :], ys_ref.at[slot, pl.ds(0, n_used), :],
                          sem.at[slot]).wait()

    def routed(r0):
        weights = _scatter_rows(dk, gk, r0, rb).astype(BF16)
        return lax.dot_general(weights, ys_ref[slot, r0:r0 + rb, :], (((0,), (0,)), ((), ())),
                               preferred_element_type=F32)

    tm = h1_ref.shape[0]
    always = tm * TOP_K // rb * rb
    for r0 in range(0, always, rb):
        ffn = ffn + routed(r0)
    ffn_ref[...] = ffn
    for r0 in range(always, rt, rb):
        @pl.when(r0 < used)
        def _():
            ffn_ref[...] += routed(r0)
    h2 = _layer_norm(alpha * h1_ref[...] + ffn_ref[...], g_ref[...], b_ref[...])
    emb_gate = jax.nn.sigmoid(jnp.dot(h2.astype(BF16), wg_ref[...], preferred_element_type=F32)
                              + bg_ref[...])
    emb = jnp.dot(p_ref[...].astype(BF16), wp_ref[...], preferred_element_type=F32)
    o_ref[...] = h2 + emb_gate * emb


def _combine(tile_off, exp_off, lens, used, yg, dk, gk, h1, h1b, p2, sg, su, sd, g, b, wg, bg, wp,
             alpha, tm, rt):
    n, d = h1.shape
    row = lambda w: pl.BlockSpec((tm, w), lambda i, *_: (i, 0))
    col = lambda: pl.BlockSpec((TOP_K, tm), lambda i, *_: (0, i))
    full = lambda a: pl.BlockSpec(a.shape, lambda i, *_: (0, 0))
    return pl.pallas_call(
        functools.partial(_combine_kernel, alpha=alpha, rb=512),
        grid_spec=pltpu.PrefetchScalarGridSpec(
            num_scalar_prefetch=4,
            grid=(n // tm,),
            in_specs=[pl.BlockSpec(memory_space=pl.ANY), col(), col(), row(d),
                      row(d), row(p2.shape[1]), full(sg), full(su), full(sd), full(g), full(b),
                      full(wg), full(bg), full(wp)],
            out_specs=row(d),
            scratch_shapes=[pltpu.VMEM((2, rt, d), yg.dtype), pltpu.VMEM((tm, d), F32),
                            pltpu.SemaphoreType.DMA((2,))]),
        out_shape=jax.ShapeDtypeStruct((n, d), F32),
        compiler_params=_cparams("arbitrary"),
    )(tile_off, exp_off, lens, used, yg, dk, gk, h1, h1b, p2, sg, su, sd, g, b, wg, bg, wp)


def _tiles(seq):
    tm = min(256, seq)
    tq_sel = min(256, seq)
    tk_sel = min(512, seq)
    tq_att = min(256, seq)
    tk_att = min(512, seq)
    rt = -(-(tm * TOP_K + N_EXPERTS * (ROW_ALIGN - 1)) // 512) * 512
    return tm, tq_sel, tk_sel, tq_att, tk_att, rt


def kernel(x, p, positions, ln_emb_g, ln_emb_b, w_in, w_o, diff_lq1, diff_lk1, diff_lq2, diff_lk2,
           diff_subln_g, ln1_g, ln1_b, w_router, router_bias, w_exp_gate, w_exp_up, w_exp_down,
           w_sh_gate, w_sh_up, w_sh_down, ln2_g, ln2_b, w_ple_gate, b_ple_gate, w_ple_proj):
    bsz, seq, d = x.shape
    depth = w_in.shape[0]
    n = bsz * seq
    n_keep = min(IDX_TOPK_MAX, seq // 4)
    alpha = (2.0 * depth) ** 0.25
    tm, tq_sel, tk_sel, tq_att, tk_att, rt = _tiles(seq)
    n_blocks = -(-((n // tm) * (tm * TOP_K + N_EXPERTS * (ROW_ALIGN - 1))
                   + N_EXPERTS * (MOE_BLOCK - 1)) // MOE_BLOCK)
    row1 = lambda v: v.reshape(1, -1)

    inv = ROPE_THETA ** (-jnp.arange(0, ROPE_DIM, 2, dtype=F32) / ROPE_DIM)
    inv_lanes = jnp.tile(inv, LANES // (ROPE_DIM // 2)).reshape(1, LANES)
    pos2 = positions.reshape(n, 1)

    h = x.reshape(n, d)
    out = None
    for li in range(depth):
        lambda_init = 0.8 - 0.6 * math.exp(-0.3 * li)
        w = w_in[li]
        aqw, akw, avw = w[:, 0:512], w[:, 512:1024], w[:, 1024:1536]
        iqw, ikw, iww = w[:, 1536:2048], w[:, 2048:2112], w[:, 2112:2120]
        bqw, bkw, bvw = w[:, 2120:2632], w[:, 2632:3144], w[:, 3144:3656]
        w_rope = jnp.concatenate([aqw, akw, iqw, ikw, ikw, bqw, bkw], axis=1).astype(BF16)
        w_plain = jnp.concatenate([avw, bvw], axis=1).astype(BF16)
        w_idx_t = jnp.concatenate(
            [iww.T, jnp.zeros((LANES - IDX_HEADS, d), F32)], axis=0).astype(BF16)

        (h0, aq, ak, iq, ik2, bq, bk, av, bv, iw) = _proj(
            h, pos2, inv_lanes, row1(ln_emb_g), row1(ln_emb_b), w_rope, w_plain, w_idx_t, tm)
        assert depth == 1

        r3 = lambda a: a.reshape(bsz, seq, a.shape[-1])
        bias = _select(r3(iq), iw, r3(ik2), n_keep, tq_sel, tk_sel)
        out_a = _dsa(r3(aq), r3(ak), r3(av), bias, tq_att, tk_att)
        out_b = _diff(r3(bq), r3(bk), r3(bv), row1(diff_lq1[li]), row1(diff_lk1[li]),
                      row1(diff_lq2[li]), row1(diff_lk2[li]), row1(diff_subln_g[li]),
                      lambda_init, tq_att, tk_att)

        wo = w_o[li].astype(BF16)
        h1, h1b, dk, gk, cnt_out = _oproj(
            out_a.reshape(n, -1), out_b.reshape(n, -1), h0, wo[0:512], wo[512:1024],
            row1(ln1_g[li]), row1(ln1_b[li]), w_router[li].T, router_bias[li].reshape(-1, 1),
            alpha, tm)

        nt = n // tm
        c16 = jnp.ceil(cnt_out[:, 0, :N_EXPERTS] / ROW_ALIGN) * ROW_ALIGN
        tri = lambda m: jnp.tril(jnp.ones((m, m), F32))
        hi = lax.Precision.HIGHEST
        tile_off = jnp.dot(c16, tri(N_EXPERTS).T, precision=hi) - c16
        cum_tiles = jnp.dot(tri(nt), c16, precision=hi)
        tot = cum_tiles[-1]
        totp = jnp.ceil(tot / MOE_BLOCK) * MOE_BLOCK
        pend = jnp.dot(totp, tri(N_EXPERTS).T, precision=hi)
        ebase = pend - totp
        exp_off = ebase[None, :] + cum_tiles - c16
        blk0 = jnp.arange(n_blocks, dtype=F32) * MOE_BLOCK
        blk_e = jnp.minimum(jnp.sum(pend[None, :] <= blk0[:, None], axis=1), N_EXPERTS - 1)
        onehot_e = (blk_e[:, None] == jnp.arange(N_EXPERTS)[None, :]).astype(F32)
        blk_end = jnp.dot(onehot_e, ebase + tot, precision=hi)
        blk_valid = jnp.clip(blk_end - blk0, 0, MOE_BLOCK)
        flat = lambda a: a.reshape(-1).astype(jnp.int32)
        n_rows = n_blocks * MOE_BLOCK
        tail0 = pend[-1] + jnp.arange(-(-n_rows // rt), dtype=F32) * rt
        gap_off = jnp.concatenate([ebase + tot, tail0])
        gap_len = jnp.concatenate([totp - tot, jnp.clip(n_rows - tail0, 0, rt)])

        weg, weu, wed = (w_exp_gate[li].astype(BF16), w_exp_up[li].astype(BF16),
                         w_exp_down[li].astype(BF16))
        used = jnp.sum(c16, axis=1)
        xg = _dispatch(h1b, dk, flat(tile_off), flat(exp_off), flat(c16), flat(used),
                       flat(gap_off), flat(gap_len), n_rows, tm, rt)
        yg = _experts(xg, flat(blk_e), flat(blk_valid), weg, weu, wed)
        out = _combine(flat(tile_off), flat(exp_off), flat(c16), flat(used), yg, dk, gk, h1, h1b,
                       p[li].reshape(n, -1), w_sh_gate[li].astype(BF16),
                       w_sh_up[li].astype(BF16), w_sh_down[li].astype(BF16), row1(ln2_g[li]),
                       row1(ln2_b[li]), w_ple_gate[li].astype(BF16), row1(b_ple_gate[li]),
                       w_ple_proj[li].astype(BF16), alpha, tm, rt)
        h = out
    return out.reshape(bsz, seq, d)
```

```python
import functools
import math

import jax
import jax.numpy as jnp
from jax import lax
from jax.experimental import pallas as pl
from jax.experimental.pallas import tpu as pltpu

CHUNK = 64
HEAD_DIM = 64
ROPE_DIM = 64
ROPE_THETA = 10000.0
A_HEADS = 8
IDX_HEADS = 8
IDX_DIM = 64
IDX_TOPK_MAX = 256
B_HEADS = 4
B_QK_DIM = 64
B_V_DIM = 128
N_EXPERTS = 64
TOP_K = 8
N_GROUPS = 8
TOPK_GROUPS = 4
ROUTED_SCALE = 2.5
LN_EPS = 1e-5
RMS_EPS = 1e-5

LANES = 128
ROW_ALIGN = 16
MOE_BLOCK = 512
VMEM_LIMIT = 56 * 1024 * 1024
NEG_BIG = -1e30
INT_MIN = -(2 ** 31)
LOG2E = math.log2(math.e)
GROUPS = 256

F32 = jnp.float32
BF16 = jnp.bfloat16


def _cparams(*sem):
    return pltpu.CompilerParams(dimension_semantics=sem, vmem_limit_bytes=VMEM_LIMIT)


def _layer_norm(x, g, b):
    mu = jnp.mean(x, axis=-1, keepdims=True)
    xc = x - mu
    var = jnp.mean(xc * xc, axis=-1, keepdims=True)
    return xc * lax.rsqrt(var + LN_EPS) * g + b


def _dot_nt(a, b):
    return lax.dot_general(a, b, (((1,), (1,)), ((), ())), preferred_element_type=F32)


def _split_lane_halves(pair):
    lo = lax.broadcasted_iota(jnp.int32, pair.shape, 1) < (LANES // 2)
    pf = pair.astype(F32)
    return (jnp.where(lo, pf, 0.0).astype(pair.dtype), jnp.where(lo, 0.0, pf).astype(pair.dtype))


def _proj_kernel(x_ref, pos_ref, inv_ref, g_ref, b_ref, wr_ref, wp_ref, wi_ref,
                 h0_ref, aq_ref, ak_ref, iq_ref, ik_ref, bq_ref, bk_ref,
                 av_ref, bv_ref, iw_ref):
    hn = _layer_norm(x_ref[...], g_ref[...], b_ref[...])
    h0_ref[...] = hn
    hb = hn.astype(BF16)

    tm = x_ref.shape[0]
    nq = LANES // (ROPE_DIM // 2)
    tr = tm // nq
    pos = pos_ref[...].astype(F32)
    lane = lax.broadcasted_iota(jnp.int32, (tr, LANES), 1)
    blk = lane // (ROPE_DIM // 2)
    pos_packed = jnp.broadcast_to(pos[0:tr], (tr, LANES))
    for q in range(1, nq):
        pos_packed = jnp.where(blk == q, pos[q * tr:(q + 1) * tr], pos_packed)
    ang = pos_packed * inv_ref[...]
    cos_packed = jnp.cos(ang)
    sin_packed = jnp.sin(ang)
    first_half = (lane % ROPE_DIM) < (ROPE_DIM // 2)

    def spread(table, q):
        own = jnp.where(blk == q, table, 0.0)
        t = own
        for s in range(1, nq):
            t = t + pltpu.roll(own, s * (ROPE_DIM // 2), axis=1)
        return t

    pr = jnp.dot(hb, wr_ref[...], preferred_element_type=F32)
    outs = ((aq_ref, 0, 4, HEAD_DIM ** -0.5 * LOG2E), (ak_ref, 4, 4, 1.0), (iq_ref, 8, 4, 1.0),
            (ik_ref, 12, 1, 1.0), (bq_ref, 13, 4, B_QK_DIM ** -0.5 * LOG2E), (bk_ref, 17, 4, 1.0))
    for q in range(nq):
        cos = spread(cos_packed, q)
        sin = spread(sin_packed, q)
        sin_signed = jnp.where(first_half, -sin, sin)
        rows = slice(q * tr, (q + 1) * tr)
        for ref, g0, ng, scale in outs:
            for j in range(ng):
                v = pr[rows, (g0 + j) * LANES:(g0 + j + 1) * LANES]
                partner = jnp.where(first_half,
                                    pltpu.roll(v, LANES - ROPE_DIM // 2, axis=1),
                                    pltpu.roll(v, ROPE_DIM // 2, axis=1))
                v = v * cos + partner * sin_signed
                if scale != 1.0:
                    v = v * scale
                ref[rows, j * LANES:(j + 1) * LANES] = v.astype(ref.dtype)

    pp = jnp.dot(hb, wp_ref[...], preferred_element_type=F32)
    av_ref[...] = pp[:, 0:512].astype(BF16)
    bv_ref[...] = pp[:, 512:1024].astype(BF16)
    iw_ref[...] = _dot_nt(wi_ref[...], hb)[0:IDX_HEADS, :]


def _proj(x2, pos2, inv, g, b, w_rope, w_plain, w_idx_t, tm):
    n, d = x2.shape
    nr = w_rope.shape[1]
    npl = w_plain.shape[1]
    row = lambda w: pl.BlockSpec((tm, w), lambda i: (i, 0))
    full = lambda a: pl.BlockSpec(a.shape, lambda i: (0, 0))
    out_shape = (
        jax.ShapeDtypeStruct((n, d), F32),
        jax.ShapeDtypeStruct((n, 512), BF16),
        jax.ShapeDtypeStruct((n, 512), BF16),
        jax.ShapeDtypeStruct((n, 512), BF16),
        jax.ShapeDtypeStruct((n, 128), BF16),
        jax.ShapeDtypeStruct((n, 512), BF16),
        jax.ShapeDtypeStruct((n, 512), BF16),
        jax.ShapeDtypeStruct((n, 512), BF16),
        jax.ShapeDtypeStruct((n, 512), BF16),
        jax.ShapeDtypeStruct((IDX_HEADS, n), F32),
    )
    return pl.pallas_call(
        _proj_kernel,
        grid=(n // tm,),
        in_specs=[row(d), row(1), full(inv), full(g), full(b), full(w_rope), full(w_plain),
                  full(w_idx_t)],
        out_specs=(row(d), row(512), row(512), row(512), row(128), row(512), row(512),
                   row(512), row(512), pl.BlockSpec((IDX_HEADS, tm), lambda i: (0, i))),
        out_shape=out_shape,
        compiler_params=_cparams("parallel"),
    )(x2, pos2, inv, g, b, w_rope, w_plain, w_idx_t)


def _sortable(s):
    i = lax.bitcast_convert_type(s, jnp.int32)
    return i ^ ((i >> 31) & 0x7FFFFFFF)


def _select_kernel(iq_ref, iw_ref, ik_ref, bias_ref, keys_ref, gmax_ref, *, tq, tk, n_keep,
                   w_scale):
    i = pl.program_id(1)
    seq = ik_ref.shape[1]
    n_t = ((i + 1) * tq + tk - 1) // tk

    q_chunk = (i * tq + lax.broadcasted_iota(jnp.int32, (tk, tq), 1)) // CHUNK
    k_row = lax.broadcasted_iota(jnp.int32, (tk, tq), 0)
    wb = iw_ref[...] * w_scale
    q_halves = []
    for hp in range(IDX_HEADS // 2):
        q_halves.extend(_split_lane_halves(iq_ref[0, :, hp * LANES:(hp + 1) * LANES]))

    def score_tile(t, carry, boundary):
        off = pl.multiple_of(t * tk, tk)
        kt = ik_ref[0, pl.ds(off, tk), :]
        sc = jnp.zeros((tk, tq), F32)
        for h in range(IDX_HEADS):
            rel = jnp.maximum(_dot_nt(kt, q_halves[h]), 0.0)
            sc = sc + wb[h:h + 1, :] * rel
        key = _sortable(sc.astype(BF16).astype(F32) + 0.0) >> (32 - 8 * jnp.dtype(BF16).itemsize)
        if boundary:
            key = jnp.where(((off + k_row) // CHUNK) <= q_chunk, key, INT_MIN)
        keys_ref[pl.ds(off, tk), :] = key
        gm = key[0:GROUPS, :]
        for g0 in range(GROUPS, tk, GROUPS):
            gm = jnp.maximum(gm, key[g0:g0 + GROUPS, :])
        gmax_ref[...] = jnp.maximum(gmax_ref[...], gm)
        return carry

    gmax_ref[...] = jnp.full(gmax_ref.shape, INT_MIN, jnp.int32)
    n_full = (i * tq + CHUNK) // tk
    lax.fori_loop(0, n_full, functools.partial(score_tile, boundary=False), 0)
    lax.fori_loop(n_full, n_t, functools.partial(score_tile, boundary=True), 0)

    def count_ge(cand):
        def body(t, acc):
            off = pl.multiple_of(t * tk, tk)
            m = (keys_ref[pl.ds(off, tk), :] >= cand).astype(jnp.int32)
            return acc + jnp.sum(m.reshape(tk // 8, 8, tq), axis=0)
        acc = lax.fori_loop(0, n_t, body, jnp.zeros((8, tq), jnp.int32))
        return jnp.sum(acc, axis=0, keepdims=True)

    gmax = gmax_ref[...]
    n_adm = ((i * tq + lax.broadcasted_iota(jnp.int32, (1, tq), 1)) // CHUNK + 1) * CHUNK
    few = n_adm <= n_keep
    lo0 = jnp.where(few, INT_MIN + 1, jnp.min(gmax, axis=0, keepdims=True))
    hi0 = jnp.where(few, INT_MIN + 2, jnp.max(gmax, axis=0, keepdims=True) + 1)

    def probe(state):
        lo, hi, thr, n_thr, done, _, it = state
        mid = (lo >> 1) + (hi >> 1) + (lo & hi & 1)
        closed = mid == lo
        cnt = count_ge(mid)
        exact = cnt == n_keep
        up = cnt >= n_keep
        live = done == 0
        settle = live & (exact | closed)
        thr = jnp.where(settle, mid, thr)
        n_thr = jnp.where(settle, cnt, n_thr)
        lo = jnp.where(live & up, mid, lo)
        hi = jnp.where(live & jnp.logical_not(up), mid, hi)
        done = jnp.where(exact | closed, 1, done)
        return lo, hi, thr, n_thr, done, jnp.min(done), it + 1

    def searching(state):
        return (state[5] == 0) & (state[6] < 34)

    done0 = few.astype(jnp.int32)
    state = lax.while_loop(
        searching, probe,
        (lo0, hi0, lo0, jnp.zeros((1, tq), jnp.int32), done0, jnp.min(done0), jnp.int32(0)))
    thr, n_ge = state[2], state[3]

    tied = n_ge > n_keep
    any_tied = jnp.max(tied.astype(jnp.int32)) > 0
    tau = thr

    def write_bias(off, keep_t):
        bias_t = jnp.where(keep_t, 0.0, NEG_BIG)
        bias_ref[0, :, pl.ds(off, tk)] = bias_t.T.astype(BF16)

    @pl.when(jnp.logical_not(any_tied))
    def _():
        def write_tile(t, carry):
            off = pl.multiple_of(t * tk, tk)
            write_bias(off, keys_ref[pl.ds(off, tk), :] >= tau)
            return carry
        lax.fori_loop(0, n_t, write_tile, 0)

    @pl.when(any_tied)
    def _():
        room = (n_keep - count_ge(tau + 1)).astype(F32)
        upto = (lax.broadcasted_iota(jnp.int32, (tk, tk), 1)
                <= lax.broadcasted_iota(jnp.int32, (tk, tk), 0)).astype(F32).astype(BF16)

        def write_tile(t, seen_eq):
            off = pl.multiple_of(t * tk, tk)
            key = keys_ref[pl.ds(off, tk), :]
            eq = (key == tau) & tied
            eq_f = jnp.where(eq, 1.0, 0.0)
            eq_rank = seen_eq + jnp.dot(upto, eq_f.astype(BF16), preferred_element_type=F32)
            write_bias(off, (key > tau) | (eq & (eq_rank <= room))
                       | ((key == tau) & jnp.logical_not(tied)))
            return seen_eq + jnp.sum(eq_f, axis=0, keepdims=True)

        lax.fori_loop(0, n_t, write_tile, jnp.zeros((1, tq), F32))

    def blank_tile(t, carry):
        off = pl.multiple_of(t * tk, tk)
        bias_ref[0, :, pl.ds(off, tk)] = jnp.full((tq, tk), NEG_BIG, BF16)
        return carry

    lax.fori_loop(n_t, seq // tk, blank_tile, 0)


def _select(iq, iw, ik2, n_keep, tq, tk):
    bsz, seq, _ = iq.shape
    kern = functools.partial(_select_kernel, tq=tq, tk=tk, n_keep=n_keep,
                             w_scale=(IDX_HEADS ** -0.5) * (IDX_DIM ** -0.5))
    return pl.pallas_call(
        kern,
        grid=(bsz, seq // tq),
        in_specs=[pl.BlockSpec((1, tq, 512), lambda b, i: (b, i, 0)),
                  pl.BlockSpec((IDX_HEADS, tq), lambda b, i: (0, b * (seq // tq) + i)),
                  pl.BlockSpec((1, seq, 128), lambda b, i: (b, 0, 0))],
        out_specs=pl.BlockSpec((1, tq, seq), lambda b, i: (b, i, 0)),
        out_shape=jax.ShapeDtypeStruct((bsz, seq, seq), BF16),
        scratch_shapes=[pltpu.VMEM((seq, tq), jnp.int32), pltpu.VMEM((GROUPS, tq), jnp.int32)],
        compiler_params=_cparams("parallel", "parallel"),
    )(iq, iw, ik2)


def _flash_update(j, s, v_ext, m_ref, acc_ref):
    tk = s.shape[1]
    m_old = m_ref[j]
    m_new = jnp.maximum(m_old, jnp.max(s, axis=1, keepdims=True))
    alpha = jnp.exp2(m_old - m_new)
    p = jnp.exp2(s - jnp.concatenate([m_new] * (tk // LANES), axis=1))
    m_ref[j] = m_new
    acc_ref[j] = (acc_ref[j] * jnp.concatenate([alpha, alpha], axis=1)
                  + jnp.dot(p.astype(BF16), v_ext, preferred_element_type=F32))


def _dsa_kernel(q_ref, k_ref, v_ref, bias_ref, o_ref, m_ref, acc_ref, *, tq, tk):
    i = pl.program_id(1)
    n_keys = (i + 1) * tq
    lo_q = lax.broadcasted_iota(jnp.int32, (tq, LANES), 1) < HEAD_DIM
    q_halves = []
    for hp in range(A_HEADS // 2):
        q_halves.extend(_split_lane_halves(q_ref[0, :, hp * LANES:(hp + 1) * LANES]))

    m_ref[...] = jnp.full(m_ref.shape, NEG_BIG, F32)
    acc_ref[...] = jnp.zeros(acc_ref.shape, F32)

    def kv_tile(off, width):
        bias = bias_ref[0, :, pl.ds(off, width)].astype(F32)
        ones = jnp.ones((width, LANES), BF16)
        for hp in range(A_HEADS // 2):
            kp = k_ref[0, pl.ds(off, width), hp * LANES:(hp + 1) * LANES]
            v_ext = jnp.concatenate(
                [v_ref[0, pl.ds(off, width), hp * LANES:(hp + 1) * LANES], ones], axis=1)
            for half in range(2):
                h = 2 * hp + half
                _flash_update(h, _dot_nt(q_halves[h], kp) + bias, v_ext, m_ref, acc_ref)

    def kv_step(t, carry):
        kv_tile(pl.multiple_of(t * tk, tk), tk)
        return carry

    n_whole = n_keys // tk
    lax.fori_loop(0, n_whole, kv_step, 0)
    if tk > tq:
        @pl.when(n_keys % tk != 0)
        def _():
            kv_tile(pl.multiple_of(n_whole * tk, tq), tq)

    for hp in range(A_HEADS // 2):
        a0 = acc_ref[2 * hp]
        a1 = acc_ref[2 * hp + 1]
        o = jnp.where(lo_q, a0[:, :LANES] / a0[:, LANES:], a1[:, :LANES] / a1[:, LANES:])
        o_ref[0, :, hp * LANES:(hp + 1) * LANES] = o.astype(o_ref.dtype)


def _dsa(aq, ak, av, bias, tq, tk):
    bsz, seq, w = aq.shape
    kern = functools.partial(_dsa_kernel, tq=tq, tk=tk)
    resident = lambda: pl.BlockSpec((1, seq, w), lambda b, i: (b, 0, 0),
                                    pipeline_mode=pl.Buffered(1))
    return pl.pallas_call(
        kern,
        grid=(bsz, seq // tq),
        in_specs=[pl.BlockSpec((1, tq, w), lambda b, i: (b, i, 0)),
                  resident(), resident(),
                  pl.BlockSpec((1, tq, seq), lambda b, i: (b, i, 0))],
        out_specs=pl.BlockSpec((1, tq, w), lambda b, i: (b, i, 0)),
        out_shape=jax.ShapeDtypeStruct((bsz, seq, w), BF16),
        scratch_shapes=[pltpu.VMEM((A_HEADS, tq, LANES), F32),
                        pltpu.VMEM((A_HEADS, tq, 2 * LANES), F32)],
        compiler_params=_cparams("parallel", "arbitrary"),
    )(aq, ak, av, bias)


def _diff_kernel(q_ref, k_ref, v_ref, lq1_ref, lk1_ref, lq2_ref, lk2_ref, g_ref, o_ref,
                 m_ref, acc_ref, *, tq, tk, lambda_init):
    i = pl.program_id(1)
    q_halves = []
    for hb in range(B_HEADS):
        q_halves.extend(_split_lane_halves(q_ref[0, :, hb * LANES:(hb + 1) * LANES]))
    lam = (jnp.exp(jnp.sum(lq1_ref[...] * lk1_ref[...], keepdims=True))
           - jnp.exp(jnp.sum(lq2_ref[...] * lk2_ref[...], keepdims=True)) + lambda_init)

    m_ref[...] = jnp.full(m_ref.shape, NEG_BIG, F32)
    acc_ref[...] = jnp.zeros(acc_ref.shape, F32)

    def tile(off, width, masked):
        ones = jnp.ones((width, LANES), BF16)
        if masked:
            row_chunk = (i * tq + lax.broadcasted_iota(jnp.int32, (tq, width), 0)) // CHUNK
            col_chunk = (off + lax.broadcasted_iota(jnp.int32, (tq, width), 1)) // CHUNK
            ok = col_chunk <= row_chunk
        for hb in range(B_HEADS):
            kp = k_ref[0, pl.ds(off, width), hb * LANES:(hb + 1) * LANES]
            v_ext = jnp.concatenate(
                [v_ref[0, pl.ds(off, width), hb * LANES:(hb + 1) * LANES], ones], axis=1)
            for mp in range(2):
                j = 2 * hb + mp
                s = _dot_nt(q_halves[j], kp)
                if masked:
                    s = jnp.where(ok, s, NEG_BIG)
                _flash_update(j, s, v_ext, m_ref, acc_ref)

    def kv_step(t, carry):
        tile(pl.multiple_of(t * tk, tk), tk, False)
        return carry

    n_whole = (i * tq) // tk
    lax.fori_loop(0, n_whole, kv_step, 0)
    if tk > tq:
        @pl.when((i * tq) % tk != 0)
        def _():
            tile(pl.multiple_of(n_whole * tk, tq), tq, False)
    tile(pl.multiple_of(i * tq, tq), tq, True)

    for hb in range(B_HEADS):
        a1 = acc_ref[2 * hb]
        a2 = acc_ref[2 * hb + 1]
        o = a1[:, :LANES] / a1[:, LANES:] - lam * (a2[:, :LANES] / a2[:, LANES:])
        o = o * lax.rsqrt(jnp.mean(o * o, axis=-1, keepdims=True) + RMS_EPS)
        o = o * g_ref[...] * (1.0 - lambda_init)
        o_ref[0, :, hb * LANES:(hb + 1) * LANES] = o.astype(o_ref.dtype)


def _diff(bq, bk, bv, lq1, lk1, lq2, lk2, subln_g, lambda_init, tq, tk):
    bsz, seq, w = bq.shape
    kern = functools.partial(_diff_kernel, tq=tq, tk=tk, lambda_init=lambda_init)
    resident = lambda: pl.BlockSpec((1, seq, w), lambda b, i: (b, 0, 0),
                                    pipeline_mode=pl.Buffered(1))
    small = lambda a: pl.BlockSpec(a.shape, lambda b, i: (0, 0))
    return pl.pallas_call(
        kern,
        grid=(bsz, seq // tq),
        in_specs=[pl.BlockSpec((1, tq, w), lambda b, i: (b, i, 0)),
                  resident(), resident(),
                  small(lq1), small(lk1), small(lq2), small(lk2), small(subln_g)],
        out_specs=pl.BlockSpec((1, tq, w), lambda b, i: (b, i, 0)),
        out_shape=jax.ShapeDtypeStruct((bsz, seq, w), BF16),
        scratch_shapes=[pltpu.VMEM((2 * B_HEADS, tq, LANES), F32),
                        pltpu.VMEM((2 * B_HEADS, tq, 2 * LANES), F32)],
        compiler_params=_cparams("parallel", "arbitrary"),
    )(bq, bk, bv, lq1, lk1, lq2, lk2, subln_g)


def _first_argmax_rows(v, row):
    m = jnp.max(v, axis=0, keepdims=True)
    idx = jnp.min(jnp.where(v == m, row, v.shape[0]), axis=0, keepdims=True)
    return m, idx


def _scatter_rows(dk, gk, r0, nrows):
    r_iota = (lax.broadcasted_iota(jnp.int32, (nrows, dk.shape[1]), 0) + r0).astype(F32)
    out = jnp.zeros(r_iota.shape, F32)
    for k in range(dk.shape[0]):
        out = jnp.where(r_iota == dk[k:k + 1, :], 1.0 if gk is None else gk[k:k + 1, :], out)
    return out


def _oproj_kernel(oa_ref, ob_ref, h0_ref, woa_ref, wob_ref, g_ref, b_ref, wrt_ref, rb_ref,
                  h1_ref, h1b_ref, dk_ref, gk_ref, cnt_ref, *, alpha):
    mix = (jnp.dot(oa_ref[...], woa_ref[...], preferred_element_type=F32)
           + jnp.dot(ob_ref[...], wob_ref[...], preferred_element_type=F32))
    h1 = _layer_norm(alpha * h0_ref[...] + mix, g_ref[...], b_ref[...])
    h1_ref[...] = h1
    h1b = h1.astype(BF16)
    h1b_ref[...] = h1b

    logits = lax.dot_general(wrt_ref[...], h1, (((1,), (1,)), ((), ())),
                             precision=lax.Precision.HIGHEST, preferred_element_type=F32)
    scores = jax.nn.sigmoid(logits)
    biased = scores + rb_ref[...]
    tm = scores.shape[1]
    per_g = N_EXPERTS // N_GROUPS
    row8 = lax.broadcasted_iota(jnp.int32, (per_g, tm), 0)

    gs = []
    for g in range(N_GROUPS):
        blk = biased[g * per_g:(g + 1) * per_g, :]
        m1, i1 = _first_argmax_rows(blk, row8)
        m2 = jnp.max(jnp.where(row8 == i1, -jnp.inf, blk), axis=0, keepdims=True)
        gs.append(m1 + m2)
    gscore = jnp.concatenate(gs, axis=0)
    rowg = lax.broadcasted_iota(jnp.int32, (N_GROUPS, tm), 0)
    gsel = jnp.zeros((N_GROUPS, tm), F32)
    for _ in range(TOPK_GROUPS):
        _, ig = _first_argmax_rows(jnp.where(gsel > 0.0, -jnp.inf, gscore), rowg)
        gsel = jnp.where(rowg == ig, 1.0, gsel)

    rowe = lax.broadcasted_iota(jnp.int32, (N_EXPERTS, tm), 0)
    live = jnp.concatenate(
        [jnp.broadcast_to(gsel[g:g + 1, :], (per_g, tm)) for g in range(N_GROUPS)], axis=0)
    esel = jnp.zeros((N_EXPERTS, tm), F32)
    for _ in range(TOP_K):
        cand = jnp.where(live > 0.0, biased, -jnp.inf)
        m = jnp.max(cand, axis=0, keepdims=True)
        idx = jnp.min(jnp.where((live > 0.0) & (cand == m), rowe, N_EXPERTS),
                      axis=0, keepdims=True)
        hit = rowe == idx
        esel = jnp.where(hit, 1.0, esel)
        live = jnp.where(hit, 0.0, live)
    picked = jnp.where(esel > 0.0, scores, 0.0)
    denom = jnp.sum(picked, axis=0, keepdims=True)
    gate = picked / denom * ROUTED_SCALE

    sel_b = esel.astype(BF16)
    before_t = (lax.broadcasted_iota(jnp.int32, (tm, tm), 0)
                < lax.broadcasted_iota(jnp.int32, (tm, tm), 1)).astype(F32).astype(BF16)
    before_e = (lax.broadcasted_iota(jnp.int32, (N_EXPERTS, N_EXPERTS), 1)
                < lax.broadcasted_iota(jnp.int32, (N_EXPERTS, N_EXPERTS), 0)).astype(F32).astype(BF16)
    rank = jnp.dot(sel_b, before_t, preferred_element_type=F32)
    order = jnp.dot(before_e, sel_b, preferred_element_type=F32)
    cnt = jnp.sum(esel, axis=1, keepdims=True)
    cnt16 = jnp.floor((cnt + (ROW_ALIGN - 1)) / ROW_ALIGN) * ROW_ALIGN
    off = jnp.dot(before_e, jnp.broadcast_to(cnt16, (N_EXPERTS, tm)).astype(BF16),
                  preferred_element_type=F32)
    dest = off + rank
    dks, gks = [], []
    for k in range(TOP_K):
        kth = (esel > 0.0) & (order == k)
        dks.append(jnp.sum(jnp.where(kth, dest, 0.0), axis=0, keepdims=True))
        gks.append(jnp.sum(jnp.where(kth, gate, 0.0), axis=0, keepdims=True))
    dk = jnp.concatenate(dks, axis=0)
    dk_ref[...] = dk
    gk_ref[...] = jnp.concatenate(gks, axis=0)
    counts = _dot_nt(jnp.ones((8, tm), BF16), sel_b)
    cnt_ref[0] = jnp.concatenate([counts, jnp.zeros((8, LANES - N_EXPERTS), F32)], axis=1)


def _oproj(oa, ob, h0, woa, wob, g, b, wrt, rb, alpha, tm):
    n, d = h0.shape
    nt = n // tm
    row = lambda w: pl.BlockSpec((tm, w), lambda i: (i, 0))
    col = lambda: pl.BlockSpec((TOP_K, tm), lambda i: (0, i))
    full = lambda a: pl.BlockSpec(a.shape, lambda i: (0, 0))
    kt = jax.ShapeDtypeStruct((TOP_K, n), F32)
    return pl.pallas_call(
        functools.partial(_oproj_kernel, alpha=alpha),
        grid=(nt,),
        in_specs=[row(512), row(512), row(d), full(woa), full(wob), full(g), full(b),
                  full(wrt), full(rb)],
        out_specs=(row(d), row(d), col(), col(),
                   pl.BlockSpec((1, 8, LANES), lambda i: (i, 0, 0))),
        out_shape=(jax.ShapeDtypeStruct((n, d), F32),
                   jax.ShapeDtypeStruct((n, d), BF16),
                   kt,
                   kt,
                   jax.ShapeDtypeStruct((nt, 8, LANES), F32)),
        compiler_params=_cparams("parallel"),
    )(oa, ob, h0, woa, wob, g, b, wrt, rb)


def _swiglu(x, wg, wu, wd):
    hid = (jax.nn.silu(jnp.dot(x, wg, preferred_element_type=F32))
           * jnp.dot(x, wu, preferred_element_type=F32))
    return jnp.dot(hid.astype(BF16), wd, preferred_element_type=F32)


def _dispatch_kernel(tile_off_ref, exp_off_ref, len_ref, used_ref, gap_off_ref, gap_len_ref,
                     h1b_ref, dk_ref, xg_ref, xs_ref, zero_ref, run_sem, gap_sem, *, rb):
    i = pl.program_id(0)
    nt = pl.num_programs(0)
    rt = xs_ref.shape[1]
    slot = i % 2

    def wait_runs(step):
        n = pl.multiple_of(used_ref[step], ROW_ALIGN)
        s = step % 2
        pltpu.make_async_copy(xs_ref.at[s, pl.ds(0, n), :], xg_ref.at[pl.ds(0, n), :],
                              run_sem.at[s]).wait()

    @pl.when(i == 0)
    def _():
        zero_ref[...] = jnp.zeros(zero_ref.shape, zero_ref.dtype)

        def issue_gap(g, carry):
            n = pl.multiple_of(gap_len_ref[g], ROW_ALIGN)

            @pl.when(n > 0)
            def _():
                pltpu.make_async_copy(
                    zero_ref.at[pl.ds(0, n), :],
                    xg_ref.at[pl.ds(pl.multiple_of(gap_off_ref[g], ROW_ALIGN), n), :],
                    gap_sem).start()
            return carry

        lax.fori_loop(0, gap_len_ref.shape[0], issue_gap, 0)

    @pl.when(i >= 2)
    def _():
        wait_runs(i - 2)

    used = used_ref[i]
    dk = dk_ref[...]

    def sort_rows(r0):
        onehot = _scatter_rows(dk, None, r0, rb).astype(BF16)
        rows = jnp.dot(onehot, h1b_ref[...], preferred_element_type=F32)
        xs_ref[slot, r0:r0 + rb, :] = rows.astype(BF16)

    always = h1b_ref.shape[0] * TOP_K // rb * rb
    for r0 in range(0, always, rb):
        sort_rows(r0)
    for r0 in range(always, rt, rb):
        @pl.when(r0 < used)
        def _():
            sort_rows(r0)

    def issue_run(e, carry):
        r = i * N_EXPERTS + e
        n = pl.multiple_of(len_ref[r], ROW_ALIGN)

        @pl.when(n > 0)
        def _():
            pltpu.make_async_copy(
                xs_ref.at[slot, pl.ds(pl.multiple_of(tile_off_ref[r], ROW_ALIGN), n), :],
                xg_ref.at[pl.ds(pl.multiple_of(exp_off_ref[r], ROW_ALIGN), n), :],
                run_sem.at[slot]).start()
        return carry

    lax.fori_loop(0, N_EXPERTS, issue_run, 0)

    @pl.when(i == nt - 1)
    def _():
        @pl.when(i >= 1)
        def _():
            wait_runs(i - 1)
        wait_runs(i)

        def gap_rows(g, rows):
            return rows + gap_len_ref[g]
        n_gap = pl.multiple_of(lax.fori_loop(0, gap_len_ref.shape[0], gap_rows, 0), ROW_ALIGN)

        @pl.when(n_gap > 0)
        def _():
            pltpu.make_async_copy(xg_ref.at[pl.ds(0, n_gap), :], xg_ref.at[pl.ds(0, n_gap), :],
                                  gap_sem).wait()


def _dispatch(h1b, dk, tile_off, exp_off, lens, used, gap_off, gap_len, n_dst_rows, tm, rt):
    n, d = h1b.shape
    return pl.pallas_call(
        functools.partial(_dispatch_kernel, rb=512),
        grid_spec=pltpu.PrefetchScalarGridSpec(
            num_scalar_prefetch=6,
            grid=(n // tm,),
            in_specs=[pl.BlockSpec((tm, d), lambda i, *_: (i, 0)),
                      pl.BlockSpec((TOP_K, tm), lambda i, *_: (0, i))],
            out_specs=pl.BlockSpec(memory_space=pl.ANY),
            scratch_shapes=[pltpu.VMEM((2, rt, d), BF16), pltpu.VMEM((rt, d), BF16),
                            pltpu.SemaphoreType.DMA((2,)), pltpu.SemaphoreType.DMA(())]),
        out_shape=jax.ShapeDtypeStruct((n_dst_rows, d), BF16),
        compiler_params=_cparams("arbitrary"),
    )(tile_off, exp_off, lens, used, gap_off, gap_len, h1b, dk)


def _expert_kernel(blk_e_ref, blk_valid_ref, x_ref, wg_ref, wu_ref, wd_ref, y_ref):
    b = pl.program_id(0)
    valid = blk_valid_ref[b]

    @pl.when(valid > 0)
    def _():
        y_ref[...] = _swiglu(x_ref[...], wg_ref[...], wu_ref[...], wd_ref[...]).astype(BF16)

    @pl.when(valid <= 0)
    def _():
        y_ref[...] = jnp.zeros(y_ref.shape, BF16)


def _experts(xg, blk_e, blk_valid, wg, wu, wd):
    nr, d = xg.shape
    f = wg.shape[2]
    rows = pl.BlockSpec((MOE_BLOCK, d), lambda b, be, bv: (b, 0))
    return pl.pallas_call(
        _expert_kernel,
        grid_spec=pltpu.PrefetchScalarGridSpec(
            num_scalar_prefetch=2,
            grid=(nr // MOE_BLOCK,),
            in_specs=[rows,
                      pl.BlockSpec((None, d, f), lambda b, be, bv: (be[b], 0, 0)),
                      pl.BlockSpec((None, d, f), lambda b, be, bv: (be[b], 0, 0)),
                      pl.BlockSpec((None, f, d), lambda b, be, bv: (be[b], 0, 0))],
            out_specs=rows),
        out_shape=jax.ShapeDtypeStruct((nr, d), BF16),
        compiler_params=_cparams("parallel"),
    )(blk_e, blk_valid, xg, wg, wu, wd)


def _combine_kernel(tile_off_ref, exp_off_ref, len_ref, used_ref, yg_ref, dk_ref, gk_ref, h1_ref,
                    h1b_ref, p_ref, sg_ref, su_ref, sd_ref, g_ref, b_ref, wg_ref, bg_ref, wp_ref,
                    o_ref, ys_ref, ffn_ref, sem, *, alpha, rb):
    i = pl.program_id(0)
    nt = pl.num_programs(0)
    rt = ys_ref.shape[1]
    slot = i % 2

    def fetch_runs(tile):
        s = tile % 2

        def issue_run(e, carry):
            r = tile * N_EXPERTS + e
            n = pl.multiple_of(len_ref[r], ROW_ALIGN)

            @pl.when(n > 0)
            def _():
                pltpu.make_async_copy(
                    yg_ref.at[pl.ds(pl.multiple_of(exp_off_ref[r], ROW_ALIGN), n), :],
                    ys_ref.at[s, pl.ds(pl.multiple_of(tile_off_ref[r], ROW_ALIGN), n), :],
                    sem.at[s]).start()
            return carry

        lax.fori_loop(0, N_EXPERTS, issue_run, 0)

    @pl.when(i == 0)
    def _():
        ys_ref[...] = jnp.zeros(ys_ref.shape, ys_ref.dtype)
        fetch_runs(i)

    @pl.when(i + 1 < nt)
    def _():
        fetch_runs(i + 1)

    dk = dk_ref[...]
    gk = gk_ref[...]
    ffn = _swiglu(h1b_ref[...], sg_ref[...], su_ref[...], sd_ref[...])
    used = used_ref[i]
    n_used = pl.multiple_of(used, ROW_ALIGN)
    pltpu.make_async_copy(yg_ref.at[pl.ds(0, n_used), :], ys_ref.at[slot, pl.ds(0, n_used), :],
                          sem.at[slot]).wait()

    def routed(r0):
        weights = _scatter_rows(dk, gk, r0, rb).astype(BF16)
        return lax.dot_general(weights, ys_ref[slot, r0:r0 + rb, :], (((0,), (0,)), ((), ())),
                               preferred_element_type=F32)

    tm = h1_ref.shape[0]
    always = tm * TOP_K // rb * rb
    for r0 in range(0, always, rb):
        ffn = ffn + routed(r0)
    ffn_ref[...] = ffn
    for r0 in range(always, rt, rb):
        @pl.when(r0 < used)
        def _():
            ffn_ref[...] += routed(r0)
    h2 = _layer_norm(alpha * h1_ref[...] + ffn_ref[...], g_ref[...], b_ref[...])
    emb_gate = jax.nn.sigmoid(jnp.dot(h2.astype(BF16), wg_ref[...], preferred_element_type=F32)
                              + bg_ref[...])
    emb = jnp.dot(p_ref[...].astype(BF16), wp_ref[...], preferred_element_type=F32)
    o_ref[...] = h2 + emb_gate * emb


def _combine(tile_off, exp_off, lens, used, yg, dk, gk, h1, h1b, p2, sg, su, sd, g, b, wg, bg, wp,
             alpha, tm, rt):
    n, d = h1.shape
    row = lambda w: pl.BlockSpec((tm, w), lambda i, *_: (i, 0))
    col = lambda: pl.BlockSpec((TOP_K, tm), lambda i, *_: (0, i))
    full = lambda a: pl.BlockSpec(a.shape, lambda i, *_: (0, 0))
    return pl.pallas_call(
        functools.partial(_combine_kernel, alpha=alpha, rb=512),
        grid_spec=pltpu.PrefetchScalarGridSpec(
            num_scalar_prefetch=4,
            grid=(n // tm,),
            in_specs=[pl.BlockSpec(memory_space=pl.ANY), col(), col(), row(d),
                      row(d), row(p2.shape[1]), full(sg), full(su), full(sd), full(g), full(b),
                      full(wg), full(bg), full(wp)],
            out_specs=row(d),
            scratch_shapes=[pltpu.VMEM((2, rt, d), yg.dtype), pltpu.VMEM((tm, d), F32),
                            pltpu.SemaphoreType.DMA((2,))]),
        out_shape=jax.ShapeDtypeStruct((n, d), F32),
        compiler_params=_cparams("arbitrary"),
    )(tile_off, exp_off, lens, used, yg, dk, gk, h1, h1b, p2, sg, su, sd, g, b, wg, bg, wp)


def _tiles(seq):
    tm = min(256, seq)
    tq_sel = min(256, seq)
    tk_sel = min(512, seq)
    tq_att = min(256, seq)
    tk_att = min(512, seq)
    rt = -(-(tm * TOP_K + N_EXPERTS * (ROW_ALIGN - 1)) // 512) * 512
    return tm, tq_sel, tk_sel, tq_att, tk_att, rt


def kernel(x, p, positions, ln_emb_g, ln_emb_b, w_in, w_o, diff_lq1, diff_lk1, diff_lq2, diff_lk2,
           diff_subln_g, ln1_g, ln1_b, w_router, router_bias, w_exp_gate, w_exp_up, w_exp_down,
           w_sh_gate, w_sh_up, w_sh_down, ln2_g, ln2_b, w_ple_gate, b_ple_gate, w_ple_proj):
    bsz, seq, d = x.shape
    depth = w_in.shape[0]
    n = bsz * seq
    n_keep = min(IDX_TOPK_MAX, seq // 4)
    alpha = (2.0 * depth) ** 0.25
    tm, tq_sel, tk_sel, tq_att, tk_att, rt = _tiles(seq)
    n_blocks = -(-((n // tm) * (tm * TOP_K + N_EXPERTS * (ROW_ALIGN - 1))
                   + N_EXPERTS * (MOE_BLOCK - 1)) // MOE_BLOCK)
    row1 = lambda v: v.reshape(1, -1)

    inv = ROPE_THETA ** (-jnp.arange(0, ROPE_DIM, 2, dtype=F32) / ROPE_DIM)
    inv_lanes = jnp.tile(inv, LANES // (ROPE_DIM // 2)).reshape(1, LANES)
    pos2 = positions.reshape(n, 1)

    h = x.reshape(n, d)
    out = None
    for li in range(depth):
        lambda_init = 0.8 - 0.6 * math.exp(-0.3 * li)
        w = w_in[li]
        aqw, akw, avw = w[:, 0:512], w[:, 512:1024], w[:, 1024:1536]
        iqw, ikw, iww = w[:, 1536:2048], w[:, 2048:2112], w[:, 2112:2120]
        bqw, bkw, bvw = w[:, 2120:2632], w[:, 2632:3144], w[:, 3144:3656]
        w_rope = jnp.concatenate([aqw, akw, iqw, ikw, ikw, bqw, bkw], axis=1).astype(BF16)
        w_plain = jnp.concatenate([avw, bvw], axis=1).astype(BF16)
        w_idx_t = jnp.concatenate(
            [iww.T, jnp.zeros((LANES - IDX_HEADS, d), F32)], axis=0).astype(BF16)

        (h0, aq, ak, iq, ik2, bq, bk, av, bv, iw) = _proj(
            h, pos2, inv_lanes, row1(ln_emb_g), row1(ln_emb_b), w_rope, w_plain, w_idx_t, tm)
        assert depth == 1

        r3 = lambda a: a.reshape(bsz, seq, a.shape[-1])
        bias = _select(r3(iq), iw, r3(ik2), n_keep, tq_sel, tk_sel)
        out_a = _dsa(r3(aq), r3(ak), r3(av), bias, tq_att, tk_att)
        out_b = _diff(r3(bq), r3(bk), r3(bv), row1(diff_lq1[li]), row1(diff_lk1[li]),
                      row1(diff_lq2[li]), row1(diff_lk2[li]), row1(diff_subln_g[li]),
                      lambda_init, tq_att, tk_att)

        wo = w_o[li].astype(BF16)
        h1, h1b, dk, gk, cnt_out = _oproj(
            out_a.reshape(n, -1), out_b.reshape(n, -1), h0, wo[0:512], wo[512:1024],
            row1(ln1_g[li]), row1(ln1_b[li]), w_router[li].T, router_bias[li].reshape(-1, 1),
            alpha, tm)

        nt = n // tm
        c16 = jnp.ceil(cnt_out[:, 0, :N_EXPERTS] / ROW_ALIGN) * ROW_ALIGN
        tri = lambda m: jnp.tril(jnp.ones((m, m), F32))
        hi = lax.Precision.HIGHEST
        tile_off = jnp.dot(c16, tri(N_EXPERTS).T, precision=hi) - c16
        cum_tiles = jnp.dot(tri(nt), c16, precision=hi)
        tot = cum_tiles[-1]
        totp = jnp.ceil(tot / MOE_BLOCK) * MOE_BLOCK
        pend = jnp.dot(totp, tri(N_EXPERTS).T, precision=hi)
        ebase = pend - totp
        exp_off = ebase[None, :] + cum_tiles - c16
        blk0 = jnp.arange(n_blocks, dtype=F32) * MOE_BLOCK
        blk_e = jnp.minimum(jnp.sum(pend[None, :] <= blk0[:, None], axis=1), N_EXPERTS - 1)
        onehot_e = (blk_e[:, None] == jnp.arange(N_EXPERTS)[None, :]).astype(F32)
        blk_end = jnp.dot(onehot_e, ebase + tot, precision=hi)
        blk_valid = jnp.clip(blk_end - blk0, 0, MOE_BLOCK)
        flat = lambda a: a.reshape(-1).astype(jnp.int32)
        n_rows = n_blocks * MOE_BLOCK
        tail0 = pend[-1] + jnp.arange(-(-n_rows // rt), dtype=F32) * rt
        gap_off = jnp.concatenate([ebase + tot, tail0])
        gap_len = jnp.concatenate([totp - tot, jnp.clip(n_rows - tail0, 0, rt)])

        weg, weu, wed = (w_exp_gate[li].astype(BF16), w_exp_up[li].astype(BF16),
                         w_exp_down[li].astype(BF16))
        used = jnp.sum(c16, axis=1)
        xg = _dispatch(h1b, dk, flat(tile_off), flat(exp_off), flat(c16), flat(used),
                       flat(gap_off), flat(gap_len), n_rows, tm, rt)
        yg = _experts(xg, flat(blk_e), flat(blk_valid), weg, weu, wed)
        out = _combine(flat(tile_off), flat(exp_off), flat(c16), flat(used), yg, dk, gk, h1, h1b,
                       p[li].reshape(n, -1), w_sh_gate[li].astype(BF16),
                       w_sh_up[li].astype(BF16), w_sh_down[li].astype(BF16), row1(ln2_g[li]),
                       row1(ln2_b[li]), w_ple_gate[li].astype(BF16), row1(b_ple_gate[li]),
                       w_ple_proj[li].astype(BF16), alpha, tm, rt)
        h = out
    return out.reshape(bsz, seq, d)
```

```python
import functools
import math

import jax
import jax.numpy as jnp
from jax import lax
from jax.experimental import pallas as pl
from jax.experimental.pallas import tpu as pltpu

CHUNK = 64
HEAD_DIM = 64
ROPE_DIM = 64
ROPE_THETA = 10000.0
A_HEADS = 8
IDX_HEADS = 8
IDX_DIM = 64
IDX_TOPK_MAX = 256
B_HEADS = 4
B_QK_DIM = 64
B_V_DIM = 128
N_EXPERTS = 64
TOP_K = 8
N_GROUPS = 8
TOPK_GROUPS = 4
ROUTED_SCALE = 2.5
LN_EPS = 1e-5
RMS_EPS = 1e-5

LANES = 128
ROW_ALIGN = 16
MOE_BLOCK = 512
VMEM_LIMIT = 56 * 1024 * 1024
NEG_BIG = -1e30
INT_MIN = -(2 ** 31)
LOG2E = math.log2(math.e)
GROUPS = 256

F32 = jnp.float32
BF16 = jnp.bfloat16


def _cparams(*sem):
    return pltpu.CompilerParams(dimension_semantics=sem, vmem_limit_bytes=VMEM_LIMIT)


def _layer_norm(x, g, b):
    mu = jnp.mean(x, axis=-1, keepdims=True)
    xc = x - mu
    var = jnp.mean(xc * xc, axis=-1, keepdims=True)
    return xc * lax.rsqrt(var + LN_EPS) * g + b


def _dot_nt(a, b):
    return lax.dot_general(a, b, (((1,), (1,)), ((), ())), preferred_element_type=F32)


def _split_lane_halves(pair):
    lo = lax.broadcasted_iota(jnp.int32, pair.shape, 1) < (LANES // 2)
    pf = pair.astype(F32)
    return (jnp.where(lo, pf, 0.0).astype(pair.dtype), jnp.where(lo, 0.0, pf).astype(pair.dtype))


def _proj_kernel(x_ref, pos_ref, inv_ref, g_ref, b_ref, wr_ref, wp_ref, wi_ref,
                 h0_ref, aq_ref, ak_ref, iq_ref, ik_ref, bq_ref, bk_ref,
                 av_ref, bv_ref, iw_ref):
    hn = _layer_norm(x_ref[...], g_ref[...], b_ref[...])
    h0_ref[...] = hn
    hb = hn.astype(BF16)

    tm = x_ref.shape[0]
    nq = LANES // (ROPE_DIM // 2)
    tr = tm // nq
    pos = pos_ref[...].astype(F32)
    lane = lax.broadcasted_iota(jnp.int32, (tr, LANES), 1)
    blk = lane // (ROPE_DIM // 2)
    pos_packed = jnp.broadcast_to(pos[0:tr], (tr, LANES))
    for q in range(1, nq):
        pos_packed = jnp.where(blk == q, pos[q * tr:(q + 1) * tr], pos_packed)
    ang = pos_packed * inv_ref[...]
    cos_packed = jnp.cos(ang)
    sin_packed = jnp.sin(ang)
    first_half = (lane % ROPE_DIM) < (ROPE_DIM // 2)

    def spread(table, q):
        own = jnp.where(blk == q, table, 0.0)
        t = own
        for s in range(1, nq):
            t = t + pltpu.roll(own, s * (ROPE_DIM // 2), axis=1)
        return t

    pr = jnp.dot(hb, wr_ref[...], preferred_element_type=F32)
    outs = ((aq_ref, 0, 4, HEAD_DIM ** -0.5 * LOG2E), (ak_ref, 4, 4, 1.0), (iq_ref, 8, 4, 1.0),
            (ik_ref, 12, 1, 1.0), (bq_ref, 13, 4, B_QK_DIM ** -0.5 * LOG2E), (bk_ref, 17, 4, 1.0))
    for q in range(nq):
        cos = spread(cos_packed, q)
        sin = spread(sin_packed, q)
        sin_signed = jnp.where(first_half, -sin, sin)
        rows = slice(q * tr, (q + 1) * tr)
        for ref, g0, ng, scale in outs:
            for j in range(ng):
                v = pr[rows, (g0 + j) * LANES:(g0 + j + 1) * LANES]
                partner = jnp.where(first_half,
                                    pltpu.roll(v, LANES - ROPE_DIM // 2, axis=1),
                                    pltpu.roll(v, ROPE_DIM // 2, axis=1))
                v = v * cos + partner * sin_signed
                if scale != 1.0:
                    v = v * scale
                ref[rows, j * LANES:(j + 1) * LANES] = v.astype(ref.dtype)

    pp = jnp.dot(hb, wp_ref[...], preferred_element_type=F32)
    av_ref[...] = pp[:, 0:512].astype(BF16)
    bv_ref[...] = pp[:, 512:1024].astype(BF16)
    iw_ref[...] = _dot_nt(wi_ref[...], hb)[0:IDX_HEADS, :]


def _proj(x2, pos2, inv, g, b, w_rope, w_plain, w_idx_t, tm):
    n, d = x2.shape
    nr = w_rope.shape[1]
    npl = w_plain.shape[1]
    row = lambda w: pl.BlockSpec((tm, w), lambda i: (i, 0))
    full = lambda a: pl.BlockSpec(a.shape, lambda i: (0, 0))
    out_shape = (
        jax.ShapeDtypeStruct((n, d), F32),
        jax.ShapeDtypeStruct((n, 512), BF16),
        jax.ShapeDtypeStruct((n, 512), BF16),
        jax.ShapeDtypeStruct((n, 512), BF16),
        jax.ShapeDtypeStruct((n, 128), BF16),
        jax.ShapeDtypeStruct((n, 512), BF16),
        jax.ShapeDtypeStruct((n, 512), BF16),
        jax.ShapeDtypeStruct((n, 512), BF16),
        jax.ShapeDtypeStruct((n, 512), BF16),
        jax.ShapeDtypeStruct((IDX_HEADS, n), F32),
    )
    return pl.pallas_call(
        _proj_kernel,
        grid=(n // tm,),
        in_specs=[row(d), row(1), full(inv), full(g), full(b), full(w_rope), full(w_plain),
                  full(w_idx_t)],
        out_specs=(row(d), row(512), row(512), row(512), row(128), row(512), row(512),
                   row(512), row(512), pl.BlockSpec((IDX_HEADS, tm), lambda i: (0, i))),
        out_shape=out_shape,
        compiler_params=_cparams("parallel"),
    )(x2, pos2, inv, g, b, w_rope, w_plain, w_idx_t)


def _sortable(s):
    i = lax.bitcast_convert_type(s, jnp.int32)
    return i ^ ((i >> 31) & 0x7FFFFFFF)


def _select_kernel(iq_ref, iw_ref, ik_ref, bias_ref, keys_ref, gmax_ref, *, tq, tk, n_keep,
                   w_scale):
    i = pl.program_id(1)
    key_drop = 32 - 8 * jnp.dtype(BF16).itemsize
    seq = ik_ref.shape[1]
    n_t = ((i + 1) * tq + tk - 1) // tk

    q_chunk = (i * tq + lax.broadcasted_iota(jnp.int32, (tk, tq), 1)) // CHUNK
    k_row = lax.broadcasted_iota(jnp.int32, (tk, tq), 0)
    wb = iw_ref[...] * w_scale
    q_halves = []
    for hp in range(IDX_HEADS // 2):
        q_halves.extend(_split_lane_halves(iq_ref[0, :, hp * LANES:(hp + 1) * LANES]))

    def score_tile(t, carry, boundary):
        off = pl.multiple_of(t * tk, tk)
        kt = ik_ref[0, pl.ds(off, tk), :]
        sc = jnp.zeros((tk, tq), F32)
        for h in range(IDX_HEADS):
            rel = jnp.maximum(_dot_nt(kt, q_halves[h]), 0.0)
            sc = sc + wb[h:h + 1, :] * rel
        key = _sortable(sc.astype(BF16).astype(F32) + 0.0) >> key_drop
        if boundary:
            key = jnp.where(((off + k_row) // CHUNK) <= q_chunk, key, INT_MIN)
        keys_ref[pl.ds(off, tk), :] = key
        gm = key[0:GROUPS, :]
        for g0 in range(GROUPS, tk, GROUPS):
            gm = jnp.maximum(gm, key[g0:g0 + GROUPS, :])
        gmax_ref[...] = jnp.maximum(gmax_ref[...], gm)
        return carry

    gmax_ref[...] = jnp.full(gmax_ref.shape, INT_MIN, jnp.int32)
    n_full = (i * tq + CHUNK) // tk
    lax.fori_loop(0, n_full, functools.partial(score_tile, boundary=False), 0)
    lax.fori_loop(n_full, n_t, functools.partial(score_tile, boundary=True), 0)

    def count_ge(cand):
        def body(t, acc):
            off = pl.multiple_of(t * tk, tk)
            m = (keys_ref[pl.ds(off, tk), :] >= cand).astype(jnp.int32)
            return acc + jnp.sum(m.reshape(tk // 8, 8, tq), axis=0)
        acc = lax.fori_loop(0, n_t, body, jnp.zeros((8, tq), jnp.int32))
        return jnp.sum(acc, axis=0, keepdims=True)

    gmax = gmax_ref[...]
    n_adm = ((i * tq + lax.broadcasted_iota(jnp.int32, (1, tq), 1)) // CHUNK + 1) * CHUNK
    few = n_adm <= n_keep
    lo0 = jnp.where(few, INT_MIN + 1, jnp.min(gmax, axis=0, keepdims=True))
    hi0 = jnp.where(few, INT_MIN + 2, jnp.max(gmax, axis=0, keepdims=True) + 1)

    def probe(state):
        lo, hi, n_hi, thr, n_thr, done, _, it = state
        mid = (lo >> 1) + (hi >> 1) + (lo & hi & 1)
        closed = mid == lo
        cnt = count_ge(mid)
        exact = cnt == n_keep
        up = cnt >= n_keep
        live = done == 0
        settle = live & (exact | closed)
        thr = jnp.where(settle, mid, thr)
        n_thr = jnp.where(settle, cnt, n_thr)
        lo = jnp.where(live & up, mid, lo)
        down = live & jnp.logical_not(up)
        hi = jnp.where(down, mid, hi)
        n_hi = jnp.where(down, cnt, n_hi)
        done = jnp.where(exact | closed, 1, done)
        return lo, hi, n_hi, thr, n_thr, done, jnp.min(done), it + 1

    def searching(state):
        return (state[6] == 0) & (state[7] < 34)

    done0 = few.astype(jnp.int32)
    none = jnp.zeros((1, tq), jnp.int32)
    state = lax.while_loop(
        searching, probe, (lo0, hi0, none, lo0, none, done0, jnp.min(done0), jnp.int32(0)))
    n_above, thr, n_ge = state[2], state[3], state[4]

    tied = n_ge > n_keep
    any_tied = jnp.max(tied.astype(jnp.int32)) > 0
    tau = thr

    def write_bias(off, keep_t):
        bias_t = jnp.where(keep_t, 0.0, NEG_BIG)
        bias_ref[0, :, pl.ds(off, tk)] = bias_t.T.astype(BF16)

    @pl.when(jnp.logical_not(any_tied))
    def _():
        def write_tile(t, carry):
            off = pl.multiple_of(t * tk, tk)
            write_bias(off, keys_ref[pl.ds(off, tk), :] >= tau)
            return carry
        lax.fori_loop(0, n_t, write_tile, 0)

    @pl.when(any_tied)
    def _():
        room = jnp.where(tied, (n_keep - n_above).astype(F32), float(seq))
        upto = (lax.broadcasted_iota(jnp.int32, (tk, tk), 1)
                <= lax.broadcasted_iota(jnp.int32, (tk, tk), 0)).astype(F32).astype(BF16)

        def write_tile(t, seen_eq):
            off = pl.multiple_of(t * tk, tk)
            key = keys_ref[pl.ds(off, tk), :]
            eq_f = jnp.where(key == tau, 1.0, 0.0)
            eq_rank = seen_eq + jnp.dot(upto, eq_f.astype(BF16), preferred_element_type=F32)
            order = jnp.where(key == tau, eq_rank, jnp.where(key > tau, 0.0, 2.0 * seq))
            write_bias(off, order <= room)
            return seen_eq + jnp.sum(eq_f, axis=0, keepdims=True)

        lax.fori_loop(0, n_t, write_tile, jnp.zeros((1, tq), F32))

    def blank_tile(t, carry):
        off = pl.multiple_of(t * tk, tk)
        bias_ref[0, :, pl.ds(off, tk)] = jnp.full((tq, tk), NEG_BIG, BF16)
        return carry

    lax.fori_loop(n_t, seq // tk, blank_tile, 0)


def _select(iq, iw, ik2, n_keep, tq, tk):
    bsz, seq, _ = iq.shape
    kern = functools.partial(_select_kernel, tq=tq, tk=tk, n_keep=n_keep,
                             w_scale=(IDX_HEADS ** -0.5) * (IDX_DIM ** -0.5))
    return pl.pallas_call(
        kern,
        grid=(bsz, seq // tq),
        in_specs=[pl.BlockSpec((1, tq, 512), lambda b, i: (b, i, 0)),
                  pl.BlockSpec((IDX_HEADS, tq), lambda b, i: (0, b * (seq // tq) + i)),
                  pl.BlockSpec((1, seq, 128), lambda b, i: (b, 0, 0))],
        out_specs=pl.BlockSpec((1, tq, seq), lambda b, i: (b, i, 0)),
        out_shape=jax.ShapeDtypeStruct((bsz, seq, seq), BF16),
        scratch_shapes=[pltpu.VMEM((seq, tq), jnp.int32), pltpu.VMEM((GROUPS, tq), jnp.int32)],
        compiler_params=_cparams("parallel", "parallel"),
    )(iq, iw, ik2)


def _flash_update(j, s, v_ext, m_ref, acc_ref):
    tk = s.shape[1]
    m_old = m_ref[j]
    m_new = jnp.maximum(m_old, jnp.max(s, axis=1, keepdims=True))
    alpha = jnp.exp2(m_old - m_new)
    p = jnp.exp2(s - jnp.concatenate([m_new] * (tk // LANES), axis=1))
    m_ref[j] = m_new
    acc_ref[j] = (acc_ref[j] * jnp.concatenate([alpha, alpha], axis=1)
                  + jnp.dot(p.astype(BF16), v_ext, preferred_element_type=F32))


def _dsa_kernel(q_ref, k_ref, v_ref, bias_ref, o_ref, m_ref, acc_ref, *, tq, tk):
    i = pl.program_id(1)
    n_keys = (i + 1) * tq
    lo_q = lax.broadcasted_iota(jnp.int32, (tq, LANES), 1) < HEAD_DIM
    q_halves = []
    for hp in range(A_HEADS // 2):
        q_halves.extend(_split_lane_halves(q_ref[0, :, hp * LANES:(hp + 1) * LANES]))

    m_ref[...] = jnp.full(m_ref.shape, NEG_BIG, F32)
    acc_ref[...] = jnp.zeros(acc_ref.shape, F32)

    def kv_tile(off, width):
        bias = bias_ref[0, :, pl.ds(off, width)].astype(F32)
        ones = jnp.ones((width, LANES), BF16)
        for hp in range(A_HEADS // 2):
            kp = k_ref[0, pl.ds(off, width), hp * LANES:(hp + 1) * LANES]
            v_ext = jnp.concatenate(
                [v_ref[0, pl.ds(off, width), hp * LANES:(hp + 1) * LANES], ones], axis=1)
            for half in range(2):
                h = 2 * hp + half
                _flash_update(h, _dot_nt(q_halves[h], kp) + bias, v_ext, m_ref, acc_ref)

    def kv_step(t, carry):
        kv_tile(pl.multiple_of(t * tk, tk), tk)
        return carry

    n_whole = n_keys // tk
    lax.fori_loop(0, n_whole, kv_step, 0)
    if tk > tq:
        @pl.when(n_keys % tk != 0)
        def _():
            kv_tile(pl.multiple_of(n_whole * tk, tq), tq)

    for hp in range(A_HEADS // 2):
        a0 = acc_ref[2 * hp]
        a1 = acc_ref[2 * hp + 1]
        o = jnp.where(lo_q, a0[:, :LANES] / a0[:, LANES:], a1[:, :LANES] / a1[:, LANES:])
        o_ref[0, :, hp * LANES:(hp + 1) * LANES] = o.astype(o_ref.dtype)


def _dsa(aq, ak, av, bias, tq, tk):
    bsz, seq, w = aq.shape
    kern = functools.partial(_dsa_kernel, tq=tq, tk=tk)
    resident = lambda: pl.BlockSpec((1, seq, w), lambda b, i: (b, 0, 0),
                                    pipeline_mode=pl.Buffered(1))
    return pl.pallas_call(
        kern,
        grid=(bsz, seq // tq),
        in_specs=[pl.BlockSpec((1, tq, w), lambda b, i: (b, i, 0)),
                  resident(), resident(),
                  pl.BlockSpec((1, tq, seq), lambda b, i: (b, i, 0))],
        out_specs=pl.BlockSpec((1, tq, w), lambda b, i: (b, i, 0)),
        out_shape=jax.ShapeDtypeStruct((bsz, seq, w), BF16),
        scratch_shapes=[pltpu.VMEM((A_HEADS, tq, LANES), F32),
                        pltpu.VMEM((A_HEADS, tq, 2 * LANES), F32)],
        compiler_params=_cparams("parallel", "arbitrary"),
    )(aq, ak, av, bias)


def _diff_kernel(q_ref, k_ref, v_ref, lq1_ref, lk1_ref, lq2_ref, lk2_ref, g_ref, o_ref,
                 m_ref, acc_ref, *, tq, tk, lambda_init):
    i = pl.program_id(1)
    q_halves = []
    for hb in range(B_HEADS):
        q_halves.extend(_split_lane_halves(q_ref[0, :, hb * LANES:(hb + 1) * LANES]))
    lam = (jnp.exp(jnp.sum(lq1_ref[...] * lk1_ref[...], keepdims=True))
           - jnp.exp(jnp.sum(lq2_ref[...] * lk2_ref[...], keepdims=True)) + lambda_init)

    m_ref[...] = jnp.full(m_ref.shape, NEG_BIG, F32)
    acc_ref[...] = jnp.zeros(acc_ref.shape, F32)

    def tile(off, width, masked):
        ones = jnp.ones((width, LANES), BF16)
        if masked:
            row_chunk = (i * tq + lax.broadcasted_iota(jnp.int32, (tq, width), 0)) // CHUNK
            col_chunk = (off + lax.broadcasted_iota(jnp.int32, (tq, width), 1)) // CHUNK
            ok = col_chunk <= row_chunk
        for hb in range(B_HEADS):
            kp = k_ref[0, pl.ds(off, width), hb * LANES:(hb + 1) * LANES]
            v_ext = jnp.concatenate(
                [v_ref[0, pl.ds(off, width), hb * LANES:(hb + 1) * LANES], ones], axis=1)
            for mp in range(2):
                j = 2 * hb + mp
                s = _dot_nt(q_halves[j], kp)
                if masked:
                    s = jnp.where(ok, s, NEG_BIG)
                _flash_update(j, s, v_ext, m_ref, acc_ref)

    def kv_step(t, carry):
        tile(pl.multiple_of(t * tk, tk), tk, False)
        return carry

    n_whole = (i * tq) // tk
    lax.fori_loop(0, n_whole, kv_step, 0)
    if tk > tq:
        @pl.when((i * tq) % tk != 0)
        def _():
            tile(pl.multiple_of(n_whole * tk, tq), tq, False)
    tile(pl.multiple_of(i * tq, tq), tq, True)

    for hb in range(B_HEADS):
        a1 = acc_ref[2 * hb]
        a2 = acc_ref[2 * hb + 1]
        o = a1[:, :LANES] / a1[:, LANES:] - lam * (a2[:, :LANES] / a2[:, LANES:])
        o = o * lax.rsqrt(jnp.mean(o * o, axis=-1, keepdims=True) + RMS_EPS)
        o = o * g_ref[...] * (1.0 - lambda_init)
        o_ref[0, :, hb * LANES:(hb + 1) * LANES] = o.astype(o_ref.dtype)


def _diff(bq, bk, bv, lq1, lk1, lq2, lk2, subln_g, lambda_init, tq, tk):
    bsz, seq, w = bq.shape
    kern = functools.partial(_diff_kernel, tq=tq, tk=tk, lambda_init=lambda_init)
    resident = lambda: pl.BlockSpec((1, seq, w), lambda b, i: (b, 0, 0),
                                    pipeline_mode=pl.Buffered(1))
    small = lambda a: pl.BlockSpec(a.shape, lambda b, i: (0, 0))
    return pl.pallas_call(
        kern,
        grid=(bsz, seq // tq),
        in_specs=[pl.BlockSpec((1, tq, w), lambda b, i: (b, i, 0)),
                  resident(), resident(),
                  small(lq1), small(lk1), small(lq2), small(lk2), small(subln_g)],
        out_specs=pl.BlockSpec((1, tq, w), lambda b, i: (b, i, 0)),
        out_shape=jax.ShapeDtypeStruct((bsz, seq, w), BF16),
        scratch_shapes=[pltpu.VMEM((2 * B_HEADS, tq, LANES), F32),
                        pltpu.VMEM((2 * B_HEADS, tq, 2 * LANES), F32)],
        compiler_params=_cparams("parallel", "arbitrary"),
    )(bq, bk, bv, lq1, lk1, lq2, lk2, subln_g)


def _first_argmax_rows(v, row):
    m = jnp.max(v, axis=0, keepdims=True)
    idx = jnp.min(jnp.where(v == m, row, v.shape[0]), axis=0, keepdims=True)
    return m, idx


def _scatter_rows(dk, gk, r0, nrows):
    r_iota = (lax.broadcasted_iota(jnp.int32, (nrows, dk.shape[1]), 0) + r0).astype(F32)
    out = jnp.zeros(r_iota.shape, F32)
    for k in range(dk.shape[0]):
        out = jnp.where(r_iota == dk[k:k + 1, :], 1.0 if gk is None else gk[k:k + 1, :], out)
    return out


def _oproj_kernel(oa_ref, ob_ref, h0_ref, woa_ref, wob_ref, g_ref, b_ref, wrt_ref, rb_ref,
                  h1_ref, h1b_ref, dk_ref, gk_ref, cnt_ref, *, alpha):
    mix = (jnp.dot(oa_ref[...], woa_ref[...], preferred_element_type=F32)
           + jnp.dot(ob_ref[...], wob_ref[...], preferred_element_type=F32))
    h1 = _layer_norm(alpha * h0_ref[...] + mix, g_ref[...], b_ref[...])
    h1_ref[...] = h1
    h1b = h1.astype(BF16)
    h1b_ref[...] = h1b

    logits = lax.dot_general(wrt_ref[...], h1, (((1,), (1,)), ((), ())),
                             precision=lax.Precision.HIGHEST, preferred_element_type=F32)
    scores = jax.nn.sigmoid(logits)
    biased = scores + rb_ref[...]
    tm = scores.shape[1]
    per_g = N_EXPERTS // N_GROUPS
    row8 = lax.broadcasted_iota(jnp.int32, (per_g, tm), 0)

    gs = []
    for g in range(N_GROUPS):
        blk = biased[g * per_g:(g + 1) * per_g, :]
        m1, i1 = _first_argmax_rows(blk, row8)
        m2 = jnp.max(jnp.where(row8 == i1, -jnp.inf, blk), axis=0, keepdims=True)
        gs.append(m1 + m2)
    gscore = jnp.concatenate(gs, axis=0)
    rowg = lax.broadcasted_iota(jnp.int32, (N_GROUPS, tm), 0)
    gsel = jnp.zeros((N_GROUPS, tm), F32)
    for _ in range(TOPK_GROUPS):
        _, ig = _first_argmax_rows(jnp.where(gsel > 0.0, -jnp.inf, gscore), rowg)
        gsel = jnp.where(rowg == ig, 1.0, gsel)

    rowe = lax.broadcasted_iota(jnp.int32, (N_EXPERTS, tm), 0)
    live = jnp.concatenate(
        [jnp.broadcast_to(gsel[g:g + 1, :], (per_g, tm)) for g in range(N_GROUPS)], axis=0)
    esel = jnp.zeros((N_EXPERTS, tm), F32)
    for _ in range(TOP_K):
        cand = jnp.where(live > 0.0, biased, -jnp.inf)
        m = jnp.max(cand, axis=0, keepdims=True)
        idx = jnp.min(jnp.where((live > 0.0) & (cand == m), rowe, N_EXPERTS),
                      axis=0, keepdims=True)
        hit = rowe == idx
        esel = jnp.where(hit, 1.0, esel)
        live = jnp.where(hit, 0.0, live)
    picked = jnp.where(esel > 0.0, scores, 0.0)
    denom = jnp.sum(picked, axis=0, keepdims=True)
    gate = picked / denom * ROUTED_SCALE

    sel_b = esel.astype(BF16)
    before_t = (lax.broadcasted_iota(jnp.int32, (tm, tm), 0)
                < lax.broadcasted_iota(jnp.int32, (tm, tm), 1)).astype(F32).astype(BF16)
    before_e = (lax.broadcasted_iota(jnp.int32, (N_EXPERTS, N_EXPERTS), 1)
                < lax.broadcasted_iota(jnp.int32, (N_EXPERTS, N_EXPERTS), 0)).astype(F32).astype(BF16)
    rank = jnp.dot(sel_b, before_t, preferred_element_type=F32)
    order = jnp.dot(before_e, sel_b, preferred_element_type=F32)
    cnt = jnp.sum(esel, axis=1, keepdims=True)
    cnt16 = jnp.floor((cnt + (ROW_ALIGN - 1)) / ROW_ALIGN) * ROW_ALIGN
    off = jnp.dot(before_e, jnp.broadcast_to(cnt16, (N_EXPERTS, tm)).astype(BF16),
                  preferred_element_type=F32)
    dest = off + rank
    dks, gks = [], []
    for k in range(TOP_K):
        kth = (esel > 0.0) & (order == k)
        dks.append(jnp.sum(jnp.where(kth, dest, 0.0), axis=0, keepdims=True))
        gks.append(jnp.sum(jnp.where(kth, gate, 0.0), axis=0, keepdims=True))
    dk = jnp.concatenate(dks, axis=0)
    dk_ref[...] = dk
    gk_ref[...] = jnp.concatenate(gks, axis=0)
    counts = _dot_nt(jnp.ones((8, tm), BF16), sel_b)
    cnt_ref[0] = jnp.concatenate([counts, jnp.zeros((8, LANES - N_EXPERTS), F32)], axis=1)


def _oproj(oa, ob, h0, woa, wob, g, b, wrt, rb, alpha, tm):
    n, d = h0.shape
    nt = n // tm
    row = lambda w: pl.BlockSpec((tm, w), lambda i: (i, 0))
    col = lambda: pl.BlockSpec((TOP_K, tm), lambda i: (0, i))
    full = lambda a: pl.BlockSpec(a.shape, lambda i: (0, 0))
    kt = jax.ShapeDtypeStruct((TOP_K, n), F32)
    return pl.pallas_call(
        functools.partial(_oproj_kernel, alpha=alpha),
        grid=(nt,),
        in_specs=[row(512), row(512), row(d), full(woa), full(wob), full(g), full(b),
                  full(wrt), full(rb)],
        out_specs=(row(d), row(d), col(), col(),
                   pl.BlockSpec((1, 8, LANES), lambda i: (i, 0, 0))),
        out_shape=(jax.ShapeDtypeStruct((n, d), F32),
                   jax.ShapeDtypeStruct((n, d), BF16),
                   kt,
                   kt,
                   jax.ShapeDtypeStruct((nt, 8, LANES), F32)),
        compiler_params=_cparams("parallel"),
    )(oa, ob, h0, woa, wob, g, b, wrt, rb)


def _swiglu(x, wg, wu, wd):
    hid = (jax.nn.silu(jnp.dot(x, wg, preferred_element_type=F32))
           * jnp.dot(x, wu, preferred_element_type=F32))
    return jnp.dot(hid.astype(BF16), wd, preferred_element_type=F32)


def _dispatch_kernel(tile_off_ref, exp_off_ref, len_ref, used_ref, gap_off_ref, gap_len_ref,
                     h1b_ref, dk_ref, xg_ref, xs_ref, zero_ref, run_sem, gap_sem, *, rb):
    i = pl.program_id(0)
    nt = pl.num_programs(0)
    rt = xs_ref.shape[1]
    slot = i % 2

    def wait_runs(step):
        n = pl.multiple_of(used_ref[step], ROW_ALIGN)
        s = step % 2
        pltpu.make_async_copy(xs_ref.at[s, pl.ds(0, n), :], xg_ref.at[pl.ds(0, n), :],
                              run_sem.at[s]).wait()

    @pl.when(i == 0)
    def _():
        zero_ref[...] = jnp.zeros(zero_ref.shape, zero_ref.dtype)

        def issue_gap(g, carry):
            n = pl.multiple_of(gap_len_ref[g], ROW_ALIGN)

            @pl.when(n > 0)
            def _():
                pltpu.make_async_copy(
                    zero_ref.at[pl.ds(0, n), :],
                    xg_ref.at[pl.ds(pl.multiple_of(gap_off_ref[g], ROW_ALIGN), n), :],
                    gap_sem).start()
            return carry

        lax.fori_loop(0, gap_len_ref.shape[0], issue_gap, 0)

    @pl.when(i >= 2)
    def _():
        wait_runs(i - 2)

    used = used_ref[i]
    dk = dk_ref[...]

    def sort_rows(r0):
        onehot = _scatter_rows(dk, None, r0, rb).astype(BF16)
        rows = jnp.dot(onehot, h1b_ref[...], preferred_element_type=F32)
        xs_ref[slot, r0:r0 + rb, :] = rows.astype(BF16)

    always = h1b_ref.shape[0] * TOP_K // rb * rb
    for r0 in range(0, always, rb):
        sort_rows(r0)
    for r0 in range(always, rt, rb):
        @pl.when(r0 < used)
        def _():
            sort_rows(r0)

    def issue_run(e, carry):
        r = i * N_EXPERTS + e
        n = pl.multiple_of(len_ref[r], ROW_ALIGN)

        @pl.when(n > 0)
        def _():
            pltpu.make_async_copy(
                xs_ref.at[slot, pl.ds(pl.multiple_of(tile_off_ref[r], ROW_ALIGN), n), :],
                xg_ref.at[pl.ds(pl.multiple_of(exp_off_ref[r], ROW_ALIGN), n), :],
                run_sem.at[slot]).start()
        return carry

    lax.fori_loop(0, N_EXPERTS, issue_run, 0)

    @pl.when(i == nt - 1)
    def _():
        @pl.when(i >= 1)
        def _():
            wait_runs(i - 1)
        wait_runs(i)

        def gap_rows(g, rows):
            return rows + gap_len_ref[g]
        n_gap = pl.multiple_of(lax.fori_loop(0, gap_len_ref.shape[0], gap_rows, 0), ROW_ALIGN)

        @pl.when(n_gap > 0)
        def _():
            pltpu.make_async_copy(xg_ref.at[pl.ds(0, n_gap), :], xg_ref.at[pl.ds(0, n_gap), :],
                                  gap_sem).wait()


def _dispatch(h1b, dk, tile_off, exp_off, lens, used, gap_off, gap_len, n_dst_rows, tm, rt):
    n, d = h1b.shape
    return pl.pallas_call(
        functools.partial(_dispatch_kernel, rb=512),
        grid_spec=pltpu.PrefetchScalarGridSpec(
            num_scalar_prefetch=6,
            grid=(n // tm,),
            in_specs=[pl.BlockSpec((tm, d), lambda i, *_: (i, 0)),
                      pl.BlockSpec((TOP_K, tm), lambda i, *_: (0, i))],
            out_specs=pl.BlockSpec(memory_space=pl.ANY),
            scratch_shapes=[pltpu.VMEM((2, rt, d), BF16), pltpu.VMEM((rt, d), BF16),
                            pltpu.SemaphoreType.DMA((2,)), pltpu.SemaphoreType.DMA(())]),
        out_shape=jax.ShapeDtypeStruct((n_dst_rows, d), BF16),
        compiler_params=_cparams("arbitrary"),
    )(tile_off, exp_off, lens, used, gap_off, gap_len, h1b, dk)


def _expert_kernel(blk_e_ref, blk_valid_ref, x_ref, wg_ref, wu_ref, wd_ref, y_ref):
    b = pl.program_id(0)
    valid = blk_valid_ref[b]

    @pl.when(valid > 0)
    def _():
        y_ref[...] = _swiglu(x_ref[...], wg_ref[...], wu_ref[...], wd_ref[...]).astype(BF16)

    @pl.when(valid <= 0)
    def _():
        y_ref[...] = jnp.zeros(y_ref.shape, BF16)


def _experts(xg, blk_e, blk_valid, wg, wu, wd):
    nr, d = xg.shape
    f = wg.shape[2]
    rows = pl.BlockSpec((MOE_BLOCK, d), lambda b, be, bv: (b, 0))
    return pl.pallas_call(
        _expert_kernel,
        grid_spec=pltpu.PrefetchScalarGridSpec(
            num_scalar_prefetch=2,
            grid=(nr // MOE_BLOCK,),
            in_specs=[rows,
                      pl.BlockSpec((None, d, f), lambda b, be, bv: (be[b], 0, 0)),
                      pl.BlockSpec((None, d, f), lambda b, be, bv: (be[b], 0, 0)),
                      pl.BlockSpec((None, f, d), lambda b, be, bv: (be[b], 0, 0))],
            out_specs=rows),
        out_shape=jax.ShapeDtypeStruct((nr, d), BF16),
        compiler_params=_cparams("parallel"),
    )(blk_e, blk_valid, xg, wg, wu, wd)


def _combine_kernel(tile_off_ref, exp_off_ref, len_ref, used_ref, yg_ref, dk_ref, gk_ref, h1_ref,
                    h1b_ref, p_ref, sg_ref, su_ref, sd_ref, g_ref, b_ref, wg_ref, bg_ref, wp_ref,
                    o_ref, ys_ref, ffn_ref, sem, *, alpha, rb):
    i = pl.program_id(0)
    nt = pl.num_programs(0)
    rt = ys_ref.shape[1]
    slot = i % 2

    def fetch_runs(tile):
        s = tile % 2

        def issue_run(e, carry):
            r = tile * N_EXPERTS + e
            n = pl.multiple_of(len_ref[r], ROW_ALIGN)

            @pl.when(n > 0)
            def _():
                pltpu.make_async_copy(
                    yg_ref.at[pl.ds(pl.multiple_of(exp_off_ref[r], ROW_ALIGN), n), :],
                    ys_ref.at[s, pl.ds(pl.multiple_of(tile_off_ref[r], ROW_ALIGN), n), :],
                    sem.at[s]).start()
            return carry

        lax.fori_loop(0, N_EXPERTS, issue_run, 0)

    @pl.when(i == 0)
    def _():
        ys_ref[...] = jnp.zeros(ys_ref.shape, ys_ref.dtype)
        fetch_runs(i)

    @pl.when(i + 1 < nt)
    def _():
        fetch_runs(i + 1)

    dk = dk_ref[...]
    gk = gk_ref[...]
    ffn = _swiglu(h1b_ref[...], sg_ref[...], su_ref[...], sd_ref[...])
    used = used_ref[i]
    n_used = pl.multiple_of(used, ROW_ALIGN)
    pltpu.make_async_copy(yg_ref.at[pl.ds(0, n_used), :], ys_ref.at[slot, pl.ds(0, n_used), :],
                          sem.at[slot]).wait()

    def routed(r0):
        weights = _scatter_rows(dk, gk, r0, rb).astype(BF16)
        return lax.dot_general(weights, ys_ref[slot, r0:r0 + rb, :], (((0,), (0,)), ((), ())),
                               preferred_element_type=F32)

    tm = h1_ref.shape[0]
    always = tm * TOP_K // rb * rb
    for r0 in range(0, always, rb):
        ffn = ffn + routed(r0)
    ffn_ref[...] = ffn
    for r0 in range(always, rt, rb):
        @pl.when(r0 < used)
        def _():
            ffn_ref[...] += routed(r0)
    h2 = _layer_norm(alpha * h1_ref[...] + ffn_ref[...], g_ref[...], b_ref[...])
    emb_gate = jax.nn.sigmoid(jnp.dot(h2.astype(BF16), wg_ref[...], preferred_element_type=F32)
                              + bg_ref[...])
    emb = jnp.dot(p_ref[...].astype(BF16), wp_ref[...], preferred_element_type=F32)
    o_ref[...] = h2 + emb_gate * emb


def _combine(tile_off, exp_off, lens, used, yg, dk, gk, h1, h1b, p2, sg, su, sd, g, b, wg, bg, wp,
             alpha, tm, rt):
    n, d = h1.shape
    row = lambda w: pl.BlockSpec((tm, w), lambda i, *_: (i, 0))
    col = lambda: pl.BlockSpec((TOP_K, tm), lambda i, *_: (0, i))
    full = lambda a: pl.BlockSpec(a.shape, lambda i, *_: (0, 0))
    return pl.pallas_call(
        functools.partial(_combine_kernel, alpha=alpha, rb=512),
        grid_spec=pltpu.PrefetchScalarGridSpec(
            num_scalar_prefetch=4,
            grid=(n // tm,),
            in_specs=[pl.BlockSpec(memory_space=pl.ANY), col(), col(), row(d),
                      row(d), row(p2.shape[1]), full(sg), full(su), full(sd), full(g), full(b),
                      full(wg), full(bg), full(wp)],
            out_specs=row(d),
            scratch_shapes=[pltpu.VMEM((2, rt, d), yg.dtype), pltpu.VMEM((tm, d), F32),
                            pltpu.SemaphoreType.DMA((2,))]),
        out_shape=jax.ShapeDtypeStruct((n, d), F32),
        compiler_params=_cparams("arbitrary"),
    )(tile_off, exp_off, lens, used, yg, dk, gk, h1, h1b, p2, sg, su, sd, g, b, wg, bg, wp)


def _tiles(seq):
    tm = min(256, seq)
    tq_sel = min(256, seq)
    tk_sel = min(512, seq)
    tq_att = min(256, seq)
    tk_att = min(512, seq)
    rt = -(-(tm * TOP_K + N_EXPERTS * (ROW_ALIGN - 1)) // 512) * 512
    return tm, tq_sel, tk_sel, tq_att, tk_att, rt


def kernel(x, p, positions, ln_emb_g, ln_emb_b, w_in, w_o, diff_lq1, diff_lk1, diff_lq2, diff_lk2,
           diff_subln_g, ln1_g, ln1_b, w_router, router_bias, w_exp_gate, w_exp_up, w_exp_down,
           w_sh_gate, w_sh_up, w_sh_down, ln2_g, ln2_b, w_ple_gate, b_ple_gate, w_ple_proj):
    bsz, seq, d = x.shape
    depth = w_in.shape[0]
    n = bsz * seq
    n_keep = min(IDX_TOPK_MAX, seq // 4)
    alpha = (2.0 * depth) ** 0.25
    tm, tq_sel, tk_sel, tq_att, tk_att, rt = _tiles(seq)
    n_blocks = -(-((n // tm) * (tm * TOP_K + N_EXPERTS * (ROW_ALIGN - 1))
                   + N_EXPERTS * (MOE_BLOCK - 1)) // MOE_BLOCK)
    row1 = lambda v: v.reshape(1, -1)

    inv = ROPE_THETA ** (-jnp.arange(0, ROPE_DIM, 2, dtype=F32) / ROPE_DIM)
    inv_lanes = jnp.tile(inv, LANES // (ROPE_DIM // 2)).reshape(1, LANES)
    pos2 = positions.reshape(n, 1)

    h = x.reshape(n, d)
    out = None
    for li in range(depth):
        lambda_init = 0.8 - 0.6 * math.exp(-0.3 * li)
        w = w_in[li]
        aqw, akw, avw = w[:, 0:512], w[:, 512:1024], w[:, 1024:1536]
        iqw, ikw, iww = w[:, 1536:2048], w[:, 2048:2112], w[:, 2112:2120]
        bqw, bkw, bvw = w[:, 2120:2632], w[:, 2632:3144], w[:, 3144:3656]
        w_rope = jnp.concatenate([aqw, akw, iqw, ikw, ikw, bqw, bkw], axis=1).astype(BF16)
        w_plain = jnp.concatenate([avw, bvw], axis=1).astype(BF16)
        w_idx_t = jnp.concatenate(
            [iww.T, jnp.zeros((LANES - IDX_HEADS, d), F32)], axis=0).astype(BF16)

        (h0, aq, ak, iq, ik2, bq, bk, av, bv, iw) = _proj(
            h, pos2, inv_lanes, row1(ln_emb_g), row1(ln_emb_b), w_rope, w_plain, w_idx_t, tm)
        assert depth == 1

        r3 = lambda a: a.reshape(bsz, seq, a.shape[-1])
        bias = _select(r3(iq), iw, r3(ik2), n_keep, tq_sel, tk_sel)
        out_a = _dsa(r3(aq), r3(ak), r3(av), bias, tq_att, tk_att)
        out_b = _diff(r3(bq), r3(bk), r3(bv), row1(diff_lq1[li]), row1(diff_lk1[li]),
                      row1(diff_lq2[li]), row1(diff_lk2[li]), row1(diff_subln_g[li]),
                      lambda_init, tq_att, tk_att)

        wo = w_o[li].astype(BF16)
        h1, h1b, dk, gk, cnt_out = _oproj(
            out_a.reshape(n, -1), out_b.reshape(n, -1), h0, wo[0:512], wo[512:1024],
            row1(ln1_g[li]), row1(ln1_b[li]), w_router[li].T, router_bias[li].reshape(-1, 1),
            alpha, tm)

        nt = n // tm
        c16 = jnp.ceil(cnt_out[:, 0, :N_EXPERTS] / ROW_ALIGN) * ROW_ALIGN
        tri = lambda m: jnp.tril(jnp.ones((m, m), F32))
        hi = lax.Precision.HIGHEST
        tile_off = jnp.dot(c16, tri(N_EXPERTS).T, precision=hi) - c16
        cum_tiles = jnp.dot(tri(nt), c16, precision=hi)
        tot = cum_tiles[-1]
        totp = jnp.ceil(tot / MOE_BLOCK) * MOE_BLOCK
        pend = jnp.dot(totp, tri(N_EXPERTS).T, precision=hi)
        ebase = pend - totp
        exp_off = ebase[None, :] + cum_tiles - c16
        blk0 = jnp.arange(n_blocks, dtype=F32) * MOE_BLOCK
        blk_e = jnp.minimum(jnp.sum(pend[None, :] <= blk0[:, None], axis=1), N_EXPERTS - 1)
        onehot_e = (blk_e[:, None] == jnp.arange(N_EXPERTS)[None, :]).astype(F32)
        blk_end = jnp.dot(onehot_e, ebase + tot, precision=hi)
        blk_valid = jnp.clip(blk_end - blk0, 0, MOE_BLOCK)
        flat = lambda a: a.reshape(-1).astype(jnp.int32)
        n_rows = n_blocks * MOE_BLOCK
        tail0 = pend[-1] + jnp.arange(-(-n_rows // rt), dtype=F32) * rt
        gap_off = jnp.concatenate([ebase + tot, tail0])
        gap_len = jnp.concatenate([totp - tot, jnp.clip(n_rows - tail0, 0, rt)])

        weg, weu, wed = (w_exp_gate[li].astype(BF16), w_exp_up[li].astype(BF16),
                         w_exp_down[li].astype(BF16))
        used = jnp.sum(c16, axis=1)
        xg = _dispatch(h1b, dk, flat(tile_off), flat(exp_off), flat(c16), flat(used),
                       flat(gap_off), flat(gap_len), n_rows, tm, rt)
        yg = _experts(xg, flat(blk_e), flat(blk_valid), weg, weu, wed)
        out = _combine(flat(tile_off), flat(exp_off), flat(c16), flat(used), yg, dk, gk, h1, h1b,
                       p[li].reshape(n, -1), w_sh_gate[li].astype(BF16),
                       w_sh_up[li].astype(BF16), w_sh_down[li].astype(BF16), row1(ln2_g[li]),
                       row1(ln2_b[li]), w_ple_gate[li].astype(BF16), row1(b_ple_gate[li]),
                       w_ple_proj[li].astype(BF16), alpha, tm, rt)
        h = out
    return out.reshape(bsz, seq, d)
```

```python
import functools
import math

import jax
import jax.numpy as jnp
from jax import lax
from jax.experimental import pallas as pl
from jax.experimental.pallas import tpu as pltpu

CHUNK = 64
HEAD_DIM = 64
ROPE_DIM = 64
ROPE_THETA = 10000.0
A_HEADS = 8
IDX_HEADS = 8
IDX_DIM = 64
IDX_TOPK_MAX = 256
B_HEADS = 4
B_QK_DIM = 64
B_V_DIM = 128
N_EXPERTS = 64
TOP_K = 8
N_GROUPS = 8
TOPK_GROUPS = 4
ROUTED_SCALE = 2.5
LN_EPS = 1e-5
RMS_EPS = 1e-5

LANES = 128
ROW_ALIGN = 16
MOE_BLOCK = 1024
VMEM_LIMIT = 56 * 1024 * 1024
NEG_BIG = -1e30
INT_MIN = -(2 ** 31)
LOG2E = math.log2(math.e)
GROUPS = 256

F32 = jnp.float32
BF16 = jnp.bfloat16


def _cparams(*sem):
    return pltpu.CompilerParams(dimension_semantics=sem, vmem_limit_bytes=VMEM_LIMIT)


def _layer_norm(x, g, b):
    mu = jnp.mean(x, axis=-1, keepdims=True)
    xc = x - mu
    var = jnp.mean(xc * xc, axis=-1, keepdims=True)
    return xc * lax.rsqrt(var + LN_EPS) * g + b


def _dot_nt(a, b):
    return lax.dot_general(a, b, (((1,), (1,)), ((), ())), preferred_element_type=F32)


def _split_lane_halves(pair):
    lo = lax.broadcasted_iota(jnp.int32, pair.shape, 1) < (LANES // 2)
    pf = pair.astype(F32)
    return (jnp.where(lo, pf, 0.0).astype(pair.dtype), jnp.where(lo, 0.0, pf).astype(pair.dtype))


def _proj_kernel(x_ref, pos_ref, inv_ref, g_ref, b_ref, wr_ref, wp_ref, wi_ref,
                 h0_ref, aq_ref, ak_ref, iq_ref, ik_ref, bq_ref, bk_ref,
                 av_ref, bv_ref, iw_ref):
    hn = _layer_norm(x_ref[...], g_ref[...], b_ref[...])
    h0_ref[...] = hn
    hb = hn.astype(BF16)

    tm = x_ref.shape[0]
    nq = LANES // (ROPE_DIM // 2)
    tr = tm // nq
    pos = pos_ref[...].astype(F32)
    lane = lax.broadcasted_iota(jnp.int32, (tr, LANES), 1)
    blk = lane // (ROPE_DIM // 2)
    pos_packed = jnp.broadcast_to(pos[0:tr], (tr, LANES))
    for q in range(1, nq):
        pos_packed = jnp.where(blk == q, pos[q * tr:(q + 1) * tr], pos_packed)
    ang = pos_packed * inv_ref[...]
    cos_packed = jnp.cos(ang)
    sin_packed = jnp.sin(ang)
    first_half = (lane % ROPE_DIM) < (ROPE_DIM // 2)

    def spread(table, q):
        own = jnp.where(blk == q, table, 0.0)
        t = own
        for s in range(1, nq):
            t = t + pltpu.roll(own, s * (ROPE_DIM // 2), axis=1)
        return t

    pr = jnp.dot(hb, wr_ref[...], preferred_element_type=F32)
    outs = ((aq_ref, 0, 4, HEAD_DIM ** -0.5 * LOG2E), (ak_ref, 4, 4, 1.0), (iq_ref, 8, 4, 1.0),
            (ik_ref, 12, 1, 1.0), (bq_ref, 13, 4, B_QK_DIM ** -0.5 * LOG2E), (bk_ref, 17, 4, 1.0))
    for q in range(nq):
        cos = spread(cos_packed, q)
        sin = spread(sin_packed, q)
        sin_signed = jnp.where(first_half, -sin, sin)
        rows = slice(q * tr, (q + 1) * tr)
        for ref, g0, ng, scale in outs:
            for j in range(ng):
                v = pr[rows, (g0 + j) * LANES:(g0 + j + 1) * LANES]
                partner = jnp.where(first_half,
                                    pltpu.roll(v, LANES - ROPE_DIM // 2, axis=1),
                                    pltpu.roll(v, ROPE_DIM // 2, axis=1))
                v = v * cos + partner * sin_signed
                if scale != 1.0:
                    v = v * scale
                ref[rows, j * LANES:(j + 1) * LANES] = v.astype(ref.dtype)

    pp = jnp.dot(hb, wp_ref[...], preferred_element_type=F32)
    av_ref[...] = pp[:, 0:512].astype(BF16)
    bv_ref[...] = pp[:, 512:1024].astype(BF16)
    iw_ref[...] = _dot_nt(wi_ref[...], hb)[0:IDX_HEADS, :]


def _proj(x2, pos2, inv, g, b, w_rope, w_plain, w_idx_t, tm):
    n, d = x2.shape
    nr = w_rope.shape[1]
    npl = w_plain.shape[1]
    row = lambda w: pl.BlockSpec((tm, w), lambda i: (i, 0))
    full = lambda a: pl.BlockSpec(a.shape, lambda i: (0, 0))
    out_shape = (
        jax.ShapeDtypeStruct((n, d), F32),
        jax.ShapeDtypeStruct((n, 512), BF16),
        jax.ShapeDtypeStruct((n, 512), BF16),
        jax.ShapeDtypeStruct((n, 512), BF16),
        jax.ShapeDtypeStruct((n, 128), BF16),
        jax.ShapeDtypeStruct((n, 512), BF16),
        jax.ShapeDtypeStruct((n, 512), BF16),
        jax.ShapeDtypeStruct((n, 512), BF16),
        jax.ShapeDtypeStruct((n, 512), BF16),
        jax.ShapeDtypeStruct((IDX_HEADS, n), F32),
    )
    return pl.pallas_call(
        _proj_kernel,
        grid=(n // tm,),
        in_specs=[row(d), row(1), full(inv), full(g), full(b), full(w_rope), full(w_plain),
                  full(w_idx_t)],
        out_specs=(row(d), row(512), row(512), row(512), row(128), row(512), row(512),
                   row(512), row(512), pl.BlockSpec((IDX_HEADS, tm), lambda i: (0, i))),
        out_shape=out_shape,
        compiler_params=_cparams("parallel"),
    )(x2, pos2, inv, g, b, w_rope, w_plain, w_idx_t)


def _sortable(s):
    i = lax.bitcast_convert_type(s, jnp.int32)
    return i ^ ((i >> 31) & 0x7FFFFFFF)


def _select_kernel(iq_ref, iw_ref, ik_ref, bias_ref, keys_ref, gmax_ref, *, tq, tk, n_keep,
                   w_scale):
    i = pl.program_id(1)
    key_drop = 32 - 8 * jnp.dtype(BF16).itemsize
    seq = ik_ref.shape[1]
    n_t = ((i + 1) * tq + tk - 1) // tk

    q_chunk = (i * tq + lax.broadcasted_iota(jnp.int32, (tk, tq), 1)) // CHUNK
    k_row = lax.broadcasted_iota(jnp.int32, (tk, tq), 0)
    wb = iw_ref[...] * w_scale
    q_halves = []
    for hp in range(IDX_HEADS // 2):
        q_halves.extend(_split_lane_halves(iq_ref[0, :, hp * LANES:(hp + 1) * LANES]))

    def score_tile(t, carry, boundary):
        off = pl.multiple_of(t * tk, tk)
        kt = ik_ref[0, pl.ds(off, tk), :]
        sc = jnp.zeros((tk, tq), F32)
        for h in range(IDX_HEADS):
            rel = jnp.maximum(_dot_nt(kt, q_halves[h]), 0.0)
            sc = sc + wb[h:h + 1, :] * rel
        key = _sortable(sc.astype(BF16).astype(F32) + 0.0) >> key_drop
        if boundary:
            key = jnp.where(((off + k_row) // CHUNK) <= q_chunk, key, INT_MIN)
        keys_ref[pl.ds(off, tk), :] = key
        gm = key[0:GROUPS, :]
        for g0 in range(GROUPS, tk, GROUPS):
            gm = jnp.maximum(gm, key[g0:g0 + GROUPS, :])
        gmax_ref[...] = jnp.maximum(gmax_ref[...], gm)
        return carry

    gmax_ref[...] = jnp.full(gmax_ref.shape, INT_MIN, jnp.int32)
    n_full = (i * tq + CHUNK) // tk
    lax.fori_loop(0, n_full, functools.partial(score_tile, boundary=False), 0)
    lax.fori_loop(n_full, n_t, functools.partial(score_tile, boundary=True), 0)

    def count_ge(cand):
        def body(t, acc):
            off = pl.multiple_of(t * tk, tk)
            m = (keys_ref[pl.ds(off, tk), :] >= cand).astype(jnp.int32)
            return acc + jnp.sum(m.reshape(tk // 8, 8, tq), axis=0)
        acc = lax.fori_loop(0, n_t, body, jnp.zeros((8, tq), jnp.int32))
        return jnp.sum(acc, axis=0, keepdims=True)

    gmax = gmax_ref[...]
    n_adm = ((i * tq + lax.broadcasted_iota(jnp.int32, (1, tq), 1)) // CHUNK + 1) * CHUNK
    few = n_adm <= n_keep
    lo0 = jnp.where(few, INT_MIN + 1, jnp.min(gmax, axis=0, keepdims=True))
    hi0 = jnp.where(few, INT_MIN + 2, jnp.max(gmax, axis=0, keepdims=True) + 1)

    def probe(state):
        lo, hi, n_hi, thr, n_thr, done, _, it = state
        mid = (lo >> 1) + (hi >> 1) + (lo & hi & 1)
        closed = mid == lo
        cnt = count_ge(mid)
        exact = cnt == n_keep
        up = cnt >= n_keep
        live = done == 0
        settle = live & (exact | closed)
        thr = jnp.where(settle, mid, thr)
        n_thr = jnp.where(settle, cnt, n_thr)
        lo = jnp.where(live & up, mid, lo)
        down = live & jnp.logical_not(up)
        hi = jnp.where(down, mid, hi)
        n_hi = jnp.where(down, cnt, n_hi)
        done = jnp.where(exact | closed, 1, done)
        return lo, hi, n_hi, thr, n_thr, done, jnp.min(done), it + 1

    def searching(state):
        return (state[6] == 0) & (state[7] < 34)

    done0 = few.astype(jnp.int32)
    none = jnp.zeros((1, tq), jnp.int32)
    state = lax.while_loop(
        searching, probe, (lo0, hi0, none, lo0, none, done0, jnp.min(done0), jnp.int32(0)))
    n_above, thr, n_ge = state[2], state[3], state[4]

    tied = n_ge > n_keep
    any_tied = jnp.max(tied.astype(jnp.int32)) > 0
    tau = thr

    def write_bias(off, keep_t):
        bias_t = jnp.where(keep_t, 0.0, NEG_BIG)
        bias_ref[0, :, pl.ds(off, tk)] = bias_t.T.astype(BF16)

    @pl.when(jnp.logical_not(any_tied))
    def _():
        def write_tile(t, carry):
            off = pl.multiple_of(t * tk, tk)
            write_bias(off, keys_ref[pl.ds(off, tk), :] >= tau)
            return carry
        lax.fori_loop(0, n_t, write_tile, 0)

    @pl.when(any_tied)
    def _():
        room = jnp.where(tied, (n_keep - n_above).astype(F32), float(seq))
        upto = (lax.broadcasted_iota(jnp.int32, (tk, tk), 1)
                <= lax.broadcasted_iota(jnp.int32, (tk, tk), 0)).astype(F32).astype(BF16)

        def write_tile(t, seen_eq):
            off = pl.multiple_of(t * tk, tk)
            key = keys_ref[pl.ds(off, tk), :]
            eq_f = jnp.where(key == tau, 1.0, 0.0)
            eq_rank = seen_eq + jnp.dot(upto, eq_f.astype(BF16), preferred_element_type=F32)
            order = jnp.where(key == tau, eq_rank, jnp.where(key > tau, 0.0, 2.0 * seq))
            write_bias(off, order <= room)
            return seen_eq + jnp.sum(eq_f, axis=0, keepdims=True)

        lax.fori_loop(0, n_t, write_tile, jnp.zeros((1, tq), F32))

    def blank_tile(t, carry):
        off = pl.multiple_of(t * tk, tk)
        bias_ref[0, :, pl.ds(off, tk)] = jnp.full((tq, tk), NEG_BIG, BF16)
        return carry

    lax.fori_loop(n_t, seq // tk, blank_tile, 0)


def _select(iq, iw, ik2, n_keep, tq, tk):
    bsz, seq, _ = iq.shape
    kern = functools.partial(_select_kernel, tq=tq, tk=tk, n_keep=n_keep,
                             w_scale=(IDX_HEADS ** -0.5) * (IDX_DIM ** -0.5))
    return pl.pallas_call(
        kern,
        grid=(bsz, seq // tq),
        in_specs=[pl.BlockSpec((1, tq, 512), lambda b, i: (b, i, 0)),
                  pl.BlockSpec((IDX_HEADS, tq), lambda b, i: (0, b * (seq // tq) + i)),
                  pl.BlockSpec((1, seq, 128), lambda b, i: (b, 0, 0))],
        out_specs=pl.BlockSpec((1, tq, seq), lambda b, i: (b, i, 0)),
        out_shape=jax.ShapeDtypeStruct((bsz, seq, seq), BF16),
        scratch_shapes=[pltpu.VMEM((seq, tq), jnp.int32), pltpu.VMEM((GROUPS, tq), jnp.int32)],
        compiler_params=_cparams("parallel", "parallel"),
    )(iq, iw, ik2)


def _flash_update(j, s, v_ext, m_ref, acc_ref):
    tk = s.shape[1]
    m_old = m_ref[j]
    m_new = jnp.maximum(m_old, jnp.max(s, axis=1, keepdims=True))
    alpha = jnp.exp2(m_old - m_new)
    p = jnp.exp2(s - jnp.concatenate([m_new] * (tk // LANES), axis=1))
    m_ref[j] = m_new
    acc_ref[j] = (acc_ref[j] * jnp.concatenate([alpha, alpha], axis=1)
                  + jnp.dot(p.astype(BF16), v_ext, preferred_element_type=F32))


def _dsa_kernel(q_ref, k_ref, v_ref, bias_ref, o_ref, m_ref, acc_ref, *, tq, tk):
    i = pl.program_id(1)
    n_keys = (i + 1) * tq
    lo_q = lax.broadcasted_iota(jnp.int32, (tq, LANES), 1) < HEAD_DIM
    q_halves = []
    for hp in range(A_HEADS // 2):
        q_halves.extend(_split_lane_halves(q_ref[0, :, hp * LANES:(hp + 1) * LANES]))

    m_ref[...] = jnp.full(m_ref.shape, NEG_BIG, F32)
    acc_ref[...] = jnp.zeros(acc_ref.shape, F32)

    def kv_tile(off, width):
        bias = bias_ref[0, :, pl.ds(off, width)].astype(F32)
        ones = jnp.ones((width, LANES), BF16)
        for hp in range(A_HEADS // 2):
            kp = k_ref[0, pl.ds(off, width), hp * LANES:(hp + 1) * LANES]
            v_ext = jnp.concatenate(
                [v_ref[0, pl.ds(off, width), hp * LANES:(hp + 1) * LANES], ones], axis=1)
            for half in range(2):
                h = 2 * hp + half
                _flash_update(h, _dot_nt(q_halves[h], kp) + bias, v_ext, m_ref, acc_ref)

    def kv_step(t, carry):
        kv_tile(pl.multiple_of(t * tk, tk), tk)
        return carry

    n_whole = n_keys // tk
    lax.fori_loop(0, n_whole, kv_step, 0)
    if tk > tq:
        @pl.when(n_keys % tk != 0)
        def _():
            kv_tile(pl.multiple_of(n_whole * tk, tq), tq)

    for hp in range(A_HEADS // 2):
        a0 = acc_ref[2 * hp]
        a1 = acc_ref[2 * hp + 1]
        o = jnp.where(lo_q, a0[:, :LANES] / a0[:, LANES:], a1[:, :LANES] / a1[:, LANES:])
        o_ref[0, :, hp * LANES:(hp + 1) * LANES] = o.astype(o_ref.dtype)


def _dsa(aq, ak, av, bias, tq, tk):
    bsz, seq, w = aq.shape
    kern = functools.partial(_dsa_kernel, tq=tq, tk=tk)
    resident = lambda: pl.BlockSpec((1, seq, w), lambda b, i: (b, 0, 0),
                                    pipeline_mode=pl.Buffered(1))
    return pl.pallas_call(
        kern,
        grid=(bsz, seq // tq),
        in_specs=[pl.BlockSpec((1, tq, w), lambda b, i: (b, i, 0)),
                  resident(), resident(),
                  pl.BlockSpec((1, tq, seq), lambda b, i: (b, i, 0))],
        out_specs=pl.BlockSpec((1, tq, w), lambda b, i: (b, i, 0)),
        out_shape=jax.ShapeDtypeStruct((bsz, seq, w), BF16),
        scratch_shapes=[pltpu.VMEM((A_HEADS, tq, LANES), F32),
                        pltpu.VMEM((A_HEADS, tq, 2 * LANES), F32)],
        compiler_params=_cparams("parallel", "arbitrary"),
    )(aq, ak, av, bias)


def _diff_kernel(q_ref, k_ref, v_ref, lq1_ref, lk1_ref, lq2_ref, lk2_ref, g_ref, o_ref,
                 m_ref, acc_ref, *, tq, tk, lambda_init):
    i = pl.program_id(1)
    q_halves = []
    for hb in range(B_HEADS):
        q_halves.extend(_split_lane_halves(q_ref[0, :, hb * LANES:(hb + 1) * LANES]))
    lam = (jnp.exp(jnp.sum(lq1_ref[...] * lk1_ref[...], keepdims=True))
           - jnp.exp(jnp.sum(lq2_ref[...] * lk2_ref[...], keepdims=True)) + lambda_init)

    m_ref[...] = jnp.full(m_ref.shape, NEG_BIG, F32)
    acc_ref[...] = jnp.zeros(acc_ref.shape, F32)

    def tile(off, width, masked):
        ones = jnp.ones((width, LANES), BF16)
        if masked:
            row_chunk = (i * tq + lax.broadcasted_iota(jnp.int32, (tq, width), 0)) // CHUNK
            col_chunk = (off + lax.broadcasted_iota(jnp.int32, (tq, width), 1)) // CHUNK
            ok = col_chunk <= row_chunk
        for hb in range(B_HEADS):
            kp = k_ref[0, pl.ds(off, width), hb * LANES:(hb + 1) * LANES]
            v_ext = jnp.concatenate(
                [v_ref[0, pl.ds(off, width), hb * LANES:(hb + 1) * LANES], ones], axis=1)
            for mp in range(2):
                j = 2 * hb + mp
                s = _dot_nt(q_halves[j], kp)
                if masked:
                    s = jnp.where(ok, s, NEG_BIG)
                _flash_update(j, s, v_ext, m_ref, acc_ref)

    def kv_step(t, carry):
        tile(pl.multiple_of(t * tk, tk), tk, False)
        return carry

    n_whole = (i * tq) // tk
    lax.fori_loop(0, n_whole, kv_step, 0)
    if tk > tq:
        @pl.when((i * tq) % tk != 0)
        def _():
            tile(pl.multiple_of(n_whole * tk, tq), tq, False)
    tile(pl.multiple_of(i * tq, tq), tq, True)

    for hb in range(B_HEADS):
        a1 = acc_ref[2 * hb]
        a2 = acc_ref[2 * hb + 1]
        o = a1[:, :LANES] / a1[:, LANES:] - lam * (a2[:, :LANES] / a2[:, LANES:])
        o = o * lax.rsqrt(jnp.mean(o * o, axis=-1, keepdims=True) + RMS_EPS)
        o = o * g_ref[...] * (1.0 - lambda_init)
        o_ref[0, :, hb * LANES:(hb + 1) * LANES] = o.astype(o_ref.dtype)


def _diff(bq, bk, bv, lq1, lk1, lq2, lk2, subln_g, lambda_init, tq, tk):
    bsz, seq, w = bq.shape
    kern = functools.partial(_diff_kernel, tq=tq, tk=tk, lambda_init=lambda_init)
    resident = lambda: pl.BlockSpec((1, seq, w), lambda b, i: (b, 0, 0),
                                    pipeline_mode=pl.Buffered(1))
    small = lambda a: pl.BlockSpec(a.shape, lambda b, i: (0, 0))
    return pl.pallas_call(
        kern,
        grid=(bsz, seq // tq),
        in_specs=[pl.BlockSpec((1, tq, w), lambda b, i: (b, i, 0)),
                  resident(), resident(),
                  small(lq1), small(lk1), small(lq2), small(lk2), small(subln_g)],
        out_specs=pl.BlockSpec((1, tq, w), lambda b, i: (b, i, 0)),
        out_shape=jax.ShapeDtypeStruct((bsz, seq, w), BF16),
        scratch_shapes=[pltpu.VMEM((2 * B_HEADS, tq, LANES), F32),
                        pltpu.VMEM((2 * B_HEADS, tq, 2 * LANES), F32)],
        compiler_params=_cparams("parallel", "arbitrary"),
    )(bq, bk, bv, lq1, lk1, lq2, lk2, subln_g)


def _first_argmax_rows(v, row):
    m = jnp.max(v, axis=0, keepdims=True)
    idx = jnp.min(jnp.where(v == m, row, v.shape[0]), axis=0, keepdims=True)
    return m, idx


def _scatter_rows(dk, gk, r0, nrows):
    r_iota = (lax.broadcasted_iota(jnp.int32, (nrows, dk.shape[1]), 0) + r0).astype(F32)
    out = jnp.zeros(r_iota.shape, F32)
    for k in range(dk.shape[0]):
        out = jnp.where(r_iota == dk[k:k + 1, :], 1.0 if gk is None else gk[k:k + 1, :], out)
    return out


def _oproj_kernel(oa_ref, ob_ref, h0_ref, woa_ref, wob_ref, g_ref, b_ref, wrt_ref, rb_ref,
                  h1_ref, h1b_ref, dk_ref, gk_ref, cnt_ref, *, alpha):
    mix = (jnp.dot(oa_ref[...], woa_ref[...], preferred_element_type=F32)
           + jnp.dot(ob_ref[...], wob_ref[...], preferred_element_type=F32))
    h1 = _layer_norm(alpha * h0_ref[...] + mix, g_ref[...], b_ref[...])
    h1_ref[...] = h1
    h1b = h1.astype(BF16)
    h1b_ref[...] = h1b

    logits = lax.dot_general(wrt_ref[...], h1, (((1,), (1,)), ((), ())),
                             precision=lax.Precision.HIGHEST, preferred_element_type=F32)
    scores = jax.nn.sigmoid(logits)
    biased = scores + rb_ref[...]
    tm = scores.shape[1]
    per_g = N_EXPERTS // N_GROUPS
    row8 = lax.broadcasted_iota(jnp.int32, (per_g, tm), 0)

    gs = []
    for g in range(N_GROUPS):
        blk = biased[g * per_g:(g + 1) * per_g, :]
        m1, i1 = _first_argmax_rows(blk, row8)
        m2 = jnp.max(jnp.where(row8 == i1, -jnp.inf, blk), axis=0, keepdims=True)
        gs.append(m1 + m2)
    gscore = jnp.concatenate(gs, axis=0)
    rowg = lax.broadcasted_iota(jnp.int32, (N_GROUPS, tm), 0)
    gsel = jnp.zeros((N_GROUPS, tm), F32)
    for _ in range(TOPK_GROUPS):
        _, ig = _first_argmax_rows(jnp.where(gsel > 0.0, -jnp.inf, gscore), rowg)
        gsel = jnp.where(rowg == ig, 1.0, gsel)

    rowe = lax.broadcasted_iota(jnp.int32, (N_EXPERTS, tm), 0)
    live = jnp.concatenate(
        [jnp.broadcast_to(gsel[g:g + 1, :], (per_g, tm)) for g in range(N_GROUPS)], axis=0)
    esel = jnp.zeros((N_EXPERTS, tm), F32)
    for _ in range(TOP_K):
        cand = jnp.where(live > 0.0, biased, -jnp.inf)
        m = jnp.max(cand, axis=0, keepdims=True)
        idx = jnp.min(jnp.where((live > 0.0) & (cand == m), rowe, N_EXPERTS),
                      axis=0, keepdims=True)
        hit = rowe == idx
        esel = jnp.where(hit, 1.0, esel)
        live = jnp.where(hit, 0.0, live)
    picked = jnp.where(esel > 0.0, scores, 0.0)
    denom = jnp.sum(picked, axis=0, keepdims=True)
    gate = picked / denom * ROUTED_SCALE

    sel_b = esel.astype(BF16)
    before_t = (lax.broadcasted_iota(jnp.int32, (tm, tm), 0)
                < lax.broadcasted_iota(jnp.int32, (tm, tm), 1)).astype(F32).astype(BF16)
    before_e = (lax.broadcasted_iota(jnp.int32, (N_EXPERTS, N_EXPERTS), 1)
                < lax.broadcasted_iota(jnp.int32, (N_EXPERTS, N_EXPERTS), 0)).astype(F32).astype(BF16)
    rank = jnp.dot(sel_b, before_t, preferred_element_type=F32)
    order = jnp.dot(before_e, sel_b, preferred_element_type=F32)
    cnt = jnp.sum(esel, axis=1, keepdims=True)
    cnt16 = jnp.floor((cnt + (ROW_ALIGN - 1)) / ROW_ALIGN) * ROW_ALIGN
    off = jnp.dot(before_e, jnp.broadcast_to(cnt16, (N_EXPERTS, tm)).astype(BF16),
                  preferred_element_type=F32)
    dest = off + rank
    dks, gks = [], []
    for k in range(TOP_K):
        kth = (esel > 0.0) & (order == k)
        dks.append(jnp.sum(jnp.where(kth, dest, 0.0), axis=0, keepdims=True))
        gks.append(jnp.sum(jnp.where(kth, gate, 0.0), axis=0, keepdims=True))
    dk = jnp.concatenate(dks, axis=0)
    dk_ref[...] = dk
    gk_ref[...] = jnp.concatenate(gks, axis=0)
    counts = _dot_nt(jnp.ones((8, tm), BF16), sel_b)
    cnt_ref[0] = jnp.concatenate([counts, jnp.zeros((8, LANES - N_EXPERTS), F32)], axis=1)


def _oproj(oa, ob, h0, woa, wob, g, b, wrt, rb, alpha, tm):
    n, d = h0.shape
    nt = n // tm
    row = lambda w: pl.BlockSpec((tm, w), lambda i: (i, 0))
    col = lambda: pl.BlockSpec((TOP_K, tm), lambda i: (0, i))
    full = lambda a: pl.BlockSpec(a.shape, lambda i: (0, 0))
    kt = jax.ShapeDtypeStruct((TOP_K, n), F32)
    return pl.pallas_call(
        functools.partial(_oproj_kernel, alpha=alpha),
        grid=(nt,),
        in_specs=[row(512), row(512), row(d), full(woa), full(wob), full(g), full(b),
                  full(wrt), full(rb)],
        out_specs=(row(d), row(d), col(), col(),
                   pl.BlockSpec((1, 8, LANES), lambda i: (i, 0, 0))),
        out_shape=(jax.ShapeDtypeStruct((n, d), F32),
                   jax.ShapeDtypeStruct((n, d), BF16),
                   kt,
                   kt,
                   jax.ShapeDtypeStruct((nt, 8, LANES), F32)),
        compiler_params=_cparams("parallel"),
    )(oa, ob, h0, woa, wob, g, b, wrt, rb)


def _swiglu(x, wg, wu, wd):
    hid = (jax.nn.silu(jnp.dot(x, wg, preferred_element_type=F32))
           * jnp.dot(x, wu, preferred_element_type=F32))
    return jnp.dot(hid.astype(BF16), wd, preferred_element_type=F32)


def _dispatch_kernel(tile_off_ref, exp_off_ref, len_ref, used_ref, gap_off_ref, gap_len_ref,
                     h1b_ref, dk_ref, xg_ref, xs_ref, zero_ref, run_sem, gap_sem, *, rb):
    i = pl.program_id(0)
    nt = pl.num_programs(0)
    rt = xs_ref.shape[1]
    slot = i % 2

    def wait_runs(step):
        n = pl.multiple_of(used_ref[step], ROW_ALIGN)
        s = step % 2
        pltpu.make_async_copy(xs_ref.at[s, pl.ds(0, n), :], xg_ref.at[pl.ds(0, n), :],
                              run_sem.at[s]).wait()

    @pl.when(i == 0)
    def _():
        zero_ref[...] = jnp.zeros(zero_ref.shape, zero_ref.dtype)

        def issue_gap(g, carry):
            n = pl.multiple_of(gap_len_ref[g], ROW_ALIGN)

            @pl.when(n > 0)
            def _():
                pltpu.make_async_copy(
                    zero_ref.at[pl.ds(0, n), :],
                    xg_ref.at[pl.ds(pl.multiple_of(gap_off_ref[g], ROW_ALIGN), n), :],
                    gap_sem).start()
            return carry

        lax.fori_loop(0, gap_len_ref.shape[0], issue_gap, 0)

    @pl.when(i >= 2)
    def _():
        wait_runs(i - 2)

    used = used_ref[i]
    dk = dk_ref[...]

    def sort_rows(r0):
        onehot = _scatter_rows(dk, None, r0, rb).astype(BF16)
        rows = jnp.dot(onehot, h1b_ref[...], preferred_element_type=F32)
        xs_ref[slot, r0:r0 + rb, :] = rows.astype(BF16)

    always = h1b_ref.shape[0] * TOP_K // rb * rb
    for r0 in range(0, always, rb):
        sort_rows(r0)
    for r0 in range(always, rt, rb):
        @pl.when(r0 < used)
        def _():
            sort_rows(r0)

    def issue_run(e, carry):
        r = i * N_EXPERTS + e
        n = pl.multiple_of(len_ref[r], ROW_ALIGN)

        @pl.when(n > 0)
        def _():
            pltpu.make_async_copy(
                xs_ref.at[slot, pl.ds(pl.multiple_of(tile_off_ref[r], ROW_ALIGN), n), :],
                xg_ref.at[pl.ds(pl.multiple_of(exp_off_ref[r], ROW_ALIGN), n), :],
                run_sem.at[slot]).start()
        return carry

    lax.fori_loop(0, N_EXPERTS, issue_run, 0)

    @pl.when(i == nt - 1)
    def _():
        @pl.when(i >= 1)
        def _():
            wait_runs(i - 1)
        wait_runs(i)

        def gap_rows(g, rows):
            return rows + gap_len_ref[g]
        n_gap = pl.multiple_of(lax.fori_loop(0, gap_len_ref.shape[0], gap_rows, 0), ROW_ALIGN)

        @pl.when(n_gap > 0)
        def _():
            pltpu.make_async_copy(xg_ref.at[pl.ds(0, n_gap), :], xg_ref.at[pl.ds(0, n_gap), :],
                                  gap_sem).wait()


def _dispatch(h1b, dk, tile_off, exp_off, lens, used, gap_off, gap_len, n_dst_rows, tm, rt):
    n, d = h1b.shape
    return pl.pallas_call(
        functools.partial(_dispatch_kernel, rb=512),
        grid_spec=pltpu.PrefetchScalarGridSpec(
            num_scalar_prefetch=6,
            grid=(n // tm,),
            in_specs=[pl.BlockSpec((tm, d), lambda i, *_: (i, 0)),
                      pl.BlockSpec((TOP_K, tm), lambda i, *_: (0, i))],
            out_specs=pl.BlockSpec(memory_space=pl.ANY),
            scratch_shapes=[pltpu.VMEM((2, rt, d), BF16), pltpu.VMEM((rt, d), BF16),
                            pltpu.SemaphoreType.DMA((2,)), pltpu.SemaphoreType.DMA(())]),
        out_shape=jax.ShapeDtypeStruct((n_dst_rows, d), BF16),
        compiler_params=_cparams("arbitrary"),
    )(tile_off, exp_off, lens, used, gap_off, gap_len, h1b, dk)


def _expert_kernel(blk_e_ref, blk_valid_ref, x_ref, wg_ref, wu_ref, wd_ref, y_ref):
    b = pl.program_id(0)
    valid = blk_valid_ref[b]

    @pl.when(valid > 0)
    def _():
        y_ref[...] = _swiglu(x_ref[...], wg_ref[...], wu_ref[...], wd_ref[...]).astype(BF16)

    @pl.when(valid <= 0)
    def _():
        y_ref[...] = jnp.zeros(y_ref.shape, BF16)


def _experts(xg, blk_e, blk_valid, wg, wu, wd):
    nr, d = xg.shape
    f = wg.shape[2]
    rows = pl.BlockSpec((MOE_BLOCK, d), lambda b, be, bv: (b, 0))
    return pl.pallas_call(
        _expert_kernel,
        grid_spec=pltpu.PrefetchScalarGridSpec(
            num_scalar_prefetch=2,
            grid=(nr // MOE_BLOCK,),
            in_specs=[rows,
                      pl.BlockSpec((None, d, f), lambda b, be, bv: (be[b], 0, 0)),
                      pl.BlockSpec((None, d, f), lambda b, be, bv: (be[b], 0, 0)),
                      pl.BlockSpec((None, f, d), lambda b, be, bv: (be[b], 0, 0))],
            out_specs=rows),
        out_shape=jax.ShapeDtypeStruct((nr, d), BF16),
        compiler_params=_cparams("parallel"),
    )(blk_e, blk_valid, xg, wg, wu, wd)


def _combine_kernel(tile_off_ref, exp_off_ref, len_ref, used_ref, yg_ref, dk_ref, gk_ref, h1_ref,
                    h1b_ref, p_ref, sg_ref, su_ref, sd_ref, g_ref, b_ref, wg_ref, bg_ref, wp_ref,
                    o_ref, ys_ref, ffn_ref, sem, *, alpha, rb):
    i = pl.program_id(0)
    nt = pl.num_programs(0)
    rt = ys_ref.shape[1]
    slot = i % 2

    def fetch_runs(tile):
        s = tile % 2

        def issue_run(e, carry):
            r = tile * N_EXPERTS + e
            n = pl.multiple_of(len_ref[r], ROW_ALIGN)

            @pl.when(n > 0)
            def _():
                pltpu.make_async_copy(
                    yg_ref.at[pl.ds(pl.multiple_of(exp_off_ref[r], ROW_ALIGN), n), :],
                    ys_ref.at[s, pl.ds(pl.multiple_of(tile_off_ref[r], ROW_ALIGN), n), :],
                    sem.at[s]).start()
            return carry

        lax.fori_loop(0, N_EXPERTS, issue_run, 0)

    @pl.when(i == 0)
    def _():
        ys_ref[...] = jnp.zeros(ys_ref.shape, ys_ref.dtype)
        fetch_runs(i)

    @pl.when(i + 1 < nt)
    def _():
        fetch_runs(i + 1)

    dk = dk_ref[...]
    gk = gk_ref[...]
    ffn = _swiglu(h1b_ref[...], sg_ref[...], su_ref[...], sd_ref[...])
    used = used_ref[i]
    n_used = pl.multiple_of(used, ROW_ALIGN)
    pltpu.make_async_copy(yg_ref.at[pl.ds(0, n_used), :], ys_ref.at[slot, pl.ds(0, n_used), :],
                          sem.at[slot]).wait()

    def routed(r0):
        weights = _scatter_rows(dk, gk, r0, rb).astype(BF16)
        return lax.dot_general(weights, ys_ref[slot, r0:r0 + rb, :], (((0,), (0,)), ((), ())),
                               preferred_element_type=F32)

    tm = h1_ref.shape[0]
    always = tm * TOP_K // rb * rb
    for r0 in range(0, always, rb):
        ffn = ffn + routed(r0)
    ffn_ref[...] = ffn
    for r0 in range(always, rt, rb):
        @pl.when(r0 < used)
        def _():
            ffn_ref[...] += routed(r0)
    h2 = _layer_norm(alpha * h1_ref[...] + ffn_ref[...], g_ref[...], b_ref[...])
    emb_gate = jax.nn.sigmoid(jnp.dot(h2.astype(BF16), wg_ref[...], preferred_element_type=F32)
                              + bg_ref[...])
    emb = jnp.dot(p_ref[...].astype(BF16), wp_ref[...], preferred_element_type=F32)
    o_ref[...] = h2 + emb_gate * emb


def _combine(tile_off, exp_off, lens, used, yg, dk, gk, h1, h1b, p2, sg, su, sd, g, b, wg, bg, wp,
             alpha, tm, rt):
    n, d = h1.shape
    row = lambda w: pl.BlockSpec((tm, w), lambda i, *_: (i, 0))
    col = lambda: pl.BlockSpec((TOP_K, tm), lambda i, *_: (0, i))
    full = lambda a: pl.BlockSpec(a.shape, lambda i, *_: (0, 0))
    return pl.pallas_call(
        functools.partial(_combine_kernel, alpha=alpha, rb=512),
        grid_spec=pltpu.PrefetchScalarGridSpec(
            num_scalar_prefetch=4,
            grid=(n // tm,),
            in_specs=[pl.BlockSpec(memory_space=pl.ANY), col(), col(), row(d),
                      row(d), row(p2.shape[1]), full(sg), full(su), full(sd), full(g), full(b),
                      full(wg), full(bg), full(wp)],
            out_specs=row(d),
            scratch_shapes=[pltpu.VMEM((2, rt, d), yg.dtype), pltpu.VMEM((tm, d), F32),
                            pltpu.SemaphoreType.DMA((2,))]),
        out_shape=jax.ShapeDtypeStruct((n, d), F32),
        compiler_params=_cparams("arbitrary"),
    )(tile_off, exp_off, lens, used, yg, dk, gk, h1, h1b, p2, sg, su, sd, g, b, wg, bg, wp)


def _tiles(seq):
    tm = min(256, seq)
    tq_sel = min(512, seq)
    tk_sel = min(512, seq)
    tq_att = min(256, seq)
    tk_att = min(512, seq)
    rt = -(-(tm * TOP_K + N_EXPERTS * (ROW_ALIGN - 1)) // 512) * 512
    return tm, tq_sel, tk_sel, tq_att, tk_att, rt


def kernel(x, p, positions, ln_emb_g, ln_emb_b, w_in, w_o, diff_lq1, diff_lk1, diff_lq2, diff_lk2,
           diff_subln_g, ln1_g, ln1_b, w_router, router_bias, w_exp_gate, w_exp_up, w_exp_down,
           w_sh_gate, w_sh_up, w_sh_down, ln2_g, ln2_b, w_ple_gate, b_ple_gate, w_ple_proj):
    bsz, seq, d = x.shape
    depth = w_in.shape[0]
    n = bsz * seq
    n_keep = min(IDX_TOPK_MAX, seq // 4)
    alpha = (2.0 * depth) ** 0.25
    tm, tq_sel, tk_sel, tq_att, tk_att, rt = _tiles(seq)
    n_blocks = -(-((n // tm) * (tm * TOP_K + N_EXPERTS * (ROW_ALIGN - 1))
                   + N_EXPERTS * (MOE_BLOCK - 1)) // MOE_BLOCK)
    row1 = lambda v: v.reshape(1, -1)

    inv = ROPE_THETA ** (-jnp.arange(0, ROPE_DIM, 2, dtype=F32) / ROPE_DIM)
    inv_lanes = jnp.tile(inv, LANES // (ROPE_DIM // 2)).reshape(1, LANES)
    pos2 = positions.reshape(n, 1)

    h = x.reshape(n, d)
    out = None
    for li in range(depth):
        lambda_init = 0.8 - 0.6 * math.exp(-0.3 * li)
        w = w_in[li]
        aqw, akw, avw = w[:, 0:512], w[:, 512:1024], w[:, 1024:1536]
        iqw, ikw, iww = w[:, 1536:2048], w[:, 2048:2112], w[:, 2112:2120]
        bqw, bkw, bvw = w[:, 2120:2632], w[:, 2632:3144], w[:, 3144:3656]
        w_rope = jnp.concatenate([aqw, akw, iqw, ikw, ikw, bqw, bkw], axis=1).astype(BF16)
        w_plain = jnp.concatenate([avw, bvw], axis=1).astype(BF16)
        w_idx_t = jnp.concatenate(
            [iww.T, jnp.zeros((LANES - IDX_HEADS, d), F32)], axis=0).astype(BF16)

        (h0, aq, ak, iq, ik2, bq, bk, av, bv, iw) = _proj(
            h, pos2, inv_lanes, row1(ln_emb_g), row1(ln_emb_b), w_rope, w_plain, w_idx_t, tm)
        assert depth == 1

        r3 = lambda a: a.reshape(bsz, seq, a.shape[-1])
        bias = _select(r3(iq), iw, r3(ik2), n_keep, tq_sel, tk_sel)
        out_a = _dsa(r3(aq), r3(ak), r3(av), bias, tq_att, tk_att)
        out_b = _diff(r3(bq), r3(bk), r3(bv), row1(diff_lq1[li]), row1(diff_lk1[li]),
                      row1(diff_lq2[li]), row1(diff_lk2[li]), row1(diff_subln_g[li]),
                      lambda_init, tq_att, tk_att)

        wo = w_o[li].astype(BF16)
        h1, h1b, dk, gk, cnt_out = _oproj(
            out_a.reshape(n, -1), out_b.reshape(n, -1), h0, wo[0:512], wo[512:1024],
            row1(ln1_g[li]), row1(ln1_b[li]), w_router[li].T, router_bias[li].reshape(-1, 1),
            alpha, tm)

        nt = n // tm
        c16 = jnp.ceil(cnt_out[:, 0, :N_EXPERTS] / ROW_ALIGN) * ROW_ALIGN
        tri = lambda m: jnp.tril(jnp.ones((m, m), F32))
        hi = lax.Precision.HIGHEST
        tile_off = jnp.dot(c16, tri(N_EXPERTS).T, precision=hi) - c16
        cum_tiles = jnp.dot(tri(nt), c16, precision=hi)
        tot = cum_tiles[-1]
        totp = jnp.ceil(tot / MOE_BLOCK) * MOE_BLOCK
        pend = jnp.dot(totp, tri(N_EXPERTS).T, precision=hi)
        ebase = pend - totp
        exp_off = ebase[None, :] + cum_tiles - c16
        blk0 = jnp.arange(n_blocks, dtype=F32) * MOE_BLOCK
        blk_e = jnp.minimum(jnp.sum(pend[None, :] <= blk0[:, None], axis=1), N_EXPERTS - 1)
        onehot_e = (blk_e[:, None] == jnp.arange(N_EXPERTS)[None, :]).astype(F32)
        blk_end = jnp.dot(onehot_e, ebase + tot, precision=hi)
        blk_valid = jnp.clip(blk_end - blk0, 0, MOE_BLOCK)
        flat = lambda a: a.reshape(-1).astype(jnp.int32)
        n_rows = n_blocks * MOE_BLOCK
        tail0 = pend[-1] + jnp.arange(-(-n_rows // rt), dtype=F32) * rt
        gap_off = jnp.concatenate([ebase + tot, tail0])
        gap_len = jnp.concatenate([totp - tot, jnp.clip(n_rows - tail0, 0, rt)])

        weg, weu, wed = (w_exp_gate[li].astype(BF16), w_exp_up[li].astype(BF16),
                         w_exp_down[li].astype(BF16))
        used = jnp.sum(c16, axis=1)
        xg = _dispatch(h1b, dk, flat(tile_off), flat(exp_off), flat(c16), flat(used),
                       flat(gap_off), flat(gap_len), n_rows, tm, rt)
        yg = _experts(xg, flat(blk_e), flat(blk_valid), weg, weu, wed)
        out = _combine(flat(tile_off), flat(exp_off), flat(c16), flat(used), yg, dk, gk, h1, h1b,
                       p[li].reshape(n, -1), w_sh_gate[li].astype(BF16),
                       w_sh_up[li].astype(BF16), w_sh_down[li].astype(BF16), row1(ln2_g[li]),
                       row1(ln2_b[li]), w_ple_gate[li].astype(BF16), row1(b_ple_gate[li]),
                       w_ple_proj[li].astype(BF16), alpha, tm, rt)
        h = out
    return out.reshape(bsz, seq, d)
```

```python
import functools
import math

import jax
import jax.numpy as jnp
from jax import lax
from jax.experimental import pallas as pl
from jax.experimental.pallas import tpu as pltpu

CHUNK = 64
HEAD_DIM = 64
ROPE_DIM = 64
ROPE_THETA = 10000.0
A_HEADS = 8
IDX_HEADS = 8
IDX_DIM = 64
IDX_TOPK_MAX = 256
B_HEADS = 4
B_QK_DIM = 64
B_V_DIM = 128
N_EXPERTS = 64
TOP_K = 8
N_GROUPS = 8
TOPK_GROUPS = 4
ROUTED_SCALE = 2.5
LN_EPS = 1e-5
RMS_EPS = 1e-5

LANES = 128
ROW_ALIGN = 16
MOE_BLOCK = 1024
VMEM_LIMIT = 56 * 1024 * 1024
NEG_BIG = -1e30
INT_MIN = -(2 ** 31)
LOG2E = math.log2(math.e)
GROUPS = 256

F32 = jnp.float32
BF16 = jnp.bfloat16


def _cparams(*sem):
    return pltpu.CompilerParams(dimension_semantics=sem, vmem_limit_bytes=VMEM_LIMIT)


def _layer_norm(x, g, b):
    mu = jnp.mean(x, axis=-1, keepdims=True)
    xc = x - mu
    var = jnp.mean(xc * xc, axis=-1, keepdims=True)
    return xc * lax.rsqrt(var + LN_EPS) * g + b


def _dot_nt(a, b):
    return lax.dot_general(a, b, (((1,), (1,)), ((), ())), preferred_element_type=F32)


def _split_lane_halves(pair):
    lo = lax.broadcasted_iota(jnp.int32, pair.shape, 1) < (LANES // 2)
    pf = pair.astype(F32)
    return (jnp.where(lo, pf, 0.0).astype(pair.dtype), jnp.where(lo, 0.0, pf).astype(pair.dtype))


def _proj_kernel(x_ref, pos_ref, inv_ref, g_ref, b_ref, wr_ref, wp_ref, wi_ref,
                 h0_ref, aq_ref, ak_ref, iq_ref, ik_ref, bq_ref, bk_ref,
                 av_ref, bv_ref, iw_ref):
    hn = _layer_norm(x_ref[...], g_ref[...], b_ref[...])
    h0_ref[...] = hn
    hb = hn.astype(BF16)

    tm = x_ref.shape[0]
    nq = LANES // (ROPE_DIM // 2)
    tr = tm // nq
    pos = pos_ref[...].astype(F32)
    lane = lax.broadcasted_iota(jnp.int32, (tr, LANES), 1)
    blk = lane // (ROPE_DIM // 2)
    pos_packed = jnp.broadcast_to(pos[0:tr], (tr, LANES))
    for q in range(1, nq):
        pos_packed = jnp.where(blk == q, pos[q * tr:(q + 1) * tr], pos_packed)
    ang = pos_packed * inv_ref[...]
    cos_packed = jnp.cos(ang)
    sin_packed = jnp.sin(ang)
    first_half = (lane % ROPE_DIM) < (ROPE_DIM // 2)

    def spread(table, q):
        own = jnp.where(blk == q, table, 0.0)
        t = own
        for s in range(1, nq):
            t = t + pltpu.roll(own, s * (ROPE_DIM // 2), axis=1)
        return t

    pr = jnp.dot(hb, wr_ref[...], preferred_element_type=F32)
    outs = ((aq_ref, 0, 4, HEAD_DIM ** -0.5 * LOG2E), (ak_ref, 4, 4, 1.0), (iq_ref, 8, 4, 1.0),
            (ik_ref, 12, 1, 1.0), (bq_ref, 13, 4, B_QK_DIM ** -0.5 * LOG2E), (bk_ref, 17, 4, 1.0))
    for q in range(nq):
        cos = spread(cos_packed, q)
        sin = spread(sin_packed, q)
        sin_signed = jnp.where(first_half, -sin, sin)
        rows = slice(q * tr, (q + 1) * tr)
        for ref, g0, ng, scale in outs:
            for j in range(ng):
                v = pr[rows, (g0 + j) * LANES:(g0 + j + 1) * LANES]
                partner = jnp.where(first_half,
                                    pltpu.roll(v, LANES - ROPE_DIM // 2, axis=1),
                                    pltpu.roll(v, ROPE_DIM // 2, axis=1))
                v = v * cos + partner * sin_signed
                if scale != 1.0:
                    v = v * scale
                ref[rows, j * LANES:(j + 1) * LANES] = v.astype(ref.dtype)

    pp = jnp.dot(hb, wp_ref[...], preferred_element_type=F32)
    av_ref[...] = pp[:, 0:512].astype(BF16)
    bv_ref[...] = pp[:, 512:1024].astype(BF16)
    iw_ref[...] = _dot_nt(wi_ref[...], hb)[0:IDX_HEADS, :]


def _proj(x2, pos2, inv, g, b, w_rope, w_plain, w_idx_t, tm):
    n, d = x2.shape
    nr = w_rope.shape[1]
    npl = w_plain.shape[1]
    row = lambda w: pl.BlockSpec((tm, w), lambda i: (i, 0))
    full = lambda a: pl.BlockSpec(a.shape, lambda i: (0, 0))
    out_shape = (
        jax.ShapeDtypeStruct((n, d), F32),
        jax.ShapeDtypeStruct((n, 512), BF16),
        jax.ShapeDtypeStruct((n, 512), BF16),
        jax.ShapeDtypeStruct((n, 512), BF16),
        jax.ShapeDtypeStruct((n, 128), BF16),
        jax.ShapeDtypeStruct((n, 512), BF16),
        jax.ShapeDtypeStruct((n, 512), BF16),
        jax.ShapeDtypeStruct((n, 512), BF16),
        jax.ShapeDtypeStruct((n, 512), BF16),
        jax.ShapeDtypeStruct((IDX_HEADS, n), F32),
    )
    return pl.pallas_call(
        _proj_kernel,
        grid=(n // tm,),
        in_specs=[row(d), row(1), full(inv), full(g), full(b), full(w_rope), full(w_plain),
                  full(w_idx_t)],
        out_specs=(row(d), row(512), row(512), row(512), row(128), row(512), row(512),
                   row(512), row(512), pl.BlockSpec((IDX_HEADS, tm), lambda i: (0, i))),
        out_shape=out_shape,
        compiler_params=_cparams("parallel"),
    )(x2, pos2, inv, g, b, w_rope, w_plain, w_idx_t)


def _sortable(s):
    i = lax.bitcast_convert_type(s, jnp.int32)
    return i ^ ((i >> 31) & 0x7FFFFFFF)


def _select_kernel(iq_ref, iw_ref, ik_ref, bias_ref, keys_ref, gmax_ref, *, tq, tk, n_keep,
                   w_scale):
    i = pl.program_id(1)
    key_drop = 32 - 8 * jnp.dtype(BF16).itemsize
    seq = ik_ref.shape[1]
    n_t = ((i + 1) * tq + tk - 1) // tk

    q_chunk = (i * tq + lax.broadcasted_iota(jnp.int32, (tk, tq), 1)) // CHUNK
    k_row = lax.broadcasted_iota(jnp.int32, (tk, tq), 0)
    wb = iw_ref[...] * w_scale
    q_halves = []
    for hp in range(IDX_HEADS // 2):
        q_halves.extend(_split_lane_halves(iq_ref[0, :, hp * LANES:(hp + 1) * LANES]))

    def score_tile(t, carry, boundary):
        off = pl.multiple_of(t * tk, tk)
        kt = ik_ref[0, pl.ds(off, tk), :]
        sc = jnp.zeros((tk, tq), F32)
        for h in range(IDX_HEADS):
            rel = jnp.maximum(_dot_nt(kt, q_halves[h]), 0.0)
            sc = sc + wb[h:h + 1, :] * rel
        key = _sortable(sc.astype(BF16).astype(F32) + 0.0) >> key_drop
        if boundary:
            key = jnp.where(((off + k_row) // CHUNK) <= q_chunk, key, INT_MIN)
        keys_ref[pl.ds(off, tk), :] = key
        gm = key[0:GROUPS, :]
        for g0 in range(GROUPS, tk, GROUPS):
            gm = jnp.maximum(gm, key[g0:g0 + GROUPS, :])
        gmax_ref[...] = jnp.maximum(gmax_ref[...], gm)
        return carry

    gmax_ref[...] = jnp.full(gmax_ref.shape, INT_MIN, jnp.int32)
    n_full = (i * tq + CHUNK) // tk
    lax.fori_loop(0, n_full, functools.partial(score_tile, boundary=False), 0)
    lax.fori_loop(n_full, n_t, functools.partial(score_tile, boundary=True), 0)

    def count_ge(cand):
        def body(t, acc):
            off = pl.multiple_of(t * tk, tk)
            m = (keys_ref[pl.ds(off, tk), :] >= cand).astype(jnp.int32)
            return acc + jnp.sum(m.reshape(tk // 8, 8, tq), axis=0)
        acc = lax.fori_loop(0, n_t, body, jnp.zeros((8, tq), jnp.int32))
        return jnp.sum(acc, axis=0, keepdims=True)

    gmax = gmax_ref[...]
    n_adm = ((i * tq + lax.broadcasted_iota(jnp.int32, (1, tq), 1)) // CHUNK + 1) * CHUNK
    few = n_adm <= n_keep
    lo0 = jnp.where(few, INT_MIN + 1, jnp.min(gmax, axis=0, keepdims=True))
    hi0 = jnp.where(few, INT_MIN + 2, jnp.max(gmax, axis=0, keepdims=True) + 1)

    def probe(state):
        lo, hi, n_hi, thr, n_thr, done, _, it = state
        mid = (lo >> 1) + (hi >> 1) + (lo & hi & 1)
        closed = mid == lo
        cnt = count_ge(mid)
        exact = cnt == n_keep
        up = cnt >= n_keep
        live = done == 0
        settle = live & (exact | closed)
        thr = jnp.where(settle, mid, thr)
        n_thr = jnp.where(settle, cnt, n_thr)
        lo = jnp.where(live & up, mid, lo)
        down = live & jnp.logical_not(up)
        hi = jnp.where(down, mid, hi)
        n_hi = jnp.where(down, cnt, n_hi)
        done = jnp.where(exact | closed, 1, done)
        return lo, hi, n_hi, thr, n_thr, done, jnp.min(done), it + 1

    def searching(state):
        return (state[6] == 0) & (state[7] < 34)

    done0 = few.astype(jnp.int32)
    none = jnp.zeros((1, tq), jnp.int32)
    state = lax.while_loop(
        searching, probe, (lo0, hi0, none, lo0, none, done0, jnp.min(done0), jnp.int32(0)))
    n_above, thr, n_ge = state[2], state[3], state[4]

    tied = n_ge > n_keep
    any_tied = jnp.max(tied.astype(jnp.int32)) > 0
    tau = thr

    def write_bias(off, keep_t):
        bias_t = jnp.where(keep_t, 0.0, NEG_BIG)
        bias_ref[0, :, pl.ds(off, tk)] = bias_t.T.astype(BF16)

    @pl.when(jnp.logical_not(any_tied))
    def _():
        def write_tile(t, carry):
            off = pl.multiple_of(t * tk, tk)
            write_bias(off, keys_ref[pl.ds(off, tk), :] >= tau)
            return carry
        lax.fori_loop(0, n_t, write_tile, 0)

    @pl.when(any_tied)
    def _():
        room = jnp.where(tied, (n_keep - n_above).astype(F32), float(seq))
        upto = (lax.broadcasted_iota(jnp.int32, (tk, tk), 1)
                <= lax.broadcasted_iota(jnp.int32, (tk, tk), 0)).astype(F32).astype(BF16)

        def write_tile(t, seen_eq):
            off = pl.multiple_of(t * tk, tk)
            key = keys_ref[pl.ds(off, tk), :]
            eq_f = jnp.where(key == tau, 1.0, 0.0)
            eq_rank = seen_eq + jnp.dot(upto, eq_f.astype(BF16), preferred_element_type=F32)
            order = jnp.where(key == tau, eq_rank, jnp.where(key > tau, 0.0, 2.0 * seq))
            write_bias(off, order <= room)
            return seen_eq + jnp.sum(eq_f, axis=0, keepdims=True)

        lax.fori_loop(0, n_t, write_tile, jnp.zeros((1, tq), F32))

    def blank_tile(t, carry):
        off = pl.multiple_of(t * tk, tk)
        bias_ref[0, :, pl.ds(off, tk)] = jnp.full((tq, tk), NEG_BIG, BF16)
        return carry

    lax.fori_loop(n_t, seq // tk, blank_tile, 0)


def _select(iq, iw, ik2, n_keep, tq, tk):
    bsz, seq, _ = iq.shape
    kern = functools.partial(_select_kernel, tq=tq, tk=tk, n_keep=n_keep,
                             w_scale=(IDX_HEADS ** -0.5) * (IDX_DIM ** -0.5))
    return pl.pallas_call(
        kern,
        grid=(bsz, seq // tq),
        in_specs=[pl.BlockSpec((1, tq, 512), lambda b, i: (b, i, 0)),
                  pl.BlockSpec((IDX_HEADS, tq), lambda b, i: (0, b * (seq // tq) + i)),
                  pl.BlockSpec((1, seq, 128), lambda b, i: (b, 0, 0))],
        out_specs=pl.BlockSpec((1, tq, seq), lambda b, i: (b, i, 0)),
        out_shape=jax.ShapeDtypeStruct((bsz, seq, seq), BF16),
        scratch_shapes=[pltpu.VMEM((seq, tq), jnp.int32), pltpu.VMEM((GROUPS, tq), jnp.int32)],
        compiler_params=_cparams("parallel", "parallel"),
    )(iq, iw, ik2)


def _flash_update(j, s, v_ext, m_ref, acc_ref):
    tk = s.shape[1]
    m_old = m_ref[j]
    m_new = jnp.maximum(m_old, jnp.max(s, axis=1, keepdims=True))
    alpha = jnp.exp2(m_old - m_new)
    p = jnp.exp2(s - jnp.concatenate([m_new] * (tk // LANES), axis=1))
    m_ref[j] = m_new
    acc_ref[j] = (acc_ref[j] * jnp.concatenate([alpha, alpha], axis=1)
                  + jnp.dot(p.astype(BF16), v_ext, preferred_element_type=F32))


def _dsa_kernel(q_ref, k_ref, v_ref, bias_ref, o_ref, m_ref, acc_ref, *, tq, tk):
    i = pl.program_id(1)
    n_keys = (i + 1) * tq
    lo_q = lax.broadcasted_iota(jnp.int32, (tq, LANES), 1) < HEAD_DIM
    q_halves = []
    for hp in range(A_HEADS // 2):
        q_halves.extend(_split_lane_halves(q_ref[0, :, hp * LANES:(hp + 1) * LANES]))

    m_ref[...] = jnp.full(m_ref.shape, NEG_BIG, F32)
    acc_ref[...] = jnp.zeros(acc_ref.shape, F32)

    def kv_tile(off, width):
        bias = bias_ref[0, :, pl.ds(off, width)].astype(F32)
        ones = jnp.ones((width, LANES), BF16)
        for hp in range(A_HEADS // 2):
            kp = k_ref[0, pl.ds(off, width), hp * LANES:(hp + 1) * LANES]
            v_ext = jnp.concatenate(
                [v_ref[0, pl.ds(off, width), hp * LANES:(hp + 1) * LANES], ones], axis=1)
            for half in range(2):
                h = 2 * hp + half
                _flash_update(h, _dot_nt(q_halves[h], kp) + bias, v_ext, m_ref, acc_ref)

    def kv_step(t, carry):
        kv_tile(pl.multiple_of(t * tk, tk), tk)
        return carry

    n_whole = n_keys // tk
    lax.fori_loop(0, n_whole, kv_step, 0)
    if tk > tq:
        @pl.when(n_keys % tk != 0)
        def _():
            kv_tile(pl.multiple_of(n_whole * tk, tq), tq)

    for hp in range(A_HEADS // 2):
        a0 = acc_ref[2 * hp]
        a1 = acc_ref[2 * hp + 1]
        o = jnp.where(lo_q, a0[:, :LANES] / a0[:, LANES:], a1[:, :LANES] / a1[:, LANES:])
        o_ref[0, :, hp * LANES:(hp + 1) * LANES] = o.astype(o_ref.dtype)


def _dsa(aq, ak, av, bias, tq, tk):
    bsz, seq, w = aq.shape
    kern = functools.partial(_dsa_kernel, tq=tq, tk=tk)
    resident = lambda: pl.BlockSpec((1, seq, w), lambda b, i: (b, 0, 0),
                                    pipeline_mode=pl.Buffered(1))
    return pl.pallas_call(
        kern,
        grid=(bsz, seq // tq),
        in_specs=[pl.BlockSpec((1, tq, w), lambda b, i: (b, i, 0)),
                  resident(), resident(),
                  pl.BlockSpec((1, tq, seq), lambda b, i: (b, i, 0))],
        out_specs=pl.BlockSpec((1, tq, w), lambda b, i: (b, i, 0)),
        out_shape=jax.ShapeDtypeStruct((bsz, seq, w), BF16),
        scratch_shapes=[pltpu.VMEM((A_HEADS, tq, LANES), F32),
                        pltpu.VMEM((A_HEADS, tq, 2 * LANES), F32)],
        compiler_params=_cparams("parallel", "arbitrary"),
    )(aq, ak, av, bias)


def _diff_kernel(q_ref, k_ref, v_ref, lq1_ref, lk1_ref, lq2_ref, lk2_ref, g_ref, o_ref,
                 m_ref, acc_ref, *, tq, tk, lambda_init):
    i = pl.program_id(1)
    q_halves = []
    for hb in range(B_HEADS):
        q_halves.extend(_split_lane_halves(q_ref[0, :, hb * LANES:(hb + 1) * LANES]))
    lam = (jnp.exp(jnp.sum(lq1_ref[...] * lk1_ref[...], keepdims=True))
           - jnp.exp(jnp.sum(lq2_ref[...] * lk2_ref[...], keepdims=True)) + lambda_init)

    m_ref[...] = jnp.full(m_ref.shape, NEG_BIG, F32)
    acc_ref[...] = jnp.zeros(acc_ref.shape, F32)

    def tile(off, width, masked):
        ones = jnp.ones((width, LANES), BF16)
        if masked:
            row_chunk = (i * tq + lax.broadcasted_iota(jnp.int32, (tq, width), 0)) // CHUNK
            col_chunk = (off + lax.broadcasted_iota(jnp.int32, (tq, width), 1)) // CHUNK
            ok = col_chunk <= row_chunk
        for hb in range(B_HEADS):
            kp = k_ref[0, pl.ds(off, width), hb * LANES:(hb + 1) * LANES]
            v_ext = jnp.concatenate(
                [v_ref[0, pl.ds(off, width), hb * LANES:(hb + 1) * LANES], ones], axis=1)
            for mp in range(2):
                j = 2 * hb + mp
                s = _dot_nt(q_halves[j], kp)
                if masked:
                    s = jnp.where(ok, s, NEG_BIG)
                _flash_update(j, s, v_ext, m_ref, acc_ref)

    def kv_step(t, carry):
        tile(pl.multiple_of(t * tk, tk), tk, False)
        return carry

    n_whole = (i * tq) // tk
    lax.fori_loop(0, n_whole, kv_step, 0)
    if tk > tq:
        @pl.when((i * tq) % tk != 0)
        def _():
            tile(pl.multiple_of(n_whole * tk, tq), tq, False)
    tile(pl.multiple_of(i * tq, tq), tq, True)

    for hb in range(B_HEADS):
        a1 = acc_ref[2 * hb]
        a2 = acc_ref[2 * hb + 1]
        o = a1[:, :LANES] / a1[:, LANES:] - lam * (a2[:, :LANES] / a2[:, LANES:])
        o = o * lax.rsqrt(jnp.mean(o * o, axis=-1, keepdims=True) + RMS_EPS)
        o = o * g_ref[...] * (1.0 - lambda_init)
        o_ref[0, :, hb * LANES:(hb + 1) * LANES] = o.astype(o_ref.dtype)


def _diff(bq, bk, bv, lq1, lk1, lq2, lk2, subln_g, lambda_init, tq, tk):
    bsz, seq, w = bq.shape
    kern = functools.partial(_diff_kernel, tq=tq, tk=tk, lambda_init=lambda_init)
    resident = lambda: pl.BlockSpec((1, seq, w), lambda b, i: (b, 0, 0),
                                    pipeline_mode=pl.Buffered(1))
    small = lambda a: pl.BlockSpec(a.shape, lambda b, i: (0, 0))
    return pl.pallas_call(
        kern,
        grid=(bsz, seq // tq),
        in_specs=[pl.BlockSpec((1, tq, w), lambda b, i: (b, i, 0)),
                  resident(), resident(),
                  small(lq1), small(lk1), small(lq2), small(lk2), small(subln_g)],
        out_specs=pl.BlockSpec((1, tq, w), lambda b, i: (b, i, 0)),
        out_shape=jax.ShapeDtypeStruct((bsz, seq, w), BF16),
        scratch_shapes=[pltpu.VMEM((2 * B_HEADS, tq, LANES), F32),
                        pltpu.VMEM((2 * B_HEADS, tq, 2 * LANES), F32)],
        compiler_params=_cparams("parallel", "arbitrary"),
    )(bq, bk, bv, lq1, lk1, lq2, lk2, subln_g)


def _first_argmax_rows(v, row):
    m = jnp.max(v, axis=0, keepdims=True)
    idx = jnp.min(jnp.where(v == m, row, v.shape[0]), axis=0, keepdims=True)
    return m, idx


def _scatter_rows(dk, gk, r0, nrows):
    r_iota = (lax.broadcasted_iota(jnp.int32, (nrows, dk.shape[1]), 0) + r0).astype(F32)
    out = jnp.zeros(r_iota.shape, F32)
    for k in range(dk.shape[0]):
        out = jnp.where(r_iota == dk[k:k + 1, :], 1.0 if gk is None else gk[k:k + 1, :], out)
    return out


def _oproj_kernel(oa_ref, ob_ref, h0_ref, woa_ref, wob_ref, g_ref, b_ref, wrt_ref, rb_ref,
                  h1_ref, h1b_ref, dk_ref, gk_ref, cnt_ref, *, alpha):
    mix = (jnp.dot(oa_ref[...], woa_ref[...], preferred_element_type=F32)
           + jnp.dot(ob_ref[...], wob_ref[...], preferred_element_type=F32))
    h1 = _layer_norm(alpha * h0_ref[...] + mix, g_ref[...], b_ref[...])
    h1_ref[...] = h1
    h1b = h1.astype(BF16)
    h1b_ref[...] = h1b

    logits = lax.dot_general(wrt_ref[...], h1, (((1,), (1,)), ((), ())),
                             precision=lax.Precision.HIGHEST, preferred_element_type=F32)
    scores = jax.nn.sigmoid(logits)
    biased = scores + rb_ref[...]
    tm = scores.shape[1]
    per_g = N_EXPERTS // N_GROUPS
    row8 = lax.broadcasted_iota(jnp.int32, (per_g, tm), 0)

    gs = []
    for g in range(N_GROUPS):
        blk = biased[g * per_g:(g + 1) * per_g, :]
        m1, i1 = _first_argmax_rows(blk, row8)
        m2 = jnp.max(jnp.where(row8 == i1, -jnp.inf, blk), axis=0, keepdims=True)
        gs.append(m1 + m2)
    gscore = jnp.concatenate(gs, axis=0)
    rowg = lax.broadcasted_iota(jnp.int32, (N_GROUPS, tm), 0)
    gsel = jnp.zeros((N_GROUPS, tm), F32)
    for _ in range(TOPK_GROUPS):
        _, ig = _first_argmax_rows(jnp.where(gsel > 0.0, -jnp.inf, gscore), rowg)
        gsel = jnp.where(rowg == ig, 1.0, gsel)

    rowe = lax.broadcasted_iota(jnp.int32, (N_EXPERTS, tm), 0)
    live = jnp.concatenate(
        [jnp.broadcast_to(gsel[g:g + 1, :], (per_g, tm)) for g in range(N_GROUPS)], axis=0)
    esel = jnp.zeros((N_EXPERTS, tm), F32)
    for _ in range(TOP_K):
        cand = jnp.where(live > 0.0, biased, -jnp.inf)
        m = jnp.max(cand, axis=0, keepdims=True)
        idx = jnp.min(jnp.where((live > 0.0) & (cand == m), rowe, N_EXPERTS),
                      axis=0, keepdims=True)
        hit = rowe == idx
        esel = jnp.where(hit, 1.0, esel)
        live = jnp.where(hit, 0.0, live)
    picked = jnp.where(esel > 0.0, scores, 0.0)
    denom = jnp.sum(picked, axis=0, keepdims=True)
    gate = picked / denom * ROUTED_SCALE

    sel_b = esel.astype(BF16)
    before_t = (lax.broadcasted_iota(jnp.int32, (tm, tm), 0)
                < lax.broadcasted_iota(jnp.int32, (tm, tm), 1)).astype(F32).astype(BF16)
    before_e = (lax.broadcasted_iota(jnp.int32, (N_EXPERTS, N_EXPERTS), 1)
                < lax.broadcasted_iota(jnp.int32, (N_EXPERTS, N_EXPERTS), 0)).astype(F32).astype(BF16)
    rank = jnp.dot(sel_b, before_t, preferred_element_type=F32)
    order = jnp.dot(before_e, sel_b, preferred_element_type=F32)
    cnt = jnp.sum(esel, axis=1, keepdims=True)
    cnt16 = jnp.floor((cnt + (ROW_ALIGN - 1)) / ROW_ALIGN) * ROW_ALIGN
    off = jnp.dot(before_e, jnp.broadcast_to(cnt16, (N_EXPERTS, tm)).astype(BF16),
                  preferred_element_type=F32)
    dest = off + rank
    dks, gks = [], []
    for k in range(TOP_K):
        kth = (esel > 0.0) & (order == k)
        dks.append(jnp.sum(jnp.where(kth, dest, 0.0), axis=0, keepdims=True))
        gks.append(jnp.sum(jnp.where(kth, gate, 0.0), axis=0, keepdims=True))
    dk = jnp.concatenate(dks, axis=0)
    dk_ref[...] = dk
    gk_ref[...] = jnp.concatenate(gks, axis=0)
    counts = _dot_nt(jnp.ones((8, tm), BF16), sel_b)
    cnt_ref[0] = jnp.concatenate([counts, jnp.zeros((8, LANES - N_EXPERTS), F32)], axis=1)


def _oproj(oa, ob, h0, woa, wob, g, b, wrt, rb, alpha, tm):
    n, d = h0.shape
    nt = n // tm
    row = lambda w: pl.BlockSpec((tm, w), lambda i: (i, 0))
    col = lambda: pl.BlockSpec((TOP_K, tm), lambda i: (0, i))
    full = lambda a: pl.BlockSpec(a.shape, lambda i: (0, 0))
    kt = jax.ShapeDtypeStruct((TOP_K, n), F32)
    return pl.pallas_call(
        functools.partial(_oproj_kernel, alpha=alpha),
        grid=(nt,),
        in_specs=[row(512), row(512), row(d), full(woa), full(wob), full(g), full(b),
                  full(wrt), full(rb)],
        out_specs=(row(d), row(d), col(), col(),
                   pl.BlockSpec((1, 8, LANES), lambda i: (i, 0, 0))),
        out_shape=(jax.ShapeDtypeStruct((n, d), F32),
                   jax.ShapeDtypeStruct((n, d), BF16),
                   kt,
                   kt,
                   jax.ShapeDtypeStruct((nt, 8, LANES), F32)),
        compiler_params=_cparams("parallel"),
    )(oa, ob, h0, woa, wob, g, b, wrt, rb)


def _swiglu(x, wg, wu, wd):
    hid = (jax.nn.silu(jnp.dot(x, wg, preferred_element_type=F32))
           * jnp.dot(x, wu, preferred_element_type=F32))
    return jnp.dot(hid.astype(BF16), wd, preferred_element_type=F32)


def _dispatch_kernel(tile_off_ref, exp_off_ref, len_ref, used_ref, gap_off_ref, gap_len_ref,
                     h1b_ref, dk_ref, xg_ref, xs_ref, zero_ref, run_sem, gap_sem, *, rb):
    i = pl.program_id(0)
    nt = pl.num_programs(0)
    rt = xs_ref.shape[1]
    slot = i % 2

    def wait_runs(step):
        n = pl.multiple_of(used_ref[step], ROW_ALIGN)
        s = step % 2
        pltpu.make_async_copy(xs_ref.at[s, pl.ds(0, n), :], xg_ref.at[pl.ds(0, n), :],
                              run_sem.at[s]).wait()

    @pl.when(i == 0)
    def _():
        zero_ref[...] = jnp.zeros(zero_ref.shape, zero_ref.dtype)

        def issue_gap(g, carry):
            n = pl.multiple_of(gap_len_ref[g], ROW_ALIGN)

            @pl.when(n > 0)
            def _():
                pltpu.make_async_copy(
                    zero_ref.at[pl.ds(0, n), :],
                    xg_ref.at[pl.ds(pl.multiple_of(gap_off_ref[g], ROW_ALIGN), n), :],
                    gap_sem).start()
            return carry

        lax.fori_loop(0, gap_len_ref.shape[0], issue_gap, 0)

    @pl.when(i >= 2)
    def _():
        wait_runs(i - 2)

    used = used_ref[i]
    dk = dk_ref[...]

    def sort_rows(r0):
        onehot = _scatter_rows(dk, None, r0, rb).astype(BF16)
        rows = jnp.dot(onehot, h1b_ref[...], preferred_element_type=F32)
        xs_ref[slot, r0:r0 + rb, :] = rows.astype(BF16)

    always = h1b_ref.shape[0] * TOP_K // rb * rb
    for r0 in range(0, always, rb):
        sort_rows(r0)
    for r0 in range(always, rt, rb):
        @pl.when(r0 < used)
        def _():
            sort_rows(r0)

    def issue_run(e, carry):
        r = i * N_EXPERTS + e
        n = pl.multiple_of(len_ref[r], ROW_ALIGN)

        @pl.when(n > 0)
        def _():
            pltpu.make_async_copy(
                xs_ref.at[slot, pl.ds(pl.multiple_of(tile_off_ref[r], ROW_ALIGN), n), :],
                xg_ref.at[pl.ds(pl.multiple_of(exp_off_ref[r], ROW_ALIGN), n), :],
                run_sem.at[slot]).start()
        return carry

    lax.fori_loop(0, N_EXPERTS, issue_run, 0)

    @pl.when(i == nt - 1)
    def _():
        @pl.when(i >= 1)
        def _():
            wait_runs(i - 1)
        wait_runs(i)

        def gap_rows(g, rows):
            return rows + gap_len_ref[g]
        n_gap = pl.multiple_of(lax.fori_loop(0, gap_len_ref.shape[0], gap_rows, 0), ROW_ALIGN)

        @pl.when(n_gap > 0)
        def _():
            pltpu.make_async_copy(xg_ref.at[pl.ds(0, n_gap), :], xg_ref.at[pl.ds(0, n_gap), :],
                                  gap_sem).wait()


def _dispatch(h1b, dk, tile_off, exp_off, lens, used, gap_off, gap_len, n_dst_rows, tm, rt):
    n, d = h1b.shape
    return pl.pallas_call(
        functools.partial(_dispatch_kernel, rb=512),
        grid_spec=pltpu.PrefetchScalarGridSpec(
            num_scalar_prefetch=6,
            grid=(n // tm,),
            in_specs=[pl.BlockSpec((tm, d), lambda i, *_: (i, 0)),
                      pl.BlockSpec((TOP_K, tm), lambda i, *_: (0, i))],
            out_specs=pl.BlockSpec(memory_space=pl.ANY),
            scratch_shapes=[pltpu.VMEM((2, rt, d), BF16), pltpu.VMEM((rt, d), BF16),
                            pltpu.SemaphoreType.DMA((2,)), pltpu.SemaphoreType.DMA(())]),
        out_shape=jax.ShapeDtypeStruct((n_dst_rows, d), BF16),
        compiler_params=_cparams("arbitrary"),
    )(tile_off, exp_off, lens, used, gap_off, gap_len, h1b, dk)


def _expert_kernel(blk_e_ref, blk_valid_ref, x_ref, wg_ref, wu_ref, wd_ref, y_ref):
    b = pl.program_id(0)
    valid = blk_valid_ref[b]

    @pl.when(valid > 0)
    def _():
        y_ref[...] = _swiglu(x_ref[...], wg_ref[...], wu_ref[...], wd_ref[...]).astype(BF16)

    @pl.when(valid <= 0)
    def _():
        y_ref[...] = jnp.zeros(y_ref.shape, BF16)


def _experts(xg, blk_e, blk_valid, wg, wu, wd):
    nr, d = xg.shape
    f = wg.shape[2]
    rows = pl.BlockSpec((MOE_BLOCK, d), lambda b, be, bv: (b, 0))
    return pl.pallas_call(
        _expert_kernel,
        grid_spec=pltpu.PrefetchScalarGridSpec(
            num_scalar_prefetch=2,
            grid=(nr // MOE_BLOCK,),
            in_specs=[rows,
                      pl.BlockSpec((None, d, f), lambda b, be, bv: (be[b], 0, 0)),
                      pl.BlockSpec((None, d, f), lambda b, be, bv: (be[b], 0, 0)),
                      pl.BlockSpec((None, f, d), lambda b, be, bv: (be[b], 0, 0))],
            out_specs=rows),
        out_shape=jax.ShapeDtypeStruct((nr, d), BF16),
        compiler_params=_cparams("parallel"),
    )(blk_e, blk_valid, xg, wg, wu, wd)


def _combine_kernel(tile_off_ref, exp_off_ref, len_ref, used_ref, yg_ref, dk_ref, gk_ref, h1_ref,
                    h1b_ref, p_ref, sg_ref, su_ref, sd_ref, g_ref, b_ref, wg_ref, bg_ref, wp_ref,
                    o_ref, ys_ref, ffn_ref, sem, *, alpha, rb):
    i = pl.program_id(0)
    nt = pl.num_programs(0)
    rt = ys_ref.shape[1]
    slot = i % 2

    def fetch_runs(tile):
        s = tile % 2

        def issue_run(e, carry):
            r = tile * N_EXPERTS + e
            n = pl.multiple_of(len_ref[r], ROW_ALIGN)

            @pl.when(n > 0)
            def _():
                pltpu.make_async_copy(
                    yg_ref.at[pl.ds(pl.multiple_of(exp_off_ref[r], ROW_ALIGN), n), :],
                    ys_ref.at[s, pl.ds(pl.multiple_of(tile_off_ref[r], ROW_ALIGN), n), :],
                    sem.at[s]).start()
            return carry

        lax.fori_loop(0, N_EXPERTS, issue_run, 0)

    @pl.when(i == 0)
    def _():
        ys_ref[...] = jnp.zeros(ys_ref.shape, ys_ref.dtype)
        fetch_runs(i)

    @pl.when(i + 1 < nt)
    def _():
        fetch_runs(i + 1)

    dk = dk_ref[...]
    gk = gk_ref[...]
    ffn = _swiglu(h1b_ref[...], sg_ref[...], su_ref[...], sd_ref[...])
    used = used_ref[i]
    n_used = pl.multiple_of(used, ROW_ALIGN)
    pltpu.make_async_copy(yg_ref.at[pl.ds(0, n_used), :], ys_ref.at[slot, pl.ds(0, n_used), :],
                          sem.at[slot]).wait()

    def routed(r0):
        weights = _scatter_rows(dk, gk, r0, rb).astype(BF16)
        return lax.dot_general(weights, ys_ref[slot, r0:r0 + rb, :], (((0,), (0,)), ((), ())),
                               preferred_element_type=F32)

    tm = h1_ref.shape[0]
    always = tm * TOP_K // rb * rb
    for r0 in range(0, always, rb):
        ffn = ffn + routed(r0)
    ffn_ref[...] = ffn
    for r0 in range(always, rt, rb):
        @pl.when(r0 < used)
        def _():
            ffn_ref[...] += routed(r0)
    h2 = _layer_norm(alpha * h1_ref[...] + ffn_ref[...], g_ref[...], b_ref[...])
    emb_gate = jax.nn.sigmoid(jnp.dot(h2.astype(BF16), wg_ref[...], preferred_element_type=F32)
                              + bg_ref[...])
    emb = jnp.dot(p_ref[...].astype(BF16), wp_ref[...], preferred_element_type=F32)
    o_ref[...] = h2 + emb_gate * emb


def _combine(tile_off, exp_off, lens, used, yg, dk, gk, h1, h1b, p2, sg, su, sd, g, b, wg, bg, wp,
             alpha, tm, rt):
    n, d = h1.shape
    row = lambda w: pl.BlockSpec((tm, w), lambda i, *_: (i, 0))
    col = lambda: pl.BlockSpec((TOP_K, tm), lambda i, *_: (0, i))
    full = lambda a: pl.BlockSpec(a.shape, lambda i, *_: (0, 0))
    return pl.pallas_call(
        functools.partial(_combine_kernel, alpha=alpha, rb=512),
        grid_spec=pltpu.PrefetchScalarGridSpec(
            num_scalar_prefetch=4,
            grid=(n // tm,),
            in_specs=[pl.BlockSpec(memory_space=pl.ANY), col(), col(), row(d),
                      row(d), row(p2.shape[1]), full(sg), full(su), full(sd), full(g), full(b),
                      full(wg), full(bg), full(wp)],
            out_specs=row(d),
            scratch_shapes=[pltpu.VMEM((2, rt, d), yg.dtype), pltpu.VMEM((tm, d), F32),
                            pltpu.SemaphoreType.DMA((2,))]),
        out_shape=jax.ShapeDtypeStruct((n, d), F32),
        compiler_params=_cparams("arbitrary"),
    )(tile_off, exp_off, lens, used, yg, dk, gk, h1, h1b, p2, sg, su, sd, g, b, wg, bg, wp)


def _tiles(seq):
    tm = min(256, seq)
    tq_sel = min(512, seq)
    tk_sel = min(512, seq)
    tq_att = min(512, seq)
    tk_att = min(512, seq)
    rt = -(-(tm * TOP_K + N_EXPERTS * (ROW_ALIGN - 1)) // 512) * 512
    return tm, tq_sel, tk_sel, tq_att, tk_att, rt


def kernel(x, p, positions, ln_emb_g, ln_emb_b, w_in, w_o, diff_lq1, diff_lk1, diff_lq2, diff_lk2,
           diff_subln_g, ln1_g, ln1_b, w_router, router_bias, w_exp_gate, w_exp_up, w_exp_down,
           w_sh_gate, w_sh_up, w_sh_down, ln2_g, ln2_b, w_ple_gate, b_ple_gate, w_ple_proj):
    bsz, seq, d = x.shape
    depth = w_in.shape[0]
    n = bsz * seq
    n_keep = min(IDX_TOPK_MAX, seq // 4)
    alpha = (2.0 * depth) ** 0.25
    tm, tq_sel, tk_sel, tq_att, tk_att, rt = _tiles(seq)
    n_blocks = -(-((n // tm) * (tm * TOP_K + N_EXPERTS * (ROW_ALIGN - 1))
                   + N_EXPERTS * (MOE_BLOCK - 1)) // MOE_BLOCK)
    row1 = lambda v: v.reshape(1, -1)

    inv = ROPE_THETA ** (-jnp.arange(0, ROPE_DIM, 2, dtype=F32) / ROPE_DIM)
    inv_lanes = jnp.tile(inv, LANES // (ROPE_DIM // 2)).reshape(1, LANES)
    pos2 = positions.reshape(n, 1)

    h = x.reshape(n, d)
    out = None
    for li in range(depth):
        lambda_init = 0.8 - 0.6 * math.exp(-0.3 * li)
        w = w_in[li]
        aqw, akw, avw = w[:, 0:512], w[:, 512:1024], w[:, 1024:1536]
        iqw, ikw, iww = w[:, 1536:2048], w[:, 2048:2112], w[:, 2112:2120]
        bqw, bkw, bvw = w[:, 2120:2632], w[:, 2632:3144], w[:, 3144:3656]
        w_rope = jnp.concatenate([aqw, akw, iqw, ikw, ikw, bqw, bkw], axis=1).astype(BF16)
        w_plain = jnp.concatenate([avw, bvw], axis=1).astype(BF16)
        w_idx_t = jnp.concatenate(
            [iww.T, jnp.zeros((LANES - IDX_HEADS, d), F32)], axis=0).astype(BF16)

        (h0, aq, ak, iq, ik2, bq, bk, av, bv, iw) = _proj(
            h, pos2, inv_lanes, row1(ln_emb_g), row1(ln_emb_b), w_rope, w_plain, w_idx_t, tm)
        assert depth == 1

        r3 = lambda a: a.reshape(bsz, seq, a.shape[-1])
        bias = _select(r3(iq), iw, r3(ik2), n_keep, tq_sel, tk_sel)
        out_a = _dsa(r3(aq), r3(ak), r3(av), bias, tq_att, tk_att)
        out_b = _diff(r3(bq), r3(bk), r3(bv), row1(diff_lq1[li]), row1(diff_lk1[li]),
                      row1(diff_lq2[li]), row1(diff_lk2[li]), row1(diff_subln_g[li]),
                      lambda_init, tq_att, tk_att)

        wo = w_o[li].astype(BF16)
        h1, h1b, dk, gk, cnt_out = _oproj(
            out_a.reshape(n, -1), out_b.reshape(n, -1), h0, wo[0:512], wo[512:1024],
            row1(ln1_g[li]), row1(ln1_b[li]), w_router[li].T, router_bias[li].reshape(-1, 1),
            alpha, tm)

        nt = n // tm
        c16 = jnp.ceil(cnt_out[:, 0, :N_EXPERTS] / ROW_ALIGN) * ROW_ALIGN
        tri = lambda m: jnp.tril(jnp.ones((m, m), F32))
        hi = lax.Precision.HIGHEST
        tile_off = jnp.dot(c16, tri(N_EXPERTS).T, precision=hi) - c16
        cum_tiles = jnp.dot(tri(nt), c16, precision=hi)
        tot = cum_tiles[-1]
        totp = jnp.ceil(tot / MOE_BLOCK) * MOE_BLOCK
        pend = jnp.dot(totp, tri(N_EXPERTS).T, precision=hi)
        ebase = pend - totp
        exp_off = ebase[None, :] + cum_tiles - c16
        blk0 = jnp.arange(n_blocks, dtype=F32) * MOE_BLOCK
        blk_e = jnp.minimum(jnp.sum(pend[None, :] <= blk0[:, None], axis=1), N_EXPERTS - 1)
        onehot_e = (blk_e[:, None] == jnp.arange(N_EXPERTS)[None, :]).astype(F32)
        blk_end = jnp.dot(onehot_e, ebase + tot, precision=hi)
        blk_valid = jnp.clip(blk_end - blk0, 0, MOE_BLOCK)
        flat = lambda a: a.reshape(-1).astype(jnp.int32)
        n_rows = n_blocks * MOE_BLOCK
        tail0 = pend[-1] + jnp.arange(-(-n_rows // rt), dtype=F32) * rt
        gap_off = jnp.concatenate([ebase + tot, tail0])
        gap_len = jnp.concatenate([totp - tot, jnp.clip(n_rows - tail0, 0, rt)])

        weg, weu, wed = (w_exp_gate[li].astype(BF16), w_exp_up[li].astype(BF16),
                         w_exp_down[li].astype(BF16))
        used = jnp.sum(c16, axis=1)
        xg = _dispatch(h1b, dk, flat(tile_off), flat(exp_off), flat(c16), flat(used),
                       flat(gap_off), flat(gap_len), n_rows, tm, rt)
        yg = _experts(xg, flat(blk_e), flat(blk_valid), weg, weu, wed)
        out = _combine(flat(tile_off), flat(exp_off), flat(c16), flat(used), yg, dk, gk, h1, h1b,
                       p[li].reshape(n, -1), w_sh_gate[li].astype(BF16),
                       w_sh_up[li].astype(BF16), w_sh_down[li].astype(BF16), row1(ln2_g[li]),
                       row1(ln2_b[li]), w_ple_gate[li].astype(BF16), row1(b_ple_gate[li]),
                       w_ple_proj[li].astype(BF16), alpha, tm, rt)
        h = out
    return out.reshape(bsz, seq, d)
```

```python
import functools
import math

import jax
import jax.numpy as jnp
from jax import lax
from jax.experimental import pallas as pl
from jax.experimental.pallas import tpu as pltpu

CHUNK = 64
HEAD_DIM = 64
ROPE_DIM = 64
ROPE_THETA = 10000.0
A_HEADS = 8
IDX_HEADS = 8
IDX_DIM = 64
IDX_TOPK_MAX = 256
B_HEADS = 4
B_QK_DIM = 64
B_V_DIM = 128
N_EXPERTS = 64
TOP_K = 8
N_GROUPS = 8
TOPK_GROUPS = 4
ROUTED_SCALE = 2.5
LN_EPS = 1e-5
RMS_EPS = 1e-5

LANES = 128
ROW_ALIGN = 16
SCATTER_SUB = 256
MOE_BLOCK = 1024
VMEM_LIMIT = 56 * 1024 * 1024
NEG_BIG = -1e30
INT_MIN = -(2 ** 31)
LOG2E = math.log2(math.e)
GROUPS = 256

F32 = jnp.float32
BF16 = jnp.bfloat16


def _cparams(*sem):
    return pltpu.CompilerParams(dimension_semantics=sem, vmem_limit_bytes=VMEM_LIMIT)


def _layer_norm(x, g, b):
    mu = jnp.mean(x, axis=-1, keepdims=True)
    xc = x - mu
    var = jnp.mean(xc * xc, axis=-1, keepdims=True)
    return xc * lax.rsqrt(var + LN_EPS) * g + b


def _dot_nt(a, b):
    return lax.dot_general(a, b, (((1,), (1,)), ((), ())), preferred_element_type=F32)


def _split_lane_halves(pair):
    lo = lax.broadcasted_iota(jnp.int32, pair.shape, 1) < (LANES // 2)
    pf = pair.astype(F32)
    return (jnp.where(lo, pf, 0.0).astype(pair.dtype), jnp.where(lo, 0.0, pf).astype(pair.dtype))


def _proj_kernel(x_ref, pos_ref, inv_ref, g_ref, b_ref, wr_ref, wp_ref, wi_ref,
                 h0_ref, aq_ref, ak_ref, iq_ref, ik_ref, bq_ref, bk_ref,
                 av_ref, bv_ref, iw_ref):
    hn = _layer_norm(x_ref[...], g_ref[...], b_ref[...])
    h0_ref[...] = hn
    hb = hn.astype(BF16)

    tm = x_ref.shape[0]
    nq = LANES // (ROPE_DIM // 2)
    tr = tm // nq
    pos = pos_ref[...].astype(F32)
    lane = lax.broadcasted_iota(jnp.int32, (tr, LANES), 1)
    blk = lane // (ROPE_DIM // 2)
    pos_packed = jnp.broadcast_to(pos[0:tr], (tr, LANES))
    for q in range(1, nq):
        pos_packed = jnp.where(blk == q, pos[q * tr:(q + 1) * tr], pos_packed)
    ang = pos_packed * inv_ref[...]
    cos_packed = jnp.cos(ang)
    sin_packed = jnp.sin(ang)
    first_half = (lane % ROPE_DIM) < (ROPE_DIM // 2)

    def spread(table, q):
        own = jnp.where(blk == q, table, 0.0)
        t = own
        for s in range(1, nq):
            t = t + pltpu.roll(own, s * (ROPE_DIM // 2), axis=1)
        return t

    pr = jnp.dot(hb, wr_ref[...], preferred_element_type=F32)
    outs = ((aq_ref, 0, 4, HEAD_DIM ** -0.5 * LOG2E), (ak_ref, 4, 4, 1.0), (iq_ref, 8, 4, 1.0),
            (ik_ref, 12, 1, 1.0), (bq_ref, 13, 4, B_QK_DIM ** -0.5 * LOG2E), (bk_ref, 17, 4, 1.0))
    for q in range(nq):
        cos = spread(cos_packed, q)
        sin = spread(sin_packed, q)
        sin_signed = jnp.where(first_half, -sin, sin)
        rows = slice(q * tr, (q + 1) * tr)
        for ref, g0, ng, scale in outs:
            for j in range(ng):
                v = pr[rows, (g0 + j) * LANES:(g0 + j + 1) * LANES]
                partner = jnp.where(first_half,
                                    pltpu.roll(v, LANES - ROPE_DIM // 2, axis=1),
                                    pltpu.roll(v, ROPE_DIM // 2, axis=1))
                v = v * cos + partner * sin_signed
                if scale != 1.0:
                    v = v * scale
                ref[rows, j * LANES:(j + 1) * LANES] = v.astype(ref.dtype)

    pp = jnp.dot(hb, wp_ref[...], preferred_element_type=F32)
    av_ref[...] = pp[:, 0:512].astype(BF16)
    bv_ref[...] = pp[:, 512:1024].astype(BF16)
    iw_ref[...] = _dot_nt(wi_ref[...], hb)[0:IDX_HEADS, :]


def _proj(x2, pos2, inv, g, b, w_rope, w_plain, w_idx_t, tm):
    n, d = x2.shape
    nr = w_rope.shape[1]
    npl = w_plain.shape[1]
    row = lambda w: pl.BlockSpec((tm, w), lambda i: (i, 0))
    full = lambda a: pl.BlockSpec(a.shape, lambda i: (0, 0))
    out_shape = (
        jax.ShapeDtypeStruct((n, d), F32),
        jax.ShapeDtypeStruct((n, 512), BF16),
        jax.ShapeDtypeStruct((n, 512), BF16),
        jax.ShapeDtypeStruct((n, 512), BF16),
        jax.ShapeDtypeStruct((n, 128), BF16),
        jax.ShapeDtypeStruct((n, 512), BF16),
        jax.ShapeDtypeStruct((n, 512), BF16),
        jax.ShapeDtypeStruct((n, 512), BF16),
        jax.ShapeDtypeStruct((n, 512), BF16),
        jax.ShapeDtypeStruct((IDX_HEADS, n), F32),
    )
    return pl.pallas_call(
        _proj_kernel,
        grid=(n // tm,),
        in_specs=[row(d), row(1), full(inv), full(g), full(b), full(w_rope), full(w_plain),
                  full(w_idx_t)],
        out_specs=(row(d), row(512), row(512), row(512), row(128), row(512), row(512),
                   row(512), row(512), pl.BlockSpec((IDX_HEADS, tm), lambda i: (0, i))),
        out_shape=out_shape,
        compiler_params=_cparams("parallel"),
    )(x2, pos2, inv, g, b, w_rope, w_plain, w_idx_t)


def _sortable(s):
    i = lax.bitcast_convert_type(s, jnp.int32)
    return i ^ ((i >> 31) & 0x7FFFFFFF)


def _select_kernel(iq_ref, iw_ref, ik_ref, bias_ref, keys_ref, gmax_ref, *, tq, tk, n_keep,
                   w_scale):
    i = pl.program_id(1)
    key_drop = 32 - 8 * jnp.dtype(BF16).itemsize
    seq = ik_ref.shape[1]
    n_t = ((i + 1) * tq + tk - 1) // tk

    q_chunk = (i * tq + lax.broadcasted_iota(jnp.int32, (tk, tq), 1)) // CHUNK
    k_row = lax.broadcasted_iota(jnp.int32, (tk, tq), 0)
    wb = iw_ref[...] * w_scale
    q_halves = []
    for hp in range(IDX_HEADS // 2):
        q_halves.extend(_split_lane_halves(iq_ref[0, :, hp * LANES:(hp + 1) * LANES]))

    def score_tile(t, carry, boundary):
        off = pl.multiple_of(t * tk, tk)
        kt = ik_ref[0, pl.ds(off, tk), :]
        sc = jnp.zeros((tk, tq), F32)
        for h in range(IDX_HEADS):
            rel = jnp.maximum(_dot_nt(kt, q_halves[h]), 0.0)
            sc = sc + wb[h:h + 1, :] * rel
        key = _sortable(sc.astype(BF16).astype(F32) + 0.0) >> key_drop
        if boundary:
            key = jnp.where(((off + k_row) // CHUNK) <= q_chunk, key, INT_MIN)
        keys_ref[pl.ds(off, tk), :] = key
        gm = key[0:GROUPS, :]
        for g0 in range(GROUPS, tk, GROUPS):
            gm = jnp.maximum(gm, key[g0:g0 + GROUPS, :])
        gmax_ref[...] = jnp.maximum(gmax_ref[...], gm)
        return carry

    gmax_ref[...] = jnp.full(gmax_ref.shape, INT_MIN, jnp.int32)
    n_full = (i * tq + CHUNK) // tk
    lax.fori_loop(0, n_full, functools.partial(score_tile, boundary=False), 0)
    lax.fori_loop(n_full, n_t, functools.partial(score_tile, boundary=True), 0)

    def count_ge(cand):
        def body(t, acc):
            off = pl.multiple_of(t * tk, tk)
            m = (keys_ref[pl.ds(off, tk), :] >= cand).astype(jnp.int32)
            return acc + jnp.sum(m.reshape(tk // 8, 8, tq), axis=0)
        acc = lax.fori_loop(0, n_t, body, jnp.zeros((8, tq), jnp.int32))
        return jnp.sum(acc, axis=0, keepdims=True)

    gmax = gmax_ref[...]
    n_adm = ((i * tq + lax.broadcasted_iota(jnp.int32, (1, tq), 1)) // CHUNK + 1) * CHUNK
    few = n_adm <= n_keep
    lo0 = jnp.where(few, INT_MIN + 1, jnp.min(gmax, axis=0, keepdims=True))
    hi0 = jnp.where(few, INT_MIN + 2, jnp.max(gmax, axis=0, keepdims=True) + 1)

    def probe(state):
        lo, hi, n_hi, thr, n_thr, done, _, it = state
        mid = (lo >> 1) + (hi >> 1) + (lo & hi & 1)
        closed = mid == lo
        cnt = count_ge(mid)
        exact = cnt == n_keep
        up = cnt >= n_keep
        live = done == 0
        settle = live & (exact | closed)
        thr = jnp.where(settle, mid, thr)
        n_thr = jnp.where(settle, cnt, n_thr)
        lo = jnp.where(live & up, mid, lo)
        down = live & jnp.logical_not(up)
        hi = jnp.where(down, mid, hi)
        n_hi = jnp.where(down, cnt, n_hi)
        done = jnp.where(exact | closed, 1, done)
        return lo, hi, n_hi, thr, n_thr, done, jnp.min(done), it + 1

    def searching(state):
        return (state[6] == 0) & (state[7] < 34)

    done0 = few.astype(jnp.int32)
    none = jnp.zeros((1, tq), jnp.int32)
    state = lax.while_loop(
        searching, probe, (lo0, hi0, none, lo0, none, done0, jnp.min(done0), jnp.int32(0)))
    n_above, thr, n_ge = state[2], state[3], state[4]

    tied = n_ge > n_keep
    any_tied = jnp.max(tied.astype(jnp.int32)) > 0
    tau = thr

    def write_bias(off, keep_t):
        bias_t = jnp.where(keep_t, 0.0, NEG_BIG)
        bias_ref[0, :, pl.ds(off, tk)] = bias_t.T.astype(BF16)

    @pl.when(jnp.logical_not(any_tied))
    def _():
        def write_tile(t, carry):
            off = pl.multiple_of(t * tk, tk)
            write_bias(off, keys_ref[pl.ds(off, tk), :] >= tau)
            return carry
        lax.fori_loop(0, n_t, write_tile, 0)

    @pl.when(any_tied)
    def _():
        room = jnp.where(tied, (n_keep - n_above).astype(F32), float(seq))
        upto = (lax.broadcasted_iota(jnp.int32, (tk, tk), 1)
                <= lax.broadcasted_iota(jnp.int32, (tk, tk), 0)).astype(F32).astype(BF16)

        def write_tile(t, seen_eq):
            off = pl.multiple_of(t * tk, tk)
            key = keys_ref[pl.ds(off, tk), :]
            eq_f = jnp.where(key == tau, 1.0, 0.0)
            eq_rank = seen_eq + jnp.dot(upto, eq_f.astype(BF16), preferred_element_type=F32)
            order = jnp.where(key == tau, eq_rank, jnp.where(key > tau, 0.0, 2.0 * seq))
            write_bias(off, order <= room)
            return seen_eq + jnp.sum(eq_f, axis=0, keepdims=True)

        lax.fori_loop(0, n_t, write_tile, jnp.zeros((1, tq), F32))

    def blank_tile(t, carry):
        off = pl.multiple_of(t * tk, tk)
        bias_ref[0, :, pl.ds(off, tk)] = jnp.full((tq, tk), NEG_BIG, BF16)
        return carry

    lax.fori_loop(n_t, seq // tk, blank_tile, 0)


def _select(iq, iw, ik2, n_keep, tq, tk):
    bsz, seq, _ = iq.shape
    kern = functools.partial(_select_kernel, tq=tq, tk=tk, n_keep=n_keep,
                             w_scale=(IDX_HEADS ** -0.5) * (IDX_DIM ** -0.5))
    return pl.pallas_call(
        kern,
        grid=(bsz, seq // tq),
        in_specs=[pl.BlockSpec((1, tq, 512), lambda b, i: (b, i, 0)),
                  pl.BlockSpec((IDX_HEADS, tq), lambda b, i: (0, b * (seq // tq) + i)),
                  pl.BlockSpec((1, seq, 128), lambda b, i: (b, 0, 0))],
        out_specs=pl.BlockSpec((1, tq, seq), lambda b, i: (b, i, 0)),
        out_shape=jax.ShapeDtypeStruct((bsz, seq, seq), BF16),
        scratch_shapes=[pltpu.VMEM((seq, tq), jnp.int32), pltpu.VMEM((GROUPS, tq), jnp.int32)],
        compiler_params=_cparams("parallel", "parallel"),
    )(iq, iw, ik2)


def _flash_update(j, s, v_ext, m_ref, acc_ref):
    tk = s.shape[1]
    m_old = m_ref[j]
    m_new = jnp.maximum(m_old, jnp.max(s, axis=1, keepdims=True))
    alpha = jnp.exp2(m_old - m_new)
    p = jnp.exp2(s - jnp.concatenate([m_new] * (tk // LANES), axis=1))
    m_ref[j] = m_new
    acc_ref[j] = (acc_ref[j] * jnp.concatenate([alpha, alpha], axis=1)
                  + jnp.dot(p.astype(BF16), v_ext, preferred_element_type=F32))


def _dsa_kernel(q_ref, k_ref, v_ref, bias_ref, o_ref, m_ref, acc_ref, *, tq, tk):
    i = pl.program_id(1)
    n_keys = (i + 1) * tq
    lo_q = lax.broadcasted_iota(jnp.int32, (tq, LANES), 1) < HEAD_DIM
    q_halves = []
    for hp in range(A_HEADS // 2):
        q_halves.extend(_split_lane_halves(q_ref[0, :, hp * LANES:(hp + 1) * LANES]))

    m_ref[...] = jnp.full(m_ref.shape, NEG_BIG, F32)
    acc_ref[...] = jnp.zeros(acc_ref.shape, F32)

    def kv_tile(off, width):
        bias = bias_ref[0, :, pl.ds(off, width)].astype(F32)
        ones = jnp.ones((width, LANES), BF16)
        for hp in range(A_HEADS // 2):
            kp = k_ref[0, pl.ds(off, width), hp * LANES:(hp + 1) * LANES]
            v_ext = jnp.concatenate(
                [v_ref[0, pl.ds(off, width), hp * LANES:(hp + 1) * LANES], ones], axis=1)
            for half in range(2):
                h = 2 * hp + half
                _flash_update(h, _dot_nt(q_halves[h], kp) + bias, v_ext, m_ref, acc_ref)

    def kv_step(t, carry):
        kv_tile(pl.multiple_of(t * tk, tk), tk)
        return carry

    n_whole = n_keys // tk
    lax.fori_loop(0, n_whole, kv_step, 0)
    if tk > tq:
        @pl.when(n_keys % tk != 0)
        def _():
            kv_tile(pl.multiple_of(n_whole * tk, tq), tq)

    for hp in range(A_HEADS // 2):
        a0 = acc_ref[2 * hp]
        a1 = acc_ref[2 * hp + 1]
        o = jnp.where(lo_q, a0[:, :LANES] / a0[:, LANES:], a1[:, :LANES] / a1[:, LANES:])
        o_ref[0, :, hp * LANES:(hp + 1) * LANES] = o.astype(o_ref.dtype)


def _dsa(aq, ak, av, bias, tq, tk):
    bsz, seq, w = aq.shape
    kern = functools.partial(_dsa_kernel, tq=tq, tk=tk)
    resident = lambda: pl.BlockSpec((1, seq, w), lambda b, i: (b, 0, 0),
                                    pipeline_mode=pl.Buffered(1))
    return pl.pallas_call(
        kern,
        grid=(bsz, seq // tq),
        in_specs=[pl.BlockSpec((1, tq, w), lambda b, i: (b, i, 0)),
                  resident(), resident(),
                  pl.BlockSpec((1, tq, seq), lambda b, i: (b, i, 0))],
        out_specs=pl.BlockSpec((1, tq, w), lambda b, i: (b, i, 0)),
        out_shape=jax.ShapeDtypeStruct((bsz, seq, w), BF16),
        scratch_shapes=[pltpu.VMEM((A_HEADS, tq, LANES), F32),
                        pltpu.VMEM((A_HEADS, tq, 2 * LANES), F32)],
        compiler_params=_cparams("parallel", "arbitrary"),
    )(aq, ak, av, bias)


def _diff_kernel(q_ref, k_ref, v_ref, lq1_ref, lk1_ref, lq2_ref, lk2_ref, g_ref, o_ref,
                 m_ref, acc_ref, *, tq, tk, lambda_init):
    i = pl.program_id(1)
    q_halves = []
    for hb in range(B_HEADS):
        q_halves.extend(_split_lane_halves(q_ref[0, :, hb * LANES:(hb + 1) * LANES]))
    lam = (jnp.exp(jnp.sum(lq1_ref[...] * lk1_ref[...], keepdims=True))
           - jnp.exp(jnp.sum(lq2_ref[...] * lk2_ref[...], keepdims=True)) + lambda_init)

    m_ref[...] = jnp.full(m_ref.shape, NEG_BIG, F32)
    acc_ref[...] = jnp.zeros(acc_ref.shape, F32)

    def tile(off, width, masked):
        ones = jnp.ones((width, LANES), BF16)
        if masked:
            row_chunk = (i * tq + lax.broadcasted_iota(jnp.int32, (tq, width), 0)) // CHUNK
            col_chunk = (off + lax.broadcasted_iota(jnp.int32, (tq, width), 1)) // CHUNK
            ok = col_chunk <= row_chunk
        for hb in range(B_HEADS):
            kp = k_ref[0, pl.ds(off, width), hb * LANES:(hb + 1) * LANES]
            v_ext = jnp.concatenate(
                [v_ref[0, pl.ds(off, width), hb * LANES:(hb + 1) * LANES], ones], axis=1)
            for mp in range(2):
                j = 2 * hb + mp
                s = _dot_nt(q_halves[j], kp)
                if masked:
                    s = jnp.where(ok, s, NEG_BIG)
                _flash_update(j, s, v_ext, m_ref, acc_ref)

    def kv_step(t, carry):
        tile(pl.multiple_of(t * tk, tk), tk, False)
        return carry

    n_whole = (i * tq) // tk
    lax.fori_loop(0, n_whole, kv_step, 0)
    if tk > tq:
        @pl.when((i * tq) % tk != 0)
        def _():
            tile(pl.multiple_of(n_whole * tk, tq), tq, False)
    tile(pl.multiple_of(i * tq, tq), tq, True)

    for hb in range(B_HEADS):
        a1 = acc_ref[2 * hb]
        a2 = acc_ref[2 * hb + 1]
        o = a1[:, :LANES] / a1[:, LANES:] - lam * (a2[:, :LANES] / a2[:, LANES:])
        o = o * lax.rsqrt(jnp.mean(o * o, axis=-1, keepdims=True) + RMS_EPS)
        o = o * g_ref[...] * (1.0 - lambda_init)
        o_ref[0, :, hb * LANES:(hb + 1) * LANES] = o.astype(o_ref.dtype)


def _diff(bq, bk, bv, lq1, lk1, lq2, lk2, subln_g, lambda_init, tq, tk):
    bsz, seq, w = bq.shape
    kern = functools.partial(_diff_kernel, tq=tq, tk=tk, lambda_init=lambda_init)
    resident = lambda: pl.BlockSpec((1, seq, w), lambda b, i: (b, 0, 0),
                                    pipeline_mode=pl.Buffered(1))
    small = lambda a: pl.BlockSpec(a.shape, lambda b, i: (0, 0))
    return pl.pallas_call(
        kern,
        grid=(bsz, seq // tq),
        in_specs=[pl.BlockSpec((1, tq, w), lambda b, i: (b, i, 0)),
                  resident(), resident(),
                  small(lq1), small(lk1), small(lq2), small(lk2), small(subln_g)],
        out_specs=pl.BlockSpec((1, tq, w), lambda b, i: (b, i, 0)),
        out_shape=jax.ShapeDtypeStruct((bsz, seq, w), BF16),
        scratch_shapes=[pltpu.VMEM((2 * B_HEADS, tq, LANES), F32),
                        pltpu.VMEM((2 * B_HEADS, tq, 2 * LANES), F32)],
        compiler_params=_cparams("parallel", "arbitrary"),
    )(bq, bk, bv, lq1, lk1, lq2, lk2, subln_g)


def _first_argmax_rows(v, row):
    m = jnp.max(v, axis=0, keepdims=True)
    idx = jnp.min(jnp.where(v == m, row, v.shape[0]), axis=0, keepdims=True)
    return m, idx


def _scatter_rows(dk, gk, r0, nrows):
    tm = dk.shape[1]
    pack = 32 // (8 * jnp.dtype(BF16).itemsize) * 8
    r_iota = lax.broadcasted_iota(jnp.int32, (SCATTER_SUB, tm), 0).astype(F32).astype(BF16)
    r_iota = r_iota.reshape(SCATTER_SUB // pack, pack, tm)
    pieces = []
    for b0 in range(r0, r0 + nrows, SCATTER_SUB):
        rel = dk - b0
        rel = jnp.where((rel >= 0) & (rel < SCATTER_SUB), rel, -1.0)
        out = jnp.zeros(r_iota.shape, BF16)
        for k in range(dk.shape[0]):
            at = jnp.broadcast_to(rel[k:k + 1, :], (pack, tm)).astype(BF16)[None]
            if gk is None:
                val = jnp.ones((), BF16)
            else:
                val = jnp.broadcast_to(gk[k:k + 1, :], (pack, tm)).astype(BF16)[None]
            out = jnp.where(r_iota == at, val, out)
        pieces.append(out.reshape(SCATTER_SUB, tm))
    return pieces[0] if len(pieces) == 1 else jnp.concatenate(pieces, axis=0)


def _oproj_kernel(oa_ref, ob_ref, h0_ref, woa_ref, wob_ref, g_ref, b_ref, wrt_ref, rb_ref,
                  h1_ref, h1b_ref, dk_ref, gk_ref, cnt_ref, *, alpha):
    mix = (jnp.dot(oa_ref[...], woa_ref[...], preferred_element_type=F32)
           + jnp.dot(ob_ref[...], wob_ref[...], preferred_element_type=F32))
    h1 = _layer_norm(alpha * h0_ref[...] + mix, g_ref[...], b_ref[...])
    h1_ref[...] = h1
    h1b = h1.astype(BF16)
    h1b_ref[...] = h1b

    logits = lax.dot_general(wrt_ref[...], h1, (((1,), (1,)), ((), ())),
                             precision=lax.Precision.HIGHEST, preferred_element_type=F32)
    scores = jax.nn.sigmoid(logits)
    biased = scores + rb_ref[...]
    tm = scores.shape[1]
    per_g = N_EXPERTS // N_GROUPS
    row8 = lax.broadcasted_iota(jnp.int32, (per_g, tm), 0)

    gs = []
    for g in range(N_GROUPS):
        blk = biased[g * per_g:(g + 1) * per_g, :]
        m1, i1 = _first_argmax_rows(blk, row8)
        m2 = jnp.max(jnp.where(row8 == i1, -jnp.inf, blk), axis=0, keepdims=True)
        gs.append(m1 + m2)
    gscore = jnp.concatenate(gs, axis=0)
    rowg = lax.broadcasted_iota(jnp.int32, (N_GROUPS, tm), 0)
    gsel = jnp.zeros((N_GROUPS, tm), F32)
    for _ in range(TOPK_GROUPS):
        _, ig = _first_argmax_rows(jnp.where(gsel > 0.0, -jnp.inf, gscore), rowg)
        gsel = jnp.where(rowg == ig, 1.0, gsel)

    rowe = lax.broadcasted_iota(jnp.int32, (N_EXPERTS, tm), 0)
    live = jnp.concatenate(
        [jnp.broadcast_to(gsel[g:g + 1, :], (per_g, tm)) for g in range(N_GROUPS)], axis=0)
    esel = jnp.zeros((N_EXPERTS, tm), F32)
    for _ in range(TOP_K):
        cand = jnp.where(live > 0.0, biased, -jnp.inf)
        m = jnp.max(cand, axis=0, keepdims=True)
        idx = jnp.min(jnp.where((live > 0.0) & (cand == m), rowe, N_EXPERTS),
                      axis=0, keepdims=True)
        hit = rowe == idx
        esel = jnp.where(hit, 1.0, esel)
        live = jnp.where(hit, 0.0, live)
    picked = jnp.where(esel > 0.0, scores, 0.0)
    denom = jnp.sum(picked, axis=0, keepdims=True)
    gate = picked / denom * ROUTED_SCALE

    sel_b = esel.astype(BF16)
    before_t = (lax.broadcasted_iota(jnp.int32, (tm, tm), 0)
                < lax.broadcasted_iota(jnp.int32, (tm, tm), 1)).astype(F32).astype(BF16)
    before_e = (lax.broadcasted_iota(jnp.int32, (N_EXPERTS, N_EXPERTS), 1)
                < lax.broadcasted_iota(jnp.int32, (N_EXPERTS, N_EXPERTS), 0)).astype(F32).astype(BF16)
    rank = jnp.dot(sel_b, before_t, preferred_element_type=F32)
    order = jnp.dot(before_e, sel_b, preferred_element_type=F32)
    cnt = jnp.sum(esel, axis=1, keepdims=True)
    cnt16 = jnp.floor((cnt + (ROW_ALIGN - 1)) / ROW_ALIGN) * ROW_ALIGN
    off = jnp.dot(before_e, jnp.broadcast_to(cnt16, (N_EXPERTS, tm)).astype(BF16),
                  preferred_element_type=F32)
    dest = off + rank
    dks, gks = [], []
    for k in range(TOP_K):
        kth = (esel > 0.0) & (order == k)
        dks.append(jnp.sum(jnp.where(kth, dest, 0.0), axis=0, keepdims=True))
        gks.append(jnp.sum(jnp.where(kth, gate, 0.0), axis=0, keepdims=True))
    dk = jnp.concatenate(dks, axis=0)
    dk_ref[...] = dk
    gk_ref[...] = jnp.concatenate(gks, axis=0)
    counts = _dot_nt(jnp.ones((8, tm), BF16), sel_b)
    cnt_ref[0] = jnp.concatenate([counts, jnp.zeros((8, LANES - N_EXPERTS), F32)], axis=1)


def _oproj(oa, ob, h0, woa, wob, g, b, wrt, rb, alpha, tm):
    n, d = h0.shape
    nt = n // tm
    row = lambda w: pl.BlockSpec((tm, w), lambda i: (i, 0))
    col = lambda: pl.BlockSpec((TOP_K, tm), lambda i: (0, i))
    full = lambda a: pl.BlockSpec(a.shape, lambda i: (0, 0))
    kt = jax.ShapeDtypeStruct((TOP_K, n), F32)
    return pl.pallas_call(
        functools.partial(_oproj_kernel, alpha=alpha),
        grid=(nt,),
        in_specs=[row(512), row(512), row(d), full(woa), full(wob), full(g), full(b),
                  full(wrt), full(rb)],
        out_specs=(row(d), row(d), col(), col(),
                   pl.BlockSpec((1, 8, LANES), lambda i: (i, 0, 0))),
        out_shape=(jax.ShapeDtypeStruct((n, d), F32),
                   jax.ShapeDtypeStruct((n, d), BF16),
                   kt,
                   kt,
                   jax.ShapeDtypeStruct((nt, 8, LANES), F32)),
        compiler_params=_cparams("parallel"),
    )(oa, ob, h0, woa, wob, g, b, wrt, rb)


def _swiglu(x, wg, wu, wd):
    hid = (jax.nn.silu(jnp.dot(x, wg, preferred_element_type=F32))
           * jnp.dot(x, wu, preferred_element_type=F32))
    return jnp.dot(hid.astype(BF16), wd, preferred_element_type=F32)


def _dispatch_kernel(tile_off_ref, exp_off_ref, len_ref, used_ref, gap_off_ref, gap_len_ref,
                     h1b_ref, dk_ref, xg_ref, xs_ref, zero_ref, run_sem, gap_sem, *, rb):
    i = pl.program_id(0)
    nt = pl.num_programs(0)
    rt = xs_ref.shape[1]
    slot = i % 2

    def wait_runs(step):
        n = pl.multiple_of(used_ref[step], ROW_ALIGN)
        s = step % 2
        pltpu.make_async_copy(xs_ref.at[s, pl.ds(0, n), :], xg_ref.at[pl.ds(0, n), :],
                              run_sem.at[s]).wait()

    @pl.when(i == 0)
    def _():
        zero_ref[...] = jnp.zeros(zero_ref.shape, zero_ref.dtype)

        def issue_gap(g, carry):
            n = pl.multiple_of(gap_len_ref[g], ROW_ALIGN)

            @pl.when(n > 0)
            def _():
                pltpu.make_async_copy(
                    zero_ref.at[pl.ds(0, n), :],
                    xg_ref.at[pl.ds(pl.multiple_of(gap_off_ref[g], ROW_ALIGN), n), :],
                    gap_sem).start()
            return carry

        lax.fori_loop(0, gap_len_ref.shape[0], issue_gap, 0)

    @pl.when(i >= 2)
    def _():
        wait_runs(i - 2)

    used = used_ref[i]
    dk = dk_ref[...]

    def sort_rows(r0):
        onehot = _scatter_rows(dk, None, r0, rb).astype(BF16)
        rows = jnp.dot(onehot, h1b_ref[...], preferred_element_type=F32)
        xs_ref[slot, r0:r0 + rb, :] = rows.astype(BF16)

    always = h1b_ref.shape[0] * TOP_K // rb * rb
    for r0 in range(0, always, rb):
        sort_rows(r0)
    for r0 in range(always, rt, rb):
        @pl.when(r0 < used)
        def _():
            sort_rows(r0)

    def issue_run(e, carry):
        r = i * N_EXPERTS + e
        n = pl.multiple_of(len_ref[r], ROW_ALIGN)

        @pl.when(n > 0)
        def _():
            pltpu.make_async_copy(
                xs_ref.at[slot, pl.ds(pl.multiple_of(tile_off_ref[r], ROW_ALIGN), n), :],
                xg_ref.at[pl.ds(pl.multiple_of(exp_off_ref[r], ROW_ALIGN), n), :],
                run_sem.at[slot]).start()
        return carry

    lax.fori_loop(0, N_EXPERTS, issue_run, 0)

    @pl.when(i == nt - 1)
    def _():
        @pl.when(i >= 1)
        def _():
            wait_runs(i - 1)
        wait_runs(i)

        def gap_rows(g, rows):
            return rows + gap_len_ref[g]
        n_gap = pl.multiple_of(lax.fori_loop(0, gap_len_ref.shape[0], gap_rows, 0), ROW_ALIGN)

        @pl.when(n_gap > 0)
        def _():
            pltpu.make_async_copy(xg_ref.at[pl.ds(0, n_gap), :], xg_ref.at[pl.ds(0, n_gap), :],
                                  gap_sem).wait()


def _dispatch(h1b, dk, tile_off, exp_off, lens, used, gap_off, gap_len, n_dst_rows, tm, rt):
    n, d = h1b.shape
    return pl.pallas_call(
        functools.partial(_dispatch_kernel, rb=512),
        grid_spec=pltpu.PrefetchScalarGridSpec(
            num_scalar_prefetch=6,
            grid=(n // tm,),
            in_specs=[pl.BlockSpec((tm, d), lambda i, *_: (i, 0)),
                      pl.BlockSpec((TOP_K, tm), lambda i, *_: (0, i))],
            out_specs=pl.BlockSpec(memory_space=pl.ANY),
            scratch_shapes=[pltpu.VMEM((2, rt, d), BF16), pltpu.VMEM((rt, d), BF16),
                            pltpu.SemaphoreType.DMA((2,)), pltpu.SemaphoreType.DMA(())]),
        out_shape=jax.ShapeDtypeStruct((n_dst_rows, d), BF16),
        compiler_params=_cparams("arbitrary"),
    )(tile_off, exp_off, lens, used, gap_off, gap_len, h1b, dk)


def _expert_kernel(blk_e_ref, blk_valid_ref, x_ref, wg_ref, wu_ref, wd_ref, y_ref):
    b = pl.program_id(0)
    valid = blk_valid_ref[b]

    @pl.when(valid > 0)
    def _():
        y_ref[...] = _swiglu(x_ref[...], wg_ref[...], wu_ref[...], wd_ref[...]).astype(BF16)

    @pl.when(valid <= 0)
    def _():
        y_ref[...] = jnp.zeros(y_ref.shape, BF16)


def _experts(xg, blk_e, blk_valid, wg, wu, wd):
    nr, d = xg.shape
    f = wg.shape[2]
    rows = pl.BlockSpec((MOE_BLOCK, d), lambda b, be, bv: (b, 0))
    return pl.pallas_call(
        _expert_kernel,
        grid_spec=pltpu.PrefetchScalarGridSpec(
            num_scalar_prefetch=2,
            grid=(nr // MOE_BLOCK,),
            in_specs=[rows,
                      pl.BlockSpec((None, d, f), lambda b, be, bv: (be[b], 0, 0)),
                      pl.BlockSpec((None, d, f), lambda b, be, bv: (be[b], 0, 0)),
                      pl.BlockSpec((None, f, d), lambda b, be, bv: (be[b], 0, 0))],
            out_specs=rows),
        out_shape=jax.ShapeDtypeStruct((nr, d), BF16),
        compiler_params=_cparams("parallel"),
    )(blk_e, blk_valid, xg, wg, wu, wd)


def _combine_kernel(tile_off_ref, exp_off_ref, len_ref, used_ref, yg_ref, dk_ref, gk_ref, h1_ref,
                    h1b_ref, p_ref, sg_ref, su_ref, sd_ref, g_ref, b_ref, wg_ref, bg_ref, wp_ref,
                    o_ref, ys_ref, ffn_ref, sem, *, alpha, rb):
    i = pl.program_id(0)
    nt = pl.num_programs(0)
    rt = ys_ref.shape[1]
    slot = i % 2

    def fetch_runs(tile):
        s = tile % 2

        def issue_run(e, carry):
            r = tile * N_EXPERTS + e
            n = pl.multiple_of(len_ref[r], ROW_ALIGN)

            @pl.when(n > 0)
            def _():
                pltpu.make_async_copy(
                    yg_ref.at[pl.ds(pl.multiple_of(exp_off_ref[r], ROW_ALIGN), n), :],
                    ys_ref.at[s, pl.ds(pl.multiple_of(tile_off_ref[r], ROW_ALIGN), n), :],
                    sem.at[s]).start()
            return carry

        lax.fori_loop(0, N_EXPERTS, issue_run, 0)

    @pl.when(i == 0)
    def _():
        ys_ref[...] = jnp.zeros(ys_ref.shape, ys_ref.dtype)
        fetch_runs(i)

    @pl.when(i + 1 < nt)
    def _():
        fetch_runs(i + 1)

    dk = dk_ref[...]
    gk = gk_ref[...]
    ffn = _swiglu(h1b_ref[...], sg_ref[...], su_ref[...], sd_ref[...])
    used = used_ref[i]
    n_used = pl.multiple_of(used, ROW_ALIGN)
    pltpu.make_async_copy(yg_ref.at[pl.ds(0, n_used), :], ys_ref.at[slot, pl.ds(0, n_used), :],
                          sem.at[slot]).wait()

    def routed(r0):
        weights = _scatter_rows(dk, gk, r0, rb).astype(BF16)
        return lax.dot_general(weights, ys_ref[slot, r0:r0 + rb, :], (((0,), (0,)), ((), ())),
                               preferred_element_type=F32)

    tm = h1_ref.shape[0]
    always = tm * TOP_K // rb * rb
    for r0 in range(0, always, rb):
        ffn = ffn + routed(r0)
    ffn_ref[...] = ffn
    for r0 in range(always, rt, rb):
        @pl.when(r0 < used)
        def _():
            ffn_ref[...] += routed(r0)
    h2 = _layer_norm(alpha * h1_ref[...] + ffn_ref[...], g_ref[...], b_ref[...])
    emb_gate = jax.nn.sigmoid(jnp.dot(h2.astype(BF16), wg_ref[...], preferred_element_type=F32)
                              + bg_ref[...])
    emb = jnp.dot(p_ref[...].astype(BF16), wp_ref[...], preferred_element_type=F32)
    o_ref[...] = h2 + emb_gate * emb


def _combine(tile_off, exp_off, lens, used, yg, dk, gk, h1, h1b, p2, sg, su, sd, g, b, wg, bg, wp,
             alpha, tm, rt):
    n, d = h1.shape
    row = lambda w: pl.BlockSpec((tm, w), lambda i, *_: (i, 0))
    col = lambda: pl.BlockSpec((TOP_K, tm), lambda i, *_: (0, i))
    full = lambda a: pl.BlockSpec(a.shape, lambda i, *_: (0, 0))
    return pl.pallas_call(
        functools.partial(_combine_kernel, alpha=alpha, rb=512),
        grid_spec=pltpu.PrefetchScalarGridSpec(
            num_scalar_prefetch=4,
            grid=(n // tm,),
            in_specs=[pl.BlockSpec(memory_space=pl.ANY), col(), col(), row(d),
                      row(d), row(p2.shape[1]), full(sg), full(su), full(sd), full(g), full(b),
                      full(wg), full(bg), full(wp)],
            out_specs=row(d),
            scratch_shapes=[pltpu.VMEM((2, rt, d), yg.dtype), pltpu.VMEM((tm, d), F32),
                            pltpu.SemaphoreType.DMA((2,))]),
        out_shape=jax.ShapeDtypeStruct((n, d), F32),
        compiler_params=_cparams("arbitrary"),
    )(tile_off, exp_off, lens, used, yg, dk, gk, h1, h1b, p2, sg, su, sd, g, b, wg, bg, wp)


def _tiles(seq):
    tm = min(256, seq)
    tq_sel = min(512, seq)
    tk_sel = min(512, seq)
    tq_att = min(512, seq)
    tk_att = min(512, seq)
    rt = -(-(tm * TOP_K + N_EXPERTS * (ROW_ALIGN - 1)) // 512) * 512
    return tm, tq_sel, tk_sel, tq_att, tk_att, rt


def kernel(x, p, positions, ln_emb_g, ln_emb_b, w_in, w_o, diff_lq1, diff_lk1, diff_lq2, diff_lk2,
           diff_subln_g, ln1_g, ln1_b, w_router, router_bias, w_exp_gate, w_exp_up, w_exp_down,
           w_sh_gate, w_sh_up, w_sh_down, ln2_g, ln2_b, w_ple_gate, b_ple_gate, w_ple_proj):
    bsz, seq, d = x.shape
    depth = w_in.shape[0]
    n = bsz * seq
    n_keep = min(IDX_TOPK_MAX, seq // 4)
    alpha = (2.0 * depth) ** 0.25
    tm, tq_sel, tk_sel, tq_att, tk_att, rt = _tiles(seq)
    n_blocks = -(-((n // tm) * (tm * TOP_K + N_EXPERTS * (ROW_ALIGN - 1))
                   + N_EXPERTS * (MOE_BLOCK - 1)) // MOE_BLOCK)
    row1 = lambda v: v.reshape(1, -1)

    inv = ROPE_THETA ** (-jnp.arange(0, ROPE_DIM, 2, dtype=F32) / ROPE_DIM)
    inv_lanes = jnp.tile(inv, LANES // (ROPE_DIM // 2)).reshape(1, LANES)
    pos2 = positions.reshape(n, 1)

    h = x.reshape(n, d)
    out = None
    for li in range(depth):
        lambda_init = 0.8 - 0.6 * math.exp(-0.3 * li)
        w = w_in[li]
        aqw, akw, avw = w[:, 0:512], w[:, 512:1024], w[:, 1024:1536]
        iqw, ikw, iww = w[:, 1536:2048], w[:, 2048:2112], w[:, 2112:2120]
        bqw, bkw, bvw = w[:, 2120:2632], w[:, 2632:3144], w[:, 3144:3656]
        w_rope = jnp.concatenate([aqw, akw, iqw, ikw, ikw, bqw, bkw], axis=1).astype(BF16)
        w_plain = jnp.concatenate([avw, bvw], axis=1).astype(BF16)
        w_idx_t = jnp.concatenate(
            [iww.T, jnp.zeros((LANES - IDX_HEADS, d), F32)], axis=0).astype(BF16)

        (h0, aq, ak, iq, ik2, bq, bk, av, bv, iw) = _proj(
            h, pos2, inv_lanes, row1(ln_emb_g), row1(ln_emb_b), w_rope, w_plain, w_idx_t, tm)
        assert depth == 1

        r3 = lambda a: a.reshape(bsz, seq, a.shape[-1])
        bias = _select(r3(iq), iw, r3(ik2), n_keep, tq_sel, tk_sel)
        out_a = _dsa(r3(aq), r3(ak), r3(av), bias, tq_att, tk_att)
        out_b = _diff(r3(bq), r3(bk), r3(bv), row1(diff_lq1[li]), row1(diff_lk1[li]),
                      row1(diff_lq2[li]), row1(diff_lk2[li]), row1(diff_subln_g[li]),
                      lambda_init, tq_att, tk_att)

        wo = w_o[li].astype(BF16)
        h1, h1b, dk, gk, cnt_out = _oproj(
            out_a.reshape(n, -1), out_b.reshape(n, -1), h0, wo[0:512], wo[512:1024],
            row1(ln1_g[li]), row1(ln1_b[li]), w_router[li].T, router_bias[li].reshape(-1, 1),
            alpha, tm)

        nt = n // tm
        c16 = jnp.ceil(cnt_out[:, 0, :N_EXPERTS] / ROW_ALIGN) * ROW_ALIGN
        tri = lambda m: jnp.tril(jnp.ones((m, m), F32))
        hi = lax.Precision.HIGHEST
        tile_off = jnp.dot(c16, tri(N_EXPERTS).T, precision=hi) - c16
        cum_tiles = jnp.dot(tri(nt), c16, precision=hi)
        tot = cum_tiles[-1]
        totp = jnp.ceil(tot / MOE_BLOCK) * MOE_BLOCK
        pend = jnp.dot(totp, tri(N_EXPERTS).T, precision=hi)
        ebase = pend - totp
        exp_off = ebase[None, :] + cum_tiles - c16
        blk0 = jnp.arange(n_blocks, dtype=F32) * MOE_BLOCK
        blk_e = jnp.minimum(jnp.sum(pend[None, :] <= blk0[:, None], axis=1), N_EXPERTS - 1)
        onehot_e = (blk_e[:, None] == jnp.arange(N_EXPERTS)[None, :]).astype(F32)
        blk_end = jnp.dot(onehot_e, ebase + tot, precision=hi)
        blk_valid = jnp.clip(blk_end - blk0, 0, MOE_BLOCK)
        flat = lambda a: a.reshape(-1).astype(jnp.int32)
        n_rows = n_blocks * MOE_BLOCK
        tail0 = pend[-1] + jnp.arange(-(-n_rows // rt), dtype=F32) * rt
        gap_off = jnp.concatenate([ebase + tot, tail0])
        gap_len = jnp.concatenate([totp - tot, jnp.clip(n_rows - tail0, 0, rt)])

        weg, weu, wed = (w_exp_gate[li].astype(BF16), w_exp_up[li].astype(BF16),
                         w_exp_down[li].astype(BF16))
        used = jnp.sum(c16, axis=1)
        xg = _dispatch(h1b, dk, flat(tile_off), flat(exp_off), flat(c16), flat(used),
                       flat(gap_off), flat(gap_len), n_rows, tm, rt)
        yg = _experts(xg, flat(blk_e), flat(blk_valid), weg, weu, wed)
        out = _combine(flat(tile_off), flat(exp_off), flat(c16), flat(used), yg, dk, gk, h1, h1b,
                       p[li].reshape(n, -1), w_sh_gate[li].astype(BF16),
                       w_sh_up[li].astype(BF16), w_sh_down[li].astype(BF16), row1(ln2_g[li]),
                       row1(ln2_b[li]), w_ple_gate[li].astype(BF16), row1(b_ple_gate[li]),
                       w_ple_proj[li].astype(BF16), alpha, tm, rt)
        h = out
    return out.reshape(bsz, seq, d)
```

```python
import functools
import math

import jax
import jax.numpy as jnp
from jax import lax
from jax.experimental import pallas as pl
from jax.experimental.pallas import tpu as pltpu

CHUNK = 64
HEAD_DIM = 64
ROPE_DIM = 64
ROPE_THETA = 10000.0
A_HEADS = 8
IDX_HEADS = 8
IDX_DIM = 64
IDX_TOPK_MAX = 256
B_HEADS = 4
B_QK_DIM = 64
B_V_DIM = 128
N_EXPERTS = 64
TOP_K = 8
N_GROUPS = 8
TOPK_GROUPS = 4
ROUTED_SCALE = 2.5
LN_EPS = 1e-5
RMS_EPS = 1e-5

LANES = 128
ROW_ALIGN = 16
SCATTER_SUB = 256
MOE_BLOCK = 1024
VMEM_LIMIT = 56 * 1024 * 1024
NEG_BIG = -1e30
INT_MIN = -(2 ** 31)
LOG2E = math.log2(math.e)
GROUPS = 256

F32 = jnp.float32
BF16 = jnp.bfloat16


def _cparams(*sem):
    return pltpu.CompilerParams(dimension_semantics=sem, vmem_limit_bytes=VMEM_LIMIT)


def _layer_norm(x, g, b):
    mu = jnp.mean(x, axis=-1, keepdims=True)
    xc = x - mu
    var = jnp.mean(xc * xc, axis=-1, keepdims=True)
    return xc * lax.rsqrt(var + LN_EPS) * g + b


def _dot_nt(a, b):
    return lax.dot_general(a, b, (((1,), (1,)), ((), ())), preferred_element_type=F32)


def _split_lane_halves(pair):
    lo = lax.broadcasted_iota(jnp.int32, pair.shape, 1) < (LANES // 2)
    pf = pair.astype(F32)
    return (jnp.where(lo, pf, 0.0).astype(pair.dtype), jnp.where(lo, 0.0, pf).astype(pair.dtype))


def _proj_kernel(x_ref, pos_ref, inv_ref, g_ref, b_ref, wr_ref, wp_ref, wi_ref,
                 h0_ref, aq_ref, ak_ref, iq_ref, ik_ref, bq_ref, bk_ref,
                 av_ref, bv_ref, iw_ref):
    hn = _layer_norm(x_ref[...], g_ref[...], b_ref[...])
    h0_ref[...] = hn
    hb = hn.astype(BF16)

    tm = x_ref.shape[0]
    nq = LANES // (ROPE_DIM // 2)
    tr = tm // nq
    pos = pos_ref[...].astype(F32)
    lane = lax.broadcasted_iota(jnp.int32, (tr, LANES), 1)
    blk = lane // (ROPE_DIM // 2)
    pos_packed = jnp.broadcast_to(pos[0:tr], (tr, LANES))
    for q in range(1, nq):
        pos_packed = jnp.where(blk == q, pos[q * tr:(q + 1) * tr], pos_packed)
    ang = pos_packed * inv_ref[...]
    cos_packed = jnp.cos(ang)
    sin_packed = jnp.sin(ang)
    first_half = (lane % ROPE_DIM) < (ROPE_DIM // 2)

    def spread(table, q):
        own = jnp.where(blk == q, table, 0.0)
        t = own
        for s in range(1, nq):
            t = t + pltpu.roll(own, s * (ROPE_DIM // 2), axis=1)
        return t

    pr = jnp.dot(hb, wr_ref[...], preferred_element_type=F32)
    outs = ((aq_ref, 0, 4, HEAD_DIM ** -0.5 * LOG2E), (ak_ref, 4, 4, 1.0), (iq_ref, 8, 4, 1.0),
            (ik_ref, 12, 1, 1.0), (bq_ref, 13, 4, B_QK_DIM ** -0.5 * LOG2E), (bk_ref, 17, 4, 1.0))
    for q in range(nq):
        cos = spread(cos_packed, q)
        sin = spread(sin_packed, q)
        sin_signed = jnp.where(first_half, -sin, sin)
        rows = slice(q * tr, (q + 1) * tr)
        for ref, g0, ng, scale in outs:
            for j in range(ng):
                v = pr[rows, (g0 + j) * LANES:(g0 + j + 1) * LANES]
                partner = jnp.where(first_half,
                                    pltpu.roll(v, LANES - ROPE_DIM // 2, axis=1),
                                    pltpu.roll(v, ROPE_DIM // 2, axis=1))
                v = v * cos + partner * sin_signed
                if scale != 1.0:
                    v = v * scale
                ref[rows, j * LANES:(j + 1) * LANES] = v.astype(ref.dtype)

    pp = jnp.dot(hb, wp_ref[...], preferred_element_type=F32)
    av_ref[...] = pp[:, 0:512].astype(BF16)
    bv_ref[...] = pp[:, 512:1024].astype(BF16)
    iw_ref[...] = _dot_nt(wi_ref[...], hb)[0:IDX_HEADS, :]


def _proj(x2, pos2, inv, g, b, w_rope, w_plain, w_idx_t, tm):
    n, d = x2.shape
    nr = w_rope.shape[1]
    npl = w_plain.shape[1]
    row = lambda w: pl.BlockSpec((tm, w), lambda i: (i, 0))
    full = lambda a: pl.BlockSpec(a.shape, lambda i: (0, 0))
    out_shape = (
        jax.ShapeDtypeStruct((n, d), F32),
        jax.ShapeDtypeStruct((n, 512), BF16),
        jax.ShapeDtypeStruct((n, 512), BF16),
        jax.ShapeDtypeStruct((n, 512), BF16),
        jax.ShapeDtypeStruct((n, 128), BF16),
        jax.ShapeDtypeStruct((n, 512), BF16),
        jax.ShapeDtypeStruct((n, 512), BF16),
        jax.ShapeDtypeStruct((n, 512), BF16),
        jax.ShapeDtypeStruct((n, 512), BF16),
        jax.ShapeDtypeStruct((IDX_HEADS, n), F32),
    )
    return pl.pallas_call(
        _proj_kernel,
        grid=(n // tm,),
        in_specs=[row(d), row(1), full(inv), full(g), full(b), full(w_rope), full(w_plain),
                  full(w_idx_t)],
        out_specs=(row(d), row(512), row(512), row(512), row(128), row(512), row(512),
                   row(512), row(512), pl.BlockSpec((IDX_HEADS, tm), lambda i: (0, i))),
        out_shape=out_shape,
        compiler_params=_cparams("parallel"),
    )(x2, pos2, inv, g, b, w_rope, w_plain, w_idx_t)


def _sortable(s):
    i = lax.bitcast_convert_type(s, jnp.int32)
    return i ^ ((i >> 31) & 0x7FFFFFFF)


def _select_kernel(iq_ref, iw_ref, ik_ref, bias_ref, keys_ref, gmax_ref, *, tq, tk, n_keep,
                   w_scale):
    i = pl.program_id(1)
    key_drop = 32 - 8 * jnp.dtype(BF16).itemsize
    seq = ik_ref.shape[1]
    n_t = ((i + 1) * tq + tk - 1) // tk

    q_chunk = (i * tq + lax.broadcasted_iota(jnp.int32, (tk, tq), 1)) // CHUNK
    k_row = lax.broadcasted_iota(jnp.int32, (tk, tq), 0)
    wb = iw_ref[...] * w_scale
    q_halves = []
    for hp in range(IDX_HEADS // 2):
        q_halves.extend(_split_lane_halves(iq_ref[0, :, hp * LANES:(hp + 1) * LANES]))

    def score_tile(t, carry, boundary):
        off = pl.multiple_of(t * tk, tk)
        kt = ik_ref[0, pl.ds(off, tk), :]
        sc = jnp.zeros((tk, tq), F32)
        for h in range(IDX_HEADS):
            rel = jnp.maximum(_dot_nt(kt, q_halves[h]), 0.0)
            sc = sc + wb[h:h + 1, :] * rel
        key = _sortable(sc.astype(BF16).astype(F32) + 0.0) >> key_drop
        if boundary:
            key = jnp.where(((off + k_row) // CHUNK) <= q_chunk, key, INT_MIN)
        keys_ref[pl.ds(off, tk), :] = key
        gm = key[0:GROUPS, :]
        for g0 in range(GROUPS, tk, GROUPS):
            gm = jnp.maximum(gm, key[g0:g0 + GROUPS, :])
        gmax_ref[...] = jnp.maximum(gmax_ref[...], gm)
        return carry

    gmax_ref[...] = jnp.full(gmax_ref.shape, INT_MIN, jnp.int32)
    n_full = (i * tq + CHUNK) // tk
    lax.fori_loop(0, n_full, functools.partial(score_tile, boundary=False), 0)
    lax.fori_loop(n_full, n_t, functools.partial(score_tile, boundary=True), 0)

    def count_ge(cand):
        def body(t, acc):
            off = pl.multiple_of(t * tk, tk)
            m = (keys_ref[pl.ds(off, tk), :] >= cand).astype(jnp.int32)
            return acc + jnp.sum(m.reshape(tk // 8, 8, tq), axis=0)
        acc = lax.fori_loop(0, n_t, body, jnp.zeros((8, tq), jnp.int32))
        return jnp.sum(acc, axis=0, keepdims=True)

    gmax = gmax_ref[...]
    n_adm = ((i * tq + lax.broadcasted_iota(jnp.int32, (1, tq), 1)) // CHUNK + 1) * CHUNK
    few = n_adm <= n_keep
    lo0 = jnp.where(few, INT_MIN + 1, jnp.min(gmax, axis=0, keepdims=True))
    hi0 = jnp.where(few, INT_MIN + 2, jnp.max(gmax, axis=0, keepdims=True) + 1)

    def probe(state):
        lo, hi, n_hi, thr, n_thr, done, _, it = state
        mid = (lo >> 1) + (hi >> 1) + (lo & hi & 1)
        closed = mid == lo
        cnt = count_ge(mid)
        exact = cnt == n_keep
        up = cnt >= n_keep
        live = done == 0
        settle = live & (exact | closed)
        thr = jnp.where(settle, mid, thr)
        n_thr = jnp.where(settle, cnt, n_thr)
        lo = jnp.where(live & up, mid, lo)
        down = live & jnp.logical_not(up)
        hi = jnp.where(down, mid, hi)
        n_hi = jnp.where(down, cnt, n_hi)
        done = jnp.where(exact | closed, 1, done)
        return lo, hi, n_hi, thr, n_thr, done, jnp.min(done), it + 1

    def searching(state):
        return (state[6] == 0) & (state[7] < 34)

    done0 = few.astype(jnp.int32)
    none = jnp.zeros((1, tq), jnp.int32)
    state = lax.while_loop(
        searching, probe, (lo0, hi0, none, lo0, none, done0, jnp.min(done0), jnp.int32(0)))
    n_above, thr, n_ge = state[2], state[3], state[4]

    tied = n_ge > n_keep
    any_tied = jnp.max(tied.astype(jnp.int32)) > 0
    tau = thr

    def write_bias(off, keep_t):
        bias_t = jnp.where(keep_t, 0.0, NEG_BIG)
        bias_ref[0, :, pl.ds(off, tk)] = bias_t.T.astype(BF16)

    @pl.when(jnp.logical_not(any_tied))
    def _():
        def write_tile(t, carry):
            off = pl.multiple_of(t * tk, tk)
            write_bias(off, keys_ref[pl.ds(off, tk), :] >= tau)
            return carry
        lax.fori_loop(0, n_t, write_tile, 0)

    @pl.when(any_tied)
    def _():
        room = jnp.where(tied, (n_keep - n_above).astype(F32), float(seq))
        upto = (lax.broadcasted_iota(jnp.int32, (tk, tk), 1)
                <= lax.broadcasted_iota(jnp.int32, (tk, tk), 0)).astype(F32).astype(BF16)

        def write_tile(t, seen_eq):
            off = pl.multiple_of(t * tk, tk)
            key = keys_ref[pl.ds(off, tk), :]
            eq_f = jnp.where(key == tau, 1.0, 0.0)
            eq_rank = seen_eq + jnp.dot(upto, eq_f.astype(BF16), preferred_element_type=F32)
            order = jnp.where(key == tau, eq_rank, jnp.where(key > tau, 0.0, 2.0 * seq))
            write_bias(off, order <= room)
            return seen_eq + jnp.sum(eq_f, axis=0, keepdims=True)

        lax.fori_loop(0, n_t, write_tile, jnp.zeros((1, tq), F32))

    def blank_tile(t, carry):
        off = pl.multiple_of(t * tk, tk)
        bias_ref[0, :, pl.ds(off, tk)] = jnp.full((tq, tk), NEG_BIG, BF16)
        return carry

    lax.fori_loop(n_t, seq // tk, blank_tile, 0)


def _select(iq, iw, ik2, n_keep, tq, tk):
    bsz, seq, _ = iq.shape
    kern = functools.partial(_select_kernel, tq=tq, tk=tk, n_keep=n_keep,
                             w_scale=(IDX_HEADS ** -0.5) * (IDX_DIM ** -0.5))
    return pl.pallas_call(
        kern,
        grid=(bsz, seq // tq),
        in_specs=[pl.BlockSpec((1, tq, 512), lambda b, i: (b, i, 0)),
                  pl.BlockSpec((IDX_HEADS, tq), lambda b, i: (0, b * (seq // tq) + i)),
                  pl.BlockSpec((1, seq, 128), lambda b, i: (b, 0, 0))],
        out_specs=pl.BlockSpec((1, tq, seq), lambda b, i: (b, i, 0)),
        out_shape=jax.ShapeDtypeStruct((bsz, seq, seq), BF16),
        scratch_shapes=[pltpu.VMEM((seq, tq), jnp.int32), pltpu.VMEM((GROUPS, tq), jnp.int32)],
        compiler_params=_cparams("parallel", "parallel"),
    )(iq, iw, ik2)


def _flash_update(j, s, v_ext, m_ref, acc_ref):
    tk = s.shape[1]
    m_old = m_ref[j]
    m_new = jnp.maximum(m_old, jnp.max(s, axis=1, keepdims=True))
    alpha = jnp.exp2(m_old - m_new)
    p = jnp.exp2(s - jnp.concatenate([m_new] * (tk // LANES), axis=1))
    m_ref[j] = m_new
    acc_ref[j] = (acc_ref[j] * jnp.concatenate([alpha, alpha], axis=1)
                  + jnp.dot(p.astype(BF16), v_ext, preferred_element_type=F32))


def _dsa_kernel(q_ref, k_ref, v_ref, bias_ref, o_ref, m_ref, acc_ref, *, tq, tk):
    i = pl.program_id(1)
    n_keys = (i + 1) * tq
    lo_q = lax.broadcasted_iota(jnp.int32, (tq, LANES), 1) < HEAD_DIM
    q_halves = []
    for hp in range(A_HEADS // 2):
        q_halves.extend(_split_lane_halves(q_ref[0, :, hp * LANES:(hp + 1) * LANES]))

    m_ref[...] = jnp.full(m_ref.shape, NEG_BIG, F32)
    acc_ref[...] = jnp.zeros(acc_ref.shape, F32)

    def kv_tile(off, width):
        bias = bias_ref[0, :, pl.ds(off, width)].astype(F32)
        ones = jnp.ones((width, LANES), BF16)
        for hp in range(A_HEADS // 2):
            kp = k_ref[0, pl.ds(off, width), hp * LANES:(hp + 1) * LANES]
            v_ext = jnp.concatenate(
                [v_ref[0, pl.ds(off, width), hp * LANES:(hp + 1) * LANES], ones], axis=1)
            for half in range(2):
                h = 2 * hp + half
                _flash_update(h, _dot_nt(q_halves[h], kp) + bias, v_ext, m_ref, acc_ref)

    def kv_step(t, carry):
        kv_tile(pl.multiple_of(t * tk, tk), tk)
        return carry

    n_whole = n_keys // tk
    lax.fori_loop(0, n_whole, kv_step, 0)
    if tk > tq:
        @pl.when(n_keys % tk != 0)
        def _():
            kv_tile(pl.multiple_of(n_whole * tk, tq), tq)

    for hp in range(A_HEADS // 2):
        a0 = acc_ref[2 * hp]
        a1 = acc_ref[2 * hp + 1]
        o = jnp.where(lo_q, a0[:, :LANES] / a0[:, LANES:], a1[:, :LANES] / a1[:, LANES:])
        o_ref[0, :, hp * LANES:(hp + 1) * LANES] = o.astype(o_ref.dtype)


def _dsa(aq, ak, av, bias, tq, tk):
    bsz, seq, w = aq.shape
    kern = functools.partial(_dsa_kernel, tq=tq, tk=tk)
    resident = lambda: pl.BlockSpec((1, seq, w), lambda b, i: (b, 0, 0),
                                    pipeline_mode=pl.Buffered(1))
    return pl.pallas_call(
        kern,
        grid=(bsz, seq // tq),
        in_specs=[pl.BlockSpec((1, tq, w), lambda b, i: (b, i, 0)),
                  resident(), resident(),
                  pl.BlockSpec((1, tq, seq), lambda b, i: (b, i, 0))],
        out_specs=pl.BlockSpec((1, tq, w), lambda b, i: (b, i, 0)),
        out_shape=jax.ShapeDtypeStruct((bsz, seq, w), BF16),
        scratch_shapes=[pltpu.VMEM((A_HEADS, tq, LANES), F32),
                        pltpu.VMEM((A_HEADS, tq, 2 * LANES), F32)],
        compiler_params=_cparams("parallel", "arbitrary"),
    )(aq, ak, av, bias)


def _diff_kernel(q_ref, k_ref, v_ref, lq1_ref, lk1_ref, lq2_ref, lk2_ref, g_ref, o_ref,
                 m_ref, acc_ref, *, tq, tk, lambda_init):
    i = pl.program_id(1)
    q_halves = []
    for hb in range(B_HEADS):
        q_halves.extend(_split_lane_halves(q_ref[0, :, hb * LANES:(hb + 1) * LANES]))
    lam = (jnp.exp(jnp.sum(lq1_ref[...] * lk1_ref[...], keepdims=True))
           - jnp.exp(jnp.sum(lq2_ref[...] * lk2_ref[...], keepdims=True)) + lambda_init)

    m_ref[...] = jnp.full(m_ref.shape, NEG_BIG, F32)
    acc_ref[...] = jnp.zeros(acc_ref.shape, F32)

    def tile(off, width, masked):
        ones = jnp.ones((width, LANES), BF16)
        if masked:
            row_chunk = (i * tq + lax.broadcasted_iota(jnp.int32, (tq, width), 0)) // CHUNK
            col_chunk = (off + lax.broadcasted_iota(jnp.int32, (tq, width), 1)) // CHUNK
            ok = col_chunk <= row_chunk
        for hb in range(B_HEADS):
            kp = k_ref[0, pl.ds(off, width), hb * LANES:(hb + 1) * LANES]
            v_ext = jnp.concatenate(
                [v_ref[0, pl.ds(off, width), hb * LANES:(hb + 1) * LANES], ones], axis=1)
            for mp in range(2):
                j = 2 * hb + mp
                s = _dot_nt(q_halves[j], kp)
                if masked:
                    s = jnp.where(ok, s, NEG_BIG)
                _flash_update(j, s, v_ext, m_ref, acc_ref)

    def kv_step(t, carry):
        tile(pl.multiple_of(t * tk, tk), tk, False)
        return carry

    n_whole = (i * tq) // tk
    lax.fori_loop(0, n_whole, kv_step, 0)
    if tk > tq:
        @pl.when((i * tq) % tk != 0)
        def _():
            tile(pl.multiple_of(n_whole * tk, tq), tq, False)
    tile(pl.multiple_of(i * tq, tq), tq, True)

    for hb in range(B_HEADS):
        a1 = acc_ref[2 * hb]
        a2 = acc_ref[2 * hb + 1]
        o = a1[:, :LANES] / a1[:, LANES:] - lam * (a2[:, :LANES] / a2[:, LANES:])
        o = o * lax.rsqrt(jnp.mean(o * o, axis=-1, keepdims=True) + RMS_EPS)
        o = o * g_ref[...] * (1.0 - lambda_init)
        o_ref[0, :, hb * LANES:(hb + 1) * LANES] = o.astype(o_ref.dtype)


def _diff(bq, bk, bv, lq1, lk1, lq2, lk2, subln_g, lambda_init, tq, tk):
    bsz, seq, w = bq.shape
    kern = functools.partial(_diff_kernel, tq=tq, tk=tk, lambda_init=lambda_init)
    resident = lambda: pl.BlockSpec((1, seq, w), lambda b, i: (b, 0, 0),
                                    pipeline_mode=pl.Buffered(1))
    small = lambda a: pl.BlockSpec(a.shape, lambda b, i: (0, 0))
    return pl.pallas_call(
        kern,
        grid=(bsz, seq // tq),
        in_specs=[pl.BlockSpec((1, tq, w), lambda b, i: (b, i, 0)),
                  resident(), resident(),
                  small(lq1), small(lk1), small(lq2), small(lk2), small(subln_g)],
        out_specs=pl.BlockSpec((1, tq, w), lambda b, i: (b, i, 0)),
        out_shape=jax.ShapeDtypeStruct((bsz, seq, w), BF16),
        scratch_shapes=[pltpu.VMEM((2 * B_HEADS, tq, LANES), F32),
                        pltpu.VMEM((2 * B_HEADS, tq, 2 * LANES), F32)],
        compiler_params=_cparams("parallel", "arbitrary"),
    )(bq, bk, bv, lq1, lk1, lq2, lk2, subln_g)


def _first_argmax_rows(v, row):
    m = jnp.max(v, axis=0, keepdims=True)
    idx = jnp.min(jnp.where(v == m, row, v.shape[0]), axis=0, keepdims=True)
    return m, idx


def _scatter_rows(dk, gk, r0, nrows):
    tm = dk.shape[1]
    pack = 32 // (8 * jnp.dtype(BF16).itemsize) * 8
    r_iota = lax.broadcasted_iota(jnp.int32, (SCATTER_SUB, tm), 0).astype(F32).astype(BF16)
    r_iota = r_iota.reshape(SCATTER_SUB // pack, pack, tm)
    pieces = []
    for b0 in range(r0, r0 + nrows, SCATTER_SUB):
        rel = dk - b0
        rel = jnp.where((rel >= 0) & (rel < SCATTER_SUB), rel, -1.0)
        out = jnp.zeros(r_iota.shape, BF16)
        for k in range(dk.shape[0]):
            at = jnp.broadcast_to(rel[k:k + 1, :], (pack, tm)).astype(BF16)[None]
            if gk is None:
                val = jnp.ones((), BF16)
            else:
                val = jnp.broadcast_to(gk[k:k + 1, :], (pack, tm)).astype(BF16)[None]
            out = jnp.where(r_iota == at, val, out)
        pieces.append(out.reshape(SCATTER_SUB, tm))
    return pieces[0] if len(pieces) == 1 else jnp.concatenate(pieces, axis=0)


def _oproj_kernel(oa_ref, ob_ref, h0_ref, woa_ref, wob_ref, g_ref, b_ref, wrt_ref, rb_ref,
                  h1_ref, h1b_ref, dk_ref, gk_ref, cnt_ref, *, alpha):
    mix = (jnp.dot(oa_ref[...], woa_ref[...], preferred_element_type=F32)
           + jnp.dot(ob_ref[...], wob_ref[...], preferred_element_type=F32))
    h1 = _layer_norm(alpha * h0_ref[...] + mix, g_ref[...], b_ref[...])
    h1_ref[...] = h1
    h1b = h1.astype(BF16)
    h1b_ref[...] = h1b

    logits = lax.dot_general(wrt_ref[...], h1, (((1,), (1,)), ((), ())),
                             precision=lax.Precision.HIGHEST, preferred_element_type=F32)
    scores = jax.nn.sigmoid(logits)
    biased = scores + rb_ref[...]
    tm = scores.shape[1]
    per_g = N_EXPERTS // N_GROUPS
    row8 = lax.broadcasted_iota(jnp.int32, (per_g, tm), 0)

    gs = []
    for g in range(N_GROUPS):
        blk = biased[g * per_g:(g + 1) * per_g, :]
        m1, i1 = _first_argmax_rows(blk, row8)
        m2 = jnp.max(jnp.where(row8 == i1, -jnp.inf, blk), axis=0, keepdims=True)
        gs.append(m1 + m2)
    gscore = jnp.concatenate(gs, axis=0)
    rowg = lax.broadcasted_iota(jnp.int32, (N_GROUPS, tm), 0)
    gsel = jnp.zeros((N_GROUPS, tm), F32)
    for _ in range(TOPK_GROUPS):
        _, ig = _first_argmax_rows(jnp.where(gsel > 0.0, -jnp.inf, gscore), rowg)
        gsel = jnp.where(rowg == ig, 1.0, gsel)

    rowe = lax.broadcasted_iota(jnp.int32, (N_EXPERTS, tm), 0)
    live = jnp.concatenate(
        [jnp.broadcast_to(gsel[g:g + 1, :], (per_g, tm)) for g in range(N_GROUPS)], axis=0)
    esel = jnp.zeros((N_EXPERTS, tm), F32)
    for _ in range(TOP_K):
        cand = jnp.where(live > 0.0, biased, -jnp.inf)
        m = jnp.max(cand, axis=0, keepdims=True)
        idx = jnp.min(jnp.where((live > 0.0) & (cand == m), rowe, N_EXPERTS),
                      axis=0, keepdims=True)
        hit = rowe == idx
        esel = jnp.where(hit, 1.0, esel)
        live = jnp.where(hit, 0.0, live)
    picked = jnp.where(esel > 0.0, scores, 0.0)
    denom = jnp.sum(picked, axis=0, keepdims=True)
    gate = picked / denom * ROUTED_SCALE

    sel_b = esel.astype(BF16)
    before_t = (lax.broadcasted_iota(jnp.int32, (tm, tm), 0)
                < lax.broadcasted_iota(jnp.int32, (tm, tm), 1)).astype(F32).astype(BF16)
    before_e = (lax.broadcasted_iota(jnp.int32, (N_EXPERTS, N_EXPERTS), 1)
                < lax.broadcasted_iota(jnp.int32, (N_EXPERTS, N_EXPERTS), 0)).astype(F32).astype(BF16)
    rank = jnp.dot(sel_b, before_t, preferred_element_type=F32)
    order = jnp.dot(before_e, sel_b, preferred_element_type=F32)
    cnt = jnp.sum(esel, axis=1, keepdims=True)
    cnt16 = jnp.floor((cnt + (ROW_ALIGN - 1)) / ROW_ALIGN) * ROW_ALIGN
    off = jnp.dot(before_e, jnp.broadcast_to(cnt16, (N_EXPERTS, tm)).astype(BF16),
                  preferred_element_type=F32)
    dest = off + rank
    dks, gks = [], []
    for k in range(TOP_K):
        kth = (esel > 0.0) & (order == k)
        dks.append(jnp.sum(jnp.where(kth, dest, 0.0), axis=0, keepdims=True))
        gks.append(jnp.sum(jnp.where(kth, gate, 0.0), axis=0, keepdims=True))
    dk = jnp.concatenate(dks, axis=0)
    dk_ref[...] = dk
    gk_ref[...] = jnp.concatenate(gks, axis=0)
    counts = _dot_nt(jnp.ones((8, tm), BF16), sel_b)
    cnt_ref[0] = jnp.concatenate([counts, jnp.zeros((8, LANES - N_EXPERTS), F32)], axis=1)


def _oproj(oa, ob, h0, woa, wob, g, b, wrt, rb, alpha, tm):
    n, d = h0.shape
    nt = n // tm
    row = lambda w: pl.BlockSpec((tm, w), lambda i: (i, 0))
    col = lambda: pl.BlockSpec((TOP_K, tm), lambda i: (0, i))
    full = lambda a: pl.BlockSpec(a.shape, lambda i: (0, 0))
    kt = jax.ShapeDtypeStruct((TOP_K, n), F32)
    return pl.pallas_call(
        functools.partial(_oproj_kernel, alpha=alpha),
        grid=(nt,),
        in_specs=[row(512), row(512), row(d), full(woa), full(wob), full(g), full(b),
                  full(wrt), full(rb)],
        out_specs=(row(d), row(d), col(), col(),
                   pl.BlockSpec((1, 8, LANES), lambda i: (i, 0, 0))),
        out_shape=(jax.ShapeDtypeStruct((n, d), F32),
                   jax.ShapeDtypeStruct((n, d), BF16),
                   kt,
                   kt,
                   jax.ShapeDtypeStruct((nt, 8, LANES), F32)),
        compiler_params=_cparams("parallel"),
    )(oa, ob, h0, woa, wob, g, b, wrt, rb)


def _swiglu(x, wg, wu, wd):
    hid = (jax.nn.silu(jnp.dot(x, wg, preferred_element_type=F32))
           * jnp.dot(x, wu, preferred_element_type=F32))
    return jnp.dot(hid.astype(BF16), wd, preferred_element_type=F32)


def _dispatch_kernel(tile_off_ref, exp_off_ref, len_ref, used_ref, gap_off_ref, gap_len_ref,
                     h1b_ref, dk_ref, xg_ref, xs_ref, zero_ref, run_sem, gap_sem, *, rb):
    i = pl.program_id(0)
    nt = pl.num_programs(0)
    rt = xs_ref.shape[1]
    slot = i % 2

    def wait_runs(step):
        n = pl.multiple_of(used_ref[step], ROW_ALIGN)
        s = step % 2
        pltpu.make_async_copy(xs_ref.at[s, pl.ds(0, n), :], xg_ref.at[pl.ds(0, n), :],
                              run_sem.at[s]).wait()

    @pl.when(i == 0)
    def _():
        zero_ref[...] = jnp.zeros(zero_ref.shape, zero_ref.dtype)

        def issue_gap(g, carry):
            n = pl.multiple_of(gap_len_ref[g], ROW_ALIGN)

            @pl.when(n > 0)
            def _():
                pltpu.make_async_copy(
                    zero_ref.at[pl.ds(0, n), :],
                    xg_ref.at[pl.ds(pl.multiple_of(gap_off_ref[g], ROW_ALIGN), n), :],
                    gap_sem).start()
            return carry

        lax.fori_loop(0, gap_len_ref.shape[0], issue_gap, 0)

    @pl.when(i >= 2)
    def _():
        wait_runs(i - 2)

    used = used_ref[i]
    dk = dk_ref[...]

    def sort_rows(r0):
        onehot = _scatter_rows(dk, None, r0, rb).astype(BF16)
        rows = jnp.dot(onehot, h1b_ref[...], preferred_element_type=F32)
        xs_ref[slot, r0:r0 + rb, :] = rows.astype(BF16)

    always = h1b_ref.shape[0] * TOP_K // rb * rb
    for r0 in range(0, always, rb):
        sort_rows(r0)
    for r0 in range(always, rt, rb):
        @pl.when(r0 < used)
        def _():
            sort_rows(r0)

    def issue_run(e, carry):
        r = i * N_EXPERTS + e
        n = pl.multiple_of(len_ref[r], ROW_ALIGN)

        @pl.when(n > 0)
        def _():
            pltpu.make_async_copy(
                xs_ref.at[slot, pl.ds(pl.multiple_of(tile_off_ref[r], ROW_ALIGN), n), :],
                xg_ref.at[pl.ds(pl.multiple_of(exp_off_ref[r], ROW_ALIGN), n), :],
                run_sem.at[slot]).start()
        return carry

    lax.fori_loop(0, N_EXPERTS, issue_run, 0)

    @pl.when(i == nt - 1)
    def _():
        @pl.when(i >= 1)
        def _():
            wait_runs(i - 1)
        wait_runs(i)

        def gap_rows(g, rows):
            return rows + gap_len_ref[g]
        n_gap = pl.multiple_of(lax.fori_loop(0, gap_len_ref.shape[0], gap_rows, 0), ROW_ALIGN)

        @pl.when(n_gap > 0)
        def _():
            pltpu.make_async_copy(xg_ref.at[pl.ds(0, n_gap), :], xg_ref.at[pl.ds(0, n_gap), :],
                                  gap_sem).wait()


def _dispatch(h1b, dk, tile_off, exp_off, lens, used, gap_off, gap_len, n_dst_rows, tm, rt):
    n, d = h1b.shape
    return pl.pallas_call(
        functools.partial(_dispatch_kernel, rb=512),
        grid_spec=pltpu.PrefetchScalarGridSpec(
            num_scalar_prefetch=6,
            grid=(n // tm,),
            in_specs=[pl.BlockSpec((tm, d), lambda i, *_: (i, 0)),
                      pl.BlockSpec((TOP_K, tm), lambda i, *_: (0, i))],
            out_specs=pl.BlockSpec(memory_space=pl.ANY),
            scratch_shapes=[pltpu.VMEM((2, rt, d), BF16), pltpu.VMEM((rt, d), BF16),
                            pltpu.SemaphoreType.DMA((2,)), pltpu.SemaphoreType.DMA(())]),
        out_shape=jax.ShapeDtypeStruct((n_dst_rows, d), BF16),
        compiler_params=_cparams("arbitrary"),
    )(tile_off, exp_off, lens, used, gap_off, gap_len, h1b, dk)


def _expert_kernel(blk_e_ref, blk_valid_ref, x_ref, wg_ref, wu_ref, wd_ref, y_ref):
    b = pl.program_id(0)
    valid = blk_valid_ref[b]

    @pl.when(valid > 0)
    def _():
        y_ref[...] = _swiglu(x_ref[...], wg_ref[...].astype(BF16), wu_ref[...].astype(BF16),
                             wd_ref[...].astype(BF16)).astype(BF16)

    @pl.when(valid <= 0)
    def _():
        y_ref[...] = jnp.zeros(y_ref.shape, BF16)


def _experts(xg, blk_e, blk_valid, wg, wu, wd):
    nr, d = xg.shape
    f = wg.shape[2]
    rows = pl.BlockSpec((MOE_BLOCK, d), lambda b, be, bv: (b, 0))
    return pl.pallas_call(
        _expert_kernel,
        grid_spec=pltpu.PrefetchScalarGridSpec(
            num_scalar_prefetch=2,
            grid=(nr // MOE_BLOCK,),
            in_specs=[rows,
                      pl.BlockSpec((None, d, f), lambda b, be, bv: (be[b], 0, 0)),
                      pl.BlockSpec((None, d, f), lambda b, be, bv: (be[b], 0, 0)),
                      pl.BlockSpec((None, f, d), lambda b, be, bv: (be[b], 0, 0))],
            out_specs=rows),
        out_shape=jax.ShapeDtypeStruct((nr, d), BF16),
        compiler_params=_cparams("parallel"),
    )(blk_e, blk_valid, xg, wg, wu, wd)


def _combine_kernel(tile_off_ref, exp_off_ref, len_ref, used_ref, yg_ref, dk_ref, gk_ref, h1_ref,
                    h1b_ref, p_ref, sg_ref, su_ref, sd_ref, g_ref, b_ref, wg_ref, bg_ref, wp_ref,
                    o_ref, ys_ref, ffn_ref, sem, *, alpha, rb):
    i = pl.program_id(0)
    nt = pl.num_programs(0)
    rt = ys_ref.shape[1]
    slot = i % 2

    def fetch_runs(tile):
        s = tile % 2

        def issue_run(e, carry):
            r = tile * N_EXPERTS + e
            n = pl.multiple_of(len_ref[r], ROW_ALIGN)

            @pl.when(n > 0)
            def _():
                pltpu.make_async_copy(
                    yg_ref.at[pl.ds(pl.multiple_of(exp_off_ref[r], ROW_ALIGN), n), :],
                    ys_ref.at[s, pl.ds(pl.multiple_of(tile_off_ref[r], ROW_ALIGN), n), :],
                    sem.at[s]).start()
            return carry

        lax.fori_loop(0, N_EXPERTS, issue_run, 0)

    @pl.when(i == 0)
    def _():
        ys_ref[...] = jnp.zeros(ys_ref.shape, ys_ref.dtype)
        fetch_runs(i)

    @pl.when(i + 1 < nt)
    def _():
        fetch_runs(i + 1)

    dk = dk_ref[...]
    gk = gk_ref[...]
    ffn = _swiglu(h1b_ref[...], sg_ref[...], su_ref[...], sd_ref[...])
    used = used_ref[i]
    n_used = pl.multiple_of(used, ROW_ALIGN)
    pltpu.make_async_copy(yg_ref.at[pl.ds(0, n_used), :], ys_ref.at[slot, pl.ds(0, n_used), :],
                          sem.at[slot]).wait()

    def routed(r0):
        weights = _scatter_rows(dk, gk, r0, rb).astype(BF16)
        return lax.dot_general(weights, ys_ref[slot, r0:r0 + rb, :], (((0,), (0,)), ((), ())),
                               preferred_element_type=F32)

    tm = h1_ref.shape[0]
    always = tm * TOP_K // rb * rb
    for r0 in range(0, always, rb):
        ffn = ffn + routed(r0)
    ffn_ref[...] = ffn
    for r0 in range(always, rt, rb):
        @pl.when(r0 < used)
        def _():
            ffn_ref[...] += routed(r0)
    h2 = _layer_norm(alpha * h1_ref[...] + ffn_ref[...], g_ref[...], b_ref[...])
    emb_gate = jax.nn.sigmoid(jnp.dot(h2.astype(BF16), wg_ref[...], preferred_element_type=F32)
                              + bg_ref[...])
    emb = jnp.dot(p_ref[...].astype(BF16), wp_ref[...], preferred_element_type=F32)
    o_ref[...] = h2 + emb_gate * emb


def _combine(tile_off, exp_off, lens, used, yg, dk, gk, h1, h1b, p2, sg, su, sd, g, b, wg, bg, wp,
             alpha, tm, rt):
    n, d = h1.shape
    row = lambda w: pl.BlockSpec((tm, w), lambda i, *_: (i, 0))
    col = lambda: pl.BlockSpec((TOP_K, tm), lambda i, *_: (0, i))
    full = lambda a: pl.BlockSpec(a.shape, lambda i, *_: (0, 0))
    return pl.pallas_call(
        functools.partial(_combine_kernel, alpha=alpha, rb=512),
        grid_spec=pltpu.PrefetchScalarGridSpec(
            num_scalar_prefetch=4,
            grid=(n // tm,),
            in_specs=[pl.BlockSpec(memory_space=pl.ANY), col(), col(), row(d),
                      row(d), row(p2.shape[1]), full(sg), full(su), full(sd), full(g), full(b),
                      full(wg), full(bg), full(wp)],
            out_specs=row(d),
            scratch_shapes=[pltpu.VMEM((2, rt, d), yg.dtype), pltpu.VMEM((tm, d), F32),
                            pltpu.SemaphoreType.DMA((2,))]),
        out_shape=jax.ShapeDtypeStruct((n, d), F32),
        compiler_params=_cparams("arbitrary"),
    )(tile_off, exp_off, lens, used, yg, dk, gk, h1, h1b, p2, sg, su, sd, g, b, wg, bg, wp)


def _tiles(seq):
    tm = min(256, seq)
    tm_proj = min(512, seq)
    tq_sel = min(512, seq)
    tk_sel = min(512, seq)
    tq_att = min(512, seq)
    tk_att = min(512, seq)
    rt = -(-(tm * TOP_K + N_EXPERTS * (ROW_ALIGN - 1)) // 512) * 512
    return tm, tm_proj, tq_sel, tk_sel, tq_att, tk_att, rt


def kernel(x, p, positions, ln_emb_g, ln_emb_b, w_in, w_o, diff_lq1, diff_lk1, diff_lq2, diff_lk2,
           diff_subln_g, ln1_g, ln1_b, w_router, router_bias, w_exp_gate, w_exp_up, w_exp_down,
           w_sh_gate, w_sh_up, w_sh_down, ln2_g, ln2_b, w_ple_gate, b_ple_gate, w_ple_proj):
    bsz, seq, d = x.shape
    depth = w_in.shape[0]
    n = bsz * seq
    n_keep = min(IDX_TOPK_MAX, seq // 4)
    alpha = (2.0 * depth) ** 0.25
    tm, tm_proj, tq_sel, tk_sel, tq_att, tk_att, rt = _tiles(seq)
    n_blocks = -(-((n // tm) * (tm * TOP_K + N_EXPERTS * (ROW_ALIGN - 1))
                   + N_EXPERTS * (MOE_BLOCK - 1)) // MOE_BLOCK)
    row1 = lambda v: v.reshape(1, -1)

    inv = ROPE_THETA ** (-jnp.arange(0, ROPE_DIM, 2, dtype=F32) / ROPE_DIM)
    inv_lanes = jnp.tile(inv, LANES // (ROPE_DIM // 2)).reshape(1, LANES)
    pos2 = positions.reshape(n, 1)

    h = x.reshape(n, d)
    out = None
    for li in range(depth):
        lambda_init = 0.8 - 0.6 * math.exp(-0.3 * li)
        w = w_in[li]
        aqw, akw, avw = w[:, 0:512], w[:, 512:1024], w[:, 1024:1536]
        iqw, ikw, iww = w[:, 1536:2048], w[:, 2048:2112], w[:, 2112:2120]
        bqw, bkw, bvw = w[:, 2120:2632], w[:, 2632:3144], w[:, 3144:3656]
        w_rope = jnp.concatenate([aqw, akw, iqw, ikw, ikw, bqw, bkw], axis=1).astype(BF16)
        w_plain = jnp.concatenate([avw, bvw], axis=1).astype(BF16)
        w_idx_t = jnp.concatenate(
            [iww.T, jnp.zeros((LANES - IDX_HEADS, d), F32)], axis=0).astype(BF16)

        (h0, aq, ak, iq, ik2, bq, bk, av, bv, iw) = _proj(
            h, pos2, inv_lanes, row1(ln_emb_g), row1(ln_emb_b), w_rope, w_plain, w_idx_t,
            tm_proj)
        assert depth == 1

        r3 = lambda a: a.reshape(bsz, seq, a.shape[-1])
        bias = _select(r3(iq), iw, r3(ik2), n_keep, tq_sel, tk_sel)
        out_a = _dsa(r3(aq), r3(ak), r3(av), bias, tq_att, tk_att)
        out_b = _diff(r3(bq), r3(bk), r3(bv), row1(diff_lq1[li]), row1(diff_lk1[li]),
                      row1(diff_lq2[li]), row1(diff_lk2[li]), row1(diff_subln_g[li]),
                      lambda_init, tq_att, tk_att)

        wo = w_o[li].astype(BF16)
        h1, h1b, dk, gk, cnt_out = _oproj(
            out_a.reshape(n, -1), out_b.reshape(n, -1), h0, wo[0:512], wo[512:1024],
            row1(ln1_g[li]), row1(ln1_b[li]), w_router[li].T, router_bias[li].reshape(-1, 1),
            alpha, tm)

        nt = n // tm
        c16 = jnp.ceil(cnt_out[:, 0, :N_EXPERTS] / ROW_ALIGN) * ROW_ALIGN
        tri = lambda m: jnp.tril(jnp.ones((m, m), F32))
        hi = lax.Precision.HIGHEST
        tile_off = jnp.dot(c16, tri(N_EXPERTS).T, precision=hi) - c16
        cum_tiles = jnp.dot(tri(nt), c16, precision=hi)
        tot = cum_tiles[-1]
        totp = jnp.ceil(tot / MOE_BLOCK) * MOE_BLOCK
        pend = jnp.dot(totp, tri(N_EXPERTS).T, precision=hi)
        ebase = pend - totp
        exp_off = ebase[None, :] + cum_tiles - c16
        blk0 = jnp.arange(n_blocks, dtype=F32) * MOE_BLOCK
        blk_e = jnp.minimum(jnp.sum(pend[None, :] <= blk0[:, None], axis=1), N_EXPERTS - 1)
        onehot_e = (blk_e[:, None] == jnp.arange(N_EXPERTS)[None, :]).astype(F32)
        blk_end = jnp.dot(onehot_e, ebase + tot, precision=hi)
        blk_valid = jnp.clip(blk_end - blk0, 0, MOE_BLOCK)
        flat = lambda a: a.reshape(-1).astype(jnp.int32)
        n_rows = n_blocks * MOE_BLOCK
        tail0 = pend[-1] + jnp.arange(-(-n_rows // rt), dtype=F32) * rt
        gap_off = jnp.concatenate([ebase + tot, tail0])
        gap_len = jnp.concatenate([totp - tot, jnp.clip(n_rows - tail0, 0, rt)])

        weg, weu, wed = w_exp_gate[li], w_exp_up[li], w_exp_down[li]
        used = jnp.sum(c16, axis=1)
        xg = _dispatch(h1b, dk, flat(tile_off), flat(exp_off), flat(c16), flat(used),
                       flat(gap_off), flat(gap_len), n_rows, tm, rt)
        yg = _experts(xg, flat(blk_e), flat(blk_valid), weg, weu, wed)
        out = _combine(flat(tile_off), flat(exp_off), flat(c16), flat(used), yg, dk, gk, h1, h1b,
                       p[li].reshape(n, -1), w_sh_gate[li].astype(BF16),
                       w_sh_up[li].astype(BF16), w_sh_down[li].astype(BF16), row1(ln2_g[li]),
                       row1(ln2_b[li]), w_ple_gate[li].astype(BF16), row1(b_ple_gate[li]),
                       w_ple_proj[li].astype(BF16), alpha, tm, rt)
        h = out
    return out.reshape(bsz, seq, d)
```

```python
import functools
import math

import jax
import jax.numpy as jnp
from jax import lax
from jax.experimental import pallas as pl
from jax.experimental.pallas import tpu as pltpu

CHUNK = 64
HEAD_DIM = 64
ROPE_DIM = 64
ROPE_THETA = 10000.0
A_HEADS = 8
IDX_HEADS = 8
IDX_DIM = 64
IDX_TOPK_MAX = 256
B_HEADS = 4
B_QK_DIM = 64
B_V_DIM = 128
N_EXPERTS = 64
TOP_K = 8
N_GROUPS = 8
TOPK_GROUPS = 4
ROUTED_SCALE = 2.5
LN_EPS = 1e-5
RMS_EPS = 1e-5

LANES = 128
ROW_ALIGN = 16
SCATTER_SUB = 256
MOE_BLOCK = 1024
VMEM_LIMIT = 56 * 1024 * 1024
NEG_BIG = -1e30
INT_MIN = -(2 ** 31)
LOG2E = math.log2(math.e)
GROUPS = 256

F32 = jnp.float32
BF16 = jnp.bfloat16


def _cparams(*sem):
    return pltpu.CompilerParams(dimension_semantics=sem, vmem_limit_bytes=VMEM_LIMIT)


def _layer_norm(x, g, b):
    mu = jnp.mean(x, axis=-1, keepdims=True)
    xc = x - mu
    var = jnp.mean(xc * xc, axis=-1, keepdims=True)
    return xc * lax.rsqrt(var + LN_EPS) * g + b


def _dot_nt(a, b):
    return lax.dot_general(a, b, (((1,), (1,)), ((), ())), preferred_element_type=F32)


def _split_lane_halves(pair):
    lo = lax.broadcasted_iota(jnp.int32, pair.shape, 1) < (LANES // 2)
    pf = pair.astype(F32)
    return (jnp.where(lo, pf, 0.0).astype(pair.dtype), jnp.where(lo, 0.0, pf).astype(pair.dtype))


def _proj_kernel(x_ref, pos_ref, inv_ref, g_ref, b_ref, wr_ref, wp_ref, wi_ref,
                 h0_ref, aq_ref, ak_ref, iq_ref, ik_ref, bq_ref, bk_ref,
                 av_ref, bv_ref, iw_ref):
    hn = _layer_norm(x_ref[...], g_ref[...], b_ref[...])
    h0_ref[...] = hn
    hb = hn.astype(BF16)

    tm = x_ref.shape[0]
    nq = LANES // (ROPE_DIM // 2)
    tr = tm // nq
    pos = pos_ref[...].astype(F32)
    lane = lax.broadcasted_iota(jnp.int32, (tr, LANES), 1)
    blk = lane // (ROPE_DIM // 2)
    pos_packed = jnp.broadcast_to(pos[0:tr], (tr, LANES))
    for q in range(1, nq):
        pos_packed = jnp.where(blk == q, pos[q * tr:(q + 1) * tr], pos_packed)
    ang = pos_packed * inv_ref[...]
    cos_packed = jnp.cos(ang)
    sin_packed = jnp.sin(ang)
    first_half = (lane % ROPE_DIM) < (ROPE_DIM // 2)

    def spread(table, q):
        own = jnp.where(blk == q, table, 0.0)
        t = own
        for s in range(1, nq):
            t = t + pltpu.roll(own, s * (ROPE_DIM // 2), axis=1)
        return t

    pr = jnp.dot(hb, wr_ref[...], preferred_element_type=F32)
    outs = ((aq_ref, 0, 4, HEAD_DIM ** -0.5 * LOG2E), (ak_ref, 4, 4, 1.0), (iq_ref, 8, 4, 1.0),
            (ik_ref, 12, 1, 1.0), (bq_ref, 13, 4, B_QK_DIM ** -0.5 * LOG2E), (bk_ref, 17, 4, 1.0))
    for q in range(nq):
        cos = spread(cos_packed, q)
        sin = spread(sin_packed, q)
        sin_signed = jnp.where(first_half, -sin, sin)
        rows = slice(q * tr, (q + 1) * tr)
        for ref, g0, ng, scale in outs:
            for j in range(ng):
                v = pr[rows, (g0 + j) * LANES:(g0 + j + 1) * LANES]
                partner = jnp.where(first_half,
                                    pltpu.roll(v, LANES - ROPE_DIM // 2, axis=1),
                                    pltpu.roll(v, ROPE_DIM // 2, axis=1))
                v = v * cos + partner * sin_signed
                if scale != 1.0:
                    v = v * scale
                ref[rows, j * LANES:(j + 1) * LANES] = v.astype(ref.dtype)

    pp = jnp.dot(hb, wp_ref[...], preferred_element_type=F32)
    av_ref[...] = pp[:, 0:512].astype(BF16)
    bv_ref[...] = pp[:, 512:1024].astype(BF16)
    iw_ref[...] = _dot_nt(wi_ref[...], hb)[0:IDX_HEADS, :]


def _proj(x2, pos2, inv, g, b, w_rope, w_plain, w_idx_t, tm):
    n, d = x2.shape
    row = lambda w: pl.BlockSpec((tm, w), lambda i: (i, 0))
    full = lambda a: pl.BlockSpec(a.shape, lambda i: (0, 0))
    out_shape = (
        jax.ShapeDtypeStruct((n, d), F32),
        jax.ShapeDtypeStruct((n, 512), BF16),
        jax.ShapeDtypeStruct((n, 512), BF16),
        jax.ShapeDtypeStruct((n, 512), BF16),
        jax.ShapeDtypeStruct((n, 128), BF16),
        jax.ShapeDtypeStruct((n, 512), BF16),
        jax.ShapeDtypeStruct((n, 512), BF16),
        jax.ShapeDtypeStruct((n, 512), BF16),
        jax.ShapeDtypeStruct((n, 512), BF16),
        jax.ShapeDtypeStruct((IDX_HEADS, n), F32),
    )
    return pl.pallas_call(
        _proj_kernel,
        grid=(n // tm,),
        in_specs=[row(d), row(1), full(inv), full(g), full(b), full(w_rope), full(w_plain),
                  full(w_idx_t)],
        out_specs=(row(d), row(512), row(512), row(512), row(128), row(512), row(512),
                   row(512), row(512), pl.BlockSpec((IDX_HEADS, tm), lambda i: (0, i))),
        out_shape=out_shape,
        compiler_params=_cparams("parallel"),
    )(x2, pos2, inv, g, b, w_rope, w_plain, w_idx_t)


def _sortable(s):
    i = lax.bitcast_convert_type(s, jnp.int32)
    return i ^ ((i >> 31) & 0x7FFFFFFF)


def _select_kernel(iq_ref, iw_ref, ik_ref, bias_ref, keys_ref, gmax_ref, *, tq, tk, n_keep,
                   w_scale):
    i = pl.program_id(1)
    key_drop = 32 - 8 * jnp.dtype(BF16).itemsize
    seq = ik_ref.shape[1]
    n_t = ((i + 1) * tq + tk - 1) // tk

    q_chunk = (i * tq + lax.broadcasted_iota(jnp.int32, (tk, tq), 1)) // CHUNK
    k_row = lax.broadcasted_iota(jnp.int32, (tk, tq), 0)
    wb = iw_ref[...] * w_scale
    q_halves = []
    for hp in range(IDX_HEADS // 2):
        q_halves.extend(_split_lane_halves(iq_ref[0, :, hp * LANES:(hp + 1) * LANES]))

    def score_tile(t, carry, boundary):
        off = pl.multiple_of(t * tk, tk)
        kt = ik_ref[0, pl.ds(off, tk), :]
        sc = jnp.zeros((tk, tq), F32)
        for h in range(IDX_HEADS):
            rel = jnp.maximum(_dot_nt(kt, q_halves[h]), 0.0)
            sc = sc + wb[h:h + 1, :] * rel
        key = _sortable(sc.astype(BF16).astype(F32) + 0.0) >> key_drop
        if boundary:
            key = jnp.where(((off + k_row) // CHUNK) <= q_chunk, key, INT_MIN)
        keys_ref[pl.ds(off, tk), :] = key
        gm = key[0:GROUPS, :]
        for g0 in range(GROUPS, tk, GROUPS):
            gm = jnp.maximum(gm, key[g0:g0 + GROUPS, :])
        gmax_ref[...] = jnp.maximum(gmax_ref[...], gm)
        return carry

    gmax_ref[...] = jnp.full(gmax_ref.shape, INT_MIN, jnp.int32)
    n_full = (i * tq + CHUNK) // tk
    lax.fori_loop(0, n_full, functools.partial(score_tile, boundary=False), 0)
    lax.fori_loop(n_full, n_t, functools.partial(score_tile, boundary=True), 0)

    def count_ge(cand):
        def body(t, acc):
            off = pl.multiple_of(t * tk, tk)
            m = (keys_ref[pl.ds(off, tk), :] >= cand).astype(jnp.int32)
            return acc + jnp.sum(m.reshape(tk // 8, 8, tq), axis=0)
        acc = lax.fori_loop(0, n_t, body, jnp.zeros((8, tq), jnp.int32))
        return jnp.sum(acc, axis=0, keepdims=True)

    gmax = gmax_ref[...]
    n_adm = ((i * tq + lax.broadcasted_iota(jnp.int32, (1, tq), 1)) // CHUNK + 1) * CHUNK
    few = n_adm <= n_keep
    lo0 = jnp.where(few, INT_MIN + 1, jnp.min(gmax, axis=0, keepdims=True))
    hi0 = jnp.where(few, INT_MIN + 2, jnp.max(gmax, axis=0, keepdims=True) + 1)

    def probe(state):
        lo, hi, n_hi, thr, n_thr, done, _, it = state
        mid = (lo >> 1) + (hi >> 1) + (lo & hi & 1)
        closed = mid == lo
        cnt = count_ge(mid)
        exact = cnt == n_keep
        up = cnt >= n_keep
        live = done == 0
        settle = live & (exact | closed)
        thr = jnp.where(settle, mid, thr)
        n_thr = jnp.where(settle, cnt, n_thr)
        lo = jnp.where(live & up, mid, lo)
        down = live & jnp.logical_not(up)
        hi = jnp.where(down, mid, hi)
        n_hi = jnp.where(down, cnt, n_hi)
        done = jnp.where(exact | closed, 1, done)
        return lo, hi, n_hi, thr, n_thr, done, jnp.min(done), it + 1

    def searching(state):
        return (state[6] == 0) & (state[7] < 34)

    done0 = few.astype(jnp.int32)
    none = jnp.zeros((1, tq), jnp.int32)
    state = lax.while_loop(
        searching, probe, (lo0, hi0, none, lo0, none, done0, jnp.min(done0), jnp.int32(0)))
    n_above, thr, n_ge = state[2], state[3], state[4]

    tied = n_ge > n_keep
    any_tied = jnp.max(tied.astype(jnp.int32)) > 0
    tau = thr

    def write_bias(off, keep_t):
        bias_t = jnp.where(keep_t, 0.0, NEG_BIG)
        bias_ref[0, :, pl.ds(off, tk)] = bias_t.T.astype(BF16)

    @pl.when(jnp.logical_not(any_tied))
    def _():
        def write_tile(t, carry):
            off = pl.multiple_of(t * tk, tk)
            write_bias(off, keys_ref[pl.ds(off, tk), :] >= tau)
            return carry
        lax.fori_loop(0, n_t, write_tile, 0)

    @pl.when(any_tied)
    def _():
        room = jnp.where(tied, (n_keep - n_above).astype(F32), float(seq))
        upto = (lax.broadcasted_iota(jnp.int32, (tk, tk), 1)
                <= lax.broadcasted_iota(jnp.int32, (tk, tk), 0)).astype(F32).astype(BF16)

        def write_tile(t, seen_eq):
            off = pl.multiple_of(t * tk, tk)
            key = keys_ref[pl.ds(off, tk), :]
            eq_f = jnp.where(key == tau, 1.0, 0.0)
            eq_rank = seen_eq + jnp.dot(upto, eq_f.astype(BF16), preferred_element_type=F32)
            order = jnp.where(key == tau, eq_rank, jnp.where(key > tau, 0.0, 2.0 * seq))
            write_bias(off, order <= room)
            return seen_eq + jnp.sum(eq_f, axis=0, keepdims=True)

        lax.fori_loop(0, n_t, write_tile, jnp.zeros((1, tq), F32))

    def blank_tile(t, carry):
        off = pl.multiple_of(t * tk, tk)
        bias_ref[0, :, pl.ds(off, tk)] = jnp.full((tq, tk), NEG_BIG, BF16)
        return carry

    lax.fori_loop(n_t, seq // tk, blank_tile, 0)


def _select(iq, iw, ik2, n_keep, tq, tk):
    bsz, seq, _ = iq.shape
    kern = functools.partial(_select_kernel, tq=tq, tk=tk, n_keep=n_keep,
                             w_scale=(IDX_HEADS ** -0.5) * (IDX_DIM ** -0.5))
    return pl.pallas_call(
        kern,
        grid=(bsz, seq // tq),
        in_specs=[pl.BlockSpec((1, tq, 512), lambda b, i: (b, i, 0)),
                  pl.BlockSpec((IDX_HEADS, tq), lambda b, i: (0, b * (seq // tq) + i)),
                  pl.BlockSpec((1, seq, 128), lambda b, i: (b, 0, 0))],
        out_specs=pl.BlockSpec((1, tq, seq), lambda b, i: (b, i, 0)),
        out_shape=jax.ShapeDtypeStruct((bsz, seq, seq), BF16),
        scratch_shapes=[pltpu.VMEM((seq, tq), jnp.int32), pltpu.VMEM((GROUPS, tq), jnp.int32)],
        compiler_params=_cparams("parallel", "parallel"),
    )(iq, iw, ik2)


def _flash_update(j, s, v_ext, m_ref, acc_ref):
    tk = s.shape[1]
    m_old = m_ref[j]
    m_new = jnp.maximum(m_old, jnp.max(s, axis=1, keepdims=True))
    alpha = jnp.exp2(m_old - m_new)
    p = jnp.exp2(s - jnp.concatenate([m_new] * (tk // LANES), axis=1))
    m_ref[j] = m_new
    acc_ref[j] = (acc_ref[j] * jnp.concatenate([alpha, alpha], axis=1)
                  + jnp.dot(p.astype(BF16), v_ext, preferred_element_type=F32))


def _dsa_kernel(q_ref, k_ref, v_ref, bias_ref, o_ref, m_ref, acc_ref, *, tq, tk):
    i = pl.program_id(1)
    n_keys = (i + 1) * tq
    lo_q = lax.broadcasted_iota(jnp.int32, (tq, LANES), 1) < HEAD_DIM
    q_halves = []
    for hp in range(A_HEADS // 2):
        q_halves.extend(_split_lane_halves(q_ref[0, :, hp * LANES:(hp + 1) * LANES]))

    m_ref[...] = jnp.full(m_ref.shape, NEG_BIG, F32)
    acc_ref[...] = jnp.zeros(acc_ref.shape, F32)

    def kv_tile(off, width):
        bias = bias_ref[0, :, pl.ds(off, width)].astype(F32)
        ones = jnp.ones((width, LANES), BF16)
        for hp in range(A_HEADS // 2):
            kp = k_ref[0, pl.ds(off, width), hp * LANES:(hp + 1) * LANES]
            v_ext = jnp.concatenate(
                [v_ref[0, pl.ds(off, width), hp * LANES:(hp + 1) * LANES], ones], axis=1)
            for half in range(2):
                h = 2 * hp + half
                _flash_update(h, _dot_nt(q_halves[h], kp) + bias, v_ext, m_ref, acc_ref)

    def kv_step(t, carry):
        kv_tile(pl.multiple_of(t * tk, tk), tk)
        return carry

    n_whole = n_keys // tk
    lax.fori_loop(0, n_whole, kv_step, 0)
    if tk > tq:
        @pl.when(n_keys % tk != 0)
        def _():
            kv_tile(pl.multiple_of(n_whole * tk, tq), tq)

    for hp in range(A_HEADS // 2):
        a0 = acc_ref[2 * hp]
        a1 = acc_ref[2 * hp + 1]
        o = jnp.where(lo_q, a0[:, :LANES] / a0[:, LANES:], a1[:, :LANES] / a1[:, LANES:])
        o_ref[0, :, hp * LANES:(hp + 1) * LANES] = o.astype(o_ref.dtype)


def _dsa(aq, ak, av, bias, tq, tk):
    bsz, seq, w = aq.shape
    kern = functools.partial(_dsa_kernel, tq=tq, tk=tk)
    resident = lambda: pl.BlockSpec((1, seq, w), lambda b, i: (b, 0, 0),
                                    pipeline_mode=pl.Buffered(1))
    return pl.pallas_call(
        kern,
        grid=(bsz, seq // tq),
        in_specs=[pl.BlockSpec((1, tq, w), lambda b, i: (b, i, 0)),
                  resident(), resident(),
                  pl.BlockSpec((1, tq, seq), lambda b, i: (b, i, 0))],
        out_specs=pl.BlockSpec((1, tq, w), lambda b, i: (b, i, 0)),
        out_shape=jax.ShapeDtypeStruct((bsz, seq, w), BF16),
        scratch_shapes=[pltpu.VMEM((A_HEADS, tq, LANES), F32),
                        pltpu.VMEM((A_HEADS, tq, 2 * LANES), F32)],
        compiler_params=_cparams("parallel", "arbitrary"),
    )(aq, ak, av, bias)


def _diff_kernel(q_ref, k_ref, v_ref, lq1_ref, lk1_ref, lq2_ref, lk2_ref, g_ref, o_ref,
                 m_ref, acc_ref, *, tq, tk, lambda_init):
    i = pl.program_id(1)
    q_halves = []
    for hb in range(B_HEADS):
        q_halves.extend(_split_lane_halves(q_ref[0, :, hb * LANES:(hb + 1) * LANES]))
    lam = (jnp.exp(jnp.sum(lq1_ref[...] * lk1_ref[...], keepdims=True))
           - jnp.exp(jnp.sum(lq2_ref[...] * lk2_ref[...], keepdims=True)) + lambda_init)

    m_ref[...] = jnp.full(m_ref.shape, NEG_BIG, F32)
    acc_ref[...] = jnp.zeros(acc_ref.shape, F32)

    def tile(off, width, masked):
        ones = jnp.ones((width, LANES), BF16)
        if masked:
            row_chunk = (i * tq + lax.broadcasted_iota(jnp.int32, (tq, width), 0)) // CHUNK
            col_chunk = (off + lax.broadcasted_iota(jnp.int32, (tq, width), 1)) // CHUNK
            ok = col_chunk <= row_chunk
        for hb in range(B_HEADS):
            kp = k_ref[0, pl.ds(off, width), hb * LANES:(hb + 1) * LANES]
            v_ext = jnp.concatenate(
                [v_ref[0, pl.ds(off, width), hb * LANES:(hb + 1) * LANES], ones], axis=1)
            for mp in range(2):
                j = 2 * hb + mp
                s = _dot_nt(q_halves[j], kp)
                if masked:
                    s = jnp.where(ok, s, NEG_BIG)
                _flash_update(j, s, v_ext, m_ref, acc_ref)

    def kv_step(t, carry):
        tile(pl.multiple_of(t * tk, tk), tk, False)
        return carry

    n_whole = (i * tq) // tk
    lax.fori_loop(0, n_whole, kv_step, 0)
    if tk > tq:
        @pl.when((i * tq) % tk != 0)
        def _():
            tile(pl.multiple_of(n_whole * tk, tq), tq, False)
    tile(pl.multiple_of(i * tq, tq), tq, True)

    for hb in range(B_HEADS):
        a1 = acc_ref[2 * hb]
        a2 = acc_ref[2 * hb + 1]
        o = a1[:, :LANES] / a1[:, LANES:] - lam * (a2[:, :LANES] / a2[:, LANES:])
        o = o * lax.rsqrt(jnp.mean(o * o, axis=-1, keepdims=True) + RMS_EPS)
        o = o * g_ref[...] * (1.0 - lambda_init)
        o_ref[0, :, hb * LANES:(hb + 1) * LANES] = o.astype(o_ref.dtype)


def _diff(bq, bk, bv, lq1, lk1, lq2, lk2, subln_g, lambda_init, tq, tk):
    bsz, seq, w = bq.shape
    kern = functools.partial(_diff_kernel, tq=tq, tk=tk, lambda_init=lambda_init)
    resident = lambda: pl.BlockSpec((1, seq, w), lambda b, i: (b, 0, 0),
                                    pipeline_mode=pl.Buffered(1))
    small = lambda a: pl.BlockSpec(a.shape, lambda b, i: (0, 0))
    return pl.pallas_call(
        kern,
        grid=(bsz, seq // tq),
        in_specs=[pl.BlockSpec((1, tq, w), lambda b, i: (b, i, 0)),
                  resident(), resident(),
                  small(lq1), small(lk1), small(lq2), small(lk2), small(subln_g)],
        out_specs=pl.BlockSpec((1, tq, w), lambda b, i: (b, i, 0)),
        out_shape=jax.ShapeDtypeStruct((bsz, seq, w), BF16),
        scratch_shapes=[pltpu.VMEM((2 * B_HEADS, tq, LANES), F32),
                        pltpu.VMEM((2 * B_HEADS, tq, 2 * LANES), F32)],
        compiler_params=_cparams("parallel", "arbitrary"),
    )(bq, bk, bv, lq1, lk1, lq2, lk2, subln_g)


def _first_argmax_rows(v, row):
    m = jnp.max(v, axis=0, keepdims=True)
    idx = jnp.min(jnp.where(v == m, row, v.shape[0]), axis=0, keepdims=True)
    return m, idx


def _scatter_rows(dk, gk, r0, nrows):
    tm = dk.shape[1]
    pack = 32 // (8 * jnp.dtype(BF16).itemsize) * 8
    r_iota = lax.broadcasted_iota(jnp.int32, (SCATTER_SUB, tm), 0).astype(F32).astype(BF16)
    r_iota = r_iota.reshape(SCATTER_SUB // pack, pack, tm)
    pieces = []
    for b0 in range(r0, r0 + nrows, SCATTER_SUB):
        rel = dk - b0
        rel = jnp.where((rel >= 0) & (rel < SCATTER_SUB), rel, -1.0)
        out = jnp.zeros(r_iota.shape, BF16)
        for k in range(dk.shape[0]):
            at = jnp.broadcast_to(rel[k:k + 1, :], (pack, tm)).astype(BF16)[None]
            if gk is None:
                val = jnp.ones((), BF16)
            else:
                val = jnp.broadcast_to(gk[k:k + 1, :], (pack, tm)).astype(BF16)[None]
            out = jnp.where(r_iota == at, val, out)
        pieces.append(out.reshape(SCATTER_SUB, tm))
    return pieces[0] if len(pieces) == 1 else jnp.concatenate(pieces, axis=0)


def _oproj_kernel(oa_ref, ob_ref, h0_ref, woa_ref, wob_ref, g_ref, b_ref, wrt_ref, rb_ref,
                  h1_ref, h1b_ref, dk_ref, gk_ref, cnt_ref, *, alpha):
    mix = (jnp.dot(oa_ref[...], woa_ref[...], preferred_element_type=F32)
           + jnp.dot(ob_ref[...], wob_ref[...], preferred_element_type=F32))
    h1 = _layer_norm(alpha * h0_ref[...] + mix, g_ref[...], b_ref[...])
    h1_ref[...] = h1
    h1b = h1.astype(BF16)
    h1b_ref[...] = h1b

    logits = lax.dot_general(wrt_ref[...], h1, (((1,), (1,)), ((), ())),
                             precision=lax.Precision.HIGHEST, preferred_element_type=F32)
    scores = jax.nn.sigmoid(logits)
    biased = scores + rb_ref[...]
    tm = scores.shape[1]
    per_g = N_EXPERTS // N_GROUPS
    row8 = lax.broadcasted_iota(jnp.int32, (per_g, tm), 0)

    gs = []
    for g in range(N_GROUPS):
        blk = biased[g * per_g:(g + 1) * per_g, :]
        m1, i1 = _first_argmax_rows(blk, row8)
        m2 = jnp.max(jnp.where(row8 == i1, -jnp.inf, blk), axis=0, keepdims=True)
        gs.append(m1 + m2)
    gscore = jnp.concatenate(gs, axis=0)
    rowg = lax.broadcasted_iota(jnp.int32, (N_GROUPS, tm), 0)
    gsel = jnp.zeros((N_GROUPS, tm), F32)
    for _ in range(TOPK_GROUPS):
        _, ig = _first_argmax_rows(jnp.where(gsel > 0.0, -jnp.inf, gscore), rowg)
        gsel = jnp.where(rowg == ig, 1.0, gsel)

    rowe = lax.broadcasted_iota(jnp.int32, (N_EXPERTS, tm), 0)
    live = jnp.concatenate(
        [jnp.broadcast_to(gsel[g:g + 1, :], (per_g, tm)) for g in range(N_GROUPS)], axis=0)
    esel = jnp.zeros((N_EXPERTS, tm), F32)
    for _ in range(TOP_K):
        cand = jnp.where(live > 0.0, biased, -jnp.inf)
        m = jnp.max(cand, axis=0, keepdims=True)
        idx = jnp.min(jnp.where((live > 0.0) & (cand == m), rowe, N_EXPERTS),
                      axis=0, keepdims=True)
        hit = rowe == idx
        esel = jnp.where(hit, 1.0, esel)
        live = jnp.where(hit, 0.0, live)
    picked = jnp.where(esel > 0.0, scores, 0.0)
    denom = jnp.sum(picked, axis=0, keepdims=True)
    gate = picked / denom * ROUTED_SCALE

    sel_b = esel.astype(BF16)
    before_t = (lax.broadcasted_iota(jnp.int32, (tm, tm), 0)
                < lax.broadcasted_iota(jnp.int32, (tm, tm), 1)).astype(F32).astype(BF16)
    before_e = (lax.broadcasted_iota(jnp.int32, (N_EXPERTS, N_EXPERTS), 1)
                < lax.broadcasted_iota(jnp.int32, (N_EXPERTS, N_EXPERTS), 0)).astype(F32).astype(BF16)
    rank = jnp.dot(sel_b, before_t, preferred_element_type=F32)
    order = jnp.dot(before_e, sel_b, preferred_element_type=F32)
    cnt = jnp.sum(esel, axis=1, keepdims=True)
    cnt16 = jnp.floor((cnt + (ROW_ALIGN - 1)) / ROW_ALIGN) * ROW_ALIGN
    off = jnp.dot(before_e, jnp.broadcast_to(cnt16, (N_EXPERTS, tm)).astype(BF16),
                  preferred_element_type=F32)
    dest = off + rank
    dks, gks = [], []
    for k in range(TOP_K):
        kth = (esel > 0.0) & (order == k)
        dks.append(jnp.sum(jnp.where(kth, dest, 0.0), axis=0, keepdims=True))
        gks.append(jnp.sum(jnp.where(kth, gate, 0.0), axis=0, keepdims=True))
    dk = jnp.concatenate(dks, axis=0)
    dk_ref[...] = dk
    gk_ref[...] = jnp.concatenate(gks, axis=0)
    counts = _dot_nt(jnp.ones((8, tm), BF16), sel_b)
    cnt_ref[0] = jnp.concatenate([counts, jnp.zeros((8, LANES - N_EXPERTS), F32)], axis=1)


def _oproj(oa, ob, h0, woa, wob, g, b, wrt, rb, alpha, tm):
    n, d = h0.shape
    nt = n // tm
    row = lambda w: pl.BlockSpec((tm, w), lambda i: (i, 0))
    col = lambda: pl.BlockSpec((TOP_K, tm), lambda i: (0, i))
    full = lambda a: pl.BlockSpec(a.shape, lambda i: (0, 0))
    kt = jax.ShapeDtypeStruct((TOP_K, n), F32)
    return pl.pallas_call(
        functools.partial(_oproj_kernel, alpha=alpha),
        grid=(nt,),
        in_specs=[row(512), row(512), row(d), full(woa), full(wob), full(g), full(b),
                  full(wrt), full(rb)],
        out_specs=(row(d), row(d), col(), col(),
                   pl.BlockSpec((1, 8, LANES), lambda i: (i, 0, 0))),
        out_shape=(jax.ShapeDtypeStruct((n, d), F32),
                   jax.ShapeDtypeStruct((n, d), BF16),
                   kt,
                   kt,
                   jax.ShapeDtypeStruct((nt, 8, LANES), F32)),
        compiler_params=_cparams("parallel"),
    )(oa, ob, h0, woa, wob, g, b, wrt, rb)


def _swiglu(x, wg, wu, wd):
    hid = (jax.nn.silu(jnp.dot(x, wg, preferred_element_type=F32))
           * jnp.dot(x, wu, preferred_element_type=F32))
    return jnp.dot(hid.astype(BF16), wd, preferred_element_type=F32)


def _dispatch_kernel(tile_off_ref, exp_off_ref, len_ref, used_ref, gap_off_ref, gap_len_ref,
                     h1b_ref, dk_ref, xg_ref, xs_ref, zero_ref, run_sem, gap_sem, *, rb):
    i = pl.program_id(0)
    nt = pl.num_programs(0)
    rt = xs_ref.shape[1]
    slot = i % 2

    def wait_runs(step):
        n = pl.multiple_of(used_ref[step], ROW_ALIGN)
        s = step % 2
        pltpu.make_async_copy(xs_ref.at[s, pl.ds(0, n), :], xg_ref.at[pl.ds(0, n), :],
                              run_sem.at[s]).wait()

    @pl.when(i == 0)
    def _():
        zero_ref[...] = jnp.zeros(zero_ref.shape, zero_ref.dtype)

        def issue_gap(g, carry):
            n = pl.multiple_of(gap_len_ref[g], ROW_ALIGN)

            @pl.when(n > 0)
            def _():
                pltpu.make_async_copy(
                    zero_ref.at[pl.ds(0, n), :],
                    xg_ref.at[pl.ds(pl.multiple_of(gap_off_ref[g], ROW_ALIGN), n), :],
                    gap_sem).start()
            return carry

        lax.fori_loop(0, gap_len_ref.shape[0], issue_gap, 0)

    @pl.when(i >= 2)
    def _():
        wait_runs(i - 2)

    used = used_ref[i]
    dk = dk_ref[...]

    def sort_rows(r0):
        onehot = _scatter_rows(dk, None, r0, rb).astype(BF16)
        rows = jnp.dot(onehot, h1b_ref[...], preferred_element_type=F32)
        xs_ref[slot, r0:r0 + rb, :] = rows.astype(BF16)

    always = h1b_ref.shape[0] * TOP_K // rb * rb
    for r0 in range(0, always, rb):
        sort_rows(r0)
    for r0 in range(always, rt, rb):
        @pl.when(r0 < used)
        def _():
            sort_rows(r0)

    def issue_run(e, carry):
        r = i * N_EXPERTS + e
        n = pl.multiple_of(len_ref[r], ROW_ALIGN)

        @pl.when(n > 0)
        def _():
            pltpu.make_async_copy(
                xs_ref.at[slot, pl.ds(pl.multiple_of(tile_off_ref[r], ROW_ALIGN), n), :],
                xg_ref.at[pl.ds(pl.multiple_of(exp_off_ref[r], ROW_ALIGN), n), :],
                run_sem.at[slot]).start()
        return carry

    lax.fori_loop(0, N_EXPERTS, issue_run, 0)

    @pl.when(i == nt - 1)
    def _():
        @pl.when(i >= 1)
        def _():
            wait_runs(i - 1)
        wait_runs(i)

        def gap_rows(g, rows):
            return rows + gap_len_ref[g]
        n_gap = pl.multiple_of(lax.fori_loop(0, gap_len_ref.shape[0], gap_rows, 0), ROW_ALIGN)

        @pl.when(n_gap > 0)
        def _():
            pltpu.make_async_copy(xg_ref.at[pl.ds(0, n_gap), :], xg_ref.at[pl.ds(0, n_gap), :],
                                  gap_sem).wait()


def _dispatch(h1b, dk, tile_off, exp_off, lens, used, gap_off, gap_len, n_dst_rows, tm, rt):
    n, d = h1b.shape
    return pl.pallas_call(
        functools.partial(_dispatch_kernel, rb=512),
        grid_spec=pltpu.PrefetchScalarGridSpec(
            num_scalar_prefetch=6,
            grid=(n // tm,),
            in_specs=[pl.BlockSpec((tm, d), lambda i, *_: (i, 0)),
                      pl.BlockSpec((TOP_K, tm), lambda i, *_: (0, i))],
            out_specs=pl.BlockSpec(memory_space=pl.ANY),
            scratch_shapes=[pltpu.VMEM((2, rt, d), BF16), pltpu.VMEM((rt, d), BF16),
                            pltpu.SemaphoreType.DMA((2,)), pltpu.SemaphoreType.DMA(())]),
        out_shape=jax.ShapeDtypeStruct((n_dst_rows, d), BF16),
        compiler_params=_cparams("arbitrary"),
    )(tile_off, exp_off, lens, used, gap_off, gap_len, h1b, dk)


def _expert_kernel(blk_e_ref, blk_valid_ref, x_ref, wg_ref, wu_ref, wd_ref, y_ref):
    b = pl.program_id(0)
    valid = blk_valid_ref[b]

    @pl.when(valid > 0)
    def _():
        y_ref[...] = _swiglu(x_ref[...], wg_ref[...].astype(BF16), wu_ref[...].astype(BF16),
                             wd_ref[...].astype(BF16)).astype(BF16)

    @pl.when(valid <= 0)
    def _():
        y_ref[...] = jnp.zeros(y_ref.shape, BF16)


def _experts(xg, blk_e, blk_valid, wg, wu, wd):
    nr, d = xg.shape
    f = wg.shape[2]
    rows = pl.BlockSpec((MOE_BLOCK, d), lambda b, be, bv: (b, 0))
    return pl.pallas_call(
        _expert_kernel,
        grid_spec=pltpu.PrefetchScalarGridSpec(
            num_scalar_prefetch=2,
            grid=(nr // MOE_BLOCK,),
            in_specs=[rows,
                      pl.BlockSpec((None, d, f), lambda b, be, bv: (be[b], 0, 0)),
                      pl.BlockSpec((None, d, f), lambda b, be, bv: (be[b], 0, 0)),
                      pl.BlockSpec((None, f, d), lambda b, be, bv: (be[b], 0, 0))],
            out_specs=rows),
        out_shape=jax.ShapeDtypeStruct((nr, d), BF16),
        compiler_params=_cparams("parallel"),
    )(blk_e, blk_valid, xg, wg, wu, wd)


def _combine_kernel(tile_off_ref, exp_off_ref, len_ref, used_ref, yg_ref, dk_ref, gk_ref, h1_ref,
                    h1b_ref, p_ref, sg_ref, su_ref, sd_ref, g_ref, b_ref, wg_ref, bg_ref, wp_ref,
                    o_ref, ys_ref, ffn_ref, sem, *, alpha, rb):
    i = pl.program_id(0)
    nt = pl.num_programs(0)
    rt = ys_ref.shape[1]
    slot = i % 2

    def fetch_runs(tile):
        s = tile % 2

        def issue_run(e, carry):
            r = tile * N_EXPERTS + e
            n = pl.multiple_of(len_ref[r], ROW_ALIGN)

            @pl.when(n > 0)
            def _():
                pltpu.make_async_copy(
                    yg_ref.at[pl.ds(pl.multiple_of(exp_off_ref[r], ROW_ALIGN), n), :],
                    ys_ref.at[s, pl.ds(pl.multiple_of(tile_off_ref[r], ROW_ALIGN), n), :],
                    sem.at[s]).start()
            return carry

        lax.fori_loop(0, N_EXPERTS, issue_run, 0)

    @pl.when(i == 0)
    def _():
        ys_ref[...] = jnp.zeros(ys_ref.shape, ys_ref.dtype)
        fetch_runs(i)

    @pl.when(i + 1 < nt)
    def _():
        fetch_runs(i + 1)

    dk = dk_ref[...]
    gk = gk_ref[...]
    ffn = _swiglu(h1b_ref[...], sg_ref[...], su_ref[...], sd_ref[...])
    used = used_ref[i]
    n_used = pl.multiple_of(used, ROW_ALIGN)
    pltpu.make_async_copy(yg_ref.at[pl.ds(0, n_used), :], ys_ref.at[slot, pl.ds(0, n_used), :],
                          sem.at[slot]).wait()

    def routed(r0):
        weights = _scatter_rows(dk, gk, r0, rb).astype(BF16)
        return lax.dot_general(weights, ys_ref[slot, r0:r0 + rb, :], (((0,), (0,)), ((), ())),
                               preferred_element_type=F32)

    tm = h1_ref.shape[0]
    always = tm * TOP_K // rb * rb
    for r0 in range(0, always, rb):
        ffn = ffn + routed(r0)
    ffn_ref[...] = ffn
    for r0 in range(always, rt, rb):
        @pl.when(r0 < used)
        def _():
            ffn_ref[...] += routed(r0)
    h2 = _layer_norm(alpha * h1_ref[...] + ffn_ref[...], g_ref[...], b_ref[...])
    emb_gate = jax.nn.sigmoid(jnp.dot(h2.astype(BF16), wg_ref[...], preferred_element_type=F32)
                              + bg_ref[...])
    emb = jnp.dot(p_ref[...].astype(BF16), wp_ref[...], preferred_element_type=F32)
    o_ref[...] = h2 + emb_gate * emb


def _combine(tile_off, exp_off, lens, used, yg, dk, gk, h1, h1b, p2, sg, su, sd, g, b, wg, bg, wp,
             alpha, tm, rt):
    n, d = h1.shape
    row = lambda w: pl.BlockSpec((tm, w), lambda i, *_: (i, 0))
    col = lambda: pl.BlockSpec((TOP_K, tm), lambda i, *_: (0, i))
    full = lambda a: pl.BlockSpec(a.shape, lambda i, *_: (0, 0))
    return pl.pallas_call(
        functools.partial(_combine_kernel, alpha=alpha, rb=512),
        grid_spec=pltpu.PrefetchScalarGridSpec(
            num_scalar_prefetch=4,
            grid=(n // tm,),
            in_specs=[pl.BlockSpec(memory_space=pl.ANY), col(), col(), row(d),
                      row(d), row(p2.shape[1]), full(sg), full(su), full(sd), full(g), full(b),
                      full(wg), full(bg), full(wp)],
            out_specs=row(d),
            scratch_shapes=[pltpu.VMEM((2, rt, d), yg.dtype), pltpu.VMEM((tm, d), F32),
                            pltpu.SemaphoreType.DMA((2,))]),
        out_shape=jax.ShapeDtypeStruct((n, d), F32),
        compiler_params=_cparams("arbitrary"),
    )(tile_off, exp_off, lens, used, yg, dk, gk, h1, h1b, p2, sg, su, sd, g, b, wg, bg, wp)


def _tiles(seq):
    tm = min(256, seq)
    tm_proj = min(512, seq)
    tq_sel = min(512, seq)
    tk_sel = min(512, seq)
    tq_att = min(512, seq)
    tk_att = min(512, seq)
    rt = -(-(tm * TOP_K + N_EXPERTS * (ROW_ALIGN - 1)) // 512) * 512
    return tm, tm_proj, tq_sel, tk_sel, tq_att, tk_att, rt


def kernel(x, p, positions, ln_emb_g, ln_emb_b, w_in, w_o, diff_lq1, diff_lk1, diff_lq2, diff_lk2,
           diff_subln_g, ln1_g, ln1_b, w_router, router_bias, w_exp_gate, w_exp_up, w_exp_down,
           w_sh_gate, w_sh_up, w_sh_down, ln2_g, ln2_b, w_ple_gate, b_ple_gate, w_ple_proj):
    bsz, seq, d = x.shape
    depth = w_in.shape[0]
    n = bsz * seq
    n_keep = min(IDX_TOPK_MAX, seq // 4)
    alpha = (2.0 * depth) ** 0.25
    tm, tm_proj, tq_sel, tk_sel, tq_att, tk_att, rt = _tiles(seq)
    n_blocks = -(-((n // tm) * (tm * TOP_K + N_EXPERTS * (ROW_ALIGN - 1))
                   + N_EXPERTS * (MOE_BLOCK - 1)) // MOE_BLOCK)
    row1 = lambda v: v.reshape(1, -1)

    inv = ROPE_THETA ** (-jnp.arange(0, ROPE_DIM, 2, dtype=F32) / ROPE_DIM)
    inv_lanes = jnp.tile(inv, LANES // (ROPE_DIM // 2)).reshape(1, LANES)
    pos2 = positions.reshape(n, 1)

    h = x.reshape(n, d)
    out = None
    for li in range(depth):
        lambda_init = 0.8 - 0.6 * math.exp(-0.3 * li)
        w = w_in[li]
        aqw, akw, avw = w[:, 0:512], w[:, 512:1024], w[:, 1024:1536]
        iqw, ikw, iww = w[:, 1536:2048], w[:, 2048:2112], w[:, 2112:2120]
        bqw, bkw, bvw = w[:, 2120:2632], w[:, 2632:3144], w[:, 3144:3656]
        w_rope = jnp.concatenate([aqw, akw, iqw, ikw, ikw, bqw, bkw], axis=1).astype(BF16)
        w_plain = jnp.concatenate([avw, bvw], axis=1).astype(BF16)
        w_idx_t = jnp.concatenate(
            [iww.T, jnp.zeros((LANES - IDX_HEADS, d), F32)], axis=0).astype(BF16)

        (h0, aq, ak, iq, ik2, bq, bk, av, bv, iw) = _proj(
            h, pos2, inv_lanes, row1(ln_emb_g), row1(ln_emb_b), w_rope, w_plain, w_idx_t,
            tm_proj)
        assert depth == 1

        r3 = lambda a: a.reshape(bsz, seq, a.shape[-1])
        bias = _select(r3(iq), iw, r3(ik2), n_keep, tq_sel, tk_sel)
        out_a = _dsa(r3(aq), r3(ak), r3(av), bias, tq_att, tk_att)
        out_b = _diff(r3(bq), r3(bk), r3(bv), row1(diff_lq1[li]), row1(diff_lk1[li]),
                      row1(diff_lq2[li]), row1(diff_lk2[li]), row1(diff_subln_g[li]),
                      lambda_init, tq_att, tk_att)

        wo = w_o[li].astype(BF16)
        h1, h1b, dk, gk, cnt_out = _oproj(
            out_a.reshape(n, -1), out_b.reshape(n, -1), h0, wo[0:512], wo[512:1024],
            row1(ln1_g[li]), row1(ln1_b[li]), w_router[li].T, router_bias[li].reshape(-1, 1),
            alpha, tm)

        nt = n // tm
        c16 = jnp.ceil(cnt_out[:, 0, :N_EXPERTS] / ROW_ALIGN) * ROW_ALIGN
        tri = lambda m: jnp.tril(jnp.ones((m, m), F32))
        hi = lax.Precision.HIGHEST
        tile_off = jnp.dot(c16, tri(N_EXPERTS).T, precision=hi) - c16
        cum_tiles = jnp.dot(tri(nt), c16, precision=hi)
        tot = cum_tiles[-1]
        totp = jnp.ceil(tot / MOE_BLOCK) * MOE_BLOCK
        pend = jnp.dot(totp, tri(N_EXPERTS).T, precision=hi)
        ebase = pend - totp
        exp_off = ebase[None, :] + cum_tiles - c16
        blk0 = jnp.arange(n_blocks, dtype=F32) * MOE_BLOCK
        blk_e = jnp.minimum(jnp.sum(pend[None, :] <= blk0[:, None], axis=1), N_EXPERTS - 1)
        onehot_e = (blk_e[:, None] == jnp.arange(N_EXPERTS)[None, :]).astype(F32)
        blk_end = jnp.dot(onehot_e, ebase + tot, precision=hi)
        blk_valid = jnp.clip(blk_end - blk0, 0, MOE_BLOCK)
        flat = lambda a: a.reshape(-1).astype(jnp.int32)
        n_rows = n_blocks * MOE_BLOCK
        tail0 = pend[-1] + jnp.arange(-(-n_rows // rt), dtype=F32) * rt
        gap_off = jnp.concatenate([ebase + tot, tail0])
        gap_len = jnp.concatenate([totp - tot, jnp.clip(n_rows - tail0, 0, rt)])

        weg, weu, wed = w_exp_gate[li], w_exp_up[li], w_exp_down[li]
        used = jnp.sum(c16, axis=1)
        xg = _dispatch(h1b, dk, flat(tile_off), flat(exp_off), flat(c16), flat(used),
                       flat(gap_off), flat(gap_len), n_rows, tm, rt)
        yg = _experts(xg, flat(blk_e), flat(blk_valid), weg, weu, wed)
        out = _combine(flat(tile_off), flat(exp_off), flat(c16), flat(used), yg, dk, gk, h1, h1b,
                       p[li].reshape(n, -1), w_sh_gate[li].astype(BF16),
                       w_sh_up[li].astype(BF16), w_sh_down[li].astype(BF16), row1(ln2_g[li]),
                       row1(ln2_b[li]), w_ple_gate[li].astype(BF16), row1(b_ple_gate[li]),
                       w_ple_proj[li].astype(BF16), alpha, tm, rt)
        h = out
    return out.reshape(bsz, seq, d)
```
